```python
import math
import jax
import jax.numpy as jnp
from jax import lax
import numpy as np

D_MODEL = 1024
BATCH = 1
SEQ = 16384
DEPTH = 4

HEAD_DIM = 64
CHUNK = 64
FOX_HEADS = 4
FOX_BLOCK = 128
GLA_HEADS = 4
GLA_LOWRANK = 16
GLA_TAU = 16.0
RET_HEADS = 4
ROPE_THETA = 10000.0
SSD_HEADS = 8
SSD_GROUPS = 2
SSD_STATE = 64
SSD_CONV = 4
SSD_INNER = SSD_HEADS * HEAD_DIM
SSD_CONV_CH = SSD_INNER + 2 * SSD_GROUPS * SSD_STATE
FOX_W = FOX_HEADS * HEAD_DIM
GLA_W = GLA_HEADS * HEAD_DIM
RET_W = RET_HEADS * HEAD_DIM
N_BRANCH = 4
FFN_HIDDEN = ((8 * D_MODEL + 3 * 256 - 1) // (3 * 256)) * 256
NORM_EPS = 1e-6
IN_SPLITS = (FOX_W, FOX_W, FOX_W, FOX_HEADS,
             GLA_W, GLA_W, GLA_W, GLA_LOWRANK, GLA_W,
             RET_W, RET_W, RET_W, RET_W,
             SSD_INNER, SSD_CONV_CH, SSD_HEADS,
             N_BRANCH * D_MODEL)
N_IN = sum(IN_SPLITS)

kernel_name = 'hybrid_fox_gla_ret_ssd_trunk'


def rms_norm(x, g):
    xf = x.astype(jnp.float32)
    y = xf * lax.rsqrt(jnp.mean(xf * xf, axis=-1, keepdims=True) + NORM_EPS)
    return (y * g.astype(jnp.float32)).astype(x.dtype)


def split_cols(u, sizes):
    return jnp.split(u, np.cumsum(sizes)[:-1].tolist(), axis=-1)


def rotary_tables(positions):
    half = HEAD_DIM // 2
    inv = ROPE_THETA ** (-jnp.arange(half, dtype=jnp.float32) / half)
    ang = positions.astype(jnp.float32)[..., None] * inv
    return jnp.cos(ang)[:, :, None, :], jnp.sin(ang)[:, :, None, :]


def apply_rotary(x, cos, sin):
    half = x.shape[-1] // 2
    xf = x.astype(jnp.float32)
    x1, x2 = xf[..., :half], xf[..., half:]
    return jnp.concatenate([x1 * cos - x2 * sin, x1 * sin + x2 * cos], axis=-1).astype(x.dtype)


def causal_depthwise_conv(x, w, b):
    k_w, ch = w.shape
    y = lax.conv_general_dilated(x, w[:, None, :], window_strides=(1,), padding=[(k_w - 1, 0)],
                                 dimension_numbers=('NWC', 'WIO', 'NWC'), feature_group_count=ch)
    return y + b


def forgetting_attention(q, k, v, f_logit, q_gain, k_gain):
    bsz, seq, nh, dh = q.shape
    nb = seq // FOX_BLOCK
    qf = rms_norm(q, q_gain).astype(jnp.float32) * (dh ** -0.5)
    kf = rms_norm(k, k_gain).astype(jnp.float32)
    vf = v.astype(jnp.float32)
    c = jnp.cumsum(jax.nn.log_sigmoid(f_logit.astype(jnp.float32)), axis=1)
    qb = qf.reshape(bsz, nb, FOX_BLOCK, nh, dh).transpose(1, 0, 3, 2, 4)
    cb = c.reshape(bsz, nb, FOX_BLOCK, nh).transpose(1, 0, 3, 2)
    ck = c.transpose(0, 2, 1)
    kpos = jnp.arange(seq)

    def block(args):
        q_blk, c_blk, i = args
        s = jnp.einsum('bhqd,bkhd->bhqk', q_blk, kf) + c_blk[..., :, None] - ck[..., None, :]
        qpos = i * FOX_BLOCK + jnp.arange(FOX_BLOCK)
        s = jnp.where(kpos[None, :] <= qpos[:, None], s, -jnp.inf)
        p = jax.nn.softmax(s, axis=-1)
        return jnp.einsum('bhqk,bkhd->bqhd', p, vf)

    o = lax.map(block, (qb, cb, jnp.arange(nb)))
    return o.transpose(1, 0, 2, 3, 4).reshape(bsz, seq, nh * dh).astype(v.dtype)


def gla_chunked(q, k, v, log_a):
    bsz, seq, nh, dk = q.shape
    dv = v.shape[-1]
    nc = seq // CHUNK
    qf = (q.astype(jnp.float32) * (dk ** -0.5)).reshape(bsz, nc, CHUNK, nh, dk)
    kf = k.astype(jnp.float32).reshape(bsz, nc, CHUNK, nh, dk)
    vf = v.astype(jnp.float32).reshape(bsz, nc, CHUNK, nh, dv)
    b = jnp.cumsum(log_a.astype(jnp.float32).reshape(bsz, nc, CHUNK, nh, dk), axis=2)
    b_last = b[:, :, -1]
    q_dec = qf * jnp.exp(b)
    k_inv = kf * jnp.exp(-b)
    k_end = kf * jnp.exp(b_last[:, :, None] - b)
    mask = jnp.tril(jnp.ones((CHUNK, CHUNK), dtype=bool))
    attn = jnp.where(mask, jnp.einsum('bcthd,bcshd->bchts', q_dec, k_inv), 0.0)
    o_intra = jnp.einsum('bchts,bcshv->bcthv', attn, vf)
    kv = jnp.einsum('bcshd,bcshv->bchdv', k_end, vf)

    def step(state, inp):
        kv_c, dec_c = inp
        return state * dec_c[..., None] + kv_c, state

    s0 = jnp.zeros((bsz, nh, dk, dv), jnp.float32)
    _, s_in = lax.scan(step, s0, (jnp.moveaxis(kv, 1, 0), jnp.moveaxis(jnp.exp(b_last), 1, 0)))
    s_in = jnp.moveaxis(s_in, 0, 1)
    o_inter = jnp.einsum('bcthd,bchdv->bcthv', q_dec, s_in)
    return (o_intra + o_inter).reshape(bsz, seq, nh, dv).astype(v.dtype)


def retention_chunked(q, k, v, log_gamma):
    bsz, seq, nh, dk = q.shape
    dv = v.shape[-1]
    nc = seq // CHUNK
    qf = q.astype(jnp.float32).reshape(bsz, nc, CHUNK, nh, dk)
    kf = (k.astype(jnp.float32) * (dk ** -0.5)).reshape(bsz, nc, CHUNK, nh, dk)
    vf = v.astype(jnp.float32).reshape(bsz, nc, CHUNK, nh, dv)
    idx = jnp.arange(CHUNK, dtype=jnp.float32)
    rel = idx[:, None] - idx[None, :]
    decay = jnp.where(rel >= 0, jnp.exp(jnp.maximum(rel, 0.0)[None] * log_gamma[:, None, None]), 0.0)
    scores = jnp.einsum('bcthd,bcshd->bchts', qf, kf) * decay
    o_intra = jnp.einsum('bchts,bcshv->bcthv', scores, vf)
    q_w = jnp.exp((idx + 1.0)[:, None] * log_gamma)
    k_w = jnp.exp((CHUNK - 1.0 - idx)[:, None] * log_gamma)
    kv = jnp.einsum('bcshd,sh,bcshv->bchdv', kf, k_w, vf)
    chunk_decay = jnp.exp(CHUNK * log_gamma)

    def step(state, kv_c):
        return state * chunk_decay[:, None, None] + kv_c, state

    s0 = jnp.zeros((bsz, nh, dk, dv), jnp.float32)
    _, s_in = lax.scan(step, s0, jnp.moveaxis(kv, 1, 0))
    s_in = jnp.moveaxis(s_in, 0, 1)
    o_inter = jnp.einsum('bcthd,th,bchdv->bcthv', qf, q_w, s_in)
    return (o_intra + o_inter).reshape(bsz, seq, nh, dv).astype(v.dtype)


def ssd_chunked(x, dt, a, bmat, cmat):
    bsz, seq, nh, hp = x.shape
    ng, ns = bmat.shape[2], bmat.shape[3]
    r = nh // ng
    nc = seq // CHUNK
    xf = x.astype(jnp.float32).reshape(bsz, nc, CHUNK, ng, r, hp)
    dtf = dt.astype(jnp.float32).reshape(bsz, nc, CHUNK, ng, r)
    bf = bmat.astype(jnp.float32).reshape(bsz, nc, CHUNK, ng, ns)
    cf = cmat.astype(jnp.float32).reshape(bsz, nc, CHUNK, ng, ns)
    xdt = xf * dtf[..., None]
    cs = jnp.cumsum(dtf * a.astype(jnp.float32).reshape(ng, r), axis=2)
    cs_h = jnp.moveaxis(cs, 2, -1)
    mask = jnp.tril(jnp.ones((CHUNK, CHUNK), dtype=bool))
    seg = cs_h[..., :, None] - cs_h[..., None, :]
    decay = jnp.exp(jnp.where(mask, seg, -jnp.inf))
    cb = jnp.einsum('bctgn,bcsgn->bcgts', cf, bf)
    y_intra = jnp.einsum('bcgts,bcgrts,bcsgrp->bctgrp', cb, decay, xdt)
    cs_last = cs[:, :, -1]
    w_end = jnp.exp(cs_last[:, :, None] - cs)
    contrib = jnp.einsum('bcsgn,bcsgr,bcsgrp->bcgrpn', bf, w_end, xdt)

    def step(h, inp):
        c_c, d_c = inp
        return h * d_c[..., None, None] + c_c, h

    h0 = jnp.zeros((bsz, ng, r, hp, ns), jnp.float32)
    _, h_in = lax.scan(step, h0, (jnp.moveaxis(contrib, 1, 0), jnp.moveaxis(jnp.exp(cs_last), 1, 0)))
    h_in = jnp.moveaxis(h_in, 0, 1)
    y_inter = jnp.einsum('bctgn,bcgrpn,bctgr->bctgrp', cf, h_in, jnp.exp(cs))
    return (y_intra + y_inter).reshape(bsz, seq, nh, hp).astype(x.dtype)


def mixer_block(h, cos, sin, w_in, fox_bf, fox_qn, fox_kn, gla_w2, gla_b, gla_norm, ret_norm,
                conv_w, conv_b, dt_bias, a_log, d_skip, ssd_norm, w_up_a, w_up_b, w_up_c, w_up_d, w_out):
    bsz, seq, _ = h.shape
    u = h @ w_in
    (fq, fk, fv, ff, gq, gk, gv, glr, gr, rq, rk, rv, rg, z, xbc, dt, gates) = split_cols(u, IN_SPLITS)
    heads = lambda t: t.reshape(bsz, seq, -1, HEAD_DIM)
    y_a = forgetting_attention(heads(fq), heads(fk), heads(fv), ff + fox_bf, fox_qn, fox_kn)
    log_a = jax.nn.log_sigmoid((glr @ gla_w2 + gla_b).astype(jnp.float32)) / GLA_TAU
    o_b = gla_chunked(heads(gq), heads(gk), heads(gv), heads(log_a))
    y_b = (rms_norm(o_b, gla_norm) * jax.nn.silu(heads(gr))).reshape(bsz, seq, GLA_W)
    log_gamma = jnp.log(1.0 - 2.0 ** (-5.0 - jnp.arange(RET_HEADS, dtype=jnp.float32)))
    o_c = retention_chunked(apply_rotary(heads(rq), cos, sin), apply_rotary(heads(rk), cos, sin), heads(rv), log_gamma)
    y_c = (rms_norm(o_c, ret_norm) * jax.nn.silu(heads(rg))).reshape(bsz, seq, RET_W)
    xbc = jax.nn.silu(causal_depthwise_conv(xbc, conv_w, conv_b))
    xs, bm, cm = jnp.split(xbc, [SSD_INNER, SSD_INNER + SSD_GROUPS * SSD_STATE], axis=-1)
    dt = jax.nn.softplus((dt + dt_bias).astype(jnp.float32))
    a = -jnp.exp(a_log.astype(jnp.float32))
    xs_h = heads(xs)
    o_d = ssd_chunked(xs_h, dt, a, bm.reshape(bsz, seq, SSD_GROUPS, SSD_STATE), cm.reshape(bsz, seq, SSD_GROUPS, SSD_STATE))
    o_d = o_d + d_skip[:, None] * xs_h
    y_d = (o_d.reshape(bsz, seq, SSD_INNER) * jax.nn.silu(z)).reshape(bsz, seq, SSD_GROUPS, -1)
    y_d = rms_norm(y_d, ssd_norm.reshape(SSD_GROUPS, -1)).reshape(bsz, seq, SSD_INNER)
    g = jax.nn.sigmoid(gates.reshape(bsz, seq, N_BRANCH, D_MODEL))
    merged = (g[:, :, 0] * (y_a @ w_up_a) + g[:, :, 1] * (y_b @ w_up_b)
              + g[:, :, 2] * (y_c @ w_up_c) + g[:, :, 3] * (y_d @ w_up_d))
    return merged @ w_out


def swiglu(h, w_in, w_out):
    gate, up = jnp.split(h @ w_in, 2, axis=-1)
    return (jax.nn.silu(gate) * up) @ w_out


def setup_inputs(seed: int = 0) -> dict:
    key = jax.random.key(seed)
    ks = jax.random.split(key, 26)
    f32 = jnp.float32
    L = DEPTH

    def normal(k, shape, fan_in):
        return jax.random.normal(k, shape, f32) * (fan_in ** -0.5)

    def gain(k, shape):
        return 1.0 + 0.02 * jax.random.normal(k, shape, f32)

    x = jax.random.normal(ks[0], (BATCH, SEQ, D_MODEL), f32)
    start = jax.random.randint(ks[1], (BATCH, 1), 0, 1024, dtype=jnp.int32)
    positions = start + jnp.arange(SEQ, dtype=jnp.int32)[None, :]
    dt0 = jnp.exp(jax.random.uniform(ks[14], (L, SSD_HEADS), f32, math.log(1e-3), math.log(1e-1)))
    return {
        'x': x,
        'positions': positions,
        'ln1': gain(ks[2], (L, D_MODEL)),
        'ln2': gain(ks[3], (L, D_MODEL)),
        'w_in': normal(ks[4], (L, D_MODEL, N_IN), D_MODEL),
        'fox_bf': jax.random.uniform(ks[5], (L, FOX_HEADS), f32, 1.0, 5.0),
        'fox_qn': gain(ks[6], (L, HEAD_DIM)),
        'fox_kn': gain(ks[7], (L, HEAD_DIM)),
        'gla_w2': normal(ks[8], (L, GLA_LOWRANK, GLA_W), GLA_LOWRANK),
        'gla_b': 0.1 * jax.random.normal(ks[9], (L, GLA_W), f32),
        'gla_norm': gain(ks[10], (L, HEAD_DIM)),
        'ret_norm': gain(ks[11], (L, HEAD_DIM)),
        'ssd_conv_w': normal(ks[12], (L, SSD_CONV, SSD_CONV_CH), SSD_CONV),
        'ssd_conv_b': 0.01 * jax.random.normal(ks[13], (L, SSD_CONV_CH), f32),
        'ssd_dt_bias': dt0 + jnp.log(-jnp.expm1(-dt0)),
        'ssd_a_log': jnp.log(jax.random.uniform(ks[15], (L, SSD_HEADS), f32, 1.0, 16.0)),
        'ssd_d': 1.0 + 0.1 * jax.random.normal(ks[16], (L, SSD_HEADS), f32),
        'ssd_norm': gain(ks[17], (L, SSD_INNER)),
        'w_up_a': normal(ks[18], (L, FOX_W, D_MODEL), FOX_W),
        'w_up_b': normal(ks[19], (L, GLA_W, D_MODEL), GLA_W),
        'w_up_c': normal(ks[20], (L, RET_W, D_MODEL), RET_W),
        'w_up_d': normal(ks[21], (L, SSD_INNER, D_MODEL), SSD_INNER),
        'w_out': normal(ks[22], (L, D_MODEL, D_MODEL), D_MODEL),
        'w_ffn_in': normal(ks[23], (L, D_MODEL, 2 * FFN_HIDDEN), D_MODEL),
        'w_ffn_out': normal(ks[24], (L, FFN_HIDDEN, D_MODEL), FFN_HIDDEN),
    }


def reference(x, positions, ln1, ln2, w_in, fox_bf, fox_qn, fox_kn, gla_w2, gla_b, gla_norm, ret_norm,
              ssd_conv_w, ssd_conv_b, ssd_dt_bias, ssd_a_log, ssd_d, ssd_norm,
              w_up_a, w_up_b, w_up_c, w_up_d, w_out, w_ffn_in, w_ffn_out):
    cos, sin = rotary_tables(positions)
    for l in range(DEPTH):
        h = rms_norm(x, ln1[l])
        x = x + mixer_block(h, cos, sin, w_in[l], fox_bf[l], fox_qn[l], fox_kn[l], gla_w2[l], gla_b[l],
                            gla_norm[l], ret_norm[l], ssd_conv_w[l], ssd_conv_b[l], ssd_dt_bias[l],
                            ssd_a_log[l], ssd_d[l], ssd_norm[l], w_up_a[l], w_up_b[l], w_up_c[l],
                            w_up_d[l], w_out[l])
        h = rms_norm(x, ln2[l])
        x = x + swiglu(h, w_ffn_in[l], w_ffn_out[l])
    return x
```

```python
import functools
import math

import numpy as np
import jax
import jax.numpy as jnp
from jax import lax
from jax.experimental import pallas as pl
from jax.experimental.pallas import tpu as pltpu

D_MODEL = 1024
HEAD_DIM = 64
CHUNK = 64
FOX_HEADS = 4
GLA_HEADS = 4
GLA_LOWRANK = 16
GLA_TAU = 16.0
RET_HEADS = 4
ROPE_THETA = 10000.0
SSD_HEADS = 8
SSD_GROUPS = 2
SSD_STATE = 64
SSD_CONV = 4
SSD_INNER = SSD_HEADS * HEAD_DIM
SSD_CONV_CH = SSD_INNER + 2 * SSD_GROUPS * SSD_STATE
FOX_W = FOX_HEADS * HEAD_DIM
GLA_W = GLA_HEADS * HEAD_DIM
RET_W = RET_HEADS * HEAD_DIM
N_BRANCH = 4
FFN_HIDDEN = ((8 * D_MODEL + 3 * 256 - 1) // (3 * 256)) * 256
NORM_EPS = 1e-6
IN_SPLITS = (FOX_W, FOX_W, FOX_W, FOX_HEADS,
             GLA_W, GLA_W, GLA_W, GLA_LOWRANK, GLA_W,
             RET_W, RET_W, RET_W, RET_W,
             SSD_INNER, SSD_CONV_CH, SSD_HEADS,
             N_BRANCH * D_MODEL)

LANES = 128
SUBLANES = 8
VMEM_LIMIT = 56 * 1024 * 1024

ROW_TILE = 512
FOX_PREP_TILE = 256
FOX_TQ = 512
FOX_TK = 512
FFN_TILE = 512

F32 = jnp.float32
BF16 = jnp.bfloat16


def _rmsnorm(x, g):
    return x * lax.rsqrt(jnp.mean(x * x, axis=-1, keepdims=True) + NORM_EPS) * g


def _log_sigmoid(x):
    return jnp.minimum(x, 0.0) - jnp.log1p(jnp.exp(-jnp.abs(x)))


def _softplus(x):
    return jnp.maximum(x, 0.0) + jnp.log1p(jnp.exp(-jnp.abs(x)))


def _sigmoid(x):
    return 1.0 / (1.0 + jnp.exp(-x))


def _silu(x):
    return x * _sigmoid(x)


def _split3(x):
    hi = x.astype(BF16)
    r1 = x - hi.astype(F32)
    mid = r1.astype(BF16)
    lo = (r1 - mid.astype(F32)).astype(BF16)
    return hi, mid, lo


def _dot(a, b):
    return jnp.dot(a, b, preferred_element_type=F32)


def _dot_nt(a, b):
    return lax.dot_general(a, b, (((1,), (1,)), ((), ())), preferred_element_type=F32)


def _dot_tn(a, b):
    return lax.dot_general(a, b, (((0,), (0,)), ((), ())), preferred_element_type=F32)


def _sel_dot(mat, x):
    hi, mid, lo = _split3(x)
    return _dot(mat, hi) + _dot(mat, mid) + _dot(mat, lo)


def _dot_sel(x, mat):
    hi, mid, lo = _split3(x)
    return _dot(hi, mat) + _dot(mid, mat) + _dot(lo, mat)


def _chunk_masks(t):
    r = lax.broadcasted_iota(jnp.int32, (t, t), 0)
    c = lax.broadcasted_iota(jnp.int32, (t, t), 1)
    same = (r // CHUNK) == (c // CHUNK)
    incl = jnp.where(same & (c <= r), 1.0, 0.0).astype(BF16)
    after = jnp.where(same & (c > r), 1.0, 0.0).astype(BF16)
    return incl, after


def _group_mean_sq(o, group):
    w = o.shape[-1]
    r = lax.broadcasted_iota(jnp.int32, (w, w), 0)
    c = lax.broadcasted_iota(jnp.int32, (w, w), 1)
    bd = jnp.where((r // group) == (c // group), 1.0, 0.0).astype(BF16)
    return _dot_sel(o * o, bd) * (1.0 / group)


def _resident(shape):
    nd = len(shape)
    return pl.BlockSpec(shape, lambda *_: (0,) * nd, pipeline_mode=pl.Buffered(1))


def _rotary_kernel(pos_ref, inv_ref, sign_ref, cos_ref, sin_ref):
    ang = pos_ref[...] * inv_ref[...]
    cos_ref[...] = jnp.cos(ang)
    sin_ref[...] = jnp.sin(ang) * sign_ref[...]


def _rotary_tables(positions, seq):
    half = HEAD_DIM // 2
    inv = ROPE_THETA ** (-jnp.arange(half, dtype=F32) / half)
    inv_row = jnp.tile(inv, 2 * RET_HEADS).reshape(1, RET_W)
    sign_row = jnp.tile(jnp.concatenate([-jnp.ones((half,), F32), jnp.ones((half,), F32)]),
                        RET_HEADS).reshape(1, RET_W)
    pos = positions.astype(F32).reshape(seq, 1)
    t = ROW_TILE
    return pl.pallas_call(
        _rotary_kernel,
        grid=(seq // t,),
        in_specs=[pl.BlockSpec((t, 1), lambda i: (i, 0)),
                  pl.BlockSpec((1, RET_W), lambda i: (0, 0)),
                  pl.BlockSpec((1, RET_W), lambda i: (0, 0))],
        out_specs=[pl.BlockSpec((t, RET_W), lambda i: (i, 0)),
                   pl.BlockSpec((t, RET_W), lambda i: (i, 0))],
        out_shape=[jax.ShapeDtypeStruct((seq, RET_W), F32)] * 2,
        name="rotary_tables",
    )(pos, inv_row, sign_row)


FOX_AUG = FOX_W * 3


def _fox_prep_kernel(x_ref, ln_ref, w_ref, bf_ref, qn_ref, kn_ref,
                     q_out, k_out, v_out, carry_ref):
    t = x_ref.shape[0]

    @pl.when(pl.program_id(0) == 0)
    def _():
        carry_ref[...] = jnp.zeros_like(carry_ref)

    h = _rmsnorm(x_ref[...], ln_ref[...]).astype(BF16)
    u = _dot(h, w_ref[...])
    ls = _log_sigmoid(u[:, FOX_AUG:FOX_AUG + LANES] + bf_ref[...])
    r = lax.broadcasted_iota(jnp.int32, (t, t), 0)
    cc = lax.broadcasted_iota(jnp.int32, (t, t), 1)
    tri = jnp.where(cc <= r, 1.0, 0.0).astype(BF16)
    c = _sel_dot(tri, ls) + carry_ref[0:1, :]
    carry_ref[0:1, :] = c[t - 1:t, :]
    c_hi, c_mid, c_lo = _split3(c)
    c_hi, c_mid, c_lo = c_hi.astype(F32), c_mid.astype(F32), c_lo.astype(F32)

    lane = lax.broadcasted_iota(jnp.int32, (t, LANES), 1)
    low = lane < HEAD_DIM
    scale = HEAD_DIM ** -0.5

    def head_norm(pair, gain, mult):
        sq = pair * pair
        s_lo = jnp.sum(jnp.where(low, sq, 0.0), axis=-1, keepdims=True)
        s_hi = jnp.sum(jnp.where(low, 0.0, sq), axis=-1, keepdims=True)
        ms = jnp.where(low, s_lo, s_hi) * (1.0 / HEAD_DIM)
        return pair * lax.rsqrt(ms + NORM_EPS) * (gain * mult)

    for p in range(FOX_HEADS // 2):
        qp = head_norm(u[:, p * LANES:(p + 1) * LANES], qn_ref[...], scale)
        kp = head_norm(u[:, FOX_W + p * LANES:FOX_W + (p + 1) * LANES], kn_ref[...], 1.0)
        vp = u[:, 2 * FOX_W + p * LANES:2 * FOX_W + (p + 1) * LANES]
        for sub in range(2):
            hd = 2 * p + sub
            if sub == 1:
                qp, kp, vp = (pltpu.roll(a, HEAD_DIM, axis=1) for a in (qp, kp, vp))
            ch, cm, cl = (jnp.broadcast_to(a[:, hd:hd + 1], (t, LANES)) for a in (c_hi, c_mid, c_lo))
            j = lane - HEAD_DIM
            aug_q = jnp.where(j == 0, ch, jnp.where(j == 1, cm, jnp.where(j == 2, cl,
                              jnp.where(j < 6, 1.0, 0.0))))
            aug_k = jnp.where(j < 3, 1.0, jnp.where(j == 3, -ch, jnp.where(j == 4, -cm,
                              jnp.where(j == 5, -cl, 0.0))))
            aug_v = jnp.where(j == 0, 1.0, 0.0)
            q_out[hd] = jnp.where(low, qp, aug_q).astype(BF16)
            k_out[hd] = jnp.where(low, kp, aug_k).astype(BF16)
            v_out[hd] = jnp.where(low, vp, aug_v).astype(BF16)


def _fox_prep(x, ln, w_fox, bf_row, qn_row, kn_row):
    seq = x.shape[0]
    t = FOX_PREP_TILE
    out = jax.ShapeDtypeStruct((FOX_HEADS, seq, LANES), BF16)
    ospec = pl.BlockSpec((FOX_HEADS, t, LANES), lambda i: (0, i, 0))
    return pl.pallas_call(
        _fox_prep_kernel,
        grid=(seq // t,),
        in_specs=[pl.BlockSpec((t, D_MODEL), lambda i: (i, 0)),
                  _resident((1, D_MODEL)), _resident(w_fox.shape), _resident((1, LANES)),
                  _resident((1, LANES)), _resident((1, LANES))],
        out_specs=[ospec, ospec, ospec],
        out_shape=[out, out, out],
        scratch_shapes=[pltpu.VMEM((SUBLANES, LANES), F32)],
        compiler_params=pltpu.CompilerParams(dimension_semantics=("arbitrary",),
                                             vmem_limit_bytes=VMEM_LIMIT),
        name="fox_prep",
    )(x, ln, w_fox, bf_row, qn_row, kn_row)


def _fox_attn_kernel(q_ref, k_ref, v_ref, o_ref):
    i = pl.program_id(1)
    q = q_ref[0]
    tq = q.shape[0]

    def step(j, carry, masked):
        m, acc = carry
        off = pl.multiple_of(j * FOX_TK, FOX_TK)
        k = k_ref[0, pl.ds(off, FOX_TK), :]
        v = v_ref[0, pl.ds(off, FOX_TK), :]
        s = _dot_nt(q, k)
        if masked:
            r = lax.broadcasted_iota(jnp.int32, s.shape, 0)
            c = lax.broadcasted_iota(jnp.int32, s.shape, 1)
            s = jnp.where(c <= r, s, -jnp.inf)
        m_new = jnp.maximum(m, jnp.max(s, axis=-1, keepdims=True))
        p = jnp.exp(s - m_new)
        acc = jnp.exp(m - m_new) * acc + _dot(p.astype(BF16), v)
        return m_new, acc

    init = (jnp.full((tq, 1), -jnp.inf, F32), jnp.zeros((tq, LANES), F32))
    carry = lax.fori_loop(0, i, lambda j, c: step(j, c, False), init)
    _, acc = step(i, carry, True)
    o_ref[0] = (acc / acc[:, HEAD_DIM:HEAD_DIM + 1]).astype(BF16)


def _fox_attn(q, k, v):
    assert FOX_TQ == FOX_TK
    nh, seq, _ = q.shape
    kv_spec = pl.BlockSpec((1, seq, LANES), lambda h, i: (h, 0, 0))
    return pl.pallas_call(
        _fox_attn_kernel,
        grid=(nh, seq // FOX_TQ),
        in_specs=[pl.BlockSpec((1, FOX_TQ, LANES), lambda h, i: (h, i, 0)), kv_spec, kv_spec],
        out_specs=pl.BlockSpec((1, FOX_TQ, LANES), lambda h, i: (h, i, 0)),
        out_shape=jax.ShapeDtypeStruct((nh, seq, LANES), BF16),
        compiler_params=pltpu.CompilerParams(dimension_semantics=("arbitrary", "arbitrary"),
                                             vmem_limit_bytes=VMEM_LIMIT),
        name="fox_attn",
    )(q, k, v)


def _linear_attn_chunks(qd_s, ki_s, ke_s, v_s, dec_s, o_s, st_s, nheads):
    t = qd_s.shape[0]
    r = lax.broadcasted_iota(jnp.int32, (CHUNK, CHUNK), 0)
    c = lax.broadcasted_iota(jnp.int32, (CHUNK, CHUNK), 1)
    tril = c <= r

    def body(ci, _):
        rows = pl.ds(pl.multiple_of(ci * CHUNK, CHUNK), CHUNK)
        tail = pl.ds(pl.multiple_of(ci * CHUNK + (CHUNK - SUBLANES), SUBLANES), SUBLANES)
        for hd in range(nheads):
            cols = slice(hd * HEAD_DIM, (hd + 1) * HEAD_DIM)
            qd = qd_s[rows, cols].astype(BF16)
            ki = ki_s[rows, cols].astype(BF16)
            ke = ke_s[rows, cols].astype(BF16)
            vv = v_s[rows, cols].astype(BF16)
            st = st_s[hd]
            attn = jnp.where(tril, _dot_nt(qd, ki), 0.0).astype(BF16)
            o_s[rows, cols] = _dot(attn, vv) + _dot_nt(qd, st.astype(BF16))
            st_s[hd] = st * dec_s[tail, cols][SUBLANES - 1:SUBLANES, :] + _dot_tn(vv, ke)
        return 0

    lax.fori_loop(0, t // CHUNK, body, 0)


def _gla_kernel(x_ref, ln_ref, w_ref, w2_ref, b2_ref, gn_ref, y_ref,
                qd_s, ki_s, ke_s, v_s, dec_s, o_s, st_s):
    t = x_ref.shape[0]

    @pl.when(pl.program_id(0) == 0)
    def _():
        st_s[...] = jnp.zeros_like(st_s)

    h = _rmsnorm(x_ref[...], ln_ref[...]).astype(BF16)
    u = _dot(h, w_ref[...])
    z = _dot(u[:, 4 * GLA_W:4 * GLA_W + LANES].astype(BF16), w2_ref[...]) + b2_ref[...]
    log_a = _log_sigmoid(z) * (1.0 / GLA_TAU)
    incl, after = _chunk_masks(t)
    b = _sel_dot(incl, log_a)
    rem = _sel_dot(after, log_a)
    eb = jnp.exp(b)
    k = u[:, GLA_W:2 * GLA_W]
    qd_s[...] = u[:, 0:GLA_W] * (HEAD_DIM ** -0.5) * eb
    ki_s[...] = k * jnp.exp(-b)
    ke_s[...] = k * jnp.exp(rem)
    v_s[...] = u[:, 2 * GLA_W:3 * GLA_W]
    dec_s[...] = eb
    _linear_attn_chunks(qd_s, ki_s, ke_s, v_s, dec_s, o_s, st_s, GLA_HEADS)
    o = o_s[...]
    y = o * lax.rsqrt(_group_mean_sq(o, HEAD_DIM) + NORM_EPS) * gn_ref[...]
    y_ref[...] = (y * _silu(u[:, 3 * GLA_W:4 * GLA_W])).astype(BF16)


def _ret_kernel(x_ref, ln_ref, w_ref, cos_ref, sin_ref, gn_ref, y_ref,
                qd_s, ki_s, ke_s, v_s, dec_s, o_s, st_s):
    t = x_ref.shape[0]

    @pl.when(pl.program_id(0) == 0)
    def _():
        st_s[...] = jnp.zeros_like(st_s)

    h = _rmsnorm(x_ref[...], ln_ref[...]).astype(BF16)
    u = _dot(h, w_ref[...])
    cos, sin = cos_ref[...], sin_ref[...]
    q = u[:, 0:RET_W] * cos + u[:, RET_W:2 * RET_W] * sin
    k = (u[:, 2 * RET_W:3 * RET_W] * cos + u[:, 3 * RET_W:4 * RET_W] * sin) * (HEAD_DIM ** -0.5)
    lane = lax.broadcasted_iota(jnp.int32, (t, RET_W), 1)
    row = lax.broadcasted_iota(jnp.int32, (t, RET_W), 0)
    lg = jnp.zeros((t, RET_W), F32)
    for hd in range(RET_HEADS):
        lg = jnp.where(lane // HEAD_DIM == hd, math.log(1.0 - 2.0 ** (-5.0 - hd)), lg)
    pos = (row % CHUNK).astype(F32)
    qw = jnp.exp((pos + 1.0) * lg)
    qd_s[...] = q * qw
    ki_s[...] = k * jnp.exp(-(pos + 1.0) * lg)
    ke_s[...] = k * jnp.exp((CHUNK - 1.0 - pos) * lg)
    v_s[...] = u[:, 4 * RET_W:5 * RET_W]
    dec_s[...] = qw
    _linear_attn_chunks(qd_s, ki_s, ke_s, v_s, dec_s, o_s, st_s, RET_HEADS)
    o = o_s[...]
    y = o * lax.rsqrt(_group_mean_sq(o, HEAD_DIM) + NORM_EPS) * gn_ref[...]
    y_ref[...] = (y * _silu(u[:, 5 * RET_W:6 * RET_W])).astype(BF16)


def _linear_scratch(t, width, nheads):
    return ([pltpu.VMEM((t, width), F32)] * 6
            + [pltpu.VMEM((nheads, HEAD_DIM, HEAD_DIM), F32)])


def _gla(x, ln, w_gla, w2, b2_row, gn_row):
    seq = x.shape[0]
    t = ROW_TILE
    return pl.pallas_call(
        _gla_kernel,
        grid=(seq // t,),
        in_specs=[pl.BlockSpec((t, D_MODEL), lambda i: (i, 0)),
                  _resident((1, D_MODEL)), _resident(w_gla.shape), _resident(w2.shape),
                  _resident((1, GLA_W)), _resident((1, GLA_W))],
        out_specs=pl.BlockSpec((t, GLA_W), lambda i: (i, 0)),
        out_shape=jax.ShapeDtypeStruct((seq, GLA_W), BF16),
        scratch_shapes=_linear_scratch(t, GLA_W, GLA_HEADS),
        compiler_params=pltpu.CompilerParams(dimension_semantics=("arbitrary",),
                                             vmem_limit_bytes=VMEM_LIMIT),
        name="gla_mixer",
    )(x, ln, w_gla, w2, b2_row, gn_row)


def _ret(x, ln, w_ret, cos_t, sin_t, gn_row):
    seq = x.shape[0]
    t = ROW_TILE
    return pl.pallas_call(
        _ret_kernel,
        grid=(seq // t,),
        in_specs=[pl.BlockSpec((t, D_MODEL), lambda i: (i, 0)),
                  _resident((1, D_MODEL)), _resident(w_ret.shape),
                  pl.BlockSpec((t, RET_W), lambda i: (i, 0)),
                  pl.BlockSpec((t, RET_W), lambda i: (i, 0)),
                  _resident((1, RET_W))],
        out_specs=pl.BlockSpec((t, RET_W), lambda i: (i, 0)),
        out_shape=jax.ShapeDtypeStruct((seq, RET_W), BF16),
        scratch_shapes=_linear_scratch(t, RET_W, RET_HEADS),
        compiler_params=pltpu.CompilerParams(dimension_semantics=("arbitrary",),
                                             vmem_limit_bytes=VMEM_LIMIT),
        name="ret_mixer",
    )(x, ln, w_ret, cos_t, sin_t, gn_row)


SSD_PROJ = SSD_INNER + SSD_CONV_CH


def _ssd_kernel(x_ref, ln_ref, w_ref, wdt_t_ref, cw_ref, cb_ref, dtb_row_ref, dtb_col_ref,
                alog_row_ref, alog_col_ref, d_row_ref, gn_ref, y_ref,
                xp_s, xs_s, bc_s, cs_s, ecs_s, xdt_s, xdtw_s, cst_s, o_s, st_s):
    t = x_ref.shape[0]
    nc = t // CHUNK

    @pl.when(pl.program_id(0) == 0)
    def _():
        st_s[...] = jnp.zeros_like(st_s)
        xp_s[0:SUBLANES, :] = jnp.zeros((SUBLANES, SSD_CONV_CH), F32)

    h = _rmsnorm(x_ref[...], ln_ref[...]).astype(BF16)
    u = _dot(h, w_ref[...])
    z = u[:, 0:SSD_INNER]

    xp_s[SUBLANES:SUBLANES + t, :] = u[:, SSD_INNER:SSD_PROJ]
    conv = cb_ref[...] + cw_ref[SSD_CONV - 1:SSD_CONV, :] * xp_s[SUBLANES:SUBLANES + t, :]
    for kk in range(SSD_CONV - 1):
        off = SUBLANES - (SSD_CONV - 1) + kk
        conv = conv + cw_ref[kk:kk + 1, :] * xp_s[off:off + t, :]
    xp_s[0:SUBLANES, :] = xp_s[t:t + SUBLANES, :]
    xbc = _silu(conv)
    xs = xbc[:, 0:SSD_INNER]
    xs_s[...] = xs
    bc_s[...] = xbc[:, SSD_INNER:SSD_CONV_CH]

    dt = _softplus(u[:, SSD_PROJ:SSD_PROJ + LANES] + dtb_row_ref[...])
    dta = dt * -jnp.exp(alog_row_ref[...])
    dt_t = _softplus(_dot_nt(wdt_t_ref[...], h) + dtb_col_ref[...])
    dta_t = dt_t * -jnp.exp(alog_col_ref[...])
    incl, after = _chunk_masks(t)
    cs = _sel_dot(incl, dta)
    rem = _sel_dot(after, dta)
    cs_t = _dot_sel(dta_t, after + jnp.where(
        lax.broadcasted_iota(jnp.int32, (t, t), 0) == lax.broadcasted_iota(jnp.int32, (t, t), 1),
        1.0, 0.0).astype(BF16))
    for ci in range(nc):
        cst_s[ci] = cs_t[:, ci * CHUNK:(ci + 1) * CHUNK]

    er = lax.broadcasted_iota(jnp.int32, (LANES, SSD_INNER), 0)
    ec = lax.broadcasted_iota(jnp.int32, (LANES, SSD_INNER), 1)
    expand = jnp.where(ec // HEAD_DIM == er, 1.0, 0.0).astype(BF16)
    cs_x = _dot_sel(cs, expand)
    xdt = xs * _dot_sel(dt, expand)
    cs_s[...] = cs_x
    ecs_s[...] = jnp.exp(cs_x)
    xdt_s[...] = xdt
    xdtw_s[...] = xdt * jnp.exp(_dot_sel(rem, expand))

    r = lax.broadcasted_iota(jnp.int32, (CHUNK, CHUNK), 0)
    c = lax.broadcasted_iota(jnp.int32, (CHUNK, CHUNK), 1)
    tril = c <= r
    gw = SSD_GROUPS * SSD_STATE

    def body(ci, _):
        rows = pl.ds(pl.multiple_of(ci * CHUNK, CHUNK), CHUNK)
        tail = pl.ds(pl.multiple_of(ci * CHUNK + (CHUNK - SUBLANES), SUBLANES), SUBLANES)
        cst = cst_s[ci]
        for g in range(SSD_GROUPS):
            bm = bc_s[rows, g * SSD_STATE:(g + 1) * SSD_STATE].astype(BF16)
            cm = bc_s[rows, gw + g * SSD_STATE:gw + (g + 1) * SSD_STATE].astype(BF16)
            cb = _dot_nt(cm, bm)
            for rr in range(SSD_HEADS // SSD_GROUPS):
                hd = g * (SSD_HEADS // SSD_GROUPS) + rr
                cols = slice(hd * HEAD_DIM, (hd + 1) * HEAD_DIM)
                seg = cs_s[rows, cols] - cst[hd:hd + 1, :]
                decay = jnp.exp(jnp.where(tril, seg, -jnp.inf))
                st = st_s[hd]
                y = _dot((cb * decay).astype(BF16), xdt_s[rows, cols].astype(BF16))
                y = y + _dot(cm, st.astype(BF16)) * ecs_s[rows, cols]
                o_s[rows, cols] = y
                st_s[hd] = (st * ecs_s[tail, cols][SUBLANES - 1:SUBLANES, :]
                            + _dot_tn(bm, xdtw_s[rows, cols].astype(BF16)))
        return 0

    lax.fori_loop(0, nc, body, 0)

    y = (o_s[...] + d_row_ref[...] * xs_s[...]) * _silu(z)
    y = y * lax.rsqrt(_group_mean_sq(y, SSD_INNER // SSD_GROUPS) + NORM_EPS) * gn_ref[...]
    y_ref[...] = y.astype(BF16)


def _ssd(x, ln, w_ssd, wdt_t, conv_w, conv_b_row, dtb_row, dtb_col, alog_row, alog_col, d_row, gn_row):
    seq = x.shape[0]
    t = ROW_TILE
    wide = pltpu.VMEM((t, SSD_INNER), F32)
    return pl.pallas_call(
        _ssd_kernel,
        grid=(seq // t,),
        in_specs=[pl.BlockSpec((t, D_MODEL), lambda i: (i, 0)),
                  _resident((1, D_MODEL)), _resident(w_ssd.shape), _resident(wdt_t.shape),
                  _resident(conv_w.shape), _resident(conv_b_row.shape),
                  _resident((1, LANES)), _resident((SSD_HEADS, 1)),
                  _resident((1, LANES)), _resident((SSD_HEADS, 1)),
                  _resident((1, SSD_INNER)), _resident((1, SSD_INNER))],
        out_specs=pl.BlockSpec((t, SSD_INNER), lambda i: (i, 0)),
        out_shape=jax.ShapeDtypeStruct((seq, SSD_INNER), BF16),
        scratch_shapes=[pltpu.VMEM((t + 2 * SUBLANES, SSD_CONV_CH), F32),
                        wide,
                        pltpu.VMEM((t, 2 * SSD_GROUPS * SSD_STATE), F32),
                        wide, wide, wide, wide,
                        pltpu.VMEM((t // CHUNK, SSD_HEADS, CHUNK), F32),
                        wide,
                        pltpu.VMEM((SSD_HEADS, SSD_STATE, HEAD_DIM), F32)],
        compiler_params=pltpu.CompilerParams(dimension_semantics=("arbitrary",),
                                             vmem_limit_bytes=VMEM_LIMIT),
        name="ssd_mixer",
    )(x, ln, w_ssd, wdt_t, conv_w, conv_b_row, dtb_row, dtb_col, alog_row, alog_col, d_row, gn_row)


def _merge_kernel(x_ref, ln_ref, wg_ref, ya_ref, yb_ref, yc_ref, yd_ref,
                  wa_ref, wb_ref, wc_ref, wd_ref, wo_ref, out_ref):
    x = x_ref[...]
    h = _rmsnorm(x, ln_ref[...]).astype(BF16)

    def gate(b):
        return _sigmoid(_dot(h, wg_ref[:, b * D_MODEL:(b + 1) * D_MODEL]))

    up_a = _dot(ya_ref[0], wa_ref[0])
    for hd in range(1, FOX_HEADS):
        up_a = up_a + _dot(ya_ref[hd], wa_ref[hd])
    merged = gate(0) * up_a
    merged = merged + gate(1) * _dot(yb_ref[...], wb_ref[...])
    merged = merged + gate(2) * _dot(yc_ref[...], wc_ref[...])
    merged = merged + gate(3) * _dot(yd_ref[...], wd_ref[...])
    out_ref[...] = x + _dot(merged.astype(BF16), wo_ref[...])


def _merge(x, ln, w_gates, y_a, y_b, y_c, y_d, w_up_a, w_up_b, w_up_c, w_up_d, w_out):
    seq = x.shape[0]
    t = ROW_TILE
    row = lambda w: pl.BlockSpec((t, w), lambda i: (i, 0))
    return pl.pallas_call(
        _merge_kernel,
        grid=(seq // t,),
        in_specs=[row(D_MODEL), _resident((1, D_MODEL)), _resident(w_gates.shape),
                  pl.BlockSpec((FOX_HEADS, t, LANES), lambda i: (0, i, 0)),
                  row(GLA_W), row(RET_W), row(SSD_INNER),
                  _resident(w_up_a.shape), _resident(w_up_b.shape), _resident(w_up_c.shape),
                  _resident(w_up_d.shape), _resident(w_out.shape)],
        out_specs=row(D_MODEL),
        out_shape=jax.ShapeDtypeStruct((seq, D_MODEL), F32),
        compiler_params=pltpu.CompilerParams(dimension_semantics=("arbitrary",),
                                             vmem_limit_bytes=VMEM_LIMIT),
        name="gated_merge",
    )(x, ln, w_gates, y_a, y_b, y_c, y_d, w_up_a, w_up_b, w_up_c, w_up_d, w_out)


def _ffn_kernel(x_ref, ln_ref, w1_ref, w2_ref, out_ref):
    x = x_ref[...]
    h = _rmsnorm(x, ln_ref[...]).astype(BF16)
    gate = _dot(h, w1_ref[:, 0:FFN_HIDDEN])
    up = _dot(h, w1_ref[:, FFN_HIDDEN:2 * FFN_HIDDEN])
    out_ref[...] = x + _dot((_silu(gate) * up).astype(BF16), w2_ref[...])


def _ffn(x, ln, w1, w2):
    seq = x.shape[0]
    t = FFN_TILE
    return pl.pallas_call(
        _ffn_kernel,
        grid=(seq // t,),
        in_specs=[pl.BlockSpec((t, D_MODEL), lambda i: (i, 0)),
                  _resident((1, D_MODEL)), _resident(w1.shape), _resident(w2.shape)],
        out_specs=pl.BlockSpec((t, D_MODEL), lambda i: (i, 0)),
        out_shape=jax.ShapeDtypeStruct((seq, D_MODEL), F32),
        compiler_params=pltpu.CompilerParams(dimension_semantics=("arbitrary",),
                                             vmem_limit_bytes=VMEM_LIMIT),
        name="swiglu",
    )(x, ln, w1, w2)


def _pad_cols(w, width):
    return jnp.pad(w, ((0, 0), (0, width - w.shape[1])))


def _pad_row(v, width):
    v = v.reshape(1, -1)
    return jnp.pad(v, ((0, 0), (0, width - v.shape[1])))


def _swap_halves(w):
    d = w.shape[0]
    half = HEAD_DIM // 2
    w4 = w.reshape(d, -1, 2, half)
    return w4[:, :, ::-1, :].reshape(d, -1)


def _layer_weights(w_in):
    offs = np.concatenate([[0], np.cumsum(IN_SPLITS)])
    (fq, fk, fv, ff, gq, gk, gv, glr, gr, rq, rk, rv, rg, z, xbc, dt, gates) = [
        w_in[:, offs[n]:offs[n + 1]] for n in range(len(IN_SPLITS))]
    cat = lambda parts: jnp.concatenate(parts, axis=1).astype(BF16)
    w_fox = cat([fq, fk, fv, _pad_cols(ff, LANES)])
    w_gla = cat([gq, gk, gv, gr, _pad_cols(glr, LANES)])
    w_ret = cat([rq, _swap_halves(rq), rk, _swap_halves(rk), rv, rg])
    w_ssd = cat([z, xbc, _pad_cols(dt, LANES)])
    wdt_t = dt.T.astype(BF16)
    return w_fox, w_gla, w_ret, w_ssd, wdt_t, gates.astype(BF16)


def kernel(x, positions, ln1, ln2, w_in, fox_bf, fox_qn, fox_kn, gla_w2, gla_b, gla_norm, ret_norm,
           ssd_conv_w, ssd_conv_b, ssd_dt_bias, ssd_a_log, ssd_d, ssd_norm,
           w_up_a, w_up_b, w_up_c, w_up_d, w_out, w_ffn_in, w_ffn_out):
    bsz, seq, d = x.shape
    assert bsz == 1 and d == D_MODEL and seq % ROW_TILE == 0 and seq % FOX_TQ == 0
    depth = ln1.shape[0]
    xr = x.reshape(seq, d)
    cos_t, sin_t = _rotary_tables(positions, seq)
    tile_heads = lambda g, n: jnp.tile(g, n).reshape(1, -1)
    for l in range(depth):
        w_fox, w_gla, w_ret, w_ssd, wdt_t, w_gates = _layer_weights(w_in[l])
        ln = ln1[l].reshape(1, d)
        q, k, v = _fox_prep(xr, ln, w_fox, _pad_row(fox_bf[l], LANES),
                            tile_heads(fox_qn[l], 2), tile_heads(fox_kn[l], 2))
        y_a = _fox_attn(q, k, v)
        w2 = jnp.pad(gla_w2[l], ((0, LANES - GLA_LOWRANK), (0, 0))).astype(BF16)
        y_b = _gla(xr, ln, w_gla, w2, gla_b[l].reshape(1, -1), tile_heads(gla_norm[l], GLA_HEADS))
        y_c = _ret(xr, ln, w_ret, cos_t, sin_t, tile_heads(ret_norm[l], RET_HEADS))
        y_d = _ssd(xr, ln, w_ssd, wdt_t, ssd_conv_w[l], ssd_conv_b[l].reshape(1, -1),
                   _pad_row(ssd_dt_bias[l], LANES), ssd_dt_bias[l].reshape(-1, 1),
                   _pad_row(ssd_a_log[l], LANES), ssd_a_log[l].reshape(-1, 1),
                   jnp.repeat(ssd_d[l], HEAD_DIM).reshape(1, -1), ssd_norm[l].reshape(1, -1))
        wa = jnp.pad(w_up_a[l].reshape(FOX_HEADS, HEAD_DIM, d),
                     ((0, 0), (0, LANES - HEAD_DIM), (0, 0))).astype(BF16)
        xr = _merge(xr, ln, w_gates, y_a, y_b, y_c, y_d, wa, w_up_b[l].astype(BF16),
                    w_up_c[l].astype(BF16), w_up_d[l].astype(BF16), w_out[l].astype(BF16))
        xr = _ffn(xr, ln2[l].reshape(1, d), w_ffn_in[l].astype(BF16), w_ffn_out[l].astype(BF16))
    return xr.reshape(bsz, seq, d)
```

```python
import functools
import math

import numpy as np
import jax
import jax.numpy as jnp
from jax import lax
from jax.experimental import pallas as pl
from jax.experimental.pallas import tpu as pltpu

D_MODEL = 1024
HEAD_DIM = 64
CHUNK = 64
FOX_HEADS = 4
GLA_HEADS = 4
GLA_LOWRANK = 16
GLA_TAU = 16.0
RET_HEADS = 4
ROPE_THETA = 10000.0
SSD_HEADS = 8
SSD_GROUPS = 2
SSD_STATE = 64
SSD_CONV = 4
SSD_INNER = SSD_HEADS * HEAD_DIM
SSD_CONV_CH = SSD_INNER + 2 * SSD_GROUPS * SSD_STATE
FOX_W = FOX_HEADS * HEAD_DIM
GLA_W = GLA_HEADS * HEAD_DIM
RET_W = RET_HEADS * HEAD_DIM
N_BRANCH = 4
FFN_HIDDEN = ((8 * D_MODEL + 3 * 256 - 1) // (3 * 256)) * 256
NORM_EPS = 1e-6
IN_SPLITS = (FOX_W, FOX_W, FOX_W, FOX_HEADS,
             GLA_W, GLA_W, GLA_W, GLA_LOWRANK, GLA_W,
             RET_W, RET_W, RET_W, RET_W,
             SSD_INNER, SSD_CONV_CH, SSD_HEADS,
             N_BRANCH * D_MODEL)

LANES = 128
SUBLANES = 8
VMEM_LIMIT = 56 * 1024 * 1024

ROW_TILE = 512
FOX_PREP_TILE = 256
FOX_TQ = 512
FOX_TK = 512
FFN_TILE = 512

F32 = jnp.float32
BF16 = jnp.bfloat16


def _rmsnorm(x, g):
    return x * lax.rsqrt(jnp.mean(x * x, axis=-1, keepdims=True) + NORM_EPS) * g


def _log_sigmoid(x):
    return jnp.minimum(x, 0.0) - jnp.log1p(jnp.exp(-jnp.abs(x)))


def _softplus(x):
    return jnp.maximum(x, 0.0) + jnp.log1p(jnp.exp(-jnp.abs(x)))


def _sigmoid(x):
    return 1.0 / (1.0 + jnp.exp(-x))


def _silu(x):
    return x * _sigmoid(x)


def _split3(x):
    hi = x.astype(BF16)
    r1 = x - hi.astype(F32)
    mid = r1.astype(BF16)
    lo = (r1 - mid.astype(F32)).astype(BF16)
    return hi, mid, lo


def _dot(a, b):
    return jnp.dot(a, b, preferred_element_type=F32)


def _dot_nt(a, b):
    return lax.dot_general(a, b, (((1,), (1,)), ((), ())), preferred_element_type=F32)


def _dot_tn(a, b):
    return lax.dot_general(a, b, (((0,), (0,)), ((), ())), preferred_element_type=F32)


def _sel_dot(mat, x):
    hi, mid, lo = _split3(x)
    return _dot(mat, hi) + _dot(mat, mid) + _dot(mat, lo)


def _dot_sel(x, mat):
    hi, mid, lo = _split3(x)
    return _dot(hi, mat) + _dot(mid, mat) + _dot(lo, mat)


def _chunk_masks(t):
    r = lax.broadcasted_iota(jnp.int32, (t, t), 0)
    c = lax.broadcasted_iota(jnp.int32, (t, t), 1)
    same = (r // CHUNK) == (c // CHUNK)
    incl = jnp.where(same & (c <= r), 1.0, 0.0).astype(BF16)
    after = jnp.where(same & (c > r), 1.0, 0.0).astype(BF16)
    return incl, after


def _group_mean_sq(o, group):
    w = o.shape[-1]
    r = lax.broadcasted_iota(jnp.int32, (w, w), 0)
    c = lax.broadcasted_iota(jnp.int32, (w, w), 1)
    bd = jnp.where((r // group) == (c // group), 1.0, 0.0).astype(BF16)
    return _dot_sel(o * o, bd) * (1.0 / group)


def _resident(shape):
    nd = len(shape)
    return pl.BlockSpec(shape, lambda *_: (0,) * nd, pipeline_mode=pl.Buffered(1))


def _rotary_kernel(pos_ref, inv_ref, sign_ref, cos_ref, sin_ref):
    ang = pos_ref[...] * inv_ref[...]
    cos_ref[...] = jnp.cos(ang)
    sin_ref[...] = jnp.sin(ang) * sign_ref[...]


def _rotary_tables(positions, seq):
    half = HEAD_DIM // 2
    inv = ROPE_THETA ** (-jnp.arange(half, dtype=F32) / half)
    inv_row = jnp.tile(inv, 2 * RET_HEADS).reshape(1, RET_W)
    sign_row = jnp.tile(jnp.concatenate([-jnp.ones((half,), F32), jnp.ones((half,), F32)]),
                        RET_HEADS).reshape(1, RET_W)
    pos = positions.astype(F32).reshape(seq, 1)
    t = ROW_TILE
    return pl.pallas_call(
        _rotary_kernel,
        grid=(seq // t,),
        in_specs=[pl.BlockSpec((t, 1), lambda i: (i, 0)),
                  pl.BlockSpec((1, RET_W), lambda i: (0, 0)),
                  pl.BlockSpec((1, RET_W), lambda i: (0, 0))],
        out_specs=[pl.BlockSpec((t, RET_W), lambda i: (i, 0)),
                   pl.BlockSpec((t, RET_W), lambda i: (i, 0))],
        out_shape=[jax.ShapeDtypeStruct((seq, RET_W), F32)] * 2,
        name="rotary_tables",
    )(pos, inv_row, sign_row)


FOX_AUG = FOX_W * 3


def _fox_prep_kernel(x_ref, ln_ref, w_ref, bf_ref, qn_ref, kn_ref,
                     q_out, k_out, v_out, carry_ref):
    t = x_ref.shape[0]

    @pl.when(pl.program_id(0) == 0)
    def _():
        carry_ref[...] = jnp.zeros_like(carry_ref)

    h = _rmsnorm(x_ref[...], ln_ref[...]).astype(BF16)
    u = _dot(h, w_ref[...])
    ls = _log_sigmoid(u[:, FOX_AUG:FOX_AUG + LANES] + bf_ref[...])
    r = lax.broadcasted_iota(jnp.int32, (t, t), 0)
    cc = lax.broadcasted_iota(jnp.int32, (t, t), 1)
    tri = jnp.where(cc <= r, 1.0, 0.0).astype(BF16)
    c = _sel_dot(tri, ls) + carry_ref[0:1, :]
    carry_ref[0:1, :] = c[t - 1:t, :]
    c_hi, c_mid, c_lo = _split3(c)
    c_hi, c_mid, c_lo = c_hi.astype(F32), c_mid.astype(F32), c_lo.astype(F32)

    lane = lax.broadcasted_iota(jnp.int32, (t, LANES), 1)
    low = lane < HEAD_DIM
    scale = HEAD_DIM ** -0.5

    def head_norm(pair, gain, mult):
        sq = pair * pair
        s_lo = jnp.sum(jnp.where(low, sq, 0.0), axis=-1, keepdims=True)
        s_hi = jnp.sum(jnp.where(low, 0.0, sq), axis=-1, keepdims=True)
        ms = jnp.where(low, s_lo, s_hi) * (1.0 / HEAD_DIM)
        return pair * lax.rsqrt(ms + NORM_EPS) * (gain * mult)

    for p in range(FOX_HEADS // 2):
        qp = head_norm(u[:, p * LANES:(p + 1) * LANES], qn_ref[...], scale)
        kp = head_norm(u[:, FOX_W + p * LANES:FOX_W + (p + 1) * LANES], kn_ref[...], 1.0)
        vp = u[:, 2 * FOX_W + p * LANES:2 * FOX_W + (p + 1) * LANES]
        for sub in range(2):
            hd = 2 * p + sub
            if sub == 1:
                qp, kp, vp = (pltpu.roll(a, HEAD_DIM, axis=1) for a in (qp, kp, vp))
            ch, cm, cl = (jnp.broadcast_to(a[:, hd:hd + 1], (t, LANES)) for a in (c_hi, c_mid, c_lo))
            j = lane - HEAD_DIM
            aug_q = jnp.where(j == 0, ch, jnp.where(j == 1, cm, jnp.where(j == 2, cl,
                              jnp.where(j < 6, 1.0, 0.0))))
            aug_k = jnp.where(j < 3, 1.0, jnp.where(j == 3, -ch, jnp.where(j == 4, -cm,
                              jnp.where(j == 5, -cl, 0.0))))
            aug_v = jnp.where(j == 0, 1.0, 0.0)
            q_out[hd] = jnp.where(low, qp, aug_q).astype(BF16)
            k_out[hd] = jnp.where(low, kp, aug_k).astype(BF16)
            v_out[hd] = jnp.where(low, vp, aug_v).astype(BF16)


def _fox_prep(x, ln, w_fox, bf_row, qn_row, kn_row):
    seq = x.shape[0]
    t = FOX_PREP_TILE
    out = jax.ShapeDtypeStruct((FOX_HEADS, seq, LANES), BF16)
    ospec = pl.BlockSpec((FOX_HEADS, t, LANES), lambda i: (0, i, 0))
    return pl.pallas_call(
        _fox_prep_kernel,
        grid=(seq // t,),
        in_specs=[pl.BlockSpec((t, D_MODEL), lambda i: (i, 0)),
                  _resident((1, D_MODEL)), _resident(w_fox.shape), _resident((1, LANES)),
                  _resident((1, LANES)), _resident((1, LANES))],
        out_specs=[ospec, ospec, ospec],
        out_shape=[out, out, out],
        scratch_shapes=[pltpu.VMEM((SUBLANES, LANES), F32)],
        compiler_params=pltpu.CompilerParams(dimension_semantics=("arbitrary",),
                                             vmem_limit_bytes=VMEM_LIMIT),
        name="fox_prep",
    )(x, ln, w_fox, bf_row, qn_row, kn_row)


def _fox_attn_kernel(q_ref, k_ref, v_ref, o_ref):
    i = pl.program_id(1)
    q = q_ref[0]
    tq = q.shape[0]

    def step(j, carry, masked):
        m, acc = carry
        off = pl.multiple_of(j * FOX_TK, FOX_TK)
        k = k_ref[0, pl.ds(off, FOX_TK), :]
        v = v_ref[0, pl.ds(off, FOX_TK), :]
        s = _dot_nt(q, k)
        if masked:
            r = lax.broadcasted_iota(jnp.int32, s.shape, 0)
            c = lax.broadcasted_iota(jnp.int32, s.shape, 1)
            s = jnp.where(c <= r, s, -jnp.inf)
        m_new = jnp.maximum(m, jnp.max(s, axis=-1, keepdims=True))
        p = jnp.exp(s - m_new)
        acc = jnp.exp(m - m_new) * acc + _dot(p.astype(BF16), v)
        return m_new, acc

    init = (jnp.full((tq, 1), -jnp.inf, F32), jnp.zeros((tq, LANES), F32))
    carry = lax.fori_loop(0, i, lambda j, c: step(j, c, False), init)
    _, acc = step(i, carry, True)
    o_ref[0] = (acc / acc[:, HEAD_DIM:HEAD_DIM + 1]).astype(BF16)


def _fox_attn(q, k, v):
    assert FOX_TQ == FOX_TK
    nh, seq, _ = q.shape
    kv_spec = pl.BlockSpec((1, seq, LANES), lambda h, i: (h, 0, 0))
    return pl.pallas_call(
        _fox_attn_kernel,
        grid=(nh, seq // FOX_TQ),
        in_specs=[pl.BlockSpec((1, FOX_TQ, LANES), lambda h, i: (h, i, 0)), kv_spec, kv_spec],
        out_specs=pl.BlockSpec((1, FOX_TQ, LANES), lambda h, i: (h, i, 0)),
        out_shape=jax.ShapeDtypeStruct((nh, seq, LANES), BF16),
        compiler_params=pltpu.CompilerParams(dimension_semantics=("arbitrary", "arbitrary"),
                                             vmem_limit_bytes=VMEM_LIMIT),
        name="fox_attn",
    )(q, k, v)


def _linear_attn_chunks(qd_s, ki_s, ke_s, v_s, dec_s, o_s, st_s, nheads):
    t = qd_s.shape[0]
    r = lax.broadcasted_iota(jnp.int32, (CHUNK, CHUNK), 0)
    c = lax.broadcasted_iota(jnp.int32, (CHUNK, CHUNK), 1)
    tril = c <= r

    def body(ci, _):
        rows = pl.ds(pl.multiple_of(ci * CHUNK, CHUNK), CHUNK)
        tail = pl.ds(pl.multiple_of(ci * CHUNK + (CHUNK - SUBLANES), SUBLANES), SUBLANES)
        for hd in range(nheads):
            cols = slice(hd * HEAD_DIM, (hd + 1) * HEAD_DIM)
            qd = qd_s[rows, cols].astype(BF16)
            ki = ki_s[rows, cols].astype(BF16)
            ke = ke_s[rows, cols].astype(BF16)
            vv = v_s[rows, cols].astype(BF16)
            st = st_s[hd]
            attn = jnp.where(tril, _dot_nt(qd, ki), 0.0).astype(BF16)
            o_s[rows, cols] = _dot(attn, vv) + _dot_nt(qd, st.astype(BF16))
            st_s[hd] = st * dec_s[tail, cols][SUBLANES - 1:SUBLANES, :] + _dot_tn(vv, ke)
        return 0

    lax.fori_loop(0, t // CHUNK, body, 0)


def _gla_kernel(x_ref, ln_ref, w_ref, w2_ref, b2_ref, gn_ref, y_ref,
                qd_s, ki_s, ke_s, v_s, dec_s, o_s, st_s):
    t = x_ref.shape[0]

    @pl.when(pl.program_id(0) == 0)
    def _():
        st_s[...] = jnp.zeros_like(st_s)

    h = _rmsnorm(x_ref[...], ln_ref[...]).astype(BF16)
    u = _dot(h, w_ref[...])
    z = _dot(u[:, 4 * GLA_W:4 * GLA_W + LANES].astype(BF16), w2_ref[...]) + b2_ref[...]
    log_a = _log_sigmoid(z) * (1.0 / GLA_TAU)
    incl, after = _chunk_masks(t)
    b = _sel_dot(incl, log_a)
    rem = _sel_dot(after, log_a)
    eb = jnp.exp(b)
    k = u[:, GLA_W:2 * GLA_W]
    qd_s[...] = u[:, 0:GLA_W] * (HEAD_DIM ** -0.5) * eb
    ki_s[...] = k * jnp.exp(-b)
    ke_s[...] = k * jnp.exp(rem)
    v_s[...] = u[:, 2 * GLA_W:3 * GLA_W]
    dec_s[...] = eb
    _linear_attn_chunks(qd_s, ki_s, ke_s, v_s, dec_s, o_s, st_s, GLA_HEADS)
    o = o_s[...]
    y = o * lax.rsqrt(_group_mean_sq(o, HEAD_DIM) + NORM_EPS) * gn_ref[...]
    y_ref[...] = (y * _silu(u[:, 3 * GLA_W:4 * GLA_W])).astype(BF16)


def _ret_kernel(x_ref, ln_ref, w_ref, cos_ref, sin_ref, gn_ref, y_ref,
                qd_s, ki_s, ke_s, v_s, dec_s, o_s, st_s):
    t = x_ref.shape[0]

    @pl.when(pl.program_id(0) == 0)
    def _():
        st_s[...] = jnp.zeros_like(st_s)

    h = _rmsnorm(x_ref[...], ln_ref[...]).astype(BF16)
    u = _dot(h, w_ref[...])
    cos, sin = cos_ref[...], sin_ref[...]
    lane = lax.broadcasted_iota(jnp.int32, (t, RET_W), 1)
    half = HEAD_DIM // 2
    first_half = (lax.broadcasted_iota(jnp.int32, (t, LANES), 1) % HEAD_DIM) < half

    def rotate(a):
        blocks = []
        for b in range(0, RET_W, LANES):
            ab = a[:, b:b + LANES]
            swapped = jnp.where(first_half, pltpu.roll(ab, LANES - half, axis=1),
                                pltpu.roll(ab, half, axis=1))
            blocks.append(ab * cos[:, b:b + LANES] + swapped * sin[:, b:b + LANES])
        return jnp.concatenate(blocks, axis=1)

    q = rotate(u[:, 0:RET_W])
    k = rotate(u[:, RET_W:2 * RET_W]) * (HEAD_DIM ** -0.5)
    row = lax.broadcasted_iota(jnp.int32, (t, RET_W), 0)
    lg = jnp.zeros((t, RET_W), F32)
    for hd in range(RET_HEADS):
        lg = jnp.where(lane // HEAD_DIM == hd, math.log(1.0 - 2.0 ** (-5.0 - hd)), lg)
    pos = (row % CHUNK).astype(F32)
    qw = jnp.exp((pos + 1.0) * lg)
    qd_s[...] = q * qw
    ki_s[...] = k * jnp.exp(-(pos + 1.0) * lg)
    ke_s[...] = k * jnp.exp((CHUNK - 1.0 - pos) * lg)
    v_s[...] = u[:, 2 * RET_W:3 * RET_W]
    dec_s[...] = qw
    _linear_attn_chunks(qd_s, ki_s, ke_s, v_s, dec_s, o_s, st_s, RET_HEADS)
    o = o_s[...]
    y = o * lax.rsqrt(_group_mean_sq(o, HEAD_DIM) + NORM_EPS) * gn_ref[...]
    y_ref[...] = (y * _silu(u[:, 3 * RET_W:4 * RET_W])).astype(BF16)


def _linear_scratch(t, width, nheads):
    return ([pltpu.VMEM((t, width), F32)] * 6
            + [pltpu.VMEM((nheads, HEAD_DIM, HEAD_DIM), F32)])


def _gla(x, ln, w_gla, w2, b2_row, gn_row):
    seq = x.shape[0]
    t = ROW_TILE
    return pl.pallas_call(
        _gla_kernel,
        grid=(seq // t,),
        in_specs=[pl.BlockSpec((t, D_MODEL), lambda i: (i, 0)),
                  _resident((1, D_MODEL)), _resident(w_gla.shape), _resident(w2.shape),
                  _resident((1, GLA_W)), _resident((1, GLA_W))],
        out_specs=pl.BlockSpec((t, GLA_W), lambda i: (i, 0)),
        out_shape=jax.ShapeDtypeStruct((seq, GLA_W), BF16),
        scratch_shapes=_linear_scratch(t, GLA_W, GLA_HEADS),
        compiler_params=pltpu.CompilerParams(dimension_semantics=("arbitrary",),
                                             vmem_limit_bytes=VMEM_LIMIT),
        name="gla_mixer",
    )(x, ln, w_gla, w2, b2_row, gn_row)


def _ret(x, ln, w_ret, cos_t, sin_t, gn_row):
    seq = x.shape[0]
    t = ROW_TILE
    return pl.pallas_call(
        _ret_kernel,
        grid=(seq // t,),
        in_specs=[pl.BlockSpec((t, D_MODEL), lambda i: (i, 0)),
                  _resident((1, D_MODEL)), _resident(w_ret.shape),
                  pl.BlockSpec((t, RET_W), lambda i: (i, 0)),
                  pl.BlockSpec((t, RET_W), lambda i: (i, 0)),
                  _resident((1, RET_W))],
        out_specs=pl.BlockSpec((t, RET_W), lambda i: (i, 0)),
        out_shape=jax.ShapeDtypeStruct((seq, RET_W), BF16),
        scratch_shapes=_linear_scratch(t, RET_W, RET_HEADS),
        compiler_params=pltpu.CompilerParams(dimension_semantics=("arbitrary",),
                                             vmem_limit_bytes=VMEM_LIMIT),
        name="ret_mixer",
    )(x, ln, w_ret, cos_t, sin_t, gn_row)


SSD_PROJ = SSD_INNER + SSD_CONV_CH


def _ssd_kernel(x_ref, ln_ref, w_ref, cw_ref, cb_ref, dtb_row_ref,
                alog_row_ref, d_row_ref, gn_ref, y_ref,
                xp_s, xs_s, bc_s, cs_s, ecs_s, xdt_s, xdtw_s, cst_s, o_s, st_s):
    t = x_ref.shape[0]
    nc = t // CHUNK

    @pl.when(pl.program_id(0) == 0)
    def _():
        st_s[...] = jnp.zeros_like(st_s)
        xp_s[0:SUBLANES, :] = jnp.zeros((SUBLANES, SSD_CONV_CH), F32)

    h = _rmsnorm(x_ref[...], ln_ref[...]).astype(BF16)
    u = _dot(h, w_ref[...])
    z = u[:, 0:SSD_INNER]

    xp_s[SUBLANES:SUBLANES + t, :] = u[:, SSD_INNER:SSD_PROJ]
    conv = cb_ref[...] + cw_ref[SSD_CONV - 1:SSD_CONV, :] * xp_s[SUBLANES:SUBLANES + t, :]
    for kk in range(SSD_CONV - 1):
        off = SUBLANES - (SSD_CONV - 1) + kk
        conv = conv + cw_ref[kk:kk + 1, :] * xp_s[off:off + t, :]
    xp_s[0:SUBLANES, :] = xp_s[t:t + SUBLANES, :]
    xbc = _silu(conv)
    xs = xbc[:, 0:SSD_INNER]
    xs_s[...] = xs
    bc_s[...] = xbc[:, SSD_INNER:SSD_CONV_CH]

    dt = _softplus(u[:, SSD_PROJ:SSD_PROJ + LANES] + dtb_row_ref[...])
    dta = dt * -jnp.exp(alog_row_ref[...])
    incl, after = _chunk_masks(t)
    cs = _sel_dot(incl, dta)
    rem = _sel_dot(after, dta)
    cs_t = cs.T[0:SSD_HEADS, :]
    for ci in range(nc):
        cst_s[ci] = cs_t[:, ci * CHUNK:(ci + 1) * CHUNK]

    er = lax.broadcasted_iota(jnp.int32, (LANES, SSD_INNER), 0)
    ec = lax.broadcasted_iota(jnp.int32, (LANES, SSD_INNER), 1)
    expand = jnp.where(ec // HEAD_DIM == er, 1.0, 0.0).astype(BF16)
    cs_x = _dot_sel(cs, expand)
    xdt = xs * _dot_sel(dt, expand)
    cs_s[...] = cs_x
    ecs_s[...] = jnp.exp(cs_x)
    xdt_s[...] = xdt
    xdtw_s[...] = xdt * jnp.exp(_dot_sel(rem, expand))

    r = lax.broadcasted_iota(jnp.int32, (CHUNK, CHUNK), 0)
    c = lax.broadcasted_iota(jnp.int32, (CHUNK, CHUNK), 1)
    tril = c <= r
    gw = SSD_GROUPS * SSD_STATE

    def body(ci, _):
        rows = pl.ds(pl.multiple_of(ci * CHUNK, CHUNK), CHUNK)
        tail = pl.ds(pl.multiple_of(ci * CHUNK + (CHUNK - SUBLANES), SUBLANES), SUBLANES)
        cst = cst_s[ci]
        for g in range(SSD_GROUPS):
            bm = bc_s[rows, g * SSD_STATE:(g + 1) * SSD_STATE].astype(BF16)
            cm = bc_s[rows, gw + g * SSD_STATE:gw + (g + 1) * SSD_STATE].astype(BF16)
            cb = _dot_nt(cm, bm)
            for rr in range(SSD_HEADS // SSD_GROUPS):
                hd = g * (SSD_HEADS // SSD_GROUPS) + rr
                cols = slice(hd * HEAD_DIM, (hd + 1) * HEAD_DIM)
                seg = cs_s[rows, cols] - cst[hd:hd + 1, :]
                decay = jnp.exp(jnp.where(tril, seg, -jnp.inf))
                st = st_s[hd]
                y = _dot((cb * decay).astype(BF16), xdt_s[rows, cols].astype(BF16))
                y = y + _dot(cm, st.astype(BF16)) * ecs_s[rows, cols]
                o_s[rows, cols] = y
                st_s[hd] = (st * ecs_s[tail, cols][SUBLANES - 1:SUBLANES, :]
                            + _dot_tn(bm, xdtw_s[rows, cols].astype(BF16)))
        return 0

    lax.fori_loop(0, nc, body, 0)

    y = (o_s[...] + d_row_ref[...] * xs_s[...]) * _silu(z)
    y = y * lax.rsqrt(_group_mean_sq(y, SSD_INNER // SSD_GROUPS) + NORM_EPS) * gn_ref[...]
    y_ref[...] = y.astype(BF16)


def _ssd(x, ln, w_ssd, conv_w, conv_b_row, dtb_row, alog_row, d_row, gn_row):
    seq = x.shape[0]
    t = ROW_TILE
    wide = pltpu.VMEM((t, SSD_INNER), F32)
    return pl.pallas_call(
        _ssd_kernel,
        grid=(seq // t,),
        in_specs=[pl.BlockSpec((t, D_MODEL), lambda i: (i, 0)),
                  _resident((1, D_MODEL)), _resident(w_ssd.shape),
                  _resident(conv_w.shape), _resident(conv_b_row.shape),
                  _resident((1, LANES)), _resident((1, LANES)),
                  _resident((1, SSD_INNER)), _resident((1, SSD_INNER))],
        out_specs=pl.BlockSpec((t, SSD_INNER), lambda i: (i, 0)),
        out_shape=jax.ShapeDtypeStruct((seq, SSD_INNER), BF16),
        scratch_shapes=[pltpu.VMEM((t + 2 * SUBLANES, SSD_CONV_CH), F32),
                        wide,
                        pltpu.VMEM((t, 2 * SSD_GROUPS * SSD_STATE), F32),
                        wide, wide, wide, wide,
                        pltpu.VMEM((t // CHUNK, SSD_HEADS, CHUNK), F32),
                        wide,
                        pltpu.VMEM((SSD_HEADS, SSD_STATE, HEAD_DIM), F32)],
        compiler_params=pltpu.CompilerParams(dimension_semantics=("arbitrary",),
                                             vmem_limit_bytes=VMEM_LIMIT),
        name="ssd_mixer",
    )(x, ln, w_ssd, conv_w, conv_b_row, dtb_row, alog_row, d_row, gn_row)


def _merge_kernel(x_ref, ln_ref, wg_ref, ya_ref, yb_ref, yc_ref, yd_ref,
                  wa_ref, wb_ref, wc_ref, wd_ref, wo_ref, out_ref):
    x = x_ref[...]
    h = _rmsnorm(x, ln_ref[...]).astype(BF16)

    def gate(b):
        return _sigmoid(_dot(h, wg_ref[:, b * D_MODEL:(b + 1) * D_MODEL]))

    up_a = _dot(ya_ref[0], wa_ref[0])
    for hd in range(1, FOX_HEADS):
        up_a = up_a + _dot(ya_ref[hd], wa_ref[hd])
    merged = gate(0) * up_a
    merged = merged + gate(1) * _dot(yb_ref[...], wb_ref[...])
    merged = merged + gate(2) * _dot(yc_ref[...], wc_ref[...])
    merged = merged + gate(3) * _dot(yd_ref[...], wd_ref[...])
    out_ref[...] = x + _dot(merged.astype(BF16), wo_ref[...])


def _merge(x, ln, w_gates, y_a, y_b, y_c, y_d, w_up_a, w_up_b, w_up_c, w_up_d, w_out):
    seq = x.shape[0]
    t = ROW_TILE
    row = lambda w: pl.BlockSpec((t, w), lambda i: (i, 0))
    return pl.pallas_call(
        _merge_kernel,
        grid=(seq // t,),
        in_specs=[row(D_MODEL), _resident((1, D_MODEL)), _resident(w_gates.shape),
                  pl.BlockSpec((FOX_HEADS, t, LANES), lambda i: (0, i, 0)),
                  row(GLA_W), row(RET_W), row(SSD_INNER),
                  _resident(w_up_a.shape), _resident(w_up_b.shape), _resident(w_up_c.shape),
                  _resident(w_up_d.shape), _resident(w_out.shape)],
        out_specs=row(D_MODEL),
        out_shape=jax.ShapeDtypeStruct((seq, D_MODEL), F32),
        compiler_params=pltpu.CompilerParams(dimension_semantics=("arbitrary",),
                                             vmem_limit_bytes=VMEM_LIMIT),
        name="gated_merge",
    )(x, ln, w_gates, y_a, y_b, y_c, y_d, w_up_a, w_up_b, w_up_c, w_up_d, w_out)


def _ffn_kernel(x_ref, ln_ref, w1_ref, w2_ref, out_ref):
    x = x_ref[...]
    h = _rmsnorm(x, ln_ref[...]).astype(BF16)
    gate = _dot(h, w1_ref[:, 0:FFN_HIDDEN])
    up = _dot(h, w1_ref[:, FFN_HIDDEN:2 * FFN_HIDDEN])
    out_ref[...] = x + _dot((_silu(gate) * up).astype(BF16), w2_ref[...])


def _ffn(x, ln, w1, w2):
    seq = x.shape[0]
    t = FFN_TILE
    return pl.pallas_call(
        _ffn_kernel,
        grid=(seq // t,),
        in_specs=[pl.BlockSpec((t, D_MODEL), lambda i: (i, 0)),
                  _resident((1, D_MODEL)), _resident(w1.shape), _resident(w2.shape)],
        out_specs=pl.BlockSpec((t, D_MODEL), lambda i: (i, 0)),
        out_shape=jax.ShapeDtypeStruct((seq, D_MODEL), F32),
        compiler_params=pltpu.CompilerParams(dimension_semantics=("arbitrary",),
                                             vmem_limit_bytes=VMEM_LIMIT),
        name="swiglu",
    )(x, ln, w1, w2)


def _pad_cols(w, width):
    return jnp.pad(w, ((0, 0), (0, width - w.shape[1])))


def _pad_row(v, width):
    v = v.reshape(1, -1)
    return jnp.pad(v, ((0, 0), (0, width - v.shape[1])))


def _layer_weights(w_in):
    offs = np.concatenate([[0], np.cumsum(IN_SPLITS)])
    (fq, fk, fv, ff, gq, gk, gv, glr, gr, rq, rk, rv, rg, z, xbc, dt, gates) = [
        w_in[:, offs[n]:offs[n + 1]] for n in range(len(IN_SPLITS))]
    cat = lambda parts: jnp.concatenate(parts, axis=1).astype(BF16)
    w_fox = cat([fq, fk, fv, _pad_cols(ff, LANES)])
    w_gla = cat([gq, gk, gv, gr, _pad_cols(glr, LANES)])
    w_ret = cat([rq, rk, rv, rg])
    w_ssd = cat([z, xbc, _pad_cols(dt, LANES)])
    return w_fox, w_gla, w_ret, w_ssd, gates.astype(BF16)


def kernel(x, positions, ln1, ln2, w_in, fox_bf, fox_qn, fox_kn, gla_w2, gla_b, gla_norm, ret_norm,
           ssd_conv_w, ssd_conv_b, ssd_dt_bias, ssd_a_log, ssd_d, ssd_norm,
           w_up_a, w_up_b, w_up_c, w_up_d, w_out, w_ffn_in, w_ffn_out):
    bsz, seq, d = x.shape
    assert bsz == 1 and d == D_MODEL and seq % ROW_TILE == 0 and seq % FOX_TQ == 0
    depth = ln1.shape[0]
    xr = x.reshape(seq, d)
    cos_t, sin_t = _rotary_tables(positions, seq)
    tile_heads = lambda g, n: jnp.tile(g, n).reshape(1, -1)
    for l in range(depth):
        w_fox, w_gla, w_ret, w_ssd, w_gates = _layer_weights(
            lax.optimization_barrier(w_in[l].astype(BF16)))
        ln = ln1[l].reshape(1, d)
        q, k, v = _fox_prep(xr, ln, w_fox, _pad_row(fox_bf[l], LANES),
                            tile_heads(fox_qn[l], 2), tile_heads(fox_kn[l], 2))
        y_a = _fox_attn(q, k, v)
        w2 = jnp.pad(gla_w2[l], ((0, LANES - GLA_LOWRANK), (0, 0))).astype(BF16)
        y_b = _gla(xr, ln, w_gla, w2, gla_b[l].reshape(1, -1), tile_heads(gla_norm[l], GLA_HEADS))
        y_c = _ret(xr, ln, w_ret, cos_t, sin_t, tile_heads(ret_norm[l], RET_HEADS))
        y_d = _ssd(xr, ln, w_ssd, ssd_conv_w[l], ssd_conv_b[l].reshape(1, -1),
                   _pad_row(ssd_dt_bias[l], LANES), _pad_row(ssd_a_log[l], LANES),
                   jnp.repeat(ssd_d[l], HEAD_DIM).reshape(1, -1), ssd_norm[l].reshape(1, -1))
        wa = jnp.pad(w_up_a[l].reshape(FOX_HEADS, HEAD_DIM, d),
                     ((0, 0), (0, LANES - HEAD_DIM), (0, 0))).astype(BF16)
        xr = _merge(xr, ln, w_gates, y_a, y_b, y_c, y_d, wa, w_up_b[l].astype(BF16),
                    w_up_c[l].astype(BF16), w_up_d[l].astype(BF16), w_out[l].astype(BF16))
        xr = _ffn(xr, ln2[l].reshape(1, d), w_ffn_in[l].astype(BF16), w_ffn_out[l].astype(BF16))
    return xr.reshape(bsz, seq, d)
```

```python
import functools
import math

import numpy as np
import jax
import jax.numpy as jnp
from jax import lax
from jax.experimental import pallas as pl
from jax.experimental.pallas import tpu as pltpu

D_MODEL = 1024
HEAD_DIM = 64
CHUNK = 64
FOX_HEADS = 4
GLA_HEADS = 4
GLA_LOWRANK = 16
GLA_TAU = 16.0
RET_HEADS = 4
ROPE_THETA = 10000.0
SSD_HEADS = 8
SSD_GROUPS = 2
SSD_STATE = 64
SSD_CONV = 4
SSD_INNER = SSD_HEADS * HEAD_DIM
SSD_CONV_CH = SSD_INNER + 2 * SSD_GROUPS * SSD_STATE
FOX_W = FOX_HEADS * HEAD_DIM
GLA_W = GLA_HEADS * HEAD_DIM
RET_W = RET_HEADS * HEAD_DIM
N_BRANCH = 4
FFN_HIDDEN = ((8 * D_MODEL + 3 * 256 - 1) // (3 * 256)) * 256
NORM_EPS = 1e-6
IN_SPLITS = (FOX_W, FOX_W, FOX_W, FOX_HEADS,
             GLA_W, GLA_W, GLA_W, GLA_LOWRANK, GLA_W,
             RET_W, RET_W, RET_W, RET_W,
             SSD_INNER, SSD_CONV_CH, SSD_HEADS,
             N_BRANCH * D_MODEL)

LANES = 128
SUBLANES = 8
VMEM_LIMIT = 56 * 1024 * 1024

ROW_TILE = 512
FOX_PREP_TILE = 256
FOX_TQ = 512
FOX_TK = 512
FFN_TILE = 512

F32 = jnp.float32
BF16 = jnp.bfloat16


def _rmsnorm(x, g):
    return x * lax.rsqrt(jnp.mean(x * x, axis=-1, keepdims=True) + NORM_EPS) * g


def _log_sigmoid(x):
    return jnp.minimum(x, 0.0) - jnp.log1p(jnp.exp(-jnp.abs(x)))


def _softplus(x):
    return jnp.maximum(x, 0.0) + jnp.log1p(jnp.exp(-jnp.abs(x)))


def _sigmoid(x):
    return 1.0 / (1.0 + jnp.exp(-x))


def _silu(x):
    return x * _sigmoid(x)


def _split3(x):
    hi = x.astype(BF16)
    r1 = x - hi.astype(F32)
    mid = r1.astype(BF16)
    lo = (r1 - mid.astype(F32)).astype(BF16)
    return hi, mid, lo


def _dot(a, b):
    return jnp.dot(a, b, preferred_element_type=F32)


def _dot_nt(a, b):
    return lax.dot_general(a, b, (((1,), (1,)), ((), ())), preferred_element_type=F32)


def _dot_tn(a, b):
    return lax.dot_general(a, b, (((0,), (0,)), ((), ())), preferred_element_type=F32)


def _sel_dot(mat, x):
    hi, mid, lo = _split3(x)
    return _dot(mat, hi) + _dot(mat, mid) + _dot(mat, lo)


def _dot_sel(x, mat):
    hi, mid, lo = _split3(x)
    return _dot(hi, mat) + _dot(mid, mat) + _dot(lo, mat)


def _chunk_masks(t):
    r = lax.broadcasted_iota(jnp.int32, (t, t), 0)
    c = lax.broadcasted_iota(jnp.int32, (t, t), 1)
    same = (r // CHUNK) == (c // CHUNK)
    incl = jnp.where(same & (c <= r), 1.0, 0.0).astype(BF16)
    after = jnp.where(same & (c > r), 1.0, 0.0).astype(BF16)
    return incl, after


def _group_mean_sq(o, group):
    w = o.shape[-1]
    r = lax.broadcasted_iota(jnp.int32, (w, w), 0)
    c = lax.broadcasted_iota(jnp.int32, (w, w), 1)
    bd = jnp.where((r // group) == (c // group), 1.0, 0.0).astype(BF16)
    return _dot_sel(o * o, bd) * (1.0 / group)


def _resident(shape):
    nd = len(shape)
    return pl.BlockSpec(shape, lambda *_: (0,) * nd, pipeline_mode=pl.Buffered(1))


def _rotary_kernel(pos_ref, inv_ref, sign_ref, cos_ref, sin_ref):
    ang = pos_ref[...] * inv_ref[...]
    cos_ref[...] = jnp.cos(ang)
    sin_ref[...] = jnp.sin(ang) * sign_ref[...]


def _rotary_tables(positions, seq):
    half = HEAD_DIM // 2
    inv = ROPE_THETA ** (-jnp.arange(half, dtype=F32) / half)
    inv_row = jnp.tile(inv, 2 * RET_HEADS).reshape(1, RET_W)
    sign_row = jnp.tile(jnp.concatenate([-jnp.ones((half,), F32), jnp.ones((half,), F32)]),
                        RET_HEADS).reshape(1, RET_W)
    pos = positions.astype(F32).reshape(seq, 1)
    t = ROW_TILE
    return pl.pallas_call(
        _rotary_kernel,
        grid=(seq // t,),
        in_specs=[pl.BlockSpec((t, 1), lambda i: (i, 0)),
                  pl.BlockSpec((1, RET_W), lambda i: (0, 0)),
                  pl.BlockSpec((1, RET_W), lambda i: (0, 0))],
        out_specs=[pl.BlockSpec((t, RET_W), lambda i: (i, 0)),
                   pl.BlockSpec((t, RET_W), lambda i: (i, 0))],
        out_shape=[jax.ShapeDtypeStruct((seq, RET_W), F32)] * 2,
        name="rotary_tables",
    )(pos, inv_row, sign_row)


FOX_AUG = FOX_W * 3
LOG2E = math.log2(math.e)
F32_EXP_ZERO = 105.0
FOX_FAST_MAX_LOGIT = 40.0


def _fox_prep_kernel(x_ref, ln_ref, w_ref, bf_ref, qn_ref, kn_ref, shift_ref,
                     q_out, k_out, v_out, c_out, carry_ref):
    t = x_ref.shape[0]

    @pl.when(pl.program_id(0) == 0)
    def _():
        carry_ref[...] = jnp.zeros_like(carry_ref)

    h = _rmsnorm(x_ref[...], ln_ref[...]).astype(BF16)
    u = _dot(h, w_ref[...])
    ls = _log_sigmoid(u[:, FOX_AUG:FOX_AUG + LANES] + bf_ref[...])
    r = lax.broadcasted_iota(jnp.int32, (t, t), 0)
    cc = lax.broadcasted_iota(jnp.int32, (t, t), 1)
    tri = jnp.where(cc <= r, 1.0, 0.0).astype(BF16)
    c = _sel_dot(tri, ls) + carry_ref[0:1, :]
    carry_ref[0:1, :] = c[t - 1:t, :]
    c_out[0] = jnp.concatenate([c[0:1, :], c[t - 1:t, :], jnp.zeros((SUBLANES - 2, LANES), F32)], axis=0)
    c_hi, c_mid, c_lo = _split3(c * LOG2E)
    c_hi, c_mid, c_lo = c_hi.astype(F32), c_mid.astype(F32), c_lo.astype(F32)

    lane = lax.broadcasted_iota(jnp.int32, (t, LANES), 1)
    low = lane < HEAD_DIM
    j = lane - HEAD_DIM
    shift = shift_ref[...]

    def head_norm(pair, gain, mult):
        sq = pair * pair
        s_lo = jnp.sum(jnp.where(low, sq, 0.0), axis=-1, keepdims=True)
        s_hi = jnp.sum(jnp.where(low, 0.0, sq), axis=-1, keepdims=True)
        ms = jnp.where(low, s_lo, s_hi) * (1.0 / HEAD_DIM)
        return pair * lax.rsqrt(ms + NORM_EPS) * (gain * mult)

    for p in range(FOX_HEADS // 2):
        qp = head_norm(u[:, p * LANES:(p + 1) * LANES], qn_ref[...], HEAD_DIM ** -0.5 * LOG2E)
        kp = head_norm(u[:, FOX_W + p * LANES:FOX_W + (p + 1) * LANES], kn_ref[...], 1.0)
        vp = u[:, 2 * FOX_W + p * LANES:2 * FOX_W + (p + 1) * LANES]
        for sub in range(2):
            hd = 2 * p + sub
            if sub == 1:
                qp, kp, vp = (pltpu.roll(a, HEAD_DIM, axis=1) for a in (qp, kp, vp))
            ch, cm, cl = (jnp.broadcast_to(a[:, hd:hd + 1], (t, LANES)) for a in (c_hi, c_mid, c_lo))
            aug_q = jnp.where(j == 0, ch, jnp.where(j == 1, cm, jnp.where(j == 2, cl,
                              jnp.where(j < 7, 1.0, 0.0))))
            aug_k = jnp.where(j < 3, 1.0, jnp.where(j == 3, -ch, jnp.where(j == 4, -cm,
                              jnp.where(j == 5, -cl, jnp.where(j == 6, -shift, 0.0)))))
            aug_v = jnp.where(j == 0, 1.0, 0.0)
            q_out[hd] = jnp.where(low, qp, aug_q).astype(BF16)
            k_out[hd] = jnp.where(low, kp, aug_k).astype(BF16)
            v_out[hd] = jnp.where(low, vp, aug_v).astype(BF16)


def _fox_prep(x, ln, w_fox, bf_row, qn_row, kn_row, shift_row):
    seq = x.shape[0]
    t = FOX_PREP_TILE
    out = jax.ShapeDtypeStruct((FOX_HEADS, seq, LANES), BF16)
    ospec = pl.BlockSpec((FOX_HEADS, t, LANES), lambda i: (0, i, 0))
    return pl.pallas_call(
        _fox_prep_kernel,
        grid=(seq // t,),
        in_specs=[pl.BlockSpec((t, D_MODEL), lambda i: (i, 0)),
                  _resident((1, D_MODEL)), _resident(w_fox.shape), _resident((1, LANES)),
                  _resident((1, LANES)), _resident((1, LANES)), _resident((1, LANES))],
        out_specs=[ospec, ospec, ospec, pl.BlockSpec((1, SUBLANES, LANES), lambda i: (i, 0, 0))],
        out_shape=[out, out, out, jax.ShapeDtypeStruct((seq // t, SUBLANES, LANES), F32)],
        scratch_shapes=[pltpu.VMEM((SUBLANES, LANES), F32)],
        compiler_params=pltpu.CompilerParams(dimension_semantics=("arbitrary",),
                                             vmem_limit_bytes=VMEM_LIMIT),
        name="fox_prep",
    )(x, ln, w_fox, bf_row, qn_row, kn_row, shift_row)


def _fox_attn_kernel(par_ref, cq0_ref, cend_ref, q_ref, k_ref, v_ref, o_ref):
    hd = pl.program_id(0)
    i = pl.program_id(1)
    q = q_ref[0]
    tq = q.shape[0]
    c0 = cq0_ref[i, hd]
    thr = par_ref[0]
    j_lo = lax.while_loop(
        lambda j: jnp.logical_and(j > 0, c0 - cend_ref[jnp.maximum(j - 1, 0), hd] >= -thr),
        lambda j: j - 1, i)

    def scores(j, masked):
        off = pl.multiple_of(j * FOX_TK, FOX_TK)
        s = _dot_nt(q, k_ref[0, pl.ds(off, FOX_TK), :])
        if masked:
            r = lax.broadcasted_iota(jnp.int32, s.shape, 0)
            c = lax.broadcasted_iota(jnp.int32, s.shape, 1)
            s = jnp.where(c <= r, s, -jnp.inf)
        return s, v_ref[0, pl.ds(off, FOX_TK), :]

    def fixed_shift():
        def block(j, masked):
            s, v = scores(j, masked)
            return _dot(jnp.exp2(s).astype(BF16), v)
        return lax.fori_loop(j_lo, i, lambda j, acc: acc + block(j, False), block(i, True))

    def running_max():
        def step(j, carry, masked):
            m, acc = carry
            s, v = scores(j, masked)
            m_new = jnp.maximum(m, jnp.max(s, axis=-1, keepdims=True))
            p = jnp.exp2(s - m_new)
            return m_new, jnp.exp2(m - m_new) * acc + _dot(p.astype(BF16), v)
        init = (jnp.full((tq, 1), -jnp.inf, F32), jnp.zeros((tq, LANES), F32))
        carry = step(i, init, True)
        return lax.fori_loop(j_lo, i, lambda j, c: step(j, c, False), carry)[1]

    acc = lax.cond(par_ref[1] > 0.0, fixed_shift, running_max)
    o_ref[0] = (acc / acc[:, HEAD_DIM:HEAD_DIM + 1]).astype(BF16)


def _fox_attn(par, cq0, cend, q, k, v):
    assert FOX_TQ == FOX_TK
    nh, seq, _ = q.shape
    kv_spec = pl.BlockSpec((1, seq, LANES), lambda h, i, *_: (h, 0, 0))
    return pl.pallas_call(
        _fox_attn_kernel,
        grid_spec=pltpu.PrefetchScalarGridSpec(
            num_scalar_prefetch=3,
            grid=(nh, seq // FOX_TQ),
            in_specs=[pl.BlockSpec((1, FOX_TQ, LANES), lambda h, i, *_: (h, i, 0)), kv_spec, kv_spec],
            out_specs=pl.BlockSpec((1, FOX_TQ, LANES), lambda h, i, *_: (h, i, 0))),
        out_shape=jax.ShapeDtypeStruct((nh, seq, LANES), BF16),
        compiler_params=pltpu.CompilerParams(dimension_semantics=("arbitrary", "arbitrary"),
                                             vmem_limit_bytes=VMEM_LIMIT),
        name="fox_attn",
    )(par, cq0, cend, q, k, v)


def _fox(x, ln, w_fox, bf, qn, kn):
    bound = 1.02 * HEAD_DIM ** 0.5 * jnp.max(jnp.abs(qn)) * jnp.max(jnp.abs(kn))
    shift_row = jnp.full((1, LANES), LOG2E, F32) * bound
    par = jnp.stack([F32_EXP_ZERO + 2.0 * bound,
                     (bound < FOX_FAST_MAX_LOGIT).astype(F32)])
    tile2 = lambda g: jnp.tile(g, 2).reshape(1, -1)
    q, k, v, c = _fox_prep(x, ln, w_fox, _pad_row(bf, LANES), tile2(qn), tile2(kn), shift_row)
    per_tile = FOX_TQ // FOX_PREP_TILE
    cq0 = c[0::per_tile, 0, 0:FOX_HEADS]
    cend = c[per_tile - 1::per_tile, 1, 0:FOX_HEADS]
    return _fox_attn(par, cq0, cend, q, k, v)


def _linear_attn_chunks(qd_s, ki_s, ke_s, v_s, dec_s, o_s, st_s, nheads):
    t = qd_s.shape[0]
    r = lax.broadcasted_iota(jnp.int32, (CHUNK, CHUNK), 0)
    c = lax.broadcasted_iota(jnp.int32, (CHUNK, CHUNK), 1)
    tril = c <= r
    states = [st_s[hd] for hd in range(nheads)]
    for ci in range(t // CHUNK):
        rows = slice(ci * CHUNK, (ci + 1) * CHUNK)
        last = slice((ci + 1) * CHUNK - 1, (ci + 1) * CHUNK)
        for hd in range(nheads):
            cols = slice(hd * HEAD_DIM, (hd + 1) * HEAD_DIM)
            qd = qd_s[rows, cols].astype(BF16)
            ki = ki_s[rows, cols].astype(BF16)
            ke = ke_s[rows, cols].astype(BF16)
            vv = v_s[rows, cols].astype(BF16)
            attn = jnp.where(tril, _dot_nt(qd, ki), 0.0).astype(BF16)
            o_s[rows, cols] = _dot(attn, vv) + _dot_nt(qd, states[hd].astype(BF16))
            states[hd] = states[hd] * dec_s[last, cols] + _dot_tn(vv, ke)
    for hd in range(nheads):
        st_s[hd] = states[hd]


def _gla_kernel(x_ref, ln_ref, w_ref, w2_ref, b2_ref, gn_ref, y_ref,
                qd_s, ki_s, ke_s, v_s, dec_s, o_s, st_s):
    t = x_ref.shape[0]

    @pl.when(pl.program_id(0) == 0)
    def _():
        st_s[...] = jnp.zeros_like(st_s)

    h = _rmsnorm(x_ref[...], ln_ref[...]).astype(BF16)
    u = _dot(h, w_ref[...])
    z = _dot(u[:, 4 * GLA_W:4 * GLA_W + LANES].astype(BF16), w2_ref[...]) + b2_ref[...]
    log_a = _log_sigmoid(z) * (1.0 / GLA_TAU)
    incl, after = _chunk_masks(t)
    b = _sel_dot(incl, log_a)
    rem = _sel_dot(after, log_a)
    eb = jnp.exp(b)
    k = u[:, GLA_W:2 * GLA_W]
    qd_s[...] = u[:, 0:GLA_W] * (HEAD_DIM ** -0.5) * eb
    ki_s[...] = k * jnp.exp(-b)
    ke_s[...] = k * jnp.exp(rem)
    v_s[...] = u[:, 2 * GLA_W:3 * GLA_W]
    dec_s[...] = eb
    _linear_attn_chunks(qd_s, ki_s, ke_s, v_s, dec_s, o_s, st_s, GLA_HEADS)
    o = o_s[...]
    y = o * lax.rsqrt(_group_mean_sq(o, HEAD_DIM) + NORM_EPS) * gn_ref[...]
    y_ref[...] = (y * _silu(u[:, 3 * GLA_W:4 * GLA_W])).astype(BF16)


def _ret_kernel(x_ref, ln_ref, w_ref, cos_ref, sin_ref, gn_ref, y_ref,
                qd_s, ki_s, ke_s, v_s, dec_s, o_s, st_s):
    t = x_ref.shape[0]

    @pl.when(pl.program_id(0) == 0)
    def _():
        st_s[...] = jnp.zeros_like(st_s)

    h = _rmsnorm(x_ref[...], ln_ref[...]).astype(BF16)
    u = _dot(h, w_ref[...])
    cos, sin = cos_ref[...], sin_ref[...]
    lane = lax.broadcasted_iota(jnp.int32, (t, RET_W), 1)
    half = HEAD_DIM // 2
    first_half = (lax.broadcasted_iota(jnp.int32, (t, LANES), 1) % HEAD_DIM) < half

    def rotate(a):
        blocks = []
        for b in range(0, RET_W, LANES):
            ab = a[:, b:b + LANES]
            swapped = jnp.where(first_half, pltpu.roll(ab, LANES - half, axis=1),
                                pltpu.roll(ab, half, axis=1))
            blocks.append(ab * cos[:, b:b + LANES] + swapped * sin[:, b:b + LANES])
        return jnp.concatenate(blocks, axis=1)

    q = rotate(u[:, 0:RET_W])
    k = rotate(u[:, RET_W:2 * RET_W]) * (HEAD_DIM ** -0.5)
    row = lax.broadcasted_iota(jnp.int32, (t, RET_W), 0)
    lg = jnp.zeros((t, RET_W), F32)
    for hd in range(RET_HEADS):
        lg = jnp.where(lane // HEAD_DIM == hd, math.log(1.0 - 2.0 ** (-5.0 - hd)), lg)
    pos = (row % CHUNK).astype(F32)
    qw = jnp.exp((pos + 1.0) * lg)
    qd_s[...] = q * qw
    ki_s[...] = k * jnp.exp(-(pos + 1.0) * lg)
    ke_s[...] = k * jnp.exp((CHUNK - 1.0 - pos) * lg)
    v_s[...] = u[:, 2 * RET_W:3 * RET_W]
    dec_s[...] = qw
    _linear_attn_chunks(qd_s, ki_s, ke_s, v_s, dec_s, o_s, st_s, RET_HEADS)
    o = o_s[...]
    y = o * lax.rsqrt(_group_mean_sq(o, HEAD_DIM) + NORM_EPS) * gn_ref[...]
    y_ref[...] = (y * _silu(u[:, 3 * RET_W:4 * RET_W])).astype(BF16)


def _linear_scratch(t, width, nheads):
    return ([pltpu.VMEM((t, width), F32)] * 6
            + [pltpu.VMEM((nheads, HEAD_DIM, HEAD_DIM), F32)])


def _gla(x, ln, w_gla, w2, b2_row, gn_row):
    seq = x.shape[0]
    t = ROW_TILE
    return pl.pallas_call(
        _gla_kernel,
        grid=(seq // t,),
        in_specs=[pl.BlockSpec((t, D_MODEL), lambda i: (i, 0)),
                  _resident((1, D_MODEL)), _resident(w_gla.shape), _resident(w2.shape),
                  _resident((1, GLA_W)), _resident((1, GLA_W))],
        out_specs=pl.BlockSpec((t, GLA_W), lambda i: (i, 0)),
        out_shape=jax.ShapeDtypeStruct((seq, GLA_W), BF16),
        scratch_shapes=_linear_scratch(t, GLA_W, GLA_HEADS),
        compiler_params=pltpu.CompilerParams(dimension_semantics=("arbitrary",),
                                             vmem_limit_bytes=VMEM_LIMIT),
        name="gla_mixer",
    )(x, ln, w_gla, w2, b2_row, gn_row)


def _ret(x, ln, w_ret, cos_t, sin_t, gn_row):
    seq = x.shape[0]
    t = ROW_TILE
    return pl.pallas_call(
        _ret_kernel,
        grid=(seq // t,),
        in_specs=[pl.BlockSpec((t, D_MODEL), lambda i: (i, 0)),
                  _resident((1, D_MODEL)), _resident(w_ret.shape),
                  pl.BlockSpec((t, RET_W), lambda i: (i, 0)),
                  pl.BlockSpec((t, RET_W), lambda i: (i, 0)),
                  _resident((1, RET_W))],
        out_specs=pl.BlockSpec((t, RET_W), lambda i: (i, 0)),
        out_shape=jax.ShapeDtypeStruct((seq, RET_W), BF16),
        scratch_shapes=_linear_scratch(t, RET_W, RET_HEADS),
        compiler_params=pltpu.CompilerParams(dimension_semantics=("arbitrary",),
                                             vmem_limit_bytes=VMEM_LIMIT),
        name="ret_mixer",
    )(x, ln, w_ret, cos_t, sin_t, gn_row)


SSD_PROJ = SSD_INNER + SSD_CONV_CH


def _ssd_kernel(x_ref, ln_ref, w_ref, cw_ref, cb_ref, dtb_row_ref,
                alog_row_ref, d_row_ref, gn_ref, y_ref,
                xp_s, xs_s, bc_s, cs_s, ecs_s, xdt_s, xdtw_s, cst_s, o_s, st_s):
    t = x_ref.shape[0]
    nc = t // CHUNK

    @pl.when(pl.program_id(0) == 0)
    def _():
        st_s[...] = jnp.zeros_like(st_s)
        xp_s[0:SUBLANES, :] = jnp.zeros((SUBLANES, SSD_CONV_CH), F32)

    h = _rmsnorm(x_ref[...], ln_ref[...]).astype(BF16)
    u = _dot(h, w_ref[...])
    z = u[:, 0:SSD_INNER]

    xp_s[SUBLANES:SUBLANES + t, :] = u[:, SSD_INNER:SSD_PROJ]
    conv = cb_ref[...] + cw_ref[SSD_CONV - 1:SSD_CONV, :] * xp_s[SUBLANES:SUBLANES + t, :]
    for kk in range(SSD_CONV - 1):
        off = SUBLANES - (SSD_CONV - 1) + kk
        conv = conv + cw_ref[kk:kk + 1, :] * xp_s[off:off + t, :]
    xp_s[0:SUBLANES, :] = xp_s[t:t + SUBLANES, :]
    xbc = _silu(conv)
    xs = xbc[:, 0:SSD_INNER]
    xs_s[...] = xs
    bc_s[...] = xbc[:, SSD_INNER:SSD_CONV_CH]

    dt = _softplus(u[:, SSD_PROJ:SSD_PROJ + LANES] + dtb_row_ref[...])
    dta = dt * -jnp.exp(alog_row_ref[...])
    incl, after = _chunk_masks(t)
    cs = _sel_dot(incl, dta)
    rem = _sel_dot(after, dta)
    cs_t = cs.T[0:SSD_HEADS, :]
    for ci in range(nc):
        cst_s[ci] = cs_t[:, ci * CHUNK:(ci + 1) * CHUNK]

    er = lax.broadcasted_iota(jnp.int32, (LANES, SSD_INNER), 0)
    ec = lax.broadcasted_iota(jnp.int32, (LANES, SSD_INNER), 1)
    expand = jnp.where(ec // HEAD_DIM == er, 1.0, 0.0).astype(BF16)
    cs_x = _dot_sel(cs, expand)
    xdt = xs * _dot_sel(dt, expand)
    cs_s[...] = cs_x
    ecs_s[...] = jnp.exp(cs_x)
    xdt_s[...] = xdt
    xdtw_s[...] = xdt * jnp.exp(_dot_sel(rem, expand))

    r = lax.broadcasted_iota(jnp.int32, (CHUNK, CHUNK), 0)
    c = lax.broadcasted_iota(jnp.int32, (CHUNK, CHUNK), 1)
    tril = c <= r
    gw = SSD_GROUPS * SSD_STATE

    states = [st_s[hd] for hd in range(SSD_HEADS)]
    for ci in range(nc):
        rows = slice(ci * CHUNK, (ci + 1) * CHUNK)
        last = slice((ci + 1) * CHUNK - 1, (ci + 1) * CHUNK)
        cst = cst_s[ci]
        for g in range(SSD_GROUPS):
            bm = bc_s[rows, g * SSD_STATE:(g + 1) * SSD_STATE].astype(BF16)
            cm = bc_s[rows, gw + g * SSD_STATE:gw + (g + 1) * SSD_STATE].astype(BF16)
            cb = _dot_nt(cm, bm)
            for rr in range(SSD_HEADS // SSD_GROUPS):
                hd = g * (SSD_HEADS // SSD_GROUPS) + rr
                cols = slice(hd * HEAD_DIM, (hd + 1) * HEAD_DIM)
                seg = cs_s[rows, cols] - cst[hd:hd + 1, :]
                decay = jnp.exp(jnp.where(tril, seg, -jnp.inf))
                y = _dot((cb * decay).astype(BF16), xdt_s[rows, cols].astype(BF16))
                y = y + _dot(cm, states[hd].astype(BF16)) * ecs_s[rows, cols]
                o_s[rows, cols] = y
                states[hd] = (states[hd] * ecs_s[last, cols]
                              + _dot_tn(bm, xdtw_s[rows, cols].astype(BF16)))
    for hd in range(SSD_HEADS):
        st_s[hd] = states[hd]

    y = (o_s[...] + d_row_ref[...] * xs_s[...]) * _silu(z)
    y = y * lax.rsqrt(_group_mean_sq(y, SSD_INNER // SSD_GROUPS) + NORM_EPS) * gn_ref[...]
    y_ref[...] = y.astype(BF16)


def _ssd(x, ln, w_ssd, conv_w, conv_b_row, dtb_row, alog_row, d_row, gn_row):
    seq = x.shape[0]
    t = ROW_TILE
    wide = pltpu.VMEM((t, SSD_INNER), F32)
    return pl.pallas_call(
        _ssd_kernel,
        grid=(seq // t,),
        in_specs=[pl.BlockSpec((t, D_MODEL), lambda i: (i, 0)),
                  _resident((1, D_MODEL)), _resident(w_ssd.shape),
                  _resident(conv_w.shape), _resident(conv_b_row.shape),
                  _resident((1, LANES)), _resident((1, LANES)),
                  _resident((1, SSD_INNER)), _resident((1, SSD_INNER))],
        out_specs=pl.BlockSpec((t, SSD_INNER), lambda i: (i, 0)),
        out_shape=jax.ShapeDtypeStruct((seq, SSD_INNER), BF16),
        scratch_shapes=[pltpu.VMEM((t + 2 * SUBLANES, SSD_CONV_CH), F32),
                        wide,
                        pltpu.VMEM((t, 2 * SSD_GROUPS * SSD_STATE), F32),
                        wide, wide, wide, wide,
                        pltpu.VMEM((t // CHUNK, SSD_HEADS, CHUNK), F32),
                        wide,
                        pltpu.VMEM((SSD_HEADS, SSD_STATE, HEAD_DIM), F32)],
        compiler_params=pltpu.CompilerParams(dimension_semantics=("arbitrary",),
                                             vmem_limit_bytes=VMEM_LIMIT),
        name="ssd_mixer",
    )(x, ln, w_ssd, conv_w, conv_b_row, dtb_row, alog_row, d_row, gn_row)


def _merge_kernel(x_ref, ln_ref, wg_ref, ya_ref, yb_ref, yc_ref, yd_ref,
                  wa_ref, wb_ref, wc_ref, wd_ref, wo_ref, out_ref):
    x = x_ref[...]
    h = _rmsnorm(x, ln_ref[...]).astype(BF16)

    def gate(b):
        return _sigmoid(_dot(h, wg_ref[:, b * D_MODEL:(b + 1) * D_MODEL]))

    up_a = _dot(ya_ref[0], wa_ref[0])
    for hd in range(1, FOX_HEADS):
        up_a = up_a + _dot(ya_ref[hd], wa_ref[hd])
    merged = gate(0) * up_a
    merged = merged + gate(1) * _dot(yb_ref[...], wb_ref[...])
    merged = merged + gate(2) * _dot(yc_ref[...], wc_ref[...])
    merged = merged + gate(3) * _dot(yd_ref[...], wd_ref[...])
    out_ref[...] = x + _dot(merged.astype(BF16), wo_ref[...])


def _merge(x, ln, w_gates, y_a, y_b, y_c, y_d, w_up_a, w_up_b, w_up_c, w_up_d, w_out):
    seq = x.shape[0]
    t = ROW_TILE
    row = lambda w: pl.BlockSpec((t, w), lambda i: (i, 0))
    return pl.pallas_call(
        _merge_kernel,
        grid=(seq // t,),
        in_specs=[row(D_MODEL), _resident((1, D_MODEL)), _resident(w_gates.shape),
                  pl.BlockSpec((FOX_HEADS, t, LANES), lambda i: (0, i, 0)),
                  row(GLA_W), row(RET_W), row(SSD_INNER),
                  _resident(w_up_a.shape), _resident(w_up_b.shape), _resident(w_up_c.shape),
                  _resident(w_up_d.shape), _resident(w_out.shape)],
        out_specs=row(D_MODEL),
        out_shape=jax.ShapeDtypeStruct((seq, D_MODEL), F32),
        compiler_params=pltpu.CompilerParams(dimension_semantics=("arbitrary",),
                                             vmem_limit_bytes=VMEM_LIMIT),
        name="gated_merge",
    )(x, ln, w_gates, y_a, y_b, y_c, y_d, w_up_a, w_up_b, w_up_c, w_up_d, w_out)


def _ffn_kernel(x_ref, ln_ref, w1_ref, w2_ref, out_ref):
    x = x_ref[...]
    h = _rmsnorm(x, ln_ref[...]).astype(BF16)
    gate = _dot(h, w1_ref[:, 0:FFN_HIDDEN])
    up = _dot(h, w1_ref[:, FFN_HIDDEN:2 * FFN_HIDDEN])
    out_ref[...] = x + _dot((_silu(gate) * up).astype(BF16), w2_ref[...])


def _ffn(x, ln, w1, w2):
    seq = x.shape[0]
    t = FFN_TILE
    return pl.pallas_call(
        _ffn_kernel,
        grid=(seq // t,),
        in_specs=[pl.BlockSpec((t, D_MODEL), lambda i: (i, 0)),
                  _resident((1, D_MODEL)), _resident(w1.shape), _resident(w2.shape)],
        out_specs=pl.BlockSpec((t, D_MODEL), lambda i: (i, 0)),
        out_shape=jax.ShapeDtypeStruct((seq, D_MODEL), F32),
        compiler_params=pltpu.CompilerParams(dimension_semantics=("arbitrary",),
                                             vmem_limit_bytes=VMEM_LIMIT),
        name="swiglu",
    )(x, ln, w1, w2)


def _pad_cols(w, width):
    return jnp.pad(w, ((0, 0), (0, width - w.shape[1])))


def _pad_row(v, width):
    v = v.reshape(1, -1)
    return jnp.pad(v, ((0, 0), (0, width - v.shape[1])))


def _layer_weights(w_in):
    offs = np.concatenate([[0], np.cumsum(IN_SPLITS)])
    (fq, fk, fv, ff, gq, gk, gv, glr, gr, rq, rk, rv, rg, z, xbc, dt, gates) = [
        w_in[:, offs[n]:offs[n + 1]] for n in range(len(IN_SPLITS))]
    cat = lambda parts: jnp.concatenate(parts, axis=1).astype(BF16)
    w_fox = cat([fq, fk, fv, _pad_cols(ff, LANES)])
    w_gla = cat([gq, gk, gv, gr, _pad_cols(glr, LANES)])
    w_ret = cat([rq, rk, rv, rg])
    w_ssd = cat([z, xbc, _pad_cols(dt, LANES)])
    return w_fox, w_gla, w_ret, w_ssd, gates.astype(BF16)


def kernel(x, positions, ln1, ln2, w_in, fox_bf, fox_qn, fox_kn, gla_w2, gla_b, gla_norm, ret_norm,
           ssd_conv_w, ssd_conv_b, ssd_dt_bias, ssd_a_log, ssd_d, ssd_norm,
           w_up_a, w_up_b, w_up_c, w_up_d, w_out, w_ffn_in, w_ffn_out):
    bsz, seq, d = x.shape
    assert bsz == 1 and d == D_MODEL and seq % ROW_TILE == 0 and seq % FOX_TQ == 0
    depth = ln1.shape[0]
    xr = x.reshape(seq, d)
    cos_t, sin_t = _rotary_tables(positions, seq)
    tile_heads = lambda g, n: jnp.tile(g, n).reshape(1, -1)
    for l in range(depth):
        w_fox, w_gla, w_ret, w_ssd, w_gates = _layer_weights(
            lax.optimization_barrier(w_in[l].astype(BF16)))
        ln = ln1[l].reshape(1, d)
        y_a = _fox(xr, ln, w_fox, fox_bf[l], fox_qn[l], fox_kn[l])
        w2 = jnp.pad(gla_w2[l], ((0, LANES - GLA_LOWRANK), (0, 0))).astype(BF16)
        y_b = _gla(xr, ln, w_gla, w2, gla_b[l].reshape(1, -1), tile_heads(gla_norm[l], GLA_HEADS))
        y_c = _ret(xr, ln, w_ret, cos_t, sin_t, tile_heads(ret_norm[l], RET_HEADS))
        y_d = _ssd(xr, ln, w_ssd, ssd_conv_w[l], ssd_conv_b[l].reshape(1, -1),
                   _pad_row(ssd_dt_bias[l], LANES), _pad_row(ssd_a_log[l], LANES),
                   jnp.repeat(ssd_d[l], HEAD_DIM).reshape(1, -1), ssd_norm[l].reshape(1, -1))
        wa = jnp.pad(w_up_a[l].reshape(FOX_HEADS, HEAD_DIM, d),
                     ((0, 0), (0, LANES - HEAD_DIM), (0, 0))).astype(BF16)
        xr = _merge(xr, ln, w_gates, y_a, y_b, y_c, y_d, wa, w_up_b[l].astype(BF16),
                    w_up_c[l].astype(BF16), w_up_d[l].astype(BF16), w_out[l].astype(BF16))
        xr = _ffn(xr, ln2[l].reshape(1, d), w_ffn_in[l].astype(BF16), w_ffn_out[l].astype(BF16))
    return xr.reshape(bsz, seq, d)
```

```python
import functools
import math

import numpy as np
import jax
import jax.numpy as jnp
from jax import lax
from jax.experimental import pallas as pl
from jax.experimental.pallas import tpu as pltpu

D_MODEL = 1024
HEAD_DIM = 64
CHUNK = 64
FOX_HEADS = 4
GLA_HEADS = 4
GLA_LOWRANK = 16
GLA_TAU = 16.0
RET_HEADS = 4
ROPE_THETA = 10000.0
SSD_HEADS = 8
SSD_GROUPS = 2
SSD_STATE = 64
SSD_CONV = 4
SSD_INNER = SSD_HEADS * HEAD_DIM
SSD_CONV_CH = SSD_INNER + 2 * SSD_GROUPS * SSD_STATE
FOX_W = FOX_HEADS * HEAD_DIM
GLA_W = GLA_HEADS * HEAD_DIM
RET_W = RET_HEADS * HEAD_DIM
N_BRANCH = 4
FFN_HIDDEN = ((8 * D_MODEL + 3 * 256 - 1) // (3 * 256)) * 256
NORM_EPS = 1e-6
IN_SPLITS = (FOX_W, FOX_W, FOX_W, FOX_HEADS,
             GLA_W, GLA_W, GLA_W, GLA_LOWRANK, GLA_W,
             RET_W, RET_W, RET_W, RET_W,
             SSD_INNER, SSD_CONV_CH, SSD_HEADS,
             N_BRANCH * D_MODEL)

LANES = 128
SUBLANES = 8
VMEM_LIMIT = 56 * 1024 * 1024

ROW_TILE = 512
FOX_PREP_TILE = 256
FOX_TQ = 512
FOX_TK = 512
FFN_TILE = 512

F32 = jnp.float32
BF16 = jnp.bfloat16


def _rmsnorm(x, g):
    return x * lax.rsqrt(jnp.mean(x * x, axis=-1, keepdims=True) + NORM_EPS) * g


def _log_sigmoid(x):
    return jnp.minimum(x, 0.0) - jnp.log1p(jnp.exp(-jnp.abs(x)))


def _softplus(x):
    return jnp.maximum(x, 0.0) + jnp.log1p(jnp.exp(-jnp.abs(x)))


def _sigmoid(x):
    return 1.0 / (1.0 + jnp.exp(-x))


def _silu(x):
    return x * _sigmoid(x)


def _split3(x):
    hi = x.astype(BF16)
    r1 = x - hi.astype(F32)
    mid = r1.astype(BF16)
    lo = (r1 - mid.astype(F32)).astype(BF16)
    return hi, mid, lo


def _split2(x):
    hi = x.astype(BF16)
    return hi, (x - hi.astype(F32)).astype(BF16)


def _dot(a, b):
    return jnp.dot(a, b, preferred_element_type=F32)


def _dot_nt(a, b):
    return lax.dot_general(a, b, (((1,), (1,)), ((), ())), preferred_element_type=F32)


def _dot_tn(a, b):
    return lax.dot_general(a, b, (((0,), (0,)), ((), ())), preferred_element_type=F32)


def _sel_dot(mat, x):
    hi, mid, lo = _split3(x)
    return _dot(mat, hi) + _dot(mat, mid) + _dot(mat, lo)


def _dot_sel2(x, mat):
    hi, lo = _split2(x)
    return _dot(hi, mat) + _dot(lo, mat)


def _chunk_cumsum(x):
    t = x.shape[0]
    r = lax.broadcasted_iota(jnp.int32, (t, t), 0)
    c = lax.broadcasted_iota(jnp.int32, (t, t), 1)
    incl = jnp.where(((r // CHUNK) == (c // CHUNK)) & (c <= r), 1.0, 0.0).astype(BF16)
    hi, lo = _split2(x)
    b = _dot(incl, hi) + _dot(incl, lo)
    return b, _chunk_last(b) - b


def _chunk_last(b):
    t, w = b.shape
    return jnp.concatenate([jnp.broadcast_to(b[e - 1:e, :], (CHUNK, w)) for e in range(CHUNK, t + 1, CHUNK)],
                           axis=0)


def _group_mean_sq(o, group):
    w = o.shape[-1]
    r = lax.broadcasted_iota(jnp.int32, (w, w), 0)
    c = lax.broadcasted_iota(jnp.int32, (w, w), 1)
    bd = jnp.where((r // group) == (c // group), 1.0, 0.0).astype(BF16)
    return _dot_sel2(o * o, bd) * (1.0 / group)


def _resident(shape):
    nd = len(shape)
    return pl.BlockSpec(shape, lambda *_: (0,) * nd, pipeline_mode=pl.Buffered(1))


def _rotary_kernel(pos_ref, inv_ref, sign_ref, cos_ref, sin_ref):
    ang = pos_ref[...] * inv_ref[...]
    cos_ref[...] = jnp.cos(ang)
    sin_ref[...] = jnp.sin(ang) * sign_ref[...]


def _rotary_tables(positions, seq):
    half = HEAD_DIM // 2
    inv = ROPE_THETA ** (-jnp.arange(half, dtype=F32) / half)
    inv_row = jnp.tile(inv, 2 * RET_HEADS).reshape(1, RET_W)
    sign_row = jnp.tile(jnp.concatenate([-jnp.ones((half,), F32), jnp.ones((half,), F32)]),
                        RET_HEADS).reshape(1, RET_W)
    pos = positions.astype(F32).reshape(seq, 1)
    t = ROW_TILE
    return pl.pallas_call(
        _rotary_kernel,
        grid=(seq // t,),
        in_specs=[pl.BlockSpec((t, 1), lambda i: (i, 0)),
                  pl.BlockSpec((1, RET_W), lambda i: (0, 0)),
                  pl.BlockSpec((1, RET_W), lambda i: (0, 0))],
        out_specs=[pl.BlockSpec((t, RET_W), lambda i: (i, 0)),
                   pl.BlockSpec((t, RET_W), lambda i: (i, 0))],
        out_shape=[jax.ShapeDtypeStruct((seq, RET_W), F32)] * 2,
        name="rotary_tables",
    )(pos, inv_row, sign_row)


FOX_AUG = FOX_W * 3
LOG2E = math.log2(math.e)
F32_EXP_ZERO = 105.0
FOX_FAST_MAX_LOGIT = 40.0


def _fox_prep_kernel(x_ref, ln_ref, w_ref, bf_ref, qn_ref, kn_ref, shift_ref,
                     q_out, k_out, v_out, c_out, carry_ref):
    t = x_ref.shape[0]

    @pl.when(pl.program_id(0) == 0)
    def _():
        carry_ref[...] = jnp.zeros_like(carry_ref)

    h = _rmsnorm(x_ref[...], ln_ref[...]).astype(BF16)
    u = _dot(h, w_ref[...])
    ls = _log_sigmoid(u[:, FOX_AUG:FOX_AUG + LANES] + bf_ref[...])
    r = lax.broadcasted_iota(jnp.int32, (t, t), 0)
    cc = lax.broadcasted_iota(jnp.int32, (t, t), 1)
    tri = jnp.where(cc <= r, 1.0, 0.0).astype(BF16)
    c = _sel_dot(tri, ls) + carry_ref[0:1, :]
    carry_ref[0:1, :] = c[t - 1:t, :]
    c_out[0] = jnp.concatenate([c[0:1, :], c[t - 1:t, :], jnp.zeros((SUBLANES - 2, LANES), F32)], axis=0)
    c_hi, c_mid, c_lo = _split3(c * LOG2E)
    c_hi, c_mid, c_lo = c_hi.astype(F32), c_mid.astype(F32), c_lo.astype(F32)

    lane = lax.broadcasted_iota(jnp.int32, (t, LANES), 1)
    low = lane < HEAD_DIM
    j = lane - HEAD_DIM
    shift = shift_ref[...]

    def head_norm(pair, gain, mult):
        sq = pair * pair
        s_lo = jnp.sum(jnp.where(low, sq, 0.0), axis=-1, keepdims=True)
        s_hi = jnp.sum(jnp.where(low, 0.0, sq), axis=-1, keepdims=True)
        ms = jnp.where(low, s_lo, s_hi) * (1.0 / HEAD_DIM)
        return pair * lax.rsqrt(ms + NORM_EPS) * (gain * mult)

    for p in range(FOX_HEADS // 2):
        qp = head_norm(u[:, p * LANES:(p + 1) * LANES], qn_ref[...], HEAD_DIM ** -0.5 * LOG2E)
        kp = head_norm(u[:, FOX_W + p * LANES:FOX_W + (p + 1) * LANES], kn_ref[...], 1.0)
        vp = u[:, 2 * FOX_W + p * LANES:2 * FOX_W + (p + 1) * LANES]
        for sub in range(2):
            hd = 2 * p + sub
            if sub == 1:
                qp, kp, vp = (pltpu.roll(a, HEAD_DIM, axis=1) for a in (qp, kp, vp))
            ch, cm, cl = (jnp.broadcast_to(a[:, hd:hd + 1], (t, LANES)) for a in (c_hi, c_mid, c_lo))
            aug_q = jnp.where(j == 0, ch, jnp.where(j == 1, cm, jnp.where(j == 2, cl,
                              jnp.where(j < 7, 1.0, 0.0))))
            aug_k = jnp.where(j < 3, 1.0, jnp.where(j == 3, -ch, jnp.where(j == 4, -cm,
                              jnp.where(j == 5, -cl, jnp.where(j == 6, -shift, 0.0)))))
            aug_v = jnp.where(j == 0, 1.0, 0.0)
            q_out[hd] = jnp.where(low, qp, aug_q).astype(BF16)
            k_out[hd] = jnp.where(low, kp, aug_k).astype(BF16)
            v_out[hd] = jnp.where(low, vp, aug_v).astype(BF16)


def _fox_prep(x, ln, w_fox, bf_row, qn_row, kn_row, shift_row):
    seq = x.shape[0]
    t = FOX_PREP_TILE
    out = jax.ShapeDtypeStruct((FOX_HEADS, seq, LANES), BF16)
    ospec = pl.BlockSpec((FOX_HEADS, t, LANES), lambda i: (0, i, 0))
    return pl.pallas_call(
        _fox_prep_kernel,
        grid=(seq // t,),
        in_specs=[pl.BlockSpec((t, D_MODEL), lambda i: (i, 0)),
                  _resident((1, D_MODEL)), _resident(w_fox.shape), _resident((1, LANES)),
                  _resident((1, LANES)), _resident((1, LANES)), _resident((1, LANES))],
        out_specs=[ospec, ospec, ospec, pl.BlockSpec((1, SUBLANES, LANES), lambda i: (i, 0, 0))],
        out_shape=[out, out, out, jax.ShapeDtypeStruct((seq // t, SUBLANES, LANES), F32)],
        scratch_shapes=[pltpu.VMEM((SUBLANES, LANES), F32)],
        compiler_params=pltpu.CompilerParams(dimension_semantics=("arbitrary",),
                                             vmem_limit_bytes=VMEM_LIMIT),
        name="fox_prep",
    )(x, ln, w_fox, bf_row, qn_row, kn_row, shift_row)


def _fox_attn_kernel(par_ref, cq0_ref, cend_ref, q_ref, k_ref, v_ref, o_ref):
    hd = pl.program_id(0)
    i = pl.program_id(1)
    q = q_ref[0]
    tq = q.shape[0]
    c0 = cq0_ref[i, hd]
    thr = par_ref[0]
    j_lo = lax.while_loop(
        lambda j: jnp.logical_and(j > 0, c0 - cend_ref[jnp.maximum(j - 1, 0), hd] >= -thr),
        lambda j: j - 1, i)

    def scores(j, masked, nblk=1):
        off = pl.multiple_of(j * FOX_TK, FOX_TK)
        s = _dot_nt(q, k_ref[0, pl.ds(off, nblk * FOX_TK), :])
        if masked:
            r = lax.broadcasted_iota(jnp.int32, s.shape, 0)
            c = lax.broadcasted_iota(jnp.int32, s.shape, 1)
            s = jnp.where(c <= r, s, -jnp.inf)
        return s, v_ref[0, pl.ds(off, nblk * FOX_TK), :]

    def fixed_shift():
        def block(j, masked, nblk):
            s, v = scores(j, masked, nblk)
            return _dot(jnp.exp2(s).astype(BF16), v)
        acc = block(i, True, 1)
        n = i - j_lo
        acc = lax.cond(n % 2 == 1, lambda: acc + block(j_lo, False, 1), lambda: acc)
        acc = lax.cond((n // 2) % 2 == 1, lambda: acc + block(j_lo + n % 2, False, 2), lambda: acc)
        first = j_lo + n % 4
        return lax.fori_loop(0, n // 4, lambda p, a: a + block(first + 4 * p, False, 4), acc)

    def running_max():
        def step(j, carry, masked):
            m, acc = carry
            s, v = scores(j, masked)
            m_new = jnp.maximum(m, jnp.max(s, axis=-1, keepdims=True))
            p = jnp.exp2(s - m_new)
            return m_new, jnp.exp2(m - m_new) * acc + _dot(p.astype(BF16), v)
        init = (jnp.full((tq, 1), -jnp.inf, F32), jnp.zeros((tq, LANES), F32))
        carry = step(i, init, True)
        return lax.fori_loop(j_lo, i, lambda j, c: step(j, c, False), carry)[1]

    acc = lax.cond(par_ref[1] > 0.0, fixed_shift, running_max)
    o_ref[0] = (acc / acc[:, HEAD_DIM:HEAD_DIM + 1]).astype(BF16)


def _fox_attn(par, cq0, cend, q, k, v):
    assert FOX_TQ == FOX_TK
    nh, seq, _ = q.shape
    kv_spec = pl.BlockSpec((1, seq, LANES), lambda h, i, *_: (h, 0, 0))
    return pl.pallas_call(
        _fox_attn_kernel,
        grid_spec=pltpu.PrefetchScalarGridSpec(
            num_scalar_prefetch=3,
            grid=(nh, seq // FOX_TQ),
            in_specs=[pl.BlockSpec((1, FOX_TQ, LANES), lambda h, i, *_: (h, i, 0)), kv_spec, kv_spec],
            out_specs=pl.BlockSpec((1, FOX_TQ, LANES), lambda h, i, *_: (h, i, 0))),
        out_shape=jax.ShapeDtypeStruct((nh, seq, LANES), BF16),
        compiler_params=pltpu.CompilerParams(dimension_semantics=("arbitrary", "arbitrary"),
                                             vmem_limit_bytes=VMEM_LIMIT),
        name="fox_attn",
    )(par, cq0, cend, q, k, v)


def _fox(x, ln, w_fox, bf, qn, kn):
    bound = 1.02 * HEAD_DIM ** 0.5 * jnp.max(jnp.abs(qn)) * jnp.max(jnp.abs(kn))
    shift_row = jnp.full((1, LANES), LOG2E, F32) * bound
    par = jnp.stack([F32_EXP_ZERO + 2.0 * bound,
                     (bound < FOX_FAST_MAX_LOGIT).astype(F32)])
    tile2 = lambda g: jnp.tile(g, 2).reshape(1, -1)
    q, k, v, c = _fox_prep(x, ln, w_fox, _pad_row(bf, LANES), tile2(qn), tile2(kn), shift_row)
    per_tile = FOX_TQ // FOX_PREP_TILE
    cq0 = c[0::per_tile, 0, 0:FOX_HEADS]
    cend = c[per_tile - 1::per_tile, 1, 0:FOX_HEADS]
    return _fox_attn(par, cq0, cend, q, k, v)


def _linear_attn_chunks(qd_s, ki_s, ke_s, v_s, dec_s, o_s, st_s, nheads):
    t = qd_s.shape[0]
    r = lax.broadcasted_iota(jnp.int32, (CHUNK, CHUNK), 0)
    c = lax.broadcasted_iota(jnp.int32, (CHUNK, CHUNK), 1)
    tril = c <= r
    states = [st_s[hd] for hd in range(nheads)]
    for ci in range(t // CHUNK):
        rows = slice(ci * CHUNK, (ci + 1) * CHUNK)
        last = slice((ci + 1) * CHUNK - 1, (ci + 1) * CHUNK)
        for hd in range(nheads):
            cols = slice(hd * HEAD_DIM, (hd + 1) * HEAD_DIM)
            qd = qd_s[rows, cols].astype(BF16)
            ki = ki_s[rows, cols].astype(BF16)
            ke = ke_s[rows, cols].astype(BF16)
            vv = v_s[rows, cols].astype(BF16)
            attn = jnp.where(tril, _dot_nt(qd, ki), 0.0).astype(BF16)
            o_s[rows, cols] = _dot(attn, vv) + _dot_nt(qd, states[hd].astype(BF16))
            states[hd] = states[hd] * dec_s[last, cols] + _dot_tn(vv, ke)
    for hd in range(nheads):
        st_s[hd] = states[hd]


def _gla_kernel(x_ref, ln_ref, w_ref, w2_ref, b2_ref, gn_ref, y_ref,
                qd_s, ki_s, ke_s, v_s, dec_s, o_s, st_s):
    t = x_ref.shape[0]

    @pl.when(pl.program_id(0) == 0)
    def _():
        st_s[...] = jnp.zeros_like(st_s)

    h = _rmsnorm(x_ref[...], ln_ref[...]).astype(BF16)
    u = _dot(h, w_ref[...])
    z = _dot(u[:, 4 * GLA_W:4 * GLA_W + LANES].astype(BF16), w2_ref[...]) + b2_ref[...]
    log_a = _log_sigmoid(z) * (1.0 / GLA_TAU)
    b, rem = _chunk_cumsum(log_a)
    eb = jnp.exp(b)
    k = u[:, GLA_W:2 * GLA_W]
    qd_s[...] = u[:, 0:GLA_W] * (HEAD_DIM ** -0.5) * eb
    ki_s[...] = k * jnp.exp(-b)
    ke_s[...] = k * jnp.exp(rem)
    v_s[...] = u[:, 2 * GLA_W:3 * GLA_W]
    dec_s[...] = eb
    _linear_attn_chunks(qd_s, ki_s, ke_s, v_s, dec_s, o_s, st_s, GLA_HEADS)
    o = o_s[...]
    y = o * lax.rsqrt(_group_mean_sq(o, HEAD_DIM) + NORM_EPS) * gn_ref[...]
    y_ref[...] = (y * _silu(u[:, 3 * GLA_W:4 * GLA_W])).astype(BF16)


def _ret_kernel(x_ref, ln_ref, w_ref, cos_ref, sin_ref, gn_ref, y_ref,
                qd_s, ki_s, ke_s, v_s, dec_s, o_s, st_s):
    t = x_ref.shape[0]

    @pl.when(pl.program_id(0) == 0)
    def _():
        st_s[...] = jnp.zeros_like(st_s)

    h = _rmsnorm(x_ref[...], ln_ref[...]).astype(BF16)
    u = _dot(h, w_ref[...])
    cos, sin = cos_ref[...], sin_ref[...]
    lane = lax.broadcasted_iota(jnp.int32, (t, RET_W), 1)
    half = HEAD_DIM // 2
    first_half = (lax.broadcasted_iota(jnp.int32, (t, LANES), 1) % HEAD_DIM) < half

    def rotate(a):
        blocks = []
        for b in range(0, RET_W, LANES):
            ab = a[:, b:b + LANES]
            swapped = jnp.where(first_half, pltpu.roll(ab, LANES - half, axis=1),
                                pltpu.roll(ab, half, axis=1))
            blocks.append(ab * cos[:, b:b + LANES] + swapped * sin[:, b:b + LANES])
        return jnp.concatenate(blocks, axis=1)

    q = rotate(u[:, 0:RET_W])
    k = rotate(u[:, RET_W:2 * RET_W]) * (HEAD_DIM ** -0.5)
    row = lax.broadcasted_iota(jnp.int32, (t, RET_W), 0)
    lg = jnp.zeros((t, RET_W), F32)
    for hd in range(RET_HEADS):
        lg = jnp.where(lane // HEAD_DIM == hd, math.log(1.0 - 2.0 ** (-5.0 - hd)), lg)
    pos = (row % CHUNK).astype(F32)
    qw = jnp.exp((pos + 1.0) * lg)
    qd_s[...] = q * qw
    ki_s[...] = k * jnp.exp(-(pos + 1.0) * lg)
    ke_s[...] = k * jnp.exp((CHUNK - 1.0 - pos) * lg)
    v_s[...] = u[:, 2 * RET_W:3 * RET_W]
    dec_s[...] = qw
    _linear_attn_chunks(qd_s, ki_s, ke_s, v_s, dec_s, o_s, st_s, RET_HEADS)
    o = o_s[...]
    y = o * lax.rsqrt(_group_mean_sq(o, HEAD_DIM) + NORM_EPS) * gn_ref[...]
    y_ref[...] = (y * _silu(u[:, 3 * RET_W:4 * RET_W])).astype(BF16)


def _linear_scratch(t, width, nheads):
    return ([pltpu.VMEM((t, width), F32)] * 6
            + [pltpu.VMEM((nheads, HEAD_DIM, HEAD_DIM), F32)])


def _gla(x, ln, w_gla, w2, b2_row, gn_row):
    seq = x.shape[0]
    t = ROW_TILE
    return pl.pallas_call(
        _gla_kernel,
        grid=(seq // t,),
        in_specs=[pl.BlockSpec((t, D_MODEL), lambda i: (i, 0)),
                  _resident((1, D_MODEL)), _resident(w_gla.shape), _resident(w2.shape),
                  _resident((1, GLA_W)), _resident((1, GLA_W))],
        out_specs=pl.BlockSpec((t, GLA_W), lambda i: (i, 0)),
        out_shape=jax.ShapeDtypeStruct((seq, GLA_W), BF16),
        scratch_shapes=_linear_scratch(t, GLA_W, GLA_HEADS),
        compiler_params=pltpu.CompilerParams(dimension_semantics=("arbitrary",),
                                             vmem_limit_bytes=VMEM_LIMIT),
        name="gla_mixer",
    )(x, ln, w_gla, w2, b2_row, gn_row)


def _ret(x, ln, w_ret, cos_t, sin_t, gn_row):
    seq = x.shape[0]
    t = ROW_TILE
    return pl.pallas_call(
        _ret_kernel,
        grid=(seq // t,),
        in_specs=[pl.BlockSpec((t, D_MODEL), lambda i: (i, 0)),
                  _resident((1, D_MODEL)), _resident(w_ret.shape),
                  pl.BlockSpec((t, RET_W), lambda i: (i, 0)),
                  pl.BlockSpec((t, RET_W), lambda i: (i, 0)),
                  _resident((1, RET_W))],
        out_specs=pl.BlockSpec((t, RET_W), lambda i: (i, 0)),
        out_shape=jax.ShapeDtypeStruct((seq, RET_W), BF16),
        scratch_shapes=_linear_scratch(t, RET_W, RET_HEADS),
        compiler_params=pltpu.CompilerParams(dimension_semantics=("arbitrary",),
                                             vmem_limit_bytes=VMEM_LIMIT),
        name="ret_mixer",
    )(x, ln, w_ret, cos_t, sin_t, gn_row)


SSD_PROJ = SSD_INNER + SSD_CONV_CH


def _ssd_kernel(x_ref, ln_ref, w_ref, cw_ref, cb_ref, dtb_row_ref,
                alog_row_ref, d_row_ref, gn_ref, y_ref,
                xp_s, xs_s, bc_s, cs_s, ecs_s, xdt_s, xdtw_s, cst_s, o_s, st_s):
    t = x_ref.shape[0]
    nc = t // CHUNK

    @pl.when(pl.program_id(0) == 0)
    def _():
        st_s[...] = jnp.zeros_like(st_s)
        xp_s[0:SUBLANES, :] = jnp.zeros((SUBLANES, SSD_CONV_CH), F32)

    h = _rmsnorm(x_ref[...], ln_ref[...]).astype(BF16)
    u = _dot(h, w_ref[...])
    z = u[:, 0:SSD_INNER]

    xp_s[SUBLANES:SUBLANES + t, :] = u[:, SSD_INNER:SSD_PROJ]
    conv = cb_ref[...] + cw_ref[SSD_CONV - 1:SSD_CONV, :] * xp_s[SUBLANES:SUBLANES + t, :]
    for kk in range(SSD_CONV - 1):
        off = SUBLANES - (SSD_CONV - 1) + kk
        conv = conv + cw_ref[kk:kk + 1, :] * xp_s[off:off + t, :]
    xp_s[0:SUBLANES, :] = xp_s[t:t + SUBLANES, :]
    xbc = _silu(conv)
    xs = xbc[:, 0:SSD_INNER]
    xs_s[...] = xs
    bc_s[...] = xbc[:, SSD_INNER:SSD_CONV_CH]

    dt = _softplus(u[:, SSD_PROJ:SSD_PROJ + LANES] + dtb_row_ref[...])
    dta = dt * -jnp.exp(alog_row_ref[...])
    cs, _ = _chunk_cumsum(dta)
    cs_t = cs.T[0:SSD_HEADS, :]
    for ci in range(nc):
        cst_s[ci] = cs_t[:, ci * CHUNK:(ci + 1) * CHUNK]

    er = lax.broadcasted_iota(jnp.int32, (LANES, SSD_INNER), 0)
    ec = lax.broadcasted_iota(jnp.int32, (LANES, SSD_INNER), 1)
    expand = jnp.where(ec // HEAD_DIM == er, 1.0, 0.0).astype(BF16)
    cs_x = _dot_sel2(cs, expand)
    xdt = xs * _dot_sel2(dt, expand)
    cs_s[...] = cs_x
    ecs_s[...] = jnp.exp(cs_x)
    xdt_s[...] = xdt
    xdtw_s[...] = xdt * jnp.exp(_chunk_last(cs_x) - cs_x)

    r = lax.broadcasted_iota(jnp.int32, (CHUNK, CHUNK), 0)
    c = lax.broadcasted_iota(jnp.int32, (CHUNK, CHUNK), 1)
    tril = c <= r
    gw = SSD_GROUPS * SSD_STATE

    hpg = SSD_HEADS // SSD_GROUPS
    gcols = hpg * HEAD_DIM
    states = [st_s[g] for g in range(SSD_GROUPS)]
    for ci in range(nc):
        rows = slice(ci * CHUNK, (ci + 1) * CHUNK)
        last = slice((ci + 1) * CHUNK - 1, (ci + 1) * CHUNK)
        cst = cst_s[ci]
        for g in range(SSD_GROUPS):
            gsl = slice(g * gcols, (g + 1) * gcols)
            bm = bc_s[rows, g * SSD_STATE:(g + 1) * SSD_STATE].astype(BF16)
            cm = bc_s[rows, gw + g * SSD_STATE:gw + (g + 1) * SSD_STATE].astype(BF16)
            cb = _dot_nt(cm, bm)
            y_inter = _dot(cm, states[g].astype(BF16)) * ecs_s[rows, gsl]
            for rr in range(hpg):
                hd = g * hpg + rr
                cols = slice(hd * HEAD_DIM, (hd + 1) * HEAD_DIM)
                seg = cs_s[rows, cols] - cst[hd:hd + 1, :]
                decay = jnp.exp(jnp.where(tril, seg, -jnp.inf))
                y = _dot((cb * decay).astype(BF16), xdt_s[rows, cols].astype(BF16))
                o_s[rows, cols] = y + y_inter[:, rr * HEAD_DIM:(rr + 1) * HEAD_DIM]
            states[g] = (states[g] * ecs_s[last, gsl]
                         + _dot_tn(bm, xdtw_s[rows, gsl].astype(BF16)))
    for g in range(SSD_GROUPS):
        st_s[g] = states[g]

    y = (o_s[...] + d_row_ref[...] * xs_s[...]) * _silu(z)
    gwid = SSD_INNER // SSD_GROUPS
    normed = []
    for g in range(SSD_GROUPS):
        yg = y[:, g * gwid:(g + 1) * gwid]
        normed.append(yg * lax.rsqrt(jnp.mean(yg * yg, axis=-1, keepdims=True) + NORM_EPS))
    y_ref[...] = (jnp.concatenate(normed, axis=1) * gn_ref[...]).astype(BF16)


def _ssd(x, ln, w_ssd, conv_w, conv_b_row, dtb_row, alog_row, d_row, gn_row):
    seq = x.shape[0]
    t = ROW_TILE
    wide = pltpu.VMEM((t, SSD_INNER), F32)
    return pl.pallas_call(
        _ssd_kernel,
        grid=(seq // t,),
        in_specs=[pl.BlockSpec((t, D_MODEL), lambda i: (i, 0)),
                  _resident((1, D_MODEL)), _resident(w_ssd.shape),
                  _resident(conv_w.shape), _resident(conv_b_row.shape),
                  _resident((1, LANES)), _resident((1, LANES)),
                  _resident((1, SSD_INNER)), _resident((1, SSD_INNER))],
        out_specs=pl.BlockSpec((t, SSD_INNER), lambda i: (i, 0)),
        out_shape=jax.ShapeDtypeStruct((seq, SSD_INNER), BF16),
        scratch_shapes=[pltpu.VMEM((t + 2 * SUBLANES, SSD_CONV_CH), F32),
                        wide,
                        pltpu.VMEM((t, 2 * SSD_GROUPS * SSD_STATE), F32),
                        wide, wide, wide, wide,
                        pltpu.VMEM((t // CHUNK, SSD_HEADS, CHUNK), F32),
                        wide,
                        pltpu.VMEM((SSD_GROUPS, SSD_STATE, SSD_INNER // SSD_GROUPS), F32)],
        compiler_params=pltpu.CompilerParams(dimension_semantics=("arbitrary",),
                                             vmem_limit_bytes=VMEM_LIMIT),
        name="ssd_mixer",
    )(x, ln, w_ssd, conv_w, conv_b_row, dtb_row, alog_row, d_row, gn_row)


def _merge_kernel(x_ref, ln_ref, wg_ref, ya_ref, yb_ref, yc_ref, yd_ref,
                  wa_ref, wb_ref, wc_ref, wd_ref, wo_ref, out_ref):
    x = x_ref[...]
    h = _rmsnorm(x, ln_ref[...]).astype(BF16)

    def gate(b):
        return _sigmoid(_dot(h, wg_ref[:, b * D_MODEL:(b + 1) * D_MODEL]))

    low = lax.broadcasted_iota(jnp.int32, (x.shape[0], LANES), 1) < HEAD_DIM
    up_a = None
    for p in range(FOX_HEADS // 2):
        pair = jnp.where(low, ya_ref[2 * p].astype(F32),
                         pltpu.roll(ya_ref[2 * p + 1].astype(F32), HEAD_DIM, axis=1)).astype(BF16)
        part = _dot(pair, wa_ref[p * LANES:(p + 1) * LANES, :])
        up_a = part if up_a is None else up_a + part
    merged = gate(0) * up_a
    merged = merged + gate(1) * _dot(yb_ref[...], wb_ref[...])
    merged = merged + gate(2) * _dot(yc_ref[...], wc_ref[...])
    merged = merged + gate(3) * _dot(yd_ref[...], wd_ref[...])
    out_ref[...] = x + _dot(merged.astype(BF16), wo_ref[...])


def _merge(x, ln, w_gates, y_a, y_b, y_c, y_d, w_up_a, w_up_b, w_up_c, w_up_d, w_out):
    seq = x.shape[0]
    t = ROW_TILE
    row = lambda w: pl.BlockSpec((t, w), lambda i: (i, 0))
    return pl.pallas_call(
        _merge_kernel,
        grid=(seq // t,),
        in_specs=[row(D_MODEL), _resident((1, D_MODEL)), _resident(w_gates.shape),
                  pl.BlockSpec((FOX_HEADS, t, LANES), lambda i: (0, i, 0)),
                  row(GLA_W), row(RET_W), row(SSD_INNER),
                  _resident(w_up_a.shape), _resident(w_up_b.shape), _resident(w_up_c.shape),
                  _resident(w_up_d.shape), _resident(w_out.shape)],
        out_specs=row(D_MODEL),
        out_shape=jax.ShapeDtypeStruct((seq, D_MODEL), F32),
        compiler_params=pltpu.CompilerParams(dimension_semantics=("arbitrary",),
                                             vmem_limit_bytes=VMEM_LIMIT),
        name="gated_merge",
    )(x, ln, w_gates, y_a, y_b, y_c, y_d, w_up_a, w_up_b, w_up_c, w_up_d, w_out)


def _ffn_kernel(x_ref, ln_ref, w1_ref, w2_ref, out_ref):
    x = x_ref[...]
    h = _rmsnorm(x, ln_ref[...]).astype(BF16)
    gate = _dot(h, w1_ref[:, 0:FFN_HIDDEN])
    up = _dot(h, w1_ref[:, FFN_HIDDEN:2 * FFN_HIDDEN])
    out_ref[...] = x + _dot((_silu(gate) * up).astype(BF16), w2_ref[...])


def _ffn(x, ln, w1, w2):
    seq = x.shape[0]
    t = FFN_TILE
    return pl.pallas_call(
        _ffn_kernel,
        grid=(seq // t,),
        in_specs=[pl.BlockSpec((t, D_MODEL), lambda i: (i, 0)),
                  _resident((1, D_MODEL)), _resident(w1.shape), _resident(w2.shape)],
        out_specs=pl.BlockSpec((t, D_MODEL), lambda i: (i, 0)),
        out_shape=jax.ShapeDtypeStruct((seq, D_MODEL), F32),
        compiler_params=pltpu.CompilerParams(dimension_semantics=("arbitrary",),
                                             vmem_limit_bytes=VMEM_LIMIT),
        name="swiglu",
    )(x, ln, w1, w2)


def _pad_cols(w, width):
    return jnp.pad(w, ((0, 0), (0, width - w.shape[1])))


def _pad_row(v, width):
    v = v.reshape(1, -1)
    return jnp.pad(v, ((0, 0), (0, width - v.shape[1])))


def _layer_weights(w_in):
    offs = np.concatenate([[0], np.cumsum(IN_SPLITS)])
    (fq, fk, fv, ff, gq, gk, gv, glr, gr, rq, rk, rv, rg, z, xbc, dt, gates) = [
        w_in[:, offs[n]:offs[n + 1]] for n in range(len(IN_SPLITS))]
    cat = lambda parts: jnp.concatenate(parts, axis=1).astype(BF16)
    w_fox = cat([fq, fk, fv, _pad_cols(ff, LANES)])
    w_gla = cat([gq, gk, gv, gr, _pad_cols(glr, LANES)])
    w_ret = cat([rq, rk, rv, rg])
    w_ssd = cat([z, xbc, _pad_cols(dt, LANES)])
    return w_fox, w_gla, w_ret, w_ssd, gates.astype(BF16)


def kernel(x, positions, ln1, ln2, w_in, fox_bf, fox_qn, fox_kn, gla_w2, gla_b, gla_norm, ret_norm,
           ssd_conv_w, ssd_conv_b, ssd_dt_bias, ssd_a_log, ssd_d, ssd_norm,
           w_up_a, w_up_b, w_up_c, w_up_d, w_out, w_ffn_in, w_ffn_out):
    bsz, seq, d = x.shape
    assert bsz == 1 and d == D_MODEL and seq % ROW_TILE == 0 and seq % FOX_TQ == 0
    depth = ln1.shape[0]
    xr = x.reshape(seq, d)
    cos_t, sin_t = _rotary_tables(positions, seq)
    tile_heads = lambda g, n: jnp.tile(g, n).reshape(1, -1)
    for l in range(depth):
        w_fox, w_gla, w_ret, w_ssd, w_gates = _layer_weights(
            lax.optimization_barrier(w_in[l].astype(BF16)))
        ln = ln1[l].reshape(1, d)
        y_a = _fox(xr, ln, w_fox, fox_bf[l], fox_qn[l], fox_kn[l])
        w2 = jnp.pad(gla_w2[l], ((0, LANES - GLA_LOWRANK), (0, 0))).astype(BF16)
        y_b = _gla(xr, ln, w_gla, w2, gla_b[l].reshape(1, -1), tile_heads(gla_norm[l], GLA_HEADS))
        y_c = _ret(xr, ln, w_ret, cos_t, sin_t, tile_heads(ret_norm[l], RET_HEADS))
        y_d = _ssd(xr, ln, w_ssd, ssd_conv_w[l], ssd_conv_b[l].reshape(1, -1),
                   _pad_row(ssd_dt_bias[l], LANES), _pad_row(ssd_a_log[l], LANES),
                   jnp.repeat(ssd_d[l], HEAD_DIM).reshape(1, -1), ssd_norm[l].reshape(1, -1))
        xr = _merge(xr, ln, w_gates, y_a, y_b, y_c, y_d, w_up_a[l].astype(BF16), w_up_b[l].astype(BF16),
                    w_up_c[l].astype(BF16), w_up_d[l].astype(BF16), w_out[l].astype(BF16))
        xr = _ffn(xr, ln2[l].reshape(1, d), w_ffn_in[l].astype(BF16), w_ffn_out[l].astype(BF16))
    return xr.reshape(bsz, seq, d)
```

```python
import functools
import math

import numpy as np
import jax
import jax.numpy as jnp
from jax import lax
from jax.experimental import pallas as pl
from jax.experimental.pallas import tpu as pltpu

D_MODEL = 1024
HEAD_DIM = 64
CHUNK = 64
FOX_HEADS = 4
GLA_HEADS = 4
GLA_LOWRANK = 16
GLA_TAU = 16.0
RET_HEADS = 4
ROPE_THETA = 10000.0
SSD_HEADS = 8
SSD_GROUPS = 2
SSD_STATE = 64
SSD_CONV = 4
SSD_INNER = SSD_HEADS * HEAD_DIM
SSD_CONV_CH = SSD_INNER + 2 * SSD_GROUPS * SSD_STATE
FOX_W = FOX_HEADS * HEAD_DIM
GLA_W = GLA_HEADS * HEAD_DIM
RET_W = RET_HEADS * HEAD_DIM
N_BRANCH = 4
FFN_HIDDEN = ((8 * D_MODEL + 3 * 256 - 1) // (3 * 256)) * 256
NORM_EPS = 1e-6
IN_SPLITS = (FOX_W, FOX_W, FOX_W, FOX_HEADS,
             GLA_W, GLA_W, GLA_W, GLA_LOWRANK, GLA_W,
             RET_W, RET_W, RET_W, RET_W,
             SSD_INNER, SSD_CONV_CH, SSD_HEADS,
             N_BRANCH * D_MODEL)

LANES = 128
SUBLANES = 8
VMEM_LIMIT = 56 * 1024 * 1024

ROW_TILE = 512
FOX_PREP_TILE = 512
FOX_TQ = 512
FOX_TK = 512
FFN_TILE = 512

F32 = jnp.float32
BF16 = jnp.bfloat16


def _rmsnorm(x, g):
    return x * lax.rsqrt(jnp.mean(x * x, axis=-1, keepdims=True) + NORM_EPS) * g


def _log_sigmoid(x):
    return jnp.minimum(x, 0.0) - jnp.log1p(jnp.exp(-jnp.abs(x)))


def _softplus(x):
    return jnp.maximum(x, 0.0) + jnp.log1p(jnp.exp(-jnp.abs(x)))


def _sigmoid(x):
    return 1.0 / (1.0 + jnp.exp(-x))


def _silu(x):
    return x * _sigmoid(x)


def _split3(x):
    hi = x.astype(BF16)
    r1 = x - hi.astype(F32)
    mid = r1.astype(BF16)
    lo = (r1 - mid.astype(F32)).astype(BF16)
    return hi, mid, lo


def _split2(x):
    hi = x.astype(BF16)
    return hi, (x - hi.astype(F32)).astype(BF16)


def _dot(a, b):
    return jnp.dot(a, b, preferred_element_type=F32)


def _dot_nt(a, b):
    return lax.dot_general(a, b, (((1,), (1,)), ((), ())), preferred_element_type=F32)


def _dot_tn(a, b):
    return lax.dot_general(a, b, (((0,), (0,)), ((), ())), preferred_element_type=F32)


def _sel_dot(mat, x):
    hi, mid, lo = _split3(x)
    return _dot(mat, hi) + _dot(mat, mid) + _dot(mat, lo)


def _dot_sel2(x, mat):
    hi, lo = _split2(x)
    return _dot(hi, mat) + _dot(lo, mat)


def _chunk_cumsum(x):
    t = x.shape[0]
    r = lax.broadcasted_iota(jnp.int32, (t, t), 0)
    c = lax.broadcasted_iota(jnp.int32, (t, t), 1)
    incl = jnp.where(((r // CHUNK) == (c // CHUNK)) & (c <= r), 1.0, 0.0).astype(BF16)
    hi, lo = _split2(x)
    b = _dot(incl, hi) + _dot(incl, lo)
    return b, _chunk_last(b) - b


def _chunk_last(b):
    t, w = b.shape
    return jnp.concatenate([jnp.broadcast_to(b[e - 1:e, :], (CHUNK, w)) for e in range(CHUNK, t + 1, CHUNK)],
                           axis=0)


def _group_mean_sq(o, group):
    w = o.shape[-1]
    r = lax.broadcasted_iota(jnp.int32, (w, w), 0)
    c = lax.broadcasted_iota(jnp.int32, (w, w), 1)
    bd = jnp.where((r // group) == (c // group), 1.0, 0.0).astype(BF16)
    return _dot_sel2(o * o, bd) * (1.0 / group)


def _resident(shape):
    nd = len(shape)
    return pl.BlockSpec(shape, lambda *_: (0,) * nd, pipeline_mode=pl.Buffered(1))


def _layer(stacked, l):
    if stacked.ndim == 2:
        return _resident(stacked.shape)
    tail = tuple(stacked.shape[1:])
    return pl.BlockSpec((None,) + tail, lambda *_: (l,) + (0,) * len(tail), pipeline_mode=pl.Buffered(1))


def _rotary_kernel(pos_ref, inv_ref, sign_ref, cos_ref, sin_ref):
    ang = pos_ref[...] * inv_ref[...]
    cos_ref[...] = jnp.cos(ang)
    sin_ref[...] = jnp.sin(ang) * sign_ref[...]


def _rotary_tables(positions, seq):
    half = HEAD_DIM // 2
    inv = ROPE_THETA ** (-jnp.arange(half, dtype=F32) / half)
    reps = LANES // HEAD_DIM
    inv_row = jnp.tile(inv, 2 * reps).reshape(1, LANES)
    sign_row = jnp.tile(jnp.concatenate([-jnp.ones((half,), F32), jnp.ones((half,), F32)]),
                        reps).reshape(1, LANES)
    pos = positions.astype(F32).reshape(seq, 1)
    t = ROW_TILE
    return pl.pallas_call(
        _rotary_kernel,
        grid=(seq // t,),
        in_specs=[pl.BlockSpec((t, 1), lambda i: (i, 0)),
                  pl.BlockSpec((1, LANES), lambda i: (0, 0)),
                  pl.BlockSpec((1, LANES), lambda i: (0, 0))],
        out_specs=[pl.BlockSpec((t, LANES), lambda i: (i, 0)),
                   pl.BlockSpec((t, LANES), lambda i: (i, 0))],
        out_shape=[jax.ShapeDtypeStruct((seq, LANES), F32)] * 2,
        name="rotary_tables",
    )(pos, inv_row, sign_row)


FOX_AUG = FOX_W * 3
LOG2E = math.log2(math.e)
F32_EXP_ZERO = 105.0
FOX_FAST_MAX_LOGIT = 40.0


def _fox_prep_kernel(x_ref, ln_ref, w_ref, bf_ref, qn_ref, kn_ref, shift_ref,
                     q_out, k_out, v_out, c_out, carry_ref):
    t = x_ref.shape[0]

    @pl.when(pl.program_id(0) == 0)
    def _():
        carry_ref[...] = jnp.zeros_like(carry_ref)

    h = _rmsnorm(x_ref[...], ln_ref[...]).astype(BF16)
    u = _dot(h, w_ref[...])
    ls = _log_sigmoid(u[:, FOX_AUG:FOX_AUG + LANES] + bf_ref[...])
    r = lax.broadcasted_iota(jnp.int32, (t, t), 0)
    cc = lax.broadcasted_iota(jnp.int32, (t, t), 1)
    tri = jnp.where(cc <= r, 1.0, 0.0).astype(BF16)
    c = _sel_dot(tri, ls) + carry_ref[0:1, :]
    carry_ref[0:1, :] = c[t - 1:t, :]
    c_out[0] = jnp.concatenate([c[0:1, :], c[t - 1:t, :], jnp.zeros((SUBLANES - 2, LANES), F32)], axis=0)
    c_hi, c_mid, c_lo = _split3(c * LOG2E)
    c_hi, c_mid, c_lo = c_hi.astype(F32), c_mid.astype(F32), c_lo.astype(F32)

    lane = lax.broadcasted_iota(jnp.int32, (t, LANES), 1)
    low = lane < HEAD_DIM
    j = lane - HEAD_DIM
    shift = shift_ref[...]

    def head_norm(pair, gain, mult):
        sq = pair * pair
        s_lo = jnp.sum(jnp.where(low, sq, 0.0), axis=-1, keepdims=True)
        s_hi = jnp.sum(jnp.where(low, 0.0, sq), axis=-1, keepdims=True)
        ms = jnp.where(low, s_lo, s_hi) * (1.0 / HEAD_DIM)
        return pair * lax.rsqrt(ms + NORM_EPS) * (gain * mult)

    for p in range(FOX_HEADS // 2):
        qp = head_norm(u[:, p * LANES:(p + 1) * LANES], qn_ref[...], HEAD_DIM ** -0.5 * LOG2E)
        kp = head_norm(u[:, FOX_W + p * LANES:FOX_W + (p + 1) * LANES], kn_ref[...], 1.0)
        vp = u[:, 2 * FOX_W + p * LANES:2 * FOX_W + (p + 1) * LANES]
        for sub in range(2):
            hd = 2 * p + sub
            if sub == 1:
                qp, kp, vp = (pltpu.roll(a, HEAD_DIM, axis=1) for a in (qp, kp, vp))
            ch, cm, cl = (jnp.broadcast_to(a[:, hd:hd + 1], (t, LANES)) for a in (c_hi, c_mid, c_lo))
            aug_q = jnp.where(j == 0, ch, jnp.where(j == 1, cm, jnp.where(j == 2, cl,
                              jnp.where(j < 7, 1.0, 0.0))))
            aug_k = jnp.where(j < 3, 1.0, jnp.where(j == 3, -ch, jnp.where(j == 4, -cm,
                              jnp.where(j == 5, -cl, jnp.where(j == 6, -shift, 0.0)))))
            aug_v = jnp.where(j == 0, 1.0, 0.0)
            q_out[hd] = jnp.where(low, qp, aug_q).astype(BF16)
            k_out[hd] = jnp.where(low, kp, aug_k).astype(BF16)
            v_out[hd] = jnp.where(low, vp, aug_v).astype(BF16)


def _fox_prep(l, x, ln, w_fox, bf_row, qn_row, kn_row, shift_row):
    seq = x.shape[0]
    t = FOX_PREP_TILE
    out = jax.ShapeDtypeStruct((FOX_HEADS, seq, LANES), BF16)
    ospec = pl.BlockSpec((FOX_HEADS, t, LANES), lambda i: (0, i, 0))
    return pl.pallas_call(
        _fox_prep_kernel,
        grid=(seq // t,),
        in_specs=[pl.BlockSpec((t, D_MODEL), lambda i: (i, 0)),
                  _layer(ln, l), _layer(w_fox, l), _layer(bf_row, l),
                  _layer(qn_row, l), _layer(kn_row, l), _layer(shift_row, l)],
        out_specs=[ospec, ospec, ospec, pl.BlockSpec((1, SUBLANES, LANES), lambda i: (i, 0, 0))],
        out_shape=[out, out, out, jax.ShapeDtypeStruct((seq // t, SUBLANES, LANES), F32)],
        scratch_shapes=[pltpu.VMEM((SUBLANES, LANES), F32)],
        compiler_params=pltpu.CompilerParams(dimension_semantics=("arbitrary",),
                                             vmem_limit_bytes=VMEM_LIMIT),
        name="fox_prep",
    )(x, ln, w_fox, bf_row, qn_row, kn_row, shift_row)


def _fox_attn_kernel(par_ref, cq0_ref, cend_ref, q_ref, k_ref, v_ref, o_ref):
    hd = pl.program_id(0)
    i = pl.program_id(1)
    q = q_ref[0]
    tq = q.shape[0]
    c0 = cq0_ref[i, hd]
    thr = par_ref[0]
    j_lo = lax.while_loop(
        lambda j: jnp.logical_and(j > 0, c0 - cend_ref[jnp.maximum(j - 1, 0), hd] >= -thr),
        lambda j: j - 1, i)

    def scores(j, masked, nblk=1):
        off = pl.multiple_of(j * FOX_TK, FOX_TK)
        s = _dot_nt(q, k_ref[0, pl.ds(off, nblk * FOX_TK), :])
        if masked:
            r = lax.broadcasted_iota(jnp.int32, s.shape, 0)
            c = lax.broadcasted_iota(jnp.int32, s.shape, 1)
            s = jnp.where(c <= r, s, -jnp.inf)
        return s, v_ref[0, pl.ds(off, nblk * FOX_TK), :]

    def fixed_shift():
        def block(j, masked, nblk):
            s, v = scores(j, masked, nblk)
            return _dot(jnp.exp2(s).astype(BF16), v)
        acc = block(i, True, 1)
        n = i - j_lo
        acc = lax.cond(n % 2 == 1, lambda: acc + block(j_lo, False, 1), lambda: acc)
        acc = lax.cond((n // 2) % 2 == 1, lambda: acc + block(j_lo + n % 2, False, 2), lambda: acc)
        first = j_lo + n % 4
        return lax.fori_loop(0, n // 4, lambda p, a: a + block(first + 4 * p, False, 4), acc)

    def running_max():
        def step(j, carry, masked):
            m, acc = carry
            s, v = scores(j, masked)
            m_new = jnp.maximum(m, jnp.max(s, axis=-1, keepdims=True))
            p = jnp.exp2(s - m_new)
            return m_new, jnp.exp2(m - m_new) * acc + _dot(p.astype(BF16), v)
        init = (jnp.full((tq, 1), -jnp.inf, F32), jnp.zeros((tq, LANES), F32))
        carry = step(i, init, True)
        return lax.fori_loop(j_lo, i, lambda j, c: step(j, c, False), carry)[1]

    acc = lax.cond(par_ref[1] > 0.0, fixed_shift, running_max)
    o_ref[0] = (acc / acc[:, HEAD_DIM:HEAD_DIM + 1]).astype(BF16)


def _fox_attn(par, cq0, cend, q, k, v):
    assert FOX_TQ == FOX_TK
    nh, seq, _ = q.shape
    kv_spec = pl.BlockSpec((1, seq, LANES), lambda h, i, *_: (h, 0, 0))
    return pl.pallas_call(
        _fox_attn_kernel,
        grid_spec=pltpu.PrefetchScalarGridSpec(
            num_scalar_prefetch=3,
            grid=(nh, seq // FOX_TQ),
            in_specs=[pl.BlockSpec((1, FOX_TQ, LANES), lambda h, i, *_: (h, i, 0)), kv_spec, kv_spec],
            out_specs=pl.BlockSpec((1, FOX_TQ, LANES), lambda h, i, *_: (h, i, 0))),
        out_shape=jax.ShapeDtypeStruct((nh, seq, LANES), BF16),
        compiler_params=pltpu.CompilerParams(dimension_semantics=("arbitrary", "arbitrary"),
                                             vmem_limit_bytes=VMEM_LIMIT),
        name="fox_attn",
    )(par, cq0, cend, q, k, v)


def _fox_params(qn, kn):
    bound = 1.02 * HEAD_DIM ** 0.5 * jnp.max(jnp.abs(qn), axis=1) * jnp.max(jnp.abs(kn), axis=1)
    shift = jnp.broadcast_to((LOG2E * bound)[:, None, None], (bound.shape[0], 1, LANES))
    par = jnp.stack([F32_EXP_ZERO + 2.0 * bound, (bound < FOX_FAST_MAX_LOGIT).astype(F32)], axis=1)
    return shift, par


def _fox(l, x, ln, w_fox, bf_row, qn_row, kn_row, shift_row, par):
    q, k, v, c = _fox_prep(l, x, ln, w_fox, bf_row, qn_row, kn_row, shift_row)
    per_tile = FOX_TQ // FOX_PREP_TILE
    cq0 = c[0::per_tile, 0, 0:FOX_HEADS]
    cend = c[per_tile - 1::per_tile, 1, 0:FOX_HEADS]
    return _fox_attn(par[l], cq0, cend, q, k, v)


def _linear_attn_chunks(qd_s, ki_s, ke_s, v_s, dec_s, o_s, st_s, nheads):
    t = qd_s.shape[0]
    r = lax.broadcasted_iota(jnp.int32, (CHUNK, CHUNK), 0)
    c = lax.broadcasted_iota(jnp.int32, (CHUNK, CHUNK), 1)
    tril = c <= r
    states = [st_s[hd] for hd in range(nheads)]
    for ci in range(t // CHUNK):
        rows = slice(ci * CHUNK, (ci + 1) * CHUNK)
        last = slice((ci + 1) * CHUNK - 1, (ci + 1) * CHUNK)
        for hd in range(nheads):
            cols = slice(hd * HEAD_DIM, (hd + 1) * HEAD_DIM)
            qd = qd_s[rows, cols].astype(BF16)
            ki = ki_s[rows, cols].astype(BF16)
            ke = ke_s[rows, cols].astype(BF16)
            vv = v_s[rows, cols].astype(BF16)
            attn = jnp.where(tril, _dot_nt(qd, ki), 0.0).astype(BF16)
            o_s[rows, cols] = _dot(attn, vv) + _dot_nt(qd, states[hd].astype(BF16))
            states[hd] = states[hd] * dec_s[last, cols] + _dot_tn(vv, ke)
    for hd in range(nheads):
        st_s[hd] = states[hd]


def _gla_kernel(x_ref, ln_ref, w_ref, w2_ref, b2_ref, gn_ref, y_ref,
                qd_s, ki_s, ke_s, v_s, dec_s, o_s, st_s):
    t = x_ref.shape[0]

    @pl.when(pl.program_id(0) == 0)
    def _():
        st_s[...] = jnp.zeros_like(st_s)

    h = _rmsnorm(x_ref[...], ln_ref[...]).astype(BF16)
    u = _dot(h, w_ref[...])
    z = _dot(u[:, 4 * GLA_W:4 * GLA_W + LANES].astype(BF16), w2_ref[...]) + b2_ref[...]
    log_a = _log_sigmoid(z) * (1.0 / GLA_TAU)
    b, rem = _chunk_cumsum(log_a)
    eb = jnp.exp(b)
    k = u[:, GLA_W:2 * GLA_W]
    qd_s[...] = u[:, 0:GLA_W] * (HEAD_DIM ** -0.5) * eb
    ki_s[...] = k * jnp.exp(-b)
    ke_s[...] = k * jnp.exp(rem)
    v_s[...] = u[:, 2 * GLA_W:3 * GLA_W]
    dec_s[...] = eb
    _linear_attn_chunks(qd_s, ki_s, ke_s, v_s, dec_s, o_s, st_s, GLA_HEADS)
    o = o_s[...]
    y = o * lax.rsqrt(_group_mean_sq(o, HEAD_DIM) + NORM_EPS) * gn_ref[...]
    y_ref[...] = (y * _silu(u[:, 3 * GLA_W:4 * GLA_W])).astype(BF16)


def _ret_kernel(x_ref, ln_ref, w_ref, cos_ref, sin_ref, gn_ref, y_ref,
                qd_s, ki_s, ke_s, v_s, dec_s, o_s, st_s):
    t = x_ref.shape[0]

    @pl.when(pl.program_id(0) == 0)
    def _():
        st_s[...] = jnp.zeros_like(st_s)

    h = _rmsnorm(x_ref[...], ln_ref[...]).astype(BF16)
    u = _dot(h, w_ref[...])
    cos, sin = cos_ref[...], sin_ref[...]
    lane = lax.broadcasted_iota(jnp.int32, (t, RET_W), 1)
    half = HEAD_DIM // 2
    first_half = (lax.broadcasted_iota(jnp.int32, (t, LANES), 1) % HEAD_DIM) < half

    def rotate(a):
        blocks = []
        for b in range(0, RET_W, LANES):
            ab = a[:, b:b + LANES]
            swapped = jnp.where(first_half, pltpu.roll(ab, LANES - half, axis=1),
                                pltpu.roll(ab, half, axis=1))
            blocks.append(ab * cos + swapped * sin)
        return jnp.concatenate(blocks, axis=1)

    q = rotate(u[:, 0:RET_W])
    k = rotate(u[:, RET_W:2 * RET_W]) * (HEAD_DIM ** -0.5)
    row = lax.broadcasted_iota(jnp.int32, (t, RET_W), 0)
    lg = jnp.zeros((t, RET_W), F32)
    for hd in range(RET_HEADS):
        lg = jnp.where(lane // HEAD_DIM == hd, math.log(1.0 - 2.0 ** (-5.0 - hd)), lg)
    pos = (row % CHUNK).astype(F32)
    qw = jnp.exp((pos + 1.0) * lg)
    qd_s[...] = q * qw
    ki_s[...] = k * jnp.exp(-(pos + 1.0) * lg)
    ke_s[...] = k * jnp.exp((CHUNK - 1.0 - pos) * lg)
    v_s[...] = u[:, 2 * RET_W:3 * RET_W]
    dec_s[...] = qw
    _linear_attn_chunks(qd_s, ki_s, ke_s, v_s, dec_s, o_s, st_s, RET_HEADS)
    o = o_s[...]
    y = o * lax.rsqrt(_group_mean_sq(o, HEAD_DIM) + NORM_EPS) * gn_ref[...]
    y_ref[...] = (y * _silu(u[:, 3 * RET_W:4 * RET_W])).astype(BF16)


def _linear_scratch(t, width, nheads):
    return ([pltpu.VMEM((t, width), F32)] * 6
            + [pltpu.VMEM((nheads, HEAD_DIM, HEAD_DIM), F32)])


def _gla(l, x, ln, w_gla, w2, b2_row, gn_row):
    seq = x.shape[0]
    t = ROW_TILE
    return pl.pallas_call(
        _gla_kernel,
        grid=(seq // t,),
        in_specs=[pl.BlockSpec((t, D_MODEL), lambda i: (i, 0)),
                  _layer(ln, l), _layer(w_gla, l), _layer(w2, l),
                  _layer(b2_row, l), _layer(gn_row, l)],
        out_specs=pl.BlockSpec((t, GLA_W), lambda i: (i, 0)),
        out_shape=jax.ShapeDtypeStruct((seq, GLA_W), BF16),
        scratch_shapes=_linear_scratch(t, GLA_W, GLA_HEADS),
        compiler_params=pltpu.CompilerParams(dimension_semantics=("arbitrary",),
                                             vmem_limit_bytes=VMEM_LIMIT),
        name="gla_mixer",
    )(x, ln, w_gla, w2, b2_row, gn_row)


def _ret(l, x, ln, w_ret, cos_t, sin_t, gn):
    seq = x.shape[0]
    t = ROW_TILE
    return pl.pallas_call(
        _ret_kernel,
        grid=(seq // t,),
        in_specs=[pl.BlockSpec((t, D_MODEL), lambda i: (i, 0)),
                  _layer(ln, l), _layer(w_ret, l),
                  pl.BlockSpec((t, LANES), lambda i: (i, 0)),
                  pl.BlockSpec((t, LANES), lambda i: (i, 0)),
                  _layer(gn, l)],
        out_specs=pl.BlockSpec((t, RET_W), lambda i: (i, 0)),
        out_shape=jax.ShapeDtypeStruct((seq, RET_W), BF16),
        scratch_shapes=_linear_scratch(t, RET_W, RET_HEADS),
        compiler_params=pltpu.CompilerParams(dimension_semantics=("arbitrary",),
                                             vmem_limit_bytes=VMEM_LIMIT),
        name="ret_mixer",
    )(x, ln, w_ret, cos_t, sin_t, gn)


SSD_PROJ = SSD_INNER + SSD_CONV_CH


def _ssd_kernel(x_ref, ln_ref, w_ref, cw_ref, cb_ref, dtb_row_ref,
                alog_row_ref, d_row_ref, gn_ref, y_ref,
                xp_s, xs_s, bc_s, cs_s, ecs_s, xdt_s, xdtw_s, cst_s, o_s, st_s):
    t = x_ref.shape[0]
    nc = t // CHUNK

    @pl.when(pl.program_id(0) == 0)
    def _():
        st_s[...] = jnp.zeros_like(st_s)
        xp_s[0:SUBLANES, :] = jnp.zeros((SUBLANES, SSD_CONV_CH), F32)

    h = _rmsnorm(x_ref[...], ln_ref[...]).astype(BF16)
    u = _dot(h, w_ref[...])
    z = u[:, 0:SSD_INNER]

    xp_s[SUBLANES:SUBLANES + t, :] = u[:, SSD_INNER:SSD_PROJ]
    conv = cb_ref[...] + cw_ref[SSD_CONV - 1:SSD_CONV, :] * xp_s[SUBLANES:SUBLANES + t, :]
    for kk in range(SSD_CONV - 1):
        off = SUBLANES - (SSD_CONV - 1) + kk
        conv = conv + cw_ref[kk:kk + 1, :] * xp_s[off:off + t, :]
    xp_s[0:SUBLANES, :] = xp_s[t:t + SUBLANES, :]
    xbc = _silu(conv)
    xs = xbc[:, 0:SSD_INNER]
    xs_s[...] = xs
    bc_s[...] = xbc[:, SSD_INNER:SSD_CONV_CH]

    dt = _softplus(u[:, SSD_PROJ:SSD_PROJ + LANES] + dtb_row_ref[...])
    dta = dt * -jnp.exp(alog_row_ref[...])
    cs, _ = _chunk_cumsum(dta)
    cs_t = cs.T[0:SSD_HEADS, :]
    for ci in range(nc):
        cst_s[ci] = cs_t[:, ci * CHUNK:(ci + 1) * CHUNK]

    er = lax.broadcasted_iota(jnp.int32, (LANES, SSD_INNER), 0)
    ec = lax.broadcasted_iota(jnp.int32, (LANES, SSD_INNER), 1)
    expand = jnp.where(ec // HEAD_DIM == er, 1.0, 0.0).astype(BF16)
    cs_x = _dot_sel2(cs, expand)
    xdt = xs * _dot_sel2(dt, expand)
    cs_s[...] = cs_x
    ecs_s[...] = jnp.exp(cs_x)
    xdt_s[...] = xdt
    xdtw_s[...] = xdt * jnp.exp(_chunk_last(cs_x) - cs_x)

    r = lax.broadcasted_iota(jnp.int32, (CHUNK, CHUNK), 0)
    c = lax.broadcasted_iota(jnp.int32, (CHUNK, CHUNK), 1)
    tril = c <= r
    gw = SSD_GROUPS * SSD_STATE

    hpg = SSD_HEADS // SSD_GROUPS
    gcols = hpg * HEAD_DIM
    states = [st_s[g] for g in range(SSD_GROUPS)]
    for ci in range(nc):
        rows = slice(ci * CHUNK, (ci + 1) * CHUNK)
        last = slice((ci + 1) * CHUNK - 1, (ci + 1) * CHUNK)
        cst = cst_s[ci]
        for g in range(SSD_GROUPS):
            gsl = slice(g * gcols, (g + 1) * gcols)
            bm = bc_s[rows, g * SSD_STATE:(g + 1) * SSD_STATE].astype(BF16)
            cm = bc_s[rows, gw + g * SSD_STATE:gw + (g + 1) * SSD_STATE].astype(BF16)
            cb = _dot_nt(cm, bm)
            y_inter = _dot(cm, states[g].astype(BF16)) * ecs_s[rows, gsl]
            for rr in range(hpg):
                hd = g * hpg + rr
                cols = slice(hd * HEAD_DIM, (hd + 1) * HEAD_DIM)
                seg = cs_s[rows, cols] - cst[hd:hd + 1, :]
                decay = jnp.exp(jnp.where(tril, seg, -jnp.inf))
                y = _dot((cb * decay).astype(BF16), xdt_s[rows, cols].astype(BF16))
                o_s[rows, cols] = y + y_inter[:, rr * HEAD_DIM:(rr + 1) * HEAD_DIM]
            states[g] = (states[g] * ecs_s[last, gsl]
                         + _dot_tn(bm, xdtw_s[rows, gsl].astype(BF16)))
    for g in range(SSD_GROUPS):
        st_s[g] = states[g]

    y = (o_s[...] + d_row_ref[...] * xs_s[...]) * _silu(z)
    gwid = SSD_INNER // SSD_GROUPS
    normed = []
    for g in range(SSD_GROUPS):
        yg = y[:, g * gwid:(g + 1) * gwid]
        normed.append(yg * lax.rsqrt(jnp.mean(yg * yg, axis=-1, keepdims=True) + NORM_EPS))
    y_ref[...] = (jnp.concatenate(normed, axis=1) * gn_ref[...]).astype(BF16)


def _ssd(l, x, ln, w_ssd, conv_w, conv_b_row, dtb_row, alog_row, d_row, gn_row):
    seq = x.shape[0]
    t = ROW_TILE
    wide = pltpu.VMEM((t, SSD_INNER), F32)
    return pl.pallas_call(
        _ssd_kernel,
        grid=(seq // t,),
        in_specs=[pl.BlockSpec((t, D_MODEL), lambda i: (i, 0)),
                  _layer(ln, l), _layer(w_ssd, l),
                  _layer(conv_w, l), _layer(conv_b_row, l),
                  _layer(dtb_row, l), _layer(alog_row, l),
                  _layer(d_row, l), _layer(gn_row, l)],
        out_specs=pl.BlockSpec((t, SSD_INNER), lambda i: (i, 0)),
        out_shape=jax.ShapeDtypeStruct((seq, SSD_INNER), BF16),
        scratch_shapes=[pltpu.VMEM((t + 2 * SUBLANES, SSD_CONV_CH), F32),
                        wide,
                        pltpu.VMEM((t, 2 * SSD_GROUPS * SSD_STATE), F32),
                        wide, wide, wide, wide,
                        pltpu.VMEM((t // CHUNK, SSD_HEADS, CHUNK), F32),
                        wide,
                        pltpu.VMEM((SSD_GROUPS, SSD_STATE, SSD_INNER // SSD_GROUPS), F32)],
        compiler_params=pltpu.CompilerParams(dimension_semantics=("arbitrary",),
                                             vmem_limit_bytes=VMEM_LIMIT),
        name="ssd_mixer",
    )(x, ln, w_ssd, conv_w, conv_b_row, dtb_row, alog_row, d_row, gn_row)


def _merge_ffn_kernel(x_ref, ln1_ref, wg_ref, ya_ref, yb_ref, yc_ref, yd_ref,
                      wa_ref, wb_ref, wc_ref, wd_ref, wo_ref, ln2_ref, w1_ref, w2_ref, out_ref):
    x = x_ref[...]
    h = _rmsnorm(x, ln1_ref[...]).astype(BF16)

    def gate(b):
        return _sigmoid(_dot(h, wg_ref[:, b * D_MODEL:(b + 1) * D_MODEL]))

    low = lax.broadcasted_iota(jnp.int32, (x.shape[0], LANES), 1) < HEAD_DIM
    up_a = None
    for p in range(FOX_HEADS // 2):
        pair = jnp.where(low, ya_ref[2 * p].astype(F32),
                         pltpu.roll(ya_ref[2 * p + 1].astype(F32), HEAD_DIM, axis=1)).astype(BF16)
        part = _dot(pair, wa_ref[p * LANES:(p + 1) * LANES, :])
        up_a = part if up_a is None else up_a + part
    merged = gate(0) * up_a
    merged = merged + gate(1) * _dot(yb_ref[...], wb_ref[...])
    merged = merged + gate(2) * _dot(yc_ref[...], wc_ref[...])
    merged = merged + gate(3) * _dot(yd_ref[...], wd_ref[...])
    x = x + _dot(merged.astype(BF16), wo_ref[...])

    h = _rmsnorm(x, ln2_ref[...]).astype(BF16)
    act = _silu(_dot(h, w1_ref[:, 0:FFN_HIDDEN])) * _dot(h, w1_ref[:, FFN_HIDDEN:2 * FFN_HIDDEN])
    out_ref[...] = x + _dot(act.astype(BF16), w2_ref[...])


def _merge_ffn(l, x, ln1, w_gates, y_a, y_b, y_c, y_d, w_up_a, w_up_b, w_up_c, w_up_d, w_out, ln2, w1, w2):
    seq = x.shape[0]
    t = FFN_TILE
    row = lambda w: pl.BlockSpec((t, w), lambda i: (i, 0))
    return pl.pallas_call(
        _merge_ffn_kernel,
        grid=(seq // t,),
        in_specs=[row(D_MODEL), _layer(ln1, l), _layer(w_gates, l),
                  pl.BlockSpec((FOX_HEADS, t, LANES), lambda i: (0, i, 0)),
                  row(GLA_W), row(RET_W), row(SSD_INNER),
                  _layer(w_up_a, l), _layer(w_up_b, l), _layer(w_up_c, l),
                  _layer(w_up_d, l), _layer(w_out, l), _layer(ln2, l), _layer(w1, l), _layer(w2, l)],
        out_specs=row(D_MODEL),
        out_shape=jax.ShapeDtypeStruct((seq, D_MODEL), F32),
        compiler_params=pltpu.CompilerParams(dimension_semantics=("arbitrary",),
                                             vmem_limit_bytes=VMEM_LIMIT),
        name="merge_swiglu",
    )(x, ln1, w_gates, y_a, y_b, y_c, y_d, w_up_a, w_up_b, w_up_c, w_up_d, w_out, ln2, w1, w2)


def _pad_last(a, width):
    return jnp.pad(a, [(0, 0)] * (a.ndim - 1) + [(0, width - a.shape[-1])])


def _rows(a):
    return a.reshape(a.shape[0], 1, -1)


def _in_proj_weights(w_in):
    offs = np.concatenate([[0], np.cumsum(IN_SPLITS)])
    (fq, fk, fv, ff, gq, gk, gv, glr, gr, rq, rk, rv, rg, z, xbc, dt, gates) = [
        w_in[:, offs[n]:offs[n + 1]] for n in range(len(IN_SPLITS))]
    cat = lambda parts: jnp.concatenate(parts, axis=1)
    w_fox = cat([fq, fk, fv, _pad_last(ff, LANES)])
    w_gla = cat([gq, gk, gv, gr, _pad_last(glr, LANES)])
    w_ret = cat([rq, rk, rv, rg])
    w_ssd = cat([z, xbc, _pad_last(dt, LANES)])
    return w_fox, w_gla, w_ret, w_ssd, gates


def kernel(x, positions, ln1, ln2, w_in, fox_bf, fox_qn, fox_kn, gla_w2, gla_b, gla_norm, ret_norm,
           ssd_conv_w, ssd_conv_b, ssd_dt_bias, ssd_a_log, ssd_d, ssd_norm,
           w_up_a, w_up_b, w_up_c, w_up_d, w_out, w_ffn_in, w_ffn_out):
    bsz, seq, d = x.shape
    assert bsz == 1 and d == D_MODEL and seq % ROW_TILE == 0 and seq % FOX_TQ == 0
    depth = ln1.shape[0]
    xr = x.reshape(seq, d)
    cos_t, sin_t = _rotary_tables(positions, seq)

    bf16 = lambda w: w.astype(BF16)
    ln1_r, ln2_r = _rows(ln1), _rows(ln2)
    fox_bf_r = _rows(_pad_last(fox_bf, LANES))
    fox_qn_r, fox_kn_r = _rows(jnp.tile(fox_qn, (1, 2))), _rows(jnp.tile(fox_kn, (1, 2)))
    fox_shift_r, fox_par = _fox_params(fox_qn, fox_kn)
    gla_w2_p = bf16(jnp.pad(gla_w2, ((0, 0), (0, LANES - GLA_LOWRANK), (0, 0))))
    gla_b_r, gla_gn_r = _rows(gla_b), _rows(jnp.tile(gla_norm, (1, GLA_HEADS)))
    ret_gn_r = _rows(jnp.tile(ret_norm, (1, RET_HEADS)))
    conv_b_r = _rows(ssd_conv_b)
    dtb_r, alog_r = _rows(_pad_last(ssd_dt_bias, LANES)), _rows(_pad_last(ssd_a_log, LANES))
    d_r, ssd_gn_r = _rows(jnp.repeat(ssd_d, HEAD_DIM, axis=1)), _rows(ssd_norm)
    up_a, up_b, up_c, up_d, w_o = bf16(w_up_a), bf16(w_up_b), bf16(w_up_c), bf16(w_up_d), bf16(w_out)
    w1, w2 = bf16(w_ffn_in), bf16(w_ffn_out)

    for l in range(depth):
        w_fox, w_gla, w_ret, w_ssd, w_gates = _in_proj_weights(lax.optimization_barrier(bf16(w_in[l])))
        y_a = _fox(l, xr, ln1_r, w_fox, fox_bf_r, fox_qn_r, fox_kn_r, fox_shift_r, fox_par)
        y_b = _gla(l, xr, ln1_r, w_gla, gla_w2_p, gla_b_r, gla_gn_r)
        y_c = _ret(l, xr, ln1_r, w_ret, cos_t, sin_t, ret_gn_r)
        y_d = _ssd(l, xr, ln1_r, w_ssd, ssd_conv_w, conv_b_r, dtb_r, alog_r, d_r, ssd_gn_r)
        xr = _merge_ffn(l, xr, ln1_r, w_gates, y_a, y_b, y_c, y_d, up_a, up_b, up_c, up_d, w_o, ln2_r, w1, w2)
    return xr.reshape(bsz, seq, d)
```

```python
import math

import numpy as np
import jax
import jax.numpy as jnp
from jax import lax
from jax.experimental import pallas as pl
from jax.experimental.pallas import tpu as pltpu

D_MODEL = 1024
HEAD_DIM = 64
CHUNK = 64
FOX_HEADS = 4
GLA_HEADS = 4
GLA_LOWRANK = 16
GLA_TAU = 16.0
RET_HEADS = 4
ROPE_THETA = 10000.0
SSD_HEADS = 8
SSD_GROUPS = 2
SSD_STATE = 64
SSD_CONV = 4
SSD_INNER = SSD_HEADS * HEAD_DIM
SSD_CONV_CH = SSD_INNER + 2 * SSD_GROUPS * SSD_STATE
FOX_W = FOX_HEADS * HEAD_DIM
GLA_W = GLA_HEADS * HEAD_DIM
RET_W = RET_HEADS * HEAD_DIM
N_BRANCH = 4
FFN_HIDDEN = ((8 * D_MODEL + 3 * 256 - 1) // (3 * 256)) * 256
NORM_EPS = 1e-6
IN_SPLITS = (FOX_W, FOX_W, FOX_W, FOX_HEADS,
             GLA_W, GLA_W, GLA_W, GLA_LOWRANK, GLA_W,
             RET_W, RET_W, RET_W, RET_W,
             SSD_INNER, SSD_CONV_CH, SSD_HEADS,
             N_BRANCH * D_MODEL)

LANES = 128
SUBLANES = 8
VMEM_LIMIT = 56 * 1024 * 1024

ROW_TILE = 512
FOX_TQ = 512
FOX_TK = 512
FFN_TILE = 512
PROJ_PIECE = 512

F32 = jnp.float32
BF16 = jnp.bfloat16


def _rmsnorm(x, g):
    return x * lax.rsqrt(jnp.mean(x * x, axis=-1, keepdims=True) + NORM_EPS) * g


def _log_sigmoid(x):
    return jnp.minimum(x, 0.0) - jnp.log(1.0 + jnp.exp(-jnp.abs(x)))


def _softplus(x):
    return jnp.maximum(x, 0.0) + jnp.log(1.0 + jnp.exp(-jnp.abs(x)))


def _sigmoid(x):
    return 0.5 * jnp.tanh(0.5 * x) + 0.5


def _silu(x):
    return x * _sigmoid(x)


def _split3(x):
    hi = x.astype(BF16)
    r1 = x - hi.astype(F32)
    mid = r1.astype(BF16)
    lo = (r1 - mid.astype(F32)).astype(BF16)
    return hi, mid, lo


def _split2(x):
    hi = x.astype(BF16)
    return hi, (x - hi.astype(F32)).astype(BF16)


def _dot(a, b):
    return jnp.dot(a, b, preferred_element_type=F32)


def _dot_nt(a, b):
    return lax.dot_general(a, b, (((1,), (1,)), ((), ())), preferred_element_type=F32)


def _dot_tn(a, b):
    return lax.dot_general(a, b, (((0,), (0,)), ((), ())), preferred_element_type=F32)


def _sel_dot(mat, x):
    hi, mid, lo = _split3(x)
    return _dot(mat, hi) + _dot(mat, mid) + _dot(mat, lo)


def _dot_sel2(x, mat):
    hi, lo = _split2(x)
    return _dot(hi, mat) + _dot(lo, mat)


def _chunk_cumsum(x):
    t = x.shape[0]
    r = lax.broadcasted_iota(jnp.int32, (t, t), 0)
    c = lax.broadcasted_iota(jnp.int32, (t, t), 1)
    incl = jnp.where(((r // CHUNK) == (c // CHUNK)) & (c <= r), 1.0, 0.0).astype(BF16)
    hi, lo = _split2(x)
    b = _dot(incl, hi) + _dot(incl, lo)
    return b, _chunk_last(b) - b


def _chunk_last(b):
    t, w = b.shape
    return jnp.concatenate([jnp.broadcast_to(b[e - 1:e, :], (CHUNK, w)) for e in range(CHUNK, t + 1, CHUNK)],
                           axis=0)


def _group_mean_sq(o, group):
    w = o.shape[-1]
    r = lax.broadcasted_iota(jnp.int32, (w, w), 0)
    c = lax.broadcasted_iota(jnp.int32, (w, w), 1)
    bd = jnp.where((r // group) == (c // group), 1.0, 0.0).astype(BF16)
    return _dot_sel2(o * o, bd) * (1.0 / group)


def _project(h, wt_ref):
    n = wt_ref.shape[0]
    parts = []
    for a in range(0, n, PROJ_PIECE):
        parts.append(_dot_nt(h, wt_ref[a:min(a + PROJ_PIECE, n), :]))
        yield
    return jnp.concatenate(parts, axis=1)


def _layer(stacked, l):
    tail = tuple(stacked.shape[1:])
    return pl.BlockSpec((None,) + tail, lambda *_: (l,) + (0,) * len(tail), pipeline_mode=pl.Buffered(1))


def _rotary_kernel(pos_ref, inv_ref, sign_ref, cos_ref, sin_ref):
    ang = pos_ref[...] * inv_ref[...]
    cos_ref[...] = jnp.cos(ang)
    sin_ref[...] = jnp.sin(ang) * sign_ref[...]


def _rotary_tables(positions, seq):
    half = HEAD_DIM // 2
    inv = ROPE_THETA ** (-jnp.arange(half, dtype=F32) / half)
    reps = LANES // HEAD_DIM
    inv_row = jnp.tile(inv, 2 * reps).reshape(1, LANES)
    sign_row = jnp.tile(jnp.concatenate([-jnp.ones((half,), F32), jnp.ones((half,), F32)]),
                        reps).reshape(1, LANES)
    pos = positions.astype(F32).reshape(seq, 1)
    t = ROW_TILE
    return pl.pallas_call(
        _rotary_kernel,
        grid=(seq // t,),
        in_specs=[pl.BlockSpec((t, 1), lambda i: (i, 0)),
                  pl.BlockSpec((1, LANES), lambda i: (0, 0)),
                  pl.BlockSpec((1, LANES), lambda i: (0, 0))],
        out_specs=[pl.BlockSpec((t, LANES), lambda i: (i, 0)),
                   pl.BlockSpec((t, LANES), lambda i: (i, 0))],
        out_shape=[jax.ShapeDtypeStruct((seq, LANES), F32)] * 2,
        name="rotary_tables",
    )(pos, inv_row, sign_row)


FOX_AUG = FOX_W * 3
LOG2E = math.log2(math.e)
F32_EXP_ZERO = 105.0
FOX_FAST_MAX_LOGIT = 40.0


def _fox_prep_body(h, w_ref, bf_ref, qn_ref, kn_ref, shift_ref, q_out, k_out, v_out, c_out, carry_ref):
    t = h.shape[0]
    u = yield from _project(h, w_ref)
    ls = _log_sigmoid(u[:, FOX_AUG:FOX_AUG + LANES] + bf_ref[...])
    r = lax.broadcasted_iota(jnp.int32, (t, t), 0)
    cc = lax.broadcasted_iota(jnp.int32, (t, t), 1)
    tri = jnp.where(cc <= r, 1.0, 0.0).astype(BF16)
    c = _sel_dot(tri, ls) + carry_ref[0:1, :]
    carry_ref[0:1, :] = c[t - 1:t, :]
    c_out[0] = jnp.concatenate([c[0:1, :], c[t - 1:t, :], jnp.zeros((SUBLANES - 2, LANES), F32)], axis=0)
    c_hi, c_mid, c_lo = _split3(c * LOG2E)
    c_hi, c_mid, c_lo = c_hi.astype(F32), c_mid.astype(F32), c_lo.astype(F32)

    lane = lax.broadcasted_iota(jnp.int32, (t, LANES), 1)
    low = lane < HEAD_DIM
    j = lane - HEAD_DIM
    shift = shift_ref[...]

    def head_norm(pair, gain, mult):
        sq = pair * pair
        s_lo = jnp.sum(jnp.where(low, sq, 0.0), axis=-1, keepdims=True)
        s_hi = jnp.sum(jnp.where(low, 0.0, sq), axis=-1, keepdims=True)
        ms = jnp.where(low, s_lo, s_hi) * (1.0 / HEAD_DIM)
        return pair * lax.rsqrt(ms + NORM_EPS) * (gain * mult)

    for p in range(FOX_HEADS // 2):
        qp = head_norm(u[:, p * LANES:(p + 1) * LANES], qn_ref[...], HEAD_DIM ** -0.5 * LOG2E)
        kp = head_norm(u[:, FOX_W + p * LANES:FOX_W + (p + 1) * LANES], kn_ref[...], 1.0)
        vp = u[:, 2 * FOX_W + p * LANES:2 * FOX_W + (p + 1) * LANES]
        for sub in range(2):
            hd = 2 * p + sub
            if sub == 1:
                qp, kp, vp = (pltpu.roll(a, HEAD_DIM, axis=1) for a in (qp, kp, vp))
            ch, cm, cl = (jnp.broadcast_to(a[:, hd:hd + 1], (t, LANES)) for a in (c_hi, c_mid, c_lo))
            aug_q = jnp.where(j == 0, ch, jnp.where(j == 1, cm, jnp.where(j == 2, cl,
                              jnp.where(j < 7, 1.0, 0.0))))
            aug_k = jnp.where(j < 3, 1.0, jnp.where(j == 3, -ch, jnp.where(j == 4, -cm,
                              jnp.where(j == 5, -cl, jnp.where(j == 6, -shift, 0.0)))))
            aug_v = jnp.where(j == 0, 1.0, 0.0)
            q_out[hd] = jnp.where(low, qp, aug_q).astype(BF16)
            k_out[hd] = jnp.where(low, kp, aug_k).astype(BF16)
            v_out[hd] = jnp.where(low, vp, aug_v).astype(BF16)


def _fox_attn_kernel(par_ref, cq0_ref, cend_ref, q_ref, k_ref, v_ref, o_ref):
    hd = pl.program_id(0)
    i = pl.program_id(1)
    q = q_ref[0]
    tq = q.shape[0]
    c0 = cq0_ref[i, hd]
    thr = par_ref[0]
    j_lo = lax.while_loop(
        lambda j: jnp.logical_and(j > 0, c0 - cend_ref[jnp.maximum(j - 1, 0), hd] >= -thr),
        lambda j: j - 1, i)

    def scores(j, masked, nblk=1):
        off = pl.multiple_of(j * FOX_TK, FOX_TK)
        s = _dot_nt(q, k_ref[0, pl.ds(off, nblk * FOX_TK), :])
        if masked:
            r = lax.broadcasted_iota(jnp.int32, s.shape, 0)
            c = lax.broadcasted_iota(jnp.int32, s.shape, 1)
            s = jnp.where(c <= r, s, -jnp.inf)
        return s, v_ref[0, pl.ds(off, nblk * FOX_TK), :]

    def fixed_shift():
        def block(j, masked, nblk):
            s, v = scores(j, masked, nblk)
            return _dot(jnp.exp2(s).astype(BF16), v)
        acc = block(i, True, 1)
        n = i - j_lo
        acc = lax.cond(n % 2 == 1, lambda: acc + block(j_lo, False, 1), lambda: acc)
        acc = lax.cond((n // 2) % 2 == 1, lambda: acc + block(j_lo + n % 2, False, 2), lambda: acc)
        first = j_lo + n % 4
        return lax.fori_loop(0, n // 4, lambda p, a: a + block(first + 4 * p, False, 4), acc)

    def running_max():
        def step(j, carry, masked):
            m, acc = carry
            s, v = scores(j, masked)
            m_new = jnp.maximum(m, jnp.max(s, axis=-1, keepdims=True))
            p = jnp.exp2(s - m_new)
            return m_new, jnp.exp2(m - m_new) * acc + _dot(p.astype(BF16), v)
        init = (jnp.full((tq, 1), -jnp.inf, F32), jnp.zeros((tq, LANES), F32))
        carry = step(i, init, True)
        return lax.fori_loop(j_lo, i, lambda j, c: step(j, c, False), carry)[1]

    acc = lax.cond(par_ref[1] > 0.0, fixed_shift, running_max)
    o_ref[0] = (acc / acc[:, HEAD_DIM:HEAD_DIM + 1]).astype(BF16)


def _fox_attn(par, cq0, cend, q, k, v):
    assert FOX_TQ == FOX_TK
    nh, seq, _ = q.shape
    kv_spec = pl.BlockSpec((1, seq, LANES), lambda h, i, *_: (h, 0, 0))
    return pl.pallas_call(
        _fox_attn_kernel,
        grid_spec=pltpu.PrefetchScalarGridSpec(
            num_scalar_prefetch=3,
            grid=(nh, seq // FOX_TQ),
            in_specs=[pl.BlockSpec((1, FOX_TQ, LANES), lambda h, i, *_: (h, i, 0)), kv_spec, kv_spec],
            out_specs=pl.BlockSpec((1, FOX_TQ, LANES), lambda h, i, *_: (h, i, 0))),
        out_shape=jax.ShapeDtypeStruct((nh, seq, LANES), BF16),
        compiler_params=pltpu.CompilerParams(dimension_semantics=("arbitrary", "arbitrary"),
                                             vmem_limit_bytes=VMEM_LIMIT),
        name="fox_attn",
    )(par, cq0, cend, q, k, v)


def _fox_params(qn, kn):
    bound = 1.02 * HEAD_DIM ** 0.5 * jnp.max(jnp.abs(qn), axis=1) * jnp.max(jnp.abs(kn), axis=1)
    shift = jnp.broadcast_to((LOG2E * bound)[:, None, None], (bound.shape[0], 1, LANES))
    par = jnp.stack([F32_EXP_ZERO + 2.0 * bound, (bound < FOX_FAST_MAX_LOGIT).astype(F32)], axis=1)
    return shift, par


def _fox_attention(par, q, k, v, c):
    per_tile = FOX_TQ // ROW_TILE
    cq0 = c[0::per_tile, 0, 0:FOX_HEADS]
    cend = c[per_tile - 1::per_tile, 1, 0:FOX_HEADS]
    return _fox_attn(par, cq0, cend, q, k, v)


def _linear_attn_chunks(qd_s, ki_s, ke_s, v_s, dec_s, o_s, st_s, nheads):
    t = qd_s.shape[0]
    r = lax.broadcasted_iota(jnp.int32, (CHUNK, CHUNK), 0)
    c = lax.broadcasted_iota(jnp.int32, (CHUNK, CHUNK), 1)
    tril = c <= r
    states = [st_s[hd] for hd in range(nheads)]
    for ci in range(t // CHUNK):
        rows = slice(ci * CHUNK, (ci + 1) * CHUNK)
        last = slice((ci + 1) * CHUNK - 1, (ci + 1) * CHUNK)
        for hd in range(nheads):
            cols = slice(hd * HEAD_DIM, (hd + 1) * HEAD_DIM)
            qd = qd_s[rows, cols].astype(BF16)
            ki = ki_s[rows, cols].astype(BF16)
            ke = ke_s[rows, cols].astype(BF16)
            vv = v_s[rows, cols].astype(BF16)
            attn = jnp.where(tril, _dot_nt(qd, ki), 0.0).astype(BF16)
            o_s[rows, cols] = _dot(attn, vv) + _dot_nt(qd, states[hd].astype(BF16))
            states[hd] = states[hd] * dec_s[last, cols] + _dot_tn(vv, ke)
        yield
    for hd in range(nheads):
        st_s[hd] = states[hd]


def _gla_body(h, w_ref, w2_ref, b2_ref, gn_ref, y_ref, qd_s, ki_s, ke_s, v_s, dec_s, o_s, st_s):
    u = yield from _project(h, w_ref)
    z = _dot(u[:, 4 * GLA_W:4 * GLA_W + LANES].astype(BF16), w2_ref[...]) + b2_ref[...]
    log_a = _log_sigmoid(z) * (1.0 / GLA_TAU)
    b, rem = _chunk_cumsum(log_a)
    eb = jnp.exp(b)
    k = u[:, GLA_W:2 * GLA_W]
    qd_s[...] = u[:, 0:GLA_W] * (HEAD_DIM ** -0.5) * eb
    ki_s[...] = k * jnp.exp(-b)
    ke_s[...] = k * jnp.exp(rem)
    v_s[...] = u[:, 2 * GLA_W:3 * GLA_W]
    dec_s[...] = eb
    yield
    yield from _linear_attn_chunks(qd_s, ki_s, ke_s, v_s, dec_s, o_s, st_s, GLA_HEADS)
    o = o_s[...]
    y = o * lax.rsqrt(_group_mean_sq(o, HEAD_DIM) + NORM_EPS) * gn_ref[...]
    y_ref[...] = (y * _silu(u[:, 3 * GLA_W:4 * GLA_W])).astype(BF16)


def _ret_body(h, w_ref, cos_ref, sin_ref, gn_ref, y_ref, qd_s, ki_s, ke_s, v_s, dec_s, o_s, st_s):
    t = h.shape[0]
    u = yield from _project(h, w_ref)
    cos, sin = cos_ref[...], sin_ref[...]
    lane = lax.broadcasted_iota(jnp.int32, (t, RET_W), 1)
    half = HEAD_DIM // 2
    first_half = (lax.broadcasted_iota(jnp.int32, (t, LANES), 1) % HEAD_DIM) < half

    def rotate(a):
        blocks = []
        for b in range(0, RET_W, LANES):
            ab = a[:, b:b + LANES]
            swapped = jnp.where(first_half, pltpu.roll(ab, LANES - half, axis=1),
                                pltpu.roll(ab, half, axis=1))
            blocks.append(ab * cos + swapped * sin)
        return jnp.concatenate(blocks, axis=1)

    q = rotate(u[:, 0:RET_W])
    k = rotate(u[:, RET_W:2 * RET_W]) * (HEAD_DIM ** -0.5)
    row = lax.broadcasted_iota(jnp.int32, (t, RET_W), 0)
    lg = jnp.zeros((t, RET_W), F32)
    for hd in range(RET_HEADS):
        lg = jnp.where(lane // HEAD_DIM == hd, math.log(1.0 - 2.0 ** (-5.0 - hd)), lg)
    pos = (row % CHUNK).astype(F32)
    qw = jnp.exp((pos + 1.0) * lg)
    qd_s[...] = q * qw
    ki_s[...] = k * jnp.exp(-(pos + 1.0) * lg)
    ke_s[...] = k * jnp.exp((CHUNK - 1.0 - pos) * lg)
    v_s[...] = u[:, 2 * RET_W:3 * RET_W]
    dec_s[...] = qw
    yield
    yield from _linear_attn_chunks(qd_s, ki_s, ke_s, v_s, dec_s, o_s, st_s, RET_HEADS)
    o = o_s[...]
    y = o * lax.rsqrt(_group_mean_sq(o, HEAD_DIM) + NORM_EPS) * gn_ref[...]
    y_ref[...] = (y * _silu(u[:, 3 * RET_W:4 * RET_W])).astype(BF16)


def _linear_scratch(t, width, nheads):
    return ([pltpu.VMEM((t, width), F32)] * 6
            + [pltpu.VMEM((nheads, HEAD_DIM, HEAD_DIM), F32)])


SSD_PROJ = SSD_INNER + SSD_CONV_CH


def _ssd_body(h, w_ref, cw_ref, cb_ref, dtb_row_ref, alog_row_ref, d_row_ref, gn_ref, y_ref,
              xp_s, xs_s, bc_s, cs_s, ecs_s, xdt_s, xdtw_s, cst_s, o_s, st_s):
    t = h.shape[0]
    nc = t // CHUNK
    u = yield from _project(h, w_ref)
    z = u[:, 0:SSD_INNER]

    xp_s[SUBLANES:SUBLANES + t, :] = u[:, SSD_INNER:SSD_PROJ]
    conv = cb_ref[...] + cw_ref[SSD_CONV - 1:SSD_CONV, :] * xp_s[SUBLANES:SUBLANES + t, :]
    for kk in range(SSD_CONV - 1):
        off = SUBLANES - (SSD_CONV - 1) + kk
        conv = conv + cw_ref[kk:kk + 1, :] * xp_s[off:off + t, :]
    xp_s[0:SUBLANES, :] = xp_s[t:t + SUBLANES, :]
    xbc = _silu(conv)
    xs = xbc[:, 0:SSD_INNER]
    xs_s[...] = xs
    bc_s[...] = xbc[:, SSD_INNER:SSD_CONV_CH]

    dt = _softplus(u[:, SSD_PROJ:SSD_PROJ + LANES] + dtb_row_ref[...])
    dta = dt * -jnp.exp(alog_row_ref[...])
    cs, _ = _chunk_cumsum(dta)
    cs_t = cs.T[0:SSD_HEADS, :]
    for ci in range(nc):
        cst_s[ci] = cs_t[:, ci * CHUNK:(ci + 1) * CHUNK]

    er = lax.broadcasted_iota(jnp.int32, (LANES, SSD_INNER), 0)
    ec = lax.broadcasted_iota(jnp.int32, (LANES, SSD_INNER), 1)
    expand = jnp.where(ec // HEAD_DIM == er, 1.0, 0.0).astype(BF16)
    cs_x = _dot_sel2(cs, expand)
    xdt = xs * _dot_sel2(dt, expand)
    cs_s[...] = cs_x
    ecs_s[...] = jnp.exp(cs_x)
    xdt_s[...] = xdt
    xdtw_s[...] = xdt * jnp.exp(_chunk_last(cs_x) - cs_x)

    r = lax.broadcasted_iota(jnp.int32, (CHUNK, CHUNK), 0)
    c = lax.broadcasted_iota(jnp.int32, (CHUNK, CHUNK), 1)
    tril = c <= r
    gw = SSD_GROUPS * SSD_STATE

    hpg = SSD_HEADS // SSD_GROUPS
    gcols = hpg * HEAD_DIM
    states = [st_s[g] for g in range(SSD_GROUPS)]
    for ci in range(nc):
        rows = slice(ci * CHUNK, (ci + 1) * CHUNK)
        last = slice((ci + 1) * CHUNK - 1, (ci + 1) * CHUNK)
        cst = cst_s[ci]
        for g in range(SSD_GROUPS):
            gsl = slice(g * gcols, (g + 1) * gcols)
            bm = bc_s[rows, g * SSD_STATE:(g + 1) * SSD_STATE].astype(BF16)
            cm = bc_s[rows, gw + g * SSD_STATE:gw + (g + 1) * SSD_STATE].astype(BF16)
            cb = _dot_nt(cm, bm)
            y_inter = _dot(cm, states[g].astype(BF16)) * ecs_s[rows, gsl]
            for rr in range(hpg):
                hd = g * hpg + rr
                cols = slice(hd * HEAD_DIM, (hd + 1) * HEAD_DIM)
                seg = cs_s[rows, cols] - cst[hd:hd + 1, :]
                decay = jnp.exp(jnp.where(tril, seg, -jnp.inf))
                y = _dot((cb * decay).astype(BF16), xdt_s[rows, cols].astype(BF16))
                o_s[rows, cols] = y + y_inter[:, rr * HEAD_DIM:(rr + 1) * HEAD_DIM]
            states[g] = (states[g] * ecs_s[last, gsl]
                         + _dot_tn(bm, xdtw_s[rows, gsl].astype(BF16)))
        yield
    for g in range(SSD_GROUPS):
        st_s[g] = states[g]

    y = (o_s[...] + d_row_ref[...] * xs_s[...]) * _silu(z)
    gwid = SSD_INNER // SSD_GROUPS
    normed = []
    for g in range(SSD_GROUPS):
        yg = y[:, g * gwid:(g + 1) * gwid]
        normed.append(yg * lax.rsqrt(jnp.mean(yg * yg, axis=-1, keepdims=True) + NORM_EPS))
    y_ref[...] = (jnp.concatenate(normed, axis=1) * gn_ref[...]).astype(BF16)


N_FOX_IN, N_GLA_IN, N_RET_IN, N_SSD_IN = 5, 4, 4, 7
N_LINEAR_SCRATCH = 7
N_SSD_SCRATCH = 10


def _mixers_kernel(x_ref, ln_ref, *refs):
    it = iter(refs)
    take = lambda n: [next(it) for _ in range(n)]
    fox_in, gla_in, ret_in, ssd_in = take(N_FOX_IN), take(N_GLA_IN), take(N_RET_IN), take(N_SSD_IN)
    fox_out, (yb_ref, yc_ref, yd_ref) = take(4), take(3)
    (carry_ref,), gla_s, ret_s, ssd_s = take(1), take(N_LINEAR_SCRATCH), take(N_LINEAR_SCRATCH), take(N_SSD_SCRATCH)

    @pl.when(pl.program_id(0) == 0)
    def _():
        carry_ref[...] = jnp.zeros_like(carry_ref)
        gla_s[-1][...] = jnp.zeros_like(gla_s[-1])
        ret_s[-1][...] = jnp.zeros_like(ret_s[-1])
        ssd_s[-1][...] = jnp.zeros_like(ssd_s[-1])
        ssd_s[0][0:SUBLANES, :] = jnp.zeros((SUBLANES, SSD_CONV_CH), F32)

    h = _rmsnorm(x_ref[...], ln_ref[...]).astype(BF16)
    waiting = [_ssd_body(h, *ssd_in, yd_ref, *ssd_s), _gla_body(h, *gla_in, yb_ref, *gla_s),
               _ret_body(h, *ret_in, yc_ref, *ret_s), _fox_prep_body(h, *fox_in, *fox_out, carry_ref)]
    pending = []
    while waiting or pending:
        if waiting:
            pending.append(waiting.pop(0))
        for body in list(pending):
            if next(body, "done") == "done":
                pending.remove(body)


def _mixers(l, x, ln, fox_in, gla_in, ret_in, ssd_in, cos_t, sin_t):
    seq = x.shape[0]
    t = ROW_TILE
    row = lambda w: pl.BlockSpec((t, w), lambda i: (i, 0))
    heads = pl.BlockSpec((FOX_HEADS, t, LANES), lambda i: (0, i, 0))
    head_shape = jax.ShapeDtypeStruct((FOX_HEADS, seq, LANES), BF16)
    w_ret, ret_gn = ret_in
    wide = pltpu.VMEM((t, SSD_INNER), F32)
    return pl.pallas_call(
        _mixers_kernel,
        grid=(seq // t,),
        in_specs=([row(D_MODEL), _layer(ln, l)] + [_layer(a, l) for a in fox_in]
                  + [_layer(a, l) for a in gla_in]
                  + [_layer(w_ret, l), row(LANES), row(LANES), _layer(ret_gn, l)]
                  + [_layer(a, l) for a in ssd_in]),
        out_specs=[heads, heads, heads, pl.BlockSpec((1, SUBLANES, LANES), lambda i: (i, 0, 0)),
                   row(GLA_W), row(RET_W), row(SSD_INNER)],
        out_shape=[head_shape, head_shape, head_shape,
                   jax.ShapeDtypeStruct((seq // t, SUBLANES, LANES), F32),
                   jax.ShapeDtypeStruct((seq, GLA_W), BF16),
                   jax.ShapeDtypeStruct((seq, RET_W), BF16),
                   jax.ShapeDtypeStruct((seq, SSD_INNER), BF16)],
        scratch_shapes=([pltpu.VMEM((SUBLANES, LANES), F32)]
                        + _linear_scratch(t, GLA_W, GLA_HEADS) + _linear_scratch(t, RET_W, RET_HEADS)
                        + [pltpu.VMEM((t + 2 * SUBLANES, SSD_CONV_CH), F32),
                           wide,
                           pltpu.VMEM((t, 2 * SSD_GROUPS * SSD_STATE), F32),
                           wide, wide, wide, wide,
                           pltpu.VMEM((t // CHUNK, SSD_HEADS, CHUNK), F32),
                           wide,
                           pltpu.VMEM((SSD_GROUPS, SSD_STATE, SSD_INNER // SSD_GROUPS), F32)]),
        compiler_params=pltpu.CompilerParams(dimension_semantics=("arbitrary",),
                                             vmem_limit_bytes=VMEM_LIMIT),
        name="mixers",
    )(x, ln, *fox_in, *gla_in, w_ret, cos_t, sin_t, ret_gn, *ssd_in)


def _merge_ffn_kernel(x_ref, ln1_ref, wg_ref, ya_ref, yb_ref, yc_ref, yd_ref,
                      wa_ref, wb_ref, wc_ref, wd_ref, wo_ref, ln2_ref, w1_ref, w2_ref, out_ref):
    x = x_ref[...]
    h = _rmsnorm(x, ln1_ref[...]).astype(BF16)

    def gate(b):
        return _sigmoid(_dot_nt(h, wg_ref[b * D_MODEL:(b + 1) * D_MODEL, :]))

    low = lax.broadcasted_iota(jnp.int32, (x.shape[0], LANES), 1) < HEAD_DIM
    up_a = None
    for p in range(FOX_HEADS // 2):
        pair = jnp.where(low, ya_ref[2 * p].astype(F32),
                         pltpu.roll(ya_ref[2 * p + 1].astype(F32), HEAD_DIM, axis=1)).astype(BF16)
        part = _dot(pair, wa_ref[p * LANES:(p + 1) * LANES, :])
        up_a = part if up_a is None else up_a + part
    merged = gate(0) * up_a
    merged = merged + gate(1) * _dot(yb_ref[...], wb_ref[...])
    merged = merged + gate(2) * _dot(yc_ref[...], wc_ref[...])
    merged = merged + gate(3) * _dot(yd_ref[...], wd_ref[...])
    x = x + _dot(merged.astype(BF16), wo_ref[...])

    h = _rmsnorm(x, ln2_ref[...]).astype(BF16)
    act = _silu(_dot(h, w1_ref[:, 0:FFN_HIDDEN])) * _dot(h, w1_ref[:, FFN_HIDDEN:2 * FFN_HIDDEN])
    out_ref[...] = x + _dot(act.astype(BF16), w2_ref[...])


def _merge_ffn(l, x, ln1, w_gates, y_a, y_b, y_c, y_d, w_up_a, w_up_b, w_up_c, w_up_d, w_out, ln2, w1, w2):
    seq = x.shape[0]
    t = FFN_TILE
    row = lambda w: pl.BlockSpec((t, w), lambda i: (i, 0))
    return pl.pallas_call(
        _merge_ffn_kernel,
        grid=(seq // t,),
        in_specs=[row(D_MODEL), _layer(ln1, l), _layer(w_gates, l),
                  pl.BlockSpec((FOX_HEADS, t, LANES), lambda i: (0, i, 0)),
                  row(GLA_W), row(RET_W), row(SSD_INNER),
                  _layer(w_up_a, l), _layer(w_up_b, l), _layer(w_up_c, l),
                  _layer(w_up_d, l), _layer(w_out, l), _layer(ln2, l), _layer(w1, l), _layer(w2, l)],
        out_specs=row(D_MODEL),
        out_shape=jax.ShapeDtypeStruct((seq, D_MODEL), F32),
        compiler_params=pltpu.CompilerParams(dimension_semantics=("arbitrary",),
                                             vmem_limit_bytes=VMEM_LIMIT),
        name="merge_swiglu",
    )(x, ln1, w_gates, y_a, y_b, y_c, y_d, w_up_a, w_up_b, w_up_c, w_up_d, w_out, ln2, w1, w2)


def _pad_last(a, width):
    return jnp.pad(a, [(0, 0)] * (a.ndim - 1) + [(0, width - a.shape[-1])])


def _rows(a):
    return a.reshape(a.shape[0], 1, -1)


def _in_proj_weights(w_in):
    wt = jnp.transpose(w_in, (0, 2, 1)).astype(BF16)
    offs = np.concatenate([[0], np.cumsum(IN_SPLITS)])
    (fq, fk, fv, ff, gq, gk, gv, glr, gr, rq, rk, rv, rg, z, xbc, dt, gates) = [
        wt[:, offs[n]:offs[n + 1], :] for n in range(len(IN_SPLITS))]
    pad = lambda a: jnp.pad(a, ((0, 0), (0, LANES - a.shape[1]), (0, 0)))
    cat = lambda parts: jnp.concatenate(parts, axis=1)
    w_fox = cat([fq, fk, fv, pad(ff)])
    w_gla = cat([gq, gk, gv, gr, pad(glr)])
    w_ret = cat([rq, rk, rv, rg])
    w_ssd = cat([z, xbc, pad(dt)])
    return w_fox, w_gla, w_ret, w_ssd, gates


def kernel(x, positions, ln1, ln2, w_in, fox_bf, fox_qn, fox_kn, gla_w2, gla_b, gla_norm, ret_norm,
           ssd_conv_w, ssd_conv_b, ssd_dt_bias, ssd_a_log, ssd_d, ssd_norm,
           w_up_a, w_up_b, w_up_c, w_up_d, w_out, w_ffn_in, w_ffn_out):
    bsz, seq, d = x.shape
    assert bsz == 1 and d == D_MODEL and seq % ROW_TILE == 0 and seq % FOX_TQ == 0 and FOX_TQ % ROW_TILE == 0
    depth = ln1.shape[0]
    xr = x.reshape(seq, d)
    cos_t, sin_t = _rotary_tables(positions, seq)

    bf16 = lambda w: w.astype(BF16)
    ln1_r, ln2_r = _rows(ln1), _rows(ln2)
    fox_bf_r = _rows(_pad_last(fox_bf, LANES))
    fox_qn_r, fox_kn_r = _rows(jnp.tile(fox_qn, (1, 2))), _rows(jnp.tile(fox_kn, (1, 2)))
    fox_shift_r, fox_par = _fox_params(fox_qn, fox_kn)
    gla_w2_p = bf16(jnp.pad(gla_w2, ((0, 0), (0, LANES - GLA_LOWRANK), (0, 0))))
    gla_b_r, gla_gn_r = _rows(gla_b), _rows(jnp.tile(gla_norm, (1, GLA_HEADS)))
    ret_gn_r = _rows(jnp.tile(ret_norm, (1, RET_HEADS)))
    conv_b_r = _rows(ssd_conv_b)
    dtb_r, alog_r = _rows(_pad_last(ssd_dt_bias, LANES)), _rows(_pad_last(ssd_a_log, LANES))
    d_r, ssd_gn_r = _rows(jnp.repeat(ssd_d, HEAD_DIM, axis=1)), _rows(ssd_norm)
    up_a, up_b, up_c, up_d, w_o = bf16(w_up_a), bf16(w_up_b), bf16(w_up_c), bf16(w_up_d), bf16(w_out)
    w1, w2 = bf16(w_ffn_in), bf16(w_ffn_out)
    w_fox, w_gla, w_ret, w_ssd, w_gates = _in_proj_weights(w_in)

    for l in range(depth):
        q, k, v, c, y_b, y_c, y_d = _mixers(
            l, xr, ln1_r,
            (w_fox, fox_bf_r, fox_qn_r, fox_kn_r, fox_shift_r),
            (w_gla, gla_w2_p, gla_b_r, gla_gn_r),
            (w_ret, ret_gn_r),
            (w_ssd, ssd_conv_w, conv_b_r, dtb_r, alog_r, d_r, ssd_gn_r),
            cos_t, sin_t)
        y_a = _fox_attention(fox_par[l], q, k, v, c)
        xr = _merge_ffn(l, xr, ln1_r, w_gates, y_a, y_b, y_c, y_d, up_a, up_b, up_c, up_d, w_o, ln2_r, w1, w2)
    return xr.reshape(bsz, seq, d)
```

```python
import math

import numpy as np
import jax
import jax.numpy as jnp
from jax import lax
from jax.experimental import pallas as pl
from jax.experimental.pallas import tpu as pltpu

D_MODEL = 1024
HEAD_DIM = 64
CHUNK = 64
FOX_HEADS = 4
GLA_HEADS = 4
GLA_LOWRANK = 16
GLA_TAU = 16.0
RET_HEADS = 4
ROPE_THETA = 10000.0
SSD_HEADS = 8
SSD_GROUPS = 2
SSD_STATE = 64
SSD_CONV = 4
SSD_INNER = SSD_HEADS * HEAD_DIM
SSD_CONV_CH = SSD_INNER + 2 * SSD_GROUPS * SSD_STATE
FOX_W = FOX_HEADS * HEAD_DIM
GLA_W = GLA_HEADS * HEAD_DIM
RET_W = RET_HEADS * HEAD_DIM
N_BRANCH = 4
FFN_HIDDEN = ((8 * D_MODEL + 3 * 256 - 1) // (3 * 256)) * 256
NORM_EPS = 1e-6
IN_SPLITS = (FOX_W, FOX_W, FOX_W, FOX_HEADS,
             GLA_W, GLA_W, GLA_W, GLA_LOWRANK, GLA_W,
             RET_W, RET_W, RET_W, RET_W,
             SSD_INNER, SSD_CONV_CH, SSD_HEADS,
             N_BRANCH * D_MODEL)

LANES = 128
SUBLANES = 8
VMEM_LIMIT = 56 * 1024 * 1024

ROW_TILE = 512
FOX_TQ = 512
FOX_TK = 512
FFN_TILE = 512
PROJ_PIECE = 512

F32 = jnp.float32
BF16 = jnp.bfloat16


def _rmsnorm(x, g):
    return x * lax.rsqrt(jnp.mean(x * x, axis=-1, keepdims=True) + NORM_EPS) * g


def _log_sigmoid(x):
    return jnp.minimum(x, 0.0) - jnp.log(1.0 + jnp.exp(-jnp.abs(x)))


def _softplus(x):
    return jnp.maximum(x, 0.0) + jnp.log(1.0 + jnp.exp(-jnp.abs(x)))


def _sigmoid(x):
    return 0.5 * jnp.tanh(0.5 * x) + 0.5


def _silu(x):
    return x * _sigmoid(x)


def _split3(x):
    hi = x.astype(BF16)
    r1 = x - hi.astype(F32)
    mid = r1.astype(BF16)
    lo = (r1 - mid.astype(F32)).astype(BF16)
    return hi, mid, lo


def _split2(x):
    hi = x.astype(BF16)
    return hi, (x - hi.astype(F32)).astype(BF16)


def _dot(a, b):
    return jnp.dot(a, b, preferred_element_type=F32)


def _dot_nt(a, b):
    return lax.dot_general(a, b, (((1,), (1,)), ((), ())), preferred_element_type=F32)


def _dot_tn(a, b):
    return lax.dot_general(a, b, (((0,), (0,)), ((), ())), preferred_element_type=F32)


def _sel_dot(mat, x):
    hi, mid, lo = _split3(x)
    return _dot(mat, hi) + _dot(mat, mid) + _dot(mat, lo)


def _dot_sel2(x, mat):
    hi, lo = _split2(x)
    return _dot(hi, mat) + _dot(lo, mat)


def _chunk_cumsum(x):
    t = x.shape[0]
    r = lax.broadcasted_iota(jnp.int32, (t, t), 0)
    c = lax.broadcasted_iota(jnp.int32, (t, t), 1)
    incl = jnp.where(((r // CHUNK) == (c // CHUNK)) & (c <= r), 1.0, 0.0).astype(BF16)
    hi, lo = _split2(x)
    b = _dot(incl, hi) + _dot(incl, lo)
    return b, _chunk_last(b) - b


def _chunk_last(b):
    t, w = b.shape
    return jnp.concatenate([jnp.broadcast_to(b[e - 1:e, :], (CHUNK, w)) for e in range(CHUNK, t + 1, CHUNK)],
                           axis=0)


def _group_mean_sq(o, group):
    w = o.shape[-1]
    r = lax.broadcasted_iota(jnp.int32, (w, w), 0)
    c = lax.broadcasted_iota(jnp.int32, (w, w), 1)
    bd = jnp.where((r // group) == (c // group), 1.0, 0.0).astype(BF16)
    return _dot_sel2(o * o, bd) * (1.0 / group)


def _project(h, wt_ref):
    n = wt_ref.shape[0]
    parts = []
    for a in range(0, n, PROJ_PIECE):
        parts.append(_dot_nt(h, wt_ref[a:min(a + PROJ_PIECE, n), :]))
        yield
    return jnp.concatenate(parts, axis=1)


def _layer(stacked, l):
    tail = tuple(stacked.shape[1:])
    return pl.BlockSpec((None,) + tail, lambda *_: (l,) + (0,) * len(tail), pipeline_mode=pl.Buffered(1))


def _rotary_kernel(pos_ref, inv_ref, sign_ref, cos_ref, sin_ref):
    ang = pos_ref[...] * inv_ref[...]
    cos_ref[...] = jnp.cos(ang)
    sin_ref[...] = jnp.sin(ang) * sign_ref[...]


def _rotary_tables(positions, seq):
    half = HEAD_DIM // 2
    inv = ROPE_THETA ** (-jnp.arange(half, dtype=F32) / half)
    reps = LANES // HEAD_DIM
    inv_row = jnp.tile(inv, 2 * reps).reshape(1, LANES)
    sign_row = jnp.tile(jnp.concatenate([-jnp.ones((half,), F32), jnp.ones((half,), F32)]),
                        reps).reshape(1, LANES)
    pos = positions.astype(F32).reshape(seq, 1)
    t = ROW_TILE
    return pl.pallas_call(
        _rotary_kernel,
        grid=(seq // t,),
        in_specs=[pl.BlockSpec((t, 1), lambda i: (i, 0)),
                  pl.BlockSpec((1, LANES), lambda i: (0, 0)),
                  pl.BlockSpec((1, LANES), lambda i: (0, 0))],
        out_specs=[pl.BlockSpec((t, LANES), lambda i: (i, 0)),
                   pl.BlockSpec((t, LANES), lambda i: (i, 0))],
        out_shape=[jax.ShapeDtypeStruct((seq, LANES), F32)] * 2,
        name="rotary_tables",
    )(pos, inv_row, sign_row)


SMALL_FF = 0
SMALL_GLR = FOX_HEADS
SMALL_DT = 24
LOG2E = math.log2(math.e)
F32_EXP_ZERO = 88.0
FOX_FAST_MAX_LOGIT = 40.0


def _fox_prep_body(h, small, w_ref, bf_ref, qn_ref, kn_ref, shift_ref, q_out, k_out, v_out, c_out, carry_ref):
    t = h.shape[0]
    u = yield from _project(h, w_ref)
    ls = _log_sigmoid(small + bf_ref[...])
    r = lax.broadcasted_iota(jnp.int32, (t, t), 0)
    cc = lax.broadcasted_iota(jnp.int32, (t, t), 1)
    tri = jnp.where(cc <= r, 1.0, 0.0).astype(BF16)
    c = _sel_dot(tri, ls) + carry_ref[0:1, :]
    carry_ref[0:1, :] = c[t - 1:t, :]
    c_out[0] = jnp.concatenate([c[0:1, :], c[t - 1:t, :], jnp.zeros((SUBLANES - 2, LANES), F32)], axis=0)
    c_hi, c_mid, c_lo = _split3(c * LOG2E)
    c_hi, c_mid, c_lo = c_hi.astype(F32), c_mid.astype(F32), c_lo.astype(F32)

    lane = lax.broadcasted_iota(jnp.int32, (t, LANES), 1)
    low = lane < HEAD_DIM
    j = lane - HEAD_DIM
    shift = shift_ref[...]

    def head_norm(pair, gain, mult):
        sq = pair * pair
        s_lo = jnp.sum(jnp.where(low, sq, 0.0), axis=-1, keepdims=True)
        s_hi = jnp.sum(jnp.where(low, 0.0, sq), axis=-1, keepdims=True)
        ms = jnp.where(low, s_lo, s_hi) * (1.0 / HEAD_DIM)
        return pair * lax.rsqrt(ms + NORM_EPS) * (gain * mult)

    for p in range(FOX_HEADS // 2):
        qp = head_norm(u[:, p * LANES:(p + 1) * LANES], qn_ref[...], HEAD_DIM ** -0.5 * LOG2E)
        kp = head_norm(u[:, FOX_W + p * LANES:FOX_W + (p + 1) * LANES], kn_ref[...], 1.0)
        vp = u[:, 2 * FOX_W + p * LANES:2 * FOX_W + (p + 1) * LANES]
        for sub in range(2):
            hd = 2 * p + sub
            if sub == 1:
                qp, kp, vp = (pltpu.roll(a, HEAD_DIM, axis=1) for a in (qp, kp, vp))
            ch, cm, cl = (jnp.broadcast_to(a[:, hd:hd + 1], (t, LANES)) for a in (c_hi, c_mid, c_lo))
            aug_q = jnp.where(j == 0, ch, jnp.where(j == 1, cm, jnp.where(j == 2, cl,
                              jnp.where(j < 7, 1.0, 0.0))))
            aug_k = jnp.where(j < 3, 1.0, jnp.where(j == 3, -ch, jnp.where(j == 4, -cm,
                              jnp.where(j == 5, -cl, jnp.where(j == 6, -shift, 0.0)))))
            aug_v = jnp.where(j == 0, 1.0, 0.0)
            q_out[hd] = jnp.where(low, qp, aug_q).astype(BF16)
            k_out[hd] = jnp.where(low, kp, aug_k).astype(BF16)
            v_out[hd] = jnp.where(low, vp, aug_v).astype(BF16)


FOX_HEADS_PER_STEP = 2


def _fox_attn_kernel(par_ref, cq0_ref, cend_ref, q_ref, k_ref, v_ref, o_ref):
    i = pl.program_id(1)
    tq = q_ref.shape[1]
    half = tq // 2

    def one_head(a):
        hd = pl.program_id(0) * FOX_HEADS_PER_STEP + a
        q = q_ref[a]
        c0 = cq0_ref[i, hd]
        thr = par_ref[0]
        j_lo = lax.while_loop(
            lambda j: jnp.logical_and(j > 0, c0 - cend_ref[jnp.maximum(j - 1, 0), hd] >= -thr),
            lambda j: j - 1, i)

        def keys(j, nblk):
            off = pl.multiple_of(j * FOX_TK, FOX_TK)
            return k_ref[a, pl.ds(off, nblk * FOX_TK), :], v_ref[a, pl.ds(off, nblk * FOX_TK), :]

        def causal(s, row0):
            r = lax.broadcasted_iota(jnp.int32, s.shape, 0) + row0
            c = lax.broadcasted_iota(jnp.int32, s.shape, 1)
            return jnp.where(c <= r, s, -jnp.inf)

        def fixed_shift():
            def block(j, nblk):
                k, v = keys(j, nblk)
                return _dot(jnp.exp2(_dot_nt(q, k)).astype(BF16), v)

            k, v = keys(i, 1)
            top = _dot(jnp.exp2(causal(_dot_nt(q[0:half], k[0:half]), 0)).astype(BF16), v[0:half])
            bot = _dot(jnp.exp2(causal(_dot_nt(q[half:tq], k), half)).astype(BF16), v)
            acc = jnp.concatenate([top, bot], axis=0)
            n = i - j_lo
            acc = lax.cond(n % 2 == 1, lambda: acc + block(j_lo, 1), lambda: acc)
            acc = lax.cond((n // 2) % 2 == 1, lambda: acc + block(j_lo + n % 2, 2), lambda: acc)
            first = j_lo + n % 4
            return lax.fori_loop(0, n // 4, lambda p, acc: acc + block(first + 4 * p, 4), acc)

        def running_max():
            def step(j, carry, masked):
                m, acc = carry
                k, v = keys(j, 1)
                s = _dot_nt(q, k)
                if masked:
                    s = causal(s, 0)
                m_new = jnp.maximum(m, jnp.max(s, axis=-1, keepdims=True))
                p = jnp.exp2(s - m_new)
                return m_new, jnp.exp2(m - m_new) * acc + _dot(p.astype(BF16), v)
            init = (jnp.full((tq, 1), -jnp.inf, F32), jnp.zeros((tq, LANES), F32))
            carry = step(i, init, True)
            return lax.fori_loop(j_lo, i, lambda j, c: step(j, c, False), carry)[1]

        acc = lax.cond(par_ref[1] > 0.0, fixed_shift, running_max)
        return acc / acc[:, HEAD_DIM:HEAD_DIM + 1]

    low = lax.broadcasted_iota(jnp.int32, (tq, LANES), 1) < HEAD_DIM
    o_ref[0] = jnp.where(low, one_head(0), pltpu.roll(one_head(1), HEAD_DIM, axis=1)).astype(BF16)


def _fox_attn(par, cq0, cend, q, k, v):
    assert FOX_TQ == FOX_TK and LANES == FOX_HEADS_PER_STEP * HEAD_DIM
    nh, seq, _ = q.shape
    g = FOX_HEADS_PER_STEP
    kv_spec = pl.BlockSpec((g, seq, LANES), lambda h, i, *_: (h, 0, 0))
    return pl.pallas_call(
        _fox_attn_kernel,
        grid_spec=pltpu.PrefetchScalarGridSpec(
            num_scalar_prefetch=3,
            grid=(nh // g, seq // FOX_TQ),
            in_specs=[pl.BlockSpec((g, FOX_TQ, LANES), lambda h, i, *_: (h, i, 0)), kv_spec, kv_spec],
            out_specs=pl.BlockSpec((1, FOX_TQ, LANES), lambda h, i, *_: (h, i, 0))),
        out_shape=jax.ShapeDtypeStruct((nh // g, seq, LANES), BF16),
        compiler_params=pltpu.CompilerParams(dimension_semantics=("arbitrary", "arbitrary"),
                                             vmem_limit_bytes=VMEM_LIMIT),
        name="fox_attn",
    )(par, cq0, cend, q, k, v)


def _fox_params(qn, kn):
    bound = 1.02 * HEAD_DIM ** 0.5 * jnp.max(jnp.abs(qn), axis=1) * jnp.max(jnp.abs(kn), axis=1)
    shift = jnp.broadcast_to((LOG2E * bound)[:, None, None], (bound.shape[0], 1, LANES))
    par = jnp.stack([F32_EXP_ZERO + 2.0 * bound, (bound < FOX_FAST_MAX_LOGIT).astype(F32)], axis=1)
    return shift, par


def _fox_attention(par, q, k, v, c):
    per_tile = FOX_TQ // ROW_TILE
    cq0 = c[0::per_tile, 0, 0:FOX_HEADS]
    cend = c[per_tile - 1::per_tile, 1, 0:FOX_HEADS]
    return _fox_attn(par, cq0, cend, q, k, v)


def _linear_attn_chunks(qd_s, ki_s, ke_s, v_s, dec_s, o_s, st_s, nheads):
    t = qd_s.shape[0]
    r = lax.broadcasted_iota(jnp.int32, (CHUNK, CHUNK), 0)
    c = lax.broadcasted_iota(jnp.int32, (CHUNK, CHUNK), 1)
    tril = c <= r
    states = [st_s[hd] for hd in range(nheads)]
    for ci in range(t // CHUNK):
        rows = slice(ci * CHUNK, (ci + 1) * CHUNK)
        last = slice((ci + 1) * CHUNK - 1, (ci + 1) * CHUNK)
        for hd in range(nheads):
            cols = slice(hd * HEAD_DIM, (hd + 1) * HEAD_DIM)
            qd = qd_s[rows, cols].astype(BF16)
            ki = ki_s[rows, cols].astype(BF16)
            ke = ke_s[rows, cols].astype(BF16)
            vv = v_s[rows, cols].astype(BF16)
            attn = jnp.where(tril, _dot_nt(qd, ki), 0.0).astype(BF16)
            o_s[rows, cols] = _dot(attn, vv) + _dot_nt(qd, states[hd].astype(BF16))
            states[hd] = states[hd] * dec_s[last, cols] + _dot_tn(vv, ke)
        yield
    for hd in range(nheads):
        st_s[hd] = states[hd]


def _gla_body(h, small, w_ref, w2_ref, b2_ref, gn_ref, y_ref, qd_s, ki_s, ke_s, v_s, dec_s, o_s, st_s):
    u = yield from _project(h, w_ref)
    z = _dot(small.astype(BF16), w2_ref[...]) + b2_ref[...]
    log_a = _log_sigmoid(z) * (1.0 / GLA_TAU)
    b, rem = _chunk_cumsum(log_a)
    eb = jnp.exp(b)
    k = u[:, GLA_W:2 * GLA_W]
    qd_s[...] = u[:, 0:GLA_W] * (HEAD_DIM ** -0.5) * eb
    ki_s[...] = k * jnp.exp(-b)
    ke_s[...] = k * jnp.exp(rem)
    v_s[...] = u[:, 2 * GLA_W:3 * GLA_W]
    dec_s[...] = eb
    yield
    yield from _linear_attn_chunks(qd_s, ki_s, ke_s, v_s, dec_s, o_s, st_s, GLA_HEADS)
    o = o_s[...]
    y = o * lax.rsqrt(_group_mean_sq(o, HEAD_DIM) + NORM_EPS) * gn_ref[...]
    y_ref[...] = (y * _silu(u[:, 3 * GLA_W:4 * GLA_W])).astype(BF16)


def _ret_body(h, w_ref, cos_ref, sin_ref, gn_ref, y_ref, qd_s, ki_s, ke_s, v_s, dec_s, o_s, st_s):
    t = h.shape[0]
    u = yield from _project(h, w_ref)
    cos, sin = cos_ref[...], sin_ref[...]
    lane = lax.broadcasted_iota(jnp.int32, (t, RET_W), 1)
    half = HEAD_DIM // 2
    first_half = (lax.broadcasted_iota(jnp.int32, (t, LANES), 1) % HEAD_DIM) < half

    def rotate(a):
        blocks = []
        for b in range(0, RET_W, LANES):
            ab = a[:, b:b + LANES]
            swapped = jnp.where(first_half, pltpu.roll(ab, LANES - half, axis=1),
                                pltpu.roll(ab, half, axis=1))
            blocks.append(ab * cos + swapped * sin)
        return jnp.concatenate(blocks, axis=1)

    q = rotate(u[:, 0:RET_W])
    k = rotate(u[:, RET_W:2 * RET_W]) * (HEAD_DIM ** -0.5)
    row = lax.broadcasted_iota(jnp.int32, (t, RET_W), 0)
    lg = jnp.zeros((t, RET_W), F32)
    for hd in range(RET_HEADS):
        lg = jnp.where(lane // HEAD_DIM == hd, math.log(1.0 - 2.0 ** (-5.0 - hd)), lg)
    pos = (row % CHUNK).astype(F32)
    qw = jnp.exp((pos + 1.0) * lg)
    qd_s[...] = q * qw
    ki_s[...] = k * jnp.exp(-(pos + 1.0) * lg)
    ke_s[...] = k * jnp.exp((CHUNK - 1.0 - pos) * lg)
    v_s[...] = u[:, 2 * RET_W:3 * RET_W]
    dec_s[...] = qw
    yield
    yield from _linear_attn_chunks(qd_s, ki_s, ke_s, v_s, dec_s, o_s, st_s, RET_HEADS)
    o = o_s[...]
    y = o * lax.rsqrt(_group_mean_sq(o, HEAD_DIM) + NORM_EPS) * gn_ref[...]
    y_ref[...] = (y * _silu(u[:, 3 * RET_W:4 * RET_W])).astype(BF16)


def _linear_scratch(t, width, nheads):
    return ([pltpu.VMEM((t, width), F32)] * 6
            + [pltpu.VMEM((nheads, HEAD_DIM, HEAD_DIM), F32)])


SSD_PROJ = SSD_INNER + SSD_CONV_CH


def _ssd_body(h, small, w_ref, cw_ref, cb_ref, dtb_row_ref, alog_row_ref, d_row_ref, gn_ref, y_ref,
              xp_s, xs_s, bc_s, cs_s, ecs_s, xdt_s, xdtw_s, cst_s, o_s, st_s):
    t = h.shape[0]
    nc = t // CHUNK
    u = yield from _project(h, w_ref)
    z = u[:, 0:SSD_INNER]

    xp_s[SUBLANES:SUBLANES + t, :] = u[:, SSD_INNER:SSD_PROJ]
    conv = cb_ref[...] + cw_ref[SSD_CONV - 1:SSD_CONV, :] * xp_s[SUBLANES:SUBLANES + t, :]
    for kk in range(SSD_CONV - 1):
        off = SUBLANES - (SSD_CONV - 1) + kk
        conv = conv + cw_ref[kk:kk + 1, :] * xp_s[off:off + t, :]
    xp_s[0:SUBLANES, :] = xp_s[t:t + SUBLANES, :]
    xbc = _silu(conv)
    xs = xbc[:, 0:SSD_INNER]
    xs_s[...] = xs
    bc_s[...] = xbc[:, SSD_INNER:SSD_CONV_CH]

    dt = _softplus(small + dtb_row_ref[...])
    dta = dt * -jnp.exp(alog_row_ref[...])
    cs, _ = _chunk_cumsum(dta)
    cs_t = cs.T[SMALL_DT:SMALL_DT + SSD_HEADS, :]
    for ci in range(nc):
        cst_s[ci] = cs_t[:, ci * CHUNK:(ci + 1) * CHUNK]

    er = lax.broadcasted_iota(jnp.int32, (LANES, SSD_INNER), 0)
    ec = lax.broadcasted_iota(jnp.int32, (LANES, SSD_INNER), 1)
    expand = jnp.where(ec // HEAD_DIM == er - SMALL_DT, 1.0, 0.0).astype(BF16)
    cs_x = _dot_sel2(cs, expand)
    xdt = xs * _dot_sel2(dt, expand)
    cs_s[...] = cs_x
    ecs_s[...] = jnp.exp(cs_x)
    xdt_s[...] = xdt
    xdtw_s[...] = xdt * jnp.exp(_chunk_last(cs_x) - cs_x)

    r = lax.broadcasted_iota(jnp.int32, (CHUNK, CHUNK), 0)
    c = lax.broadcasted_iota(jnp.int32, (CHUNK, CHUNK), 1)
    tril = c <= r
    gw = SSD_GROUPS * SSD_STATE

    hpg = SSD_HEADS // SSD_GROUPS
    gcols = hpg * HEAD_DIM
    states = [st_s[g] for g in range(SSD_GROUPS)]
    for ci in range(nc):
        rows = slice(ci * CHUNK, (ci + 1) * CHUNK)
        last = slice((ci + 1) * CHUNK - 1, (ci + 1) * CHUNK)
        cst = cst_s[ci]
        for g in range(SSD_GROUPS):
            gsl = slice(g * gcols, (g + 1) * gcols)
            bm = bc_s[rows, g * SSD_STATE:(g + 1) * SSD_STATE].astype(BF16)
            cm = bc_s[rows, gw + g * SSD_STATE:gw + (g + 1) * SSD_STATE].astype(BF16)
            cb = _dot_nt(cm, bm)
            y_inter = _dot(cm, states[g].astype(BF16)) * ecs_s[rows, gsl]
            for rr in range(hpg):
                hd = g * hpg + rr
                cols = slice(hd * HEAD_DIM, (hd + 1) * HEAD_DIM)
                seg = cs_s[rows, cols] - cst[hd:hd + 1, :]
                decay = jnp.exp(jnp.where(tril, seg, -jnp.inf))
                y = _dot((cb * decay).astype(BF16), xdt_s[rows, cols].astype(BF16))
                o_s[rows, cols] = y + y_inter[:, rr * HEAD_DIM:(rr + 1) * HEAD_DIM]
            states[g] = (states[g] * ecs_s[last, gsl]
                         + _dot_tn(bm, xdtw_s[rows, gsl].astype(BF16)))
        yield
    for g in range(SSD_GROUPS):
        st_s[g] = states[g]

    y = (o_s[...] + d_row_ref[...] * xs_s[...]) * _silu(z)
    gwid = SSD_INNER // SSD_GROUPS
    normed = []
    for g in range(SSD_GROUPS):
        yg = y[:, g * gwid:(g + 1) * gwid]
        normed.append(yg * lax.rsqrt(jnp.mean(yg * yg, axis=-1, keepdims=True) + NORM_EPS))
    y_ref[...] = (jnp.concatenate(normed, axis=1) * gn_ref[...]).astype(BF16)


N_FOX_IN, N_GLA_IN, N_RET_IN, N_SSD_IN = 5, 4, 4, 7
N_LINEAR_SCRATCH = 7
N_SSD_SCRATCH = 10


def _mixers_kernel(x_ref, ln_ref, ws_ref, *refs):
    it = iter(refs)
    take = lambda n: [next(it) for _ in range(n)]
    fox_in, gla_in, ret_in, ssd_in = take(N_FOX_IN), take(N_GLA_IN), take(N_RET_IN), take(N_SSD_IN)
    fox_out, (yb_ref, yc_ref, yd_ref) = take(4), take(3)
    (carry_ref,), gla_s, ret_s, ssd_s = take(1), take(N_LINEAR_SCRATCH), take(N_LINEAR_SCRATCH), take(N_SSD_SCRATCH)

    @pl.when(pl.program_id(0) == 0)
    def _():
        carry_ref[...] = jnp.zeros_like(carry_ref)
        gla_s[-1][...] = jnp.zeros_like(gla_s[-1])
        ret_s[-1][...] = jnp.zeros_like(ret_s[-1])
        ssd_s[-1][...] = jnp.zeros_like(ssd_s[-1])
        ssd_s[0][0:SUBLANES, :] = jnp.zeros((SUBLANES, SSD_CONV_CH), F32)

    h = _rmsnorm(x_ref[...], ln_ref[...]).astype(BF16)
    small = _dot_nt(h, ws_ref[...])
    waiting = [_ssd_body(h, small, *ssd_in, yd_ref, *ssd_s), _gla_body(h, small, *gla_in, yb_ref, *gla_s),
               _ret_body(h, *ret_in, yc_ref, *ret_s), _fox_prep_body(h, small, *fox_in, *fox_out, carry_ref)]
    pending = []
    while waiting or pending:
        if waiting:
            pending.append(waiting.pop(0))
        for body in list(pending):
            if next(body, "done") == "done":
                pending.remove(body)


def _mixers(l, x, ln, w_small, fox_in, gla_in, ret_in, ssd_in, cos_t, sin_t):
    seq = x.shape[0]
    t = ROW_TILE
    row = lambda w: pl.BlockSpec((t, w), lambda i: (i, 0))
    heads = pl.BlockSpec((FOX_HEADS, t, LANES), lambda i: (0, i, 0))
    head_shape = jax.ShapeDtypeStruct((FOX_HEADS, seq, LANES), BF16)
    w_ret, ret_gn = ret_in
    wide = pltpu.VMEM((t, SSD_INNER), F32)
    return pl.pallas_call(
        _mixers_kernel,
        grid=(seq // t,),
        in_specs=([row(D_MODEL), _layer(ln, l), _layer(w_small, l)] + [_layer(a, l) for a in fox_in]
                  + [_layer(a, l) for a in gla_in]
                  + [_layer(w_ret, l), row(LANES), row(LANES), _layer(ret_gn, l)]
                  + [_layer(a, l) for a in ssd_in]),
        out_specs=[heads, heads, heads, pl.BlockSpec((1, SUBLANES, LANES), lambda i: (i, 0, 0)),
                   row(GLA_W), row(RET_W), row(SSD_INNER)],
        out_shape=[head_shape, head_shape, head_shape,
                   jax.ShapeDtypeStruct((seq // t, SUBLANES, LANES), F32),
                   jax.ShapeDtypeStruct((seq, GLA_W), BF16),
                   jax.ShapeDtypeStruct((seq, RET_W), BF16),
                   jax.ShapeDtypeStruct((seq, SSD_INNER), BF16)],
        scratch_shapes=([pltpu.VMEM((SUBLANES, LANES), F32)]
                        + _linear_scratch(t, GLA_W, GLA_HEADS) + _linear_scratch(t, RET_W, RET_HEADS)
                        + [pltpu.VMEM((t + 2 * SUBLANES, SSD_CONV_CH), F32),
                           wide,
                           pltpu.VMEM((t, 2 * SSD_GROUPS * SSD_STATE), F32),
                           wide, wide, wide, wide,
                           pltpu.VMEM((t // CHUNK, SSD_HEADS, CHUNK), F32),
                           wide,
                           pltpu.VMEM((SSD_GROUPS, SSD_STATE, SSD_INNER // SSD_GROUPS), F32)]),
        compiler_params=pltpu.CompilerParams(dimension_semantics=("arbitrary",),
                                             vmem_limit_bytes=VMEM_LIMIT),
        name="mixers",
    )(x, ln, w_small, *fox_in, *gla_in, w_ret, cos_t, sin_t, ret_gn, *ssd_in)


def _merge_ffn_kernel(x_ref, ln1_ref, wg_ref, ya_ref, yb_ref, yc_ref, yd_ref,
                      wa_ref, wb_ref, wc_ref, wd_ref, wo_ref, ln2_ref, w1_ref, w2_ref, out_ref):
    x = x_ref[...]
    h = _rmsnorm(x, ln1_ref[...]).astype(BF16)

    def gate(b):
        return _sigmoid(_dot_nt(h, wg_ref[b * D_MODEL:(b + 1) * D_MODEL, :]))

    up_a = _dot(ya_ref[0], wa_ref[0:LANES, :])
    for p in range(1, FOX_HEADS // FOX_HEADS_PER_STEP):
        up_a = up_a + _dot(ya_ref[p], wa_ref[p * LANES:(p + 1) * LANES, :])
    merged = gate(0) * up_a
    merged = merged + gate(1) * _dot(yb_ref[...], wb_ref[...])
    merged = merged + gate(2) * _dot(yc_ref[...], wc_ref[...])
    merged = merged + gate(3) * _dot(yd_ref[...], wd_ref[...])
    x = x + _dot(merged.astype(BF16), wo_ref[...])

    h = _rmsnorm(x, ln2_ref[...]).astype(BF16)
    act = _silu(_dot(h, w1_ref[:, 0:FFN_HIDDEN])) * _dot(h, w1_ref[:, FFN_HIDDEN:2 * FFN_HIDDEN])
    out_ref[...] = x + _dot(act.astype(BF16), w2_ref[...])


def _merge_ffn(l, x, ln1, w_gates, y_a, y_b, y_c, y_d, w_up_a, w_up_b, w_up_c, w_up_d, w_out, ln2, w1, w2):
    seq = x.shape[0]
    t = FFN_TILE
    row = lambda w: pl.BlockSpec((t, w), lambda i: (i, 0))
    return pl.pallas_call(
        _merge_ffn_kernel,
        grid=(seq // t,),
        in_specs=[row(D_MODEL), _layer(ln1, l), _layer(w_gates, l),
                  pl.BlockSpec((FOX_HEADS // FOX_HEADS_PER_STEP, t, LANES), lambda i: (0, i, 0)),
                  row(GLA_W), row(RET_W), row(SSD_INNER),
                  _layer(w_up_a, l), _layer(w_up_b, l), _layer(w_up_c, l),
                  _layer(w_up_d, l), _layer(w_out, l), _layer(ln2, l), _layer(w1, l), _layer(w2, l)],
        out_specs=row(D_MODEL),
        out_shape=jax.ShapeDtypeStruct((seq, D_MODEL), F32),
        compiler_params=pltpu.CompilerParams(dimension_semantics=("arbitrary",),
                                             vmem_limit_bytes=VMEM_LIMIT),
        name="merge_swiglu",
    )(x, ln1, w_gates, y_a, y_b, y_c, y_d, w_up_a, w_up_b, w_up_c, w_up_d, w_out, ln2, w1, w2)


def _pad_last(a, width):
    return jnp.pad(a, [(0, 0)] * (a.ndim - 1) + [(0, width - a.shape[-1])])


def _rows(a):
    return a.reshape(a.shape[0], 1, -1)


def _in_proj_weights(w_in):
    wt = jnp.transpose(w_in, (0, 2, 1)).astype(BF16)
    offs = np.concatenate([[0], np.cumsum(IN_SPLITS)])
    (fq, fk, fv, ff, gq, gk, gv, glr, gr, rq, rk, rv, rg, z, xbc, dt, gates) = [
        wt[:, offs[n]:offs[n + 1], :] for n in range(len(IN_SPLITS))]
    cat = lambda parts: jnp.concatenate(parts, axis=1)
    gap = lambda n: jnp.zeros((wt.shape[0], n, wt.shape[2]), BF16)
    w_small = cat([ff, glr, gap(SMALL_DT - SMALL_GLR - GLA_LOWRANK), dt, gap(LANES - SMALL_DT - SSD_HEADS)])
    return w_small, cat([fq, fk, fv]), cat([gq, gk, gv, gr]), cat([rq, rk, rv, rg]), cat([z, xbc]), gates


def kernel(x, positions, ln1, ln2, w_in, fox_bf, fox_qn, fox_kn, gla_w2, gla_b, gla_norm, ret_norm,
           ssd_conv_w, ssd_conv_b, ssd_dt_bias, ssd_a_log, ssd_d, ssd_norm,
           w_up_a, w_up_b, w_up_c, w_up_d, w_out, w_ffn_in, w_ffn_out):
    bsz, seq, d = x.shape
    assert bsz == 1 and d == D_MODEL and seq % ROW_TILE == 0 and seq % FOX_TQ == 0 and FOX_TQ % ROW_TILE == 0
    depth = ln1.shape[0]
    xr = x.reshape(seq, d)
    cos_t, sin_t = _rotary_tables(positions, seq)

    bf16 = lambda w: w.astype(BF16)
    ln1_r, ln2_r = _rows(ln1), _rows(ln2)
    fox_bf_r = _rows(_pad_last(fox_bf, LANES))
    fox_qn_r, fox_kn_r = _rows(jnp.tile(fox_qn, (1, 2))), _rows(jnp.tile(fox_kn, (1, 2)))
    fox_shift_r, fox_par = _fox_params(fox_qn, fox_kn)
    gla_w2_p = bf16(jnp.pad(gla_w2, ((0, 0), (SMALL_GLR, LANES - SMALL_GLR - GLA_LOWRANK), (0, 0))))
    gla_b_r, gla_gn_r = _rows(gla_b), _rows(jnp.tile(gla_norm, (1, GLA_HEADS)))
    ret_gn_r = _rows(jnp.tile(ret_norm, (1, RET_HEADS)))
    conv_b_r = _rows(ssd_conv_b)
    at_dt = lambda a: _rows(jnp.pad(a, ((0, 0), (SMALL_DT, LANES - SMALL_DT - SSD_HEADS))))
    dtb_r, alog_r = at_dt(ssd_dt_bias), at_dt(ssd_a_log)
    d_r, ssd_gn_r = _rows(jnp.repeat(ssd_d, HEAD_DIM, axis=1)), _rows(ssd_norm)
    up_a, up_b, up_c, up_d, w_o = bf16(w_up_a), bf16(w_up_b), bf16(w_up_c), bf16(w_up_d), bf16(w_out)
    w1, w2 = bf16(w_ffn_in), bf16(w_ffn_out)
    w_small, w_fox, w_gla, w_ret, w_ssd, w_gates = _in_proj_weights(w_in)

    for l in range(depth):
        q, k, v, c, y_b, y_c, y_d = _mixers(
            l, xr, ln1_r, w_small,
            (w_fox, fox_bf_r, fox_qn_r, fox_kn_r, fox_shift_r),
            (w_gla, gla_w2_p, gla_b_r, gla_gn_r),
            (w_ret, ret_gn_r),
            (w_ssd, ssd_conv_w, conv_b_r, dtb_r, alog_r, d_r, ssd_gn_r),
            cos_t, sin_t)
        y_a = _fox_attention(fox_par[l], q, k, v, c)
        xr = _merge_ffn(l, xr, ln1_r, w_gates, y_a, y_b, y_c, y_d, up_a, up_b, up_c, up_d, w_o, ln2_r, w1, w2)
    return xr.reshape(bsz, seq, d)
```

```python
import math

import numpy as np
import jax
import jax.numpy as jnp
from jax import lax
from jax.experimental import pallas as pl
from jax.experimental.pallas import tpu as pltpu

D_MODEL = 1024
HEAD_DIM = 64
CHUNK = 64
FOX_HEADS = 4
GLA_HEADS = 4
GLA_LOWRANK = 16
GLA_TAU = 16.0
RET_HEADS = 4
ROPE_THETA = 10000.0
SSD_HEADS = 8
SSD_GROUPS = 2
SSD_STATE = 64
SSD_CONV = 4
SSD_INNER = SSD_HEADS * HEAD_DIM
SSD_CONV_CH = SSD_INNER + 2 * SSD_GROUPS * SSD_STATE
FOX_W = FOX_HEADS * HEAD_DIM
GLA_W = GLA_HEADS * HEAD_DIM
RET_W = RET_HEADS * HEAD_DIM
N_BRANCH = 4
FFN_HIDDEN = ((8 * D_MODEL + 3 * 256 - 1) // (3 * 256)) * 256
NORM_EPS = 1e-6
IN_SPLITS = (FOX_W, FOX_W, FOX_W, FOX_HEADS,
             GLA_W, GLA_W, GLA_W, GLA_LOWRANK, GLA_W,
             RET_W, RET_W, RET_W, RET_W,
             SSD_INNER, SSD_CONV_CH, SSD_HEADS,
             N_BRANCH * D_MODEL)

LANES = 128
SUBLANES = 8
VMEM_LIMIT = 56 * 1024 * 1024

ROW_TILE = 512
FOX_TQ = 512
FOX_TK = 512
FFN_TILE = 512
PROJ_PIECE = 512

F32 = jnp.float32
BF16 = jnp.bfloat16


def _rmsnorm(x, g):
    return x * lax.rsqrt(jnp.mean(x * x, axis=-1, keepdims=True) + NORM_EPS) * g


def _log_sigmoid(x):
    return jnp.minimum(x, 0.0) - jnp.log(1.0 + jnp.exp(-jnp.abs(x)))


def _softplus(x):
    return jnp.maximum(x, 0.0) + jnp.log(1.0 + jnp.exp(-jnp.abs(x)))


def _sigmoid(x):
    return 0.5 * jnp.tanh(0.5 * x) + 0.5


def _silu(x):
    return x * _sigmoid(x)


def _split3(x):
    hi = x.astype(BF16)
    r1 = x - hi.astype(F32)
    mid = r1.astype(BF16)
    lo = (r1 - mid.astype(F32)).astype(BF16)
    return hi, mid, lo


def _split2(x):
    hi = x.astype(BF16)
    return hi, (x - hi.astype(F32)).astype(BF16)


def _dot(a, b):
    return jnp.dot(a, b, preferred_element_type=F32)


def _dot_nt(a, b):
    return lax.dot_general(a, b, (((1,), (1,)), ((), ())), preferred_element_type=F32)


def _dot_tn(a, b):
    return lax.dot_general(a, b, (((0,), (0,)), ((), ())), preferred_element_type=F32)


def _sel_dot(mat, x):
    hi, mid, lo = _split3(x)
    return _dot(mat, hi) + _dot(mat, mid) + _dot(mat, lo)


def _dot_sel2(x, mat):
    hi, lo = _split2(x)
    return _dot(hi, mat) + _dot(lo, mat)


def _chunk_cumsum(x):
    t = x.shape[0]
    r = lax.broadcasted_iota(jnp.int32, (t, t), 0)
    c = lax.broadcasted_iota(jnp.int32, (t, t), 1)
    incl = jnp.where(((r // CHUNK) == (c // CHUNK)) & (c <= r), 1.0, 0.0).astype(BF16)
    hi, lo = _split2(x)
    b = _dot(incl, hi) + _dot(incl, lo)
    return b, _chunk_last(b) - b


def _chunk_last(b):
    t, w = b.shape
    return jnp.concatenate([jnp.broadcast_to(b[e - 1:e, :], (CHUNK, w)) for e in range(CHUNK, t + 1, CHUNK)],
                           axis=0)


def _group_mean_sq(o, group):
    w = o.shape[-1]
    r = lax.broadcasted_iota(jnp.int32, (w, w), 0)
    c = lax.broadcasted_iota(jnp.int32, (w, w), 1)
    bd = jnp.where((r // group) == (c // group), 1.0, 0.0).astype(BF16)
    return _dot_sel2(o * o, bd) * (1.0 / group)


def _project(h, wt_ref):
    n = wt_ref.shape[0]
    parts = []
    for a in range(0, n, PROJ_PIECE):
        parts.append(_dot_nt(h, wt_ref[a:min(a + PROJ_PIECE, n), :]))
        yield
    return jnp.concatenate(parts, axis=1)


def _layer(stacked, l):
    tail = tuple(stacked.shape[1:])
    return pl.BlockSpec((None,) + tail, lambda *_: (l,) + (0,) * len(tail), pipeline_mode=pl.Buffered(1))


def _rotary_kernel(pos_ref, inv_ref, sign_ref, cos_ref, sin_ref):
    ang = pos_ref[...] * inv_ref[...]
    cos_ref[...] = jnp.cos(ang)
    sin_ref[...] = jnp.sin(ang) * sign_ref[...]


def _rotary_tables(positions, seq):
    half = HEAD_DIM // 2
    inv = ROPE_THETA ** (-jnp.arange(half, dtype=F32) / half)
    reps = LANES // HEAD_DIM
    inv_row = jnp.tile(inv, 2 * reps).reshape(1, LANES)
    sign_row = jnp.tile(jnp.concatenate([-jnp.ones((half,), F32), jnp.ones((half,), F32)]),
                        reps).reshape(1, LANES)
    pos = positions.astype(F32).reshape(seq, 1)
    t = ROW_TILE
    return pl.pallas_call(
        _rotary_kernel,
        grid=(seq // t,),
        in_specs=[pl.BlockSpec((t, 1), lambda i: (i, 0)),
                  pl.BlockSpec((1, LANES), lambda i: (0, 0)),
                  pl.BlockSpec((1, LANES), lambda i: (0, 0))],
        out_specs=[pl.BlockSpec((t, LANES), lambda i: (i, 0)),
                   pl.BlockSpec((t, LANES), lambda i: (i, 0))],
        out_shape=[jax.ShapeDtypeStruct((seq, LANES), F32)] * 2,
        name="rotary_tables",
    )(pos, inv_row, sign_row)


SMALL_FF = 0
SMALL_GLR = FOX_HEADS
SMALL_DT = 24
LOG2E = math.log2(math.e)
F32_EXP_ZERO = 88.0
FOX_FAST_MAX_LOGIT = 40.0


def _fox_prep_body(h, small, w_ref, bf_ref, qn_ref, kn_ref, shift_ref, q_out, k_out, v_out, c_out, carry_ref):
    t = h.shape[0]
    u = yield from _project(h, w_ref)
    ls = _log_sigmoid(small + bf_ref[...])
    r = lax.broadcasted_iota(jnp.int32, (t, t), 0)
    cc = lax.broadcasted_iota(jnp.int32, (t, t), 1)
    tri = jnp.where(cc <= r, 1.0, 0.0).astype(BF16)
    c = _sel_dot(tri, ls) + carry_ref[0:1, :]
    carry_ref[0:1, :] = c[t - 1:t, :]
    c_out[0] = jnp.concatenate([c[0:1, :], c[t - 1:t, :], jnp.zeros((SUBLANES - 2, LANES), F32)], axis=0)
    c_hi, c_mid, c_lo = _split3(c * LOG2E)
    c_hi, c_mid, c_lo = c_hi.astype(F32), c_mid.astype(F32), c_lo.astype(F32)

    lane = lax.broadcasted_iota(jnp.int32, (t, LANES), 1)
    low = lane < HEAD_DIM
    j = lane - HEAD_DIM
    shift = shift_ref[...]

    def head_norm(pair, gain, mult):
        sq = pair * pair
        s_lo = jnp.sum(jnp.where(low, sq, 0.0), axis=-1, keepdims=True)
        s_hi = jnp.sum(jnp.where(low, 0.0, sq), axis=-1, keepdims=True)
        ms = jnp.where(low, s_lo, s_hi) * (1.0 / HEAD_DIM)
        return pair * lax.rsqrt(ms + NORM_EPS) * (gain * mult)

    for p in range(FOX_HEADS // 2):
        qp = head_norm(u[:, p * LANES:(p + 1) * LANES], qn_ref[...], HEAD_DIM ** -0.5 * LOG2E)
        kp = head_norm(u[:, FOX_W + p * LANES:FOX_W + (p + 1) * LANES], kn_ref[...], 1.0)
        vp = u[:, 2 * FOX_W + p * LANES:2 * FOX_W + (p + 1) * LANES]
        for sub in range(2):
            hd = 2 * p + sub
            if sub == 1:
                qp, kp, vp = (pltpu.roll(a, HEAD_DIM, axis=1) for a in (qp, kp, vp))
            ch, cm, cl = (jnp.broadcast_to(a[:, hd:hd + 1], (t, LANES)) for a in (c_hi, c_mid, c_lo))
            aug_q = jnp.where(j == 0, ch, jnp.where(j == 1, cm, jnp.where(j == 2, cl,
                              jnp.where(j < 7, 1.0, 0.0))))
            aug_k = jnp.where(j < 3, 1.0, jnp.where(j == 3, -ch, jnp.where(j == 4, -cm,
                              jnp.where(j == 5, -cl, jnp.where(j == 6, -shift, 0.0)))))
            aug_v = jnp.where(j == 0, 1.0, 0.0)
            q_out[hd] = jnp.where(low, qp, aug_q).astype(BF16)
            k_out[hd] = jnp.where(low, kp, aug_k).astype(BF16)
            v_out[hd] = jnp.where(low, vp, aug_v).astype(BF16)


def _fox_attn_kernel(par_ref, cq0_ref, cend_ref, q_ref, k_ref, v_ref, o_ref):
    i = pl.program_id(0)
    nh, tq, _ = q_ref.shape
    thr = par_ref[0]

    def first_block(hd):
        c0 = cq0_ref[i, hd]
        return lax.while_loop(
            lambda j: jnp.logical_and(j > 0, c0 - cend_ref[jnp.maximum(j - 1, 0), hd] >= -thr),
            lambda j: j - 1, i)

    def keys(hd, j, nblk):
        off = pl.multiple_of(j * FOX_TK, FOX_TK)
        return k_ref[hd, pl.ds(off, nblk * FOX_TK), :], v_ref[hd, pl.ds(off, nblk * FOX_TK), :]

    def causal(s):
        r = lax.broadcasted_iota(jnp.int32, s.shape, 0)
        c = lax.broadcasted_iota(jnp.int32, s.shape, 1)
        return jnp.where(c <= r, s, -jnp.inf)

    def fixed_shift():
        def block(hd, j, nblk):
            k, v = keys(hd, j, nblk)
            return _dot(jnp.exp2(_dot_nt(q_ref[hd], k)).astype(BF16), v)

        def diagonal(hd):
            k, v = keys(hd, i, 1)
            return _dot(jnp.exp2(causal(_dot_nt(q_ref[hd], k))).astype(BF16), v)

        accs = [diagonal(hd) for hd in range(nh)]
        for hd in range(nh):
            acc, j_lo = accs[hd], first_block(hd)
            n = i - j_lo
            acc = lax.cond(n % 2 == 1, lambda: acc + block(hd, j_lo, 1), lambda: acc)
            acc = lax.cond((n // 2) % 2 == 1, lambda: acc + block(hd, j_lo + n % 2, 2), lambda: acc)
            first = j_lo + n % 4
            accs[hd] = lax.fori_loop(0, n // 4, lambda p, acc: acc + block(hd, first + 4 * p, 4), acc)
        return accs

    def running_max():
        def one_head(hd):
            q = q_ref[hd]

            def step(j, carry, masked):
                m, acc = carry
                k, v = keys(hd, j, 1)
                s = _dot_nt(q, k)
                if masked:
                    s = causal(s)
                m_new = jnp.maximum(m, jnp.max(s, axis=-1, keepdims=True))
                p = jnp.exp2(s - m_new)
                return m_new, jnp.exp2(m - m_new) * acc + _dot(p.astype(BF16), v)
            init = (jnp.full((tq, 1), -jnp.inf, F32), jnp.zeros((tq, LANES), F32))
            carry = step(i, init, True)
            return lax.fori_loop(first_block(hd), i, lambda j, c: step(j, c, False), carry)[1]
        return [one_head(hd) for hd in range(nh)]

    accs = lax.cond(par_ref[1] > 0.0, fixed_shift, running_max)
    outs = [acc / acc[:, HEAD_DIM:HEAD_DIM + 1] for acc in accs]
    low = lax.broadcasted_iota(jnp.int32, (tq, LANES), 1) < HEAD_DIM
    for p in range(nh // FOX_HEADS_PER_TILE):
        o_ref[p] = jnp.where(low, outs[2 * p], pltpu.roll(outs[2 * p + 1], HEAD_DIM, axis=1)).astype(BF16)


FOX_HEADS_PER_TILE = LANES // HEAD_DIM


def _fox_attn(par, cq0, cend, q, k, v):
    assert FOX_TQ == FOX_TK and FOX_HEADS_PER_TILE == 2
    nh, seq, _ = q.shape
    kv_spec = pl.BlockSpec((nh, seq, LANES), lambda i, *_: (0, 0, 0), pipeline_mode=pl.Buffered(1))
    return pl.pallas_call(
        _fox_attn_kernel,
        grid_spec=pltpu.PrefetchScalarGridSpec(
            num_scalar_prefetch=3,
            grid=(seq // FOX_TQ,),
            in_specs=[pl.BlockSpec((nh, FOX_TQ, LANES), lambda i, *_: (0, i, 0)), kv_spec, kv_spec],
            out_specs=pl.BlockSpec((nh // FOX_HEADS_PER_TILE, FOX_TQ, LANES), lambda i, *_: (0, i, 0))),
        out_shape=jax.ShapeDtypeStruct((nh // FOX_HEADS_PER_TILE, seq, LANES), BF16),
        compiler_params=pltpu.CompilerParams(dimension_semantics=("arbitrary",),
                                             vmem_limit_bytes=VMEM_LIMIT),
        name="fox_attn",
    )(par, cq0, cend, q, k, v)


def _fox_params(qn, kn):
    bound = 1.02 * HEAD_DIM ** 0.5 * jnp.max(jnp.abs(qn), axis=1) * jnp.max(jnp.abs(kn), axis=1)
    shift = jnp.broadcast_to((LOG2E * bound)[:, None, None], (bound.shape[0], 1, LANES))
    par = jnp.stack([F32_EXP_ZERO + 2.0 * bound, (bound < FOX_FAST_MAX_LOGIT).astype(F32)], axis=1)
    return shift, par


def _fox_attention(par, q, k, v, c):
    per_tile = FOX_TQ // ROW_TILE
    cq0 = c[0::per_tile, 0, 0:FOX_HEADS]
    cend = c[per_tile - 1::per_tile, 1, 0:FOX_HEADS]
    return _fox_attn(par, cq0, cend, q, k, v)


def _linear_attn_chunks(qd_s, ki_s, ke_s, v_s, dec_s, o_s, st_s, nheads):
    t = qd_s.shape[0]
    r = lax.broadcasted_iota(jnp.int32, (CHUNK, CHUNK), 0)
    c = lax.broadcasted_iota(jnp.int32, (CHUNK, CHUNK), 1)
    tril = c <= r
    states = [st_s[hd] for hd in range(nheads)]
    for ci in range(t // CHUNK):
        rows = slice(ci * CHUNK, (ci + 1) * CHUNK)
        last = slice((ci + 1) * CHUNK - 1, (ci + 1) * CHUNK)
        for hd in range(nheads):
            cols = slice(hd * HEAD_DIM, (hd + 1) * HEAD_DIM)
            qd = qd_s[rows, cols].astype(BF16)
            ki = ki_s[rows, cols].astype(BF16)
            ke = ke_s[rows, cols].astype(BF16)
            vv = v_s[rows, cols].astype(BF16)
            attn = jnp.where(tril, _dot_nt(qd, ki), 0.0).astype(BF16)
            o_s[rows, cols] = _dot(attn, vv) + _dot_nt(qd, states[hd].astype(BF16))
            states[hd] = states[hd] * dec_s[last, cols] + _dot_tn(vv, ke)
        yield
    for hd in range(nheads):
        st_s[hd] = states[hd]


def _gla_body(h, small, w_ref, w2_ref, b2_ref, gn_ref, y_ref, qd_s, ki_s, ke_s, v_s, dec_s, o_s, st_s):
    u = yield from _project(h, w_ref)
    z = _dot(small.astype(BF16), w2_ref[...]) + b2_ref[...]
    log_a = _log_sigmoid(z) * (1.0 / GLA_TAU)
    b, rem = _chunk_cumsum(log_a)
    eb = jnp.exp(b)
    k = u[:, GLA_W:2 * GLA_W]
    qd_s[...] = u[:, 0:GLA_W] * (HEAD_DIM ** -0.5) * eb
    ki_s[...] = k * jnp.exp(-b)
    ke_s[...] = k * jnp.exp(rem)
    v_s[...] = u[:, 2 * GLA_W:3 * GLA_W]
    dec_s[...] = eb
    yield
    yield from _linear_attn_chunks(qd_s, ki_s, ke_s, v_s, dec_s, o_s, st_s, GLA_HEADS)
    o = o_s[...]
    y = o * lax.rsqrt(_group_mean_sq(o, HEAD_DIM) + NORM_EPS) * gn_ref[...]
    y_ref[...] = (y * _silu(u[:, 3 * GLA_W:4 * GLA_W])).astype(BF16)


def _ret_body(h, w_ref, cos_ref, sin_ref, gn_ref, y_ref, qd_s, ki_s, ke_s, v_s, dec_s, o_s, st_s):
    t = h.shape[0]
    u = yield from _project(h, w_ref)
    cos, sin = cos_ref[...], sin_ref[...]
    lane = lax.broadcasted_iota(jnp.int32, (t, RET_W), 1)
    half = HEAD_DIM // 2
    first_half = (lax.broadcasted_iota(jnp.int32, (t, LANES), 1) % HEAD_DIM) < half

    def rotate(a):
        blocks = []
        for b in range(0, RET_W, LANES):
            ab = a[:, b:b + LANES]
            swapped = jnp.where(first_half, pltpu.roll(ab, LANES - half, axis=1),
                                pltpu.roll(ab, half, axis=1))
            blocks.append(ab * cos + swapped * sin)
        return jnp.concatenate(blocks, axis=1)

    q = rotate(u[:, 0:RET_W])
    k = rotate(u[:, RET_W:2 * RET_W]) * (HEAD_DIM ** -0.5)
    row = lax.broadcasted_iota(jnp.int32, (t, RET_W), 0)
    lg = jnp.zeros((t, RET_W), F32)
    for hd in range(RET_HEADS):
        lg = jnp.where(lane // HEAD_DIM == hd, math.log(1.0 - 2.0 ** (-5.0 - hd)), lg)
    pos = (row % CHUNK).astype(F32)
    qw = jnp.exp((pos + 1.0) * lg)
    qd_s[...] = q * qw
    ki_s[...] = k * jnp.exp(-(pos + 1.0) * lg)
    ke_s[...] = k * jnp.exp((CHUNK - 1.0 - pos) * lg)
    v_s[...] = u[:, 2 * RET_W:3 * RET_W]
    dec_s[...] = qw
    yield
    yield from _linear_attn_chunks(qd_s, ki_s, ke_s, v_s, dec_s, o_s, st_s, RET_HEADS)
    o = o_s[...]
    y = o * lax.rsqrt(_group_mean_sq(o, HEAD_DIM) + NORM_EPS) * gn_ref[...]
    y_ref[...] = (y * _silu(u[:, 3 * RET_W:4 * RET_W])).astype(BF16)


def _linear_scratch(t, width, nheads):
    return ([pltpu.VMEM((t, width), F32)] * 6
            + [pltpu.VMEM((nheads, HEAD_DIM, HEAD_DIM), F32)])


SSD_PROJ = SSD_INNER + SSD_CONV_CH


def _ssd_body(h, small, w_ref, cw_ref, cb_ref, dtb_row_ref, alog_row_ref, d_row_ref, gn_ref, y_ref,
              xp_s, xs_s, bc_s, cs_s, ecs_s, xdt_s, xdtw_s, cst_s, o_s, st_s):
    t = h.shape[0]
    nc = t // CHUNK
    u = yield from _project(h, w_ref)
    z = u[:, 0:SSD_INNER]

    xp_s[SUBLANES:SUBLANES + t, :] = u[:, SSD_INNER:SSD_PROJ]
    conv = cb_ref[...] + cw_ref[SSD_CONV - 1:SSD_CONV, :] * xp_s[SUBLANES:SUBLANES + t, :]
    for kk in range(SSD_CONV - 1):
        off = SUBLANES - (SSD_CONV - 1) + kk
        conv = conv + cw_ref[kk:kk + 1, :] * xp_s[off:off + t, :]
    xp_s[0:SUBLANES, :] = xp_s[t:t + SUBLANES, :]
    xbc = _silu(conv)
    xs = xbc[:, 0:SSD_INNER]
    xs_s[...] = xs
    bc_s[...] = xbc[:, SSD_INNER:SSD_CONV_CH]

    dt = _softplus(small + dtb_row_ref[...])
    dta = dt * -jnp.exp(alog_row_ref[...])
    cs, _ = _chunk_cumsum(dta)
    cs_t = cs.T[SMALL_DT:SMALL_DT + SSD_HEADS, :]
    for ci in range(nc):
        cst_s[ci] = cs_t[:, ci * CHUNK:(ci + 1) * CHUNK]

    er = lax.broadcasted_iota(jnp.int32, (LANES, SSD_INNER), 0)
    ec = lax.broadcasted_iota(jnp.int32, (LANES, SSD_INNER), 1)
    expand = jnp.where(ec // HEAD_DIM == er - SMALL_DT, 1.0, 0.0).astype(BF16)
    cs_x = _dot_sel2(cs, expand)
    xdt = xs * _dot_sel2(dt, expand)
    cs_s[...] = cs_x
    ecs_s[...] = jnp.exp(cs_x)
    xdt_s[...] = xdt
    xdtw_s[...] = xdt * jnp.exp(_chunk_last(cs_x) - cs_x)

    r = lax.broadcasted_iota(jnp.int32, (CHUNK, CHUNK), 0)
    c = lax.broadcasted_iota(jnp.int32, (CHUNK, CHUNK), 1)
    tril = c <= r
    gw = SSD_GROUPS * SSD_STATE

    hpg = SSD_HEADS // SSD_GROUPS
    gcols = hpg * HEAD_DIM
    states = [st_s[g] for g in range(SSD_GROUPS)]
    for ci in range(nc):
        rows = slice(ci * CHUNK, (ci + 1) * CHUNK)
        last = slice((ci + 1) * CHUNK - 1, (ci + 1) * CHUNK)
        cst = cst_s[ci]
        for g in range(SSD_GROUPS):
            gsl = slice(g * gcols, (g + 1) * gcols)
            bm = bc_s[rows, g * SSD_STATE:(g + 1) * SSD_STATE].astype(BF16)
            cm = bc_s[rows, gw + g * SSD_STATE:gw + (g + 1) * SSD_STATE].astype(BF16)
            cb = _dot_nt(cm, bm)
            y_inter = _dot(cm, states[g].astype(BF16)) * ecs_s[rows, gsl]
            for rr in range(hpg):
                hd = g * hpg + rr
                cols = slice(hd * HEAD_DIM, (hd + 1) * HEAD_DIM)
                seg = cs_s[rows, cols] - cst[hd:hd + 1, :]
                decay = jnp.exp(jnp.where(tril, seg, -jnp.inf))
                y = _dot((cb * decay).astype(BF16), xdt_s[rows, cols].astype(BF16))
                o_s[rows, cols] = y + y_inter[:, rr * HEAD_DIM:(rr + 1) * HEAD_DIM]
            states[g] = (states[g] * ecs_s[last, gsl]
                         + _dot_tn(bm, xdtw_s[rows, gsl].astype(BF16)))
        yield
    for g in range(SSD_GROUPS):
        st_s[g] = states[g]

    y = (o_s[...] + d_row_ref[...] * xs_s[...]) * _silu(z)
    gwid = SSD_INNER // SSD_GROUPS
    normed = []
    for g in range(SSD_GROUPS):
        yg = y[:, g * gwid:(g + 1) * gwid]
        normed.append(yg * lax.rsqrt(jnp.mean(yg * yg, axis=-1, keepdims=True) + NORM_EPS))
    y_ref[...] = (jnp.concatenate(normed, axis=1) * gn_ref[...]).astype(BF16)


N_FOX_IN, N_GLA_IN, N_RET_IN, N_SSD_IN = 5, 4, 4, 7
N_LINEAR_SCRATCH = 7
N_SSD_SCRATCH = 10


def _mixers_kernel(x_ref, ln_ref, ws_ref, *refs):
    it = iter(refs)
    take = lambda n: [next(it) for _ in range(n)]
    fox_in, gla_in, ret_in, ssd_in = take(N_FOX_IN), take(N_GLA_IN), take(N_RET_IN), take(N_SSD_IN)
    fox_out, (yb_ref, yc_ref, yd_ref) = take(4), take(3)
    (carry_ref,), gla_s, ret_s, ssd_s = take(1), take(N_LINEAR_SCRATCH), take(N_LINEAR_SCRATCH), take(N_SSD_SCRATCH)

    @pl.when(pl.program_id(0) == 0)
    def _():
        carry_ref[...] = jnp.zeros_like(carry_ref)
        gla_s[-1][...] = jnp.zeros_like(gla_s[-1])
        ret_s[-1][...] = jnp.zeros_like(ret_s[-1])
        ssd_s[-1][...] = jnp.zeros_like(ssd_s[-1])
        ssd_s[0][0:SUBLANES, :] = jnp.zeros((SUBLANES, SSD_CONV_CH), F32)

    h = _rmsnorm(x_ref[...], ln_ref[...]).astype(BF16)
    small = _dot_nt(h, ws_ref[...])
    waiting = [_ssd_body(h, small, *ssd_in, yd_ref, *ssd_s), _gla_body(h, small, *gla_in, yb_ref, *gla_s),
               _ret_body(h, *ret_in, yc_ref, *ret_s), _fox_prep_body(h, small, *fox_in, *fox_out, carry_ref)]
    pending = []
    while waiting or pending:
        if waiting:
            pending.append(waiting.pop(0))
        for body in list(pending):
            if next(body, "done") == "done":
                pending.remove(body)


def _mixers(l, x, ln, w_small, fox_in, gla_in, ret_in, ssd_in, cos_t, sin_t):
    seq = x.shape[0]
    t = ROW_TILE
    row = lambda w: pl.BlockSpec((t, w), lambda i: (i, 0))
    heads = pl.BlockSpec((FOX_HEADS, t, LANES), lambda i: (0, i, 0))
    head_shape = jax.ShapeDtypeStruct((FOX_HEADS, seq, LANES), BF16)
    w_ret, ret_gn = ret_in
    wide = pltpu.VMEM((t, SSD_INNER), F32)
    return pl.pallas_call(
        _mixers_kernel,
        grid=(seq // t,),
        in_specs=([row(D_MODEL), _layer(ln, l), _layer(w_small, l)] + [_layer(a, l) for a in fox_in]
                  + [_layer(a, l) for a in gla_in]
                  + [_layer(w_ret, l), row(LANES), row(LANES), _layer(ret_gn, l)]
                  + [_layer(a, l) for a in ssd_in]),
        out_specs=[heads, heads, heads, pl.BlockSpec((1, SUBLANES, LANES), lambda i: (i, 0, 0)),
                   row(GLA_W), row(RET_W), row(SSD_INNER)],
        out_shape=[head_shape, head_shape, head_shape,
                   jax.ShapeDtypeStruct((seq // t, SUBLANES, LANES), F32),
                   jax.ShapeDtypeStruct((seq, GLA_W), BF16),
                   jax.ShapeDtypeStruct((seq, RET_W), BF16),
                   jax.ShapeDtypeStruct((seq, SSD_INNER), BF16)],
        scratch_shapes=([pltpu.VMEM((SUBLANES, LANES), F32)]
                        + _linear_scratch(t, GLA_W, GLA_HEADS) + _linear_scratch(t, RET_W, RET_HEADS)
                        + [pltpu.VMEM((t + 2 * SUBLANES, SSD_CONV_CH), F32),
                           wide,
                           pltpu.VMEM((t, 2 * SSD_GROUPS * SSD_STATE), F32),
                           wide, wide, wide, wide,
                           pltpu.VMEM((t // CHUNK, SSD_HEADS, CHUNK), F32),
                           wide,
                           pltpu.VMEM((SSD_GROUPS, SSD_STATE, SSD_INNER // SSD_GROUPS), F32)]),
        compiler_params=pltpu.CompilerParams(dimension_semantics=("arbitrary",),
                                             vmem_limit_bytes=VMEM_LIMIT),
        name="mixers",
    )(x, ln, w_small, *fox_in, *gla_in, w_ret, cos_t, sin_t, ret_gn, *ssd_in)


def _merge_ffn_kernel(x_ref, ln1_ref, wg_ref, ya_ref, yb_ref, yc_ref, yd_ref,
                      wa_ref, wb_ref, wc_ref, wd_ref, wo_ref, ln2_ref, w1_ref, w2_ref, out_ref):
    x = x_ref[...]
    h = _rmsnorm(x, ln1_ref[...]).astype(BF16)

    def gate(b):
        return _sigmoid(_dot_nt(h, wg_ref[b * D_MODEL:(b + 1) * D_MODEL, :]))

    up_a = _dot(ya_ref[0], wa_ref[0:LANES, :])
    for p in range(1, FOX_HEADS // FOX_HEADS_PER_TILE):
        up_a = up_a + _dot(ya_ref[p], wa_ref[p * LANES:(p + 1) * LANES, :])
    merged = gate(0) * up_a
    merged = merged + gate(1) * _dot(yb_ref[...], wb_ref[...])
    merged = merged + gate(2) * _dot(yc_ref[...], wc_ref[...])
    merged = merged + gate(3) * _dot(yd_ref[...], wd_ref[...])
    x = x + _dot(merged.astype(BF16), wo_ref[...])

    h = _rmsnorm(x, ln2_ref[...]).astype(BF16)
    act = _silu(_dot(h, w1_ref[:, 0:FFN_HIDDEN])) * _dot(h, w1_ref[:, FFN_HIDDEN:2 * FFN_HIDDEN])
    out_ref[...] = x + _dot(act.astype(BF16), w2_ref[...])


def _merge_ffn(l, x, ln1, w_gates, y_a, y_b, y_c, y_d, w_up_a, w_up_b, w_up_c, w_up_d, w_out, ln2, w1, w2):
    seq = x.shape[0]
    t = FFN_TILE
    row = lambda w: pl.BlockSpec((t, w), lambda i: (i, 0))
    return pl.pallas_call(
        _merge_ffn_kernel,
        grid=(seq // t,),
        in_specs=[row(D_MODEL), _layer(ln1, l), _layer(w_gates, l),
                  pl.BlockSpec((FOX_HEADS // FOX_HEADS_PER_TILE, t, LANES), lambda i: (0, i, 0)),
                  row(GLA_W), row(RET_W), row(SSD_INNER),
                  _layer(w_up_a, l), _layer(w_up_b, l), _layer(w_up_c, l),
                  _layer(w_up_d, l), _layer(w_out, l), _layer(ln2, l), _layer(w1, l), _layer(w2, l)],
        out_specs=row(D_MODEL),
        out_shape=jax.ShapeDtypeStruct((seq, D_MODEL), F32),
        compiler_params=pltpu.CompilerParams(dimension_semantics=("arbitrary",),
                                             vmem_limit_bytes=VMEM_LIMIT),
        name="merge_swiglu",
    )(x, ln1, w_gates, y_a, y_b, y_c, y_d, w_up_a, w_up_b, w_up_c, w_up_d, w_out, ln2, w1, w2)


def _pad_last(a, width):
    return jnp.pad(a, [(0, 0)] * (a.ndim - 1) + [(0, width - a.shape[-1])])


def _rows(a):
    return a.reshape(a.shape[0], 1, -1)


def _in_proj_weights(w_in):
    wt = jnp.transpose(w_in, (0, 2, 1)).astype(BF16)
    offs = np.concatenate([[0], np.cumsum(IN_SPLITS)])
    (fq, fk, fv, ff, gq, gk, gv, glr, gr, rq, rk, rv, rg, z, xbc, dt, gates) = [
        wt[:, offs[n]:offs[n + 1], :] for n in range(len(IN_SPLITS))]
    cat = lambda parts: jnp.concatenate(parts, axis=1)
    gap = lambda n: jnp.zeros((wt.shape[0], n, wt.shape[2]), BF16)
    w_small = cat([ff, glr, gap(SMALL_DT - SMALL_GLR - GLA_LOWRANK), dt, gap(LANES - SMALL_DT - SSD_HEADS)])
    return w_small, cat([fq, fk, fv]), cat([gq, gk, gv, gr]), cat([rq, rk, rv, rg]), cat([z, xbc]), gates


def kernel(x, positions, ln1, ln2, w_in, fox_bf, fox_qn, fox_kn, gla_w2, gla_b, gla_norm, ret_norm,
           ssd_conv_w, ssd_conv_b, ssd_dt_bias, ssd_a_log, ssd_d, ssd_norm,
           w_up_a, w_up_b, w_up_c, w_up_d, w_out, w_ffn_in, w_ffn_out):
    bsz, seq, d = x.shape
    assert bsz == 1 and d == D_MODEL and seq % ROW_TILE == 0 and seq % FOX_TQ == 0 and FOX_TQ % ROW_TILE == 0
    depth = ln1.shape[0]
    xr = x.reshape(seq, d)
    cos_t, sin_t = _rotary_tables(positions, seq)

    bf16 = lambda w: w.astype(BF16)
    ln1_r, ln2_r = _rows(ln1), _rows(ln2)
    fox_bf_r = _rows(_pad_last(fox_bf, LANES))
    fox_qn_r, fox_kn_r = _rows(jnp.tile(fox_qn, (1, 2))), _rows(jnp.tile(fox_kn, (1, 2)))
    fox_shift_r, fox_par = _fox_params(fox_qn, fox_kn)
    gla_w2_p = bf16(jnp.pad(gla_w2, ((0, 0), (SMALL_GLR, LANES - SMALL_GLR - GLA_LOWRANK), (0, 0))))
    gla_b_r, gla_gn_r = _rows(gla_b), _rows(jnp.tile(gla_norm, (1, GLA_HEADS)))
    ret_gn_r = _rows(jnp.tile(ret_norm, (1, RET_HEADS)))
    conv_b_r = _rows(ssd_conv_b)
    at_dt = lambda a: _rows(jnp.pad(a, ((0, 0), (SMALL_DT, LANES - SMALL_DT - SSD_HEADS))))
    dtb_r, alog_r = at_dt(ssd_dt_bias), at_dt(ssd_a_log)
    d_r, ssd_gn_r = _rows(jnp.repeat(ssd_d, HEAD_DIM, axis=1)), _rows(ssd_norm)
    up_a, up_b, up_c, up_d, w_o = bf16(w_up_a), bf16(w_up_b), bf16(w_up_c), bf16(w_up_d), bf16(w_out)
    w1, w2 = bf16(w_ffn_in), bf16(w_ffn_out)
    w_small, w_fox, w_gla, w_ret, w_ssd, w_gates = _in_proj_weights(w_in)

    for l in range(depth):
        q, k, v, c, y_b, y_c, y_d = _mixers(
            l, xr, ln1_r, w_small,
            (w_fox, fox_bf_r, fox_qn_r, fox_kn_r, fox_shift_r),
            (w_gla, gla_w2_p, gla_b_r, gla_gn_r),
            (w_ret, ret_gn_r),
            (w_ssd, ssd_conv_w, conv_b_r, dtb_r, alog_r, d_r, ssd_gn_r),
            cos_t, sin_t)
        y_a = _fox_attention(fox_par[l], q, k, v, c)
        xr = _merge_ffn(l, xr, ln1_r, w_gates, y_a, y_b, y_c, y_d, up_a, up_b, up_c, up_d, w_o, ln2_r, w1, w2)
    return xr.reshape(bsz, seq, d)
```

```python
import math

import numpy as np
import jax
import jax.numpy as jnp
from jax import lax
from jax.experimental import pallas as pl
from jax.experimental.pallas import tpu as pltpu

D_MODEL = 1024
HEAD_DIM = 64
CHUNK = 64
FOX_HEADS = 4
GLA_HEADS = 4
GLA_LOWRANK = 16
GLA_TAU = 16.0
RET_HEADS = 4
ROPE_THETA = 10000.0
SSD_HEADS = 8
SSD_GROUPS = 2
SSD_STATE = 64
SSD_CONV = 4
SSD_INNER = SSD_HEADS * HEAD_DIM
SSD_CONV_CH = SSD_INNER + 2 * SSD_GROUPS * SSD_STATE
FOX_W = FOX_HEADS * HEAD_DIM
GLA_W = GLA_HEADS * HEAD_DIM
RET_W = RET_HEADS * HEAD_DIM
N_BRANCH = 4
FFN_HIDDEN = ((8 * D_MODEL + 3 * 256 - 1) // (3 * 256)) * 256
NORM_EPS = 1e-6
IN_SPLITS = (FOX_W, FOX_W, FOX_W, FOX_HEADS,
             GLA_W, GLA_W, GLA_W, GLA_LOWRANK, GLA_W,
             RET_W, RET_W, RET_W, RET_W,
             SSD_INNER, SSD_CONV_CH, SSD_HEADS,
             N_BRANCH * D_MODEL)

LANES = 128
SUBLANES = 8
VMEM_LIMIT = 56 * 1024 * 1024

ROW_TILE = 512
FOX_TQ = 512
FOX_TK = 512
FFN_TILE = 512
PROJ_PIECE = 512

F32 = jnp.float32
BF16 = jnp.bfloat16


def _rmsnorm(x, g):
    return x * lax.rsqrt(jnp.mean(x * x, axis=-1, keepdims=True) + NORM_EPS) * g


def _log_sigmoid(x):
    return jnp.minimum(x, 0.0) - jnp.log(1.0 + jnp.exp(-jnp.abs(x)))


def _softplus(x):
    return jnp.maximum(x, 0.0) + jnp.log(1.0 + jnp.exp(-jnp.abs(x)))


def _sigmoid(x):
    return 0.5 * jnp.tanh(0.5 * x) + 0.5


def _silu(x):
    return x * _sigmoid(x)


def _split3(x):
    hi = x.astype(BF16)
    r1 = x - hi.astype(F32)
    mid = r1.astype(BF16)
    lo = (r1 - mid.astype(F32)).astype(BF16)
    return hi, mid, lo


def _split2(x):
    hi = x.astype(BF16)
    return hi, (x - hi.astype(F32)).astype(BF16)


def _dot(a, b):
    return jnp.dot(a, b, preferred_element_type=F32)


def _dot_nt(a, b):
    return lax.dot_general(a, b, (((1,), (1,)), ((), ())), preferred_element_type=F32)


def _dot_tn(a, b):
    return lax.dot_general(a, b, (((0,), (0,)), ((), ())), preferred_element_type=F32)


def _sel_dot(mat, x):
    hi, mid, lo = _split3(x)
    return _dot(mat, hi) + _dot(mat, mid) + _dot(mat, lo)


def _dot_sel2(x, mat):
    hi, lo = _split2(x)
    return _dot(hi, mat) + _dot(lo, mat)


def _chunk_cumsum(x):
    t = x.shape[0]
    r = lax.broadcasted_iota(jnp.int32, (t, t), 0)
    c = lax.broadcasted_iota(jnp.int32, (t, t), 1)
    incl = jnp.where(((r // CHUNK) == (c // CHUNK)) & (c <= r), 1.0, 0.0).astype(BF16)
    hi, lo = _split2(x)
    b = _dot(incl, hi) + _dot(incl, lo)
    return b, _chunk_last(b) - b


def _chunk_last(b):
    t, w = b.shape
    return jnp.concatenate([jnp.broadcast_to(b[e - 1:e, :], (CHUNK, w)) for e in range(CHUNK, t + 1, CHUNK)],
                           axis=0)


def _group_mean_sq(o, group):
    w = o.shape[-1]
    r = lax.broadcasted_iota(jnp.int32, (w, w), 0)
    c = lax.broadcasted_iota(jnp.int32, (w, w), 1)
    bd = jnp.where((r // group) == (c // group), 1.0, 0.0).astype(BF16)
    return _dot_sel2(o * o, bd) * (1.0 / group)


def _project(h, wt_ref):
    n = wt_ref.shape[0]
    parts = []
    for a in range(0, n, PROJ_PIECE):
        parts.append(_dot_nt(h, wt_ref[a:min(a + PROJ_PIECE, n), :]))
        yield
    return jnp.concatenate(parts, axis=1)


def _layer(stacked, l):
    tail = tuple(stacked.shape[1:])
    return pl.BlockSpec((None,) + tail, lambda *_: (l,) + (0,) * len(tail), pipeline_mode=pl.Buffered(1))


def _rotary_kernel(pos_ref, inv_ref, sign_ref, cos_ref, sin_ref):
    ang = pos_ref[...] * inv_ref[...]
    cos_ref[...] = jnp.cos(ang)
    sin_ref[...] = jnp.sin(ang) * sign_ref[...]


def _rotary_tables(positions, seq):
    half = HEAD_DIM // 2
    inv = ROPE_THETA ** (-jnp.arange(half, dtype=F32) / half)
    reps = LANES // HEAD_DIM
    inv_row = jnp.tile(inv, 2 * reps).reshape(1, LANES)
    sign_row = jnp.tile(jnp.concatenate([-jnp.ones((half,), F32), jnp.ones((half,), F32)]),
                        reps).reshape(1, LANES)
    pos = positions.astype(F32).reshape(seq, 1)
    t = ROW_TILE
    return pl.pallas_call(
        _rotary_kernel,
        grid=(seq // t,),
        in_specs=[pl.BlockSpec((t, 1), lambda i: (i, 0)),
                  pl.BlockSpec((1, LANES), lambda i: (0, 0)),
                  pl.BlockSpec((1, LANES), lambda i: (0, 0))],
        out_specs=[pl.BlockSpec((t, LANES), lambda i: (i, 0)),
                   pl.BlockSpec((t, LANES), lambda i: (i, 0))],
        out_shape=[jax.ShapeDtypeStruct((seq, LANES), F32)] * 2,
        name="rotary_tables",
    )(pos, inv_row, sign_row)


SMALL_FF = 0
SMALL_GLR = FOX_HEADS
SMALL_DT = 24
LOG2E = math.log2(math.e)
F32_EXP_ZERO = 88.0
FOX_FAST_MAX_LOGIT = 40.0
FOX_BOUND_MARGIN = 1.02


def _fox_prep_body(h, small, w_ref, bf_ref, qn_ref, kn_ref, shift_ref, q_out, k_out, v_out, c_out, carry_ref):
    t = h.shape[0]
    u = yield from _project(h, w_ref)
    ls = _log_sigmoid(small + bf_ref[...])
    r = lax.broadcasted_iota(jnp.int32, (t, t), 0)
    cc = lax.broadcasted_iota(jnp.int32, (t, t), 1)
    tri = jnp.where(cc <= r, 1.0, 0.0).astype(BF16)
    c = _sel_dot(tri, ls) + carry_ref[0:1, :]
    carry_ref[0:1, :] = c[t - 1:t, :]
    c_out[0] = jnp.concatenate([c[0:1, :], c[t - 1:t, :], jnp.zeros((SUBLANES - 2, LANES), F32)], axis=0)
    c_hi, c_mid, c_lo = _split3(c * LOG2E)
    c_hi, c_mid, c_lo = c_hi.astype(F32), c_mid.astype(F32), c_lo.astype(F32)

    lane = lax.broadcasted_iota(jnp.int32, (t, LANES), 1)
    low = lane < HEAD_DIM
    j = lane - HEAD_DIM
    shift = shift_ref[...]

    def head_norm(pair, gain, mult):
        sq = pair * pair
        s_lo = jnp.sum(jnp.where(low, sq, 0.0), axis=-1, keepdims=True)
        s_hi = jnp.sum(jnp.where(low, 0.0, sq), axis=-1, keepdims=True)
        ms = jnp.where(low, s_lo, s_hi) * (1.0 / HEAD_DIM)
        return pair * lax.rsqrt(ms + NORM_EPS) * (gain * mult)

    for p in range(FOX_HEADS // 2):
        qp = head_norm(u[:, p * LANES:(p + 1) * LANES], qn_ref[...], HEAD_DIM ** -0.5 * LOG2E)
        kp = head_norm(u[:, FOX_W + p * LANES:FOX_W + (p + 1) * LANES], kn_ref[...], 1.0)
        vp = u[:, 2 * FOX_W + p * LANES:2 * FOX_W + (p + 1) * LANES]
        for sub in range(2):
            hd = 2 * p + sub
            if sub == 1:
                qp, kp, vp = (pltpu.roll(a, HEAD_DIM, axis=1) for a in (qp, kp, vp))
            ch, cm, cl = (jnp.broadcast_to(a[:, SMALL_FF + hd:SMALL_FF + hd + 1], (t, LANES)) for a in (c_hi, c_mid, c_lo))
            aug_q = jnp.where(j == 0, ch, jnp.where(j == 1, cm, jnp.where(j == 2, cl,
                              jnp.where(j < 7, 1.0, 0.0))))
            aug_k = jnp.where(j < 3, 1.0, jnp.where(j == 3, -ch, jnp.where(j == 4, -cm,
                              jnp.where(j == 5, -cl, jnp.where(j == 6, -shift, 0.0)))))
            aug_v = jnp.where(j == 0, 1.0, 0.0)
            q_out[hd] = jnp.where(low, qp, aug_q).astype(BF16)
            k_out[hd] = jnp.where(low, kp, aug_k).astype(BF16)
            v_out[hd] = jnp.where(low, vp, aug_v).astype(BF16)


def _fox_attn_kernel(par_ref, cq0_ref, cend_ref, q_ref, k_ref, v_ref, o_ref):
    i = pl.program_id(0)
    nh, tq, _ = q_ref.shape
    thr = par_ref[0]

    def first_block(hd):
        c0 = cq0_ref[i, hd]
        return lax.while_loop(
            lambda j: jnp.logical_and(j > 0, c0 - cend_ref[jnp.maximum(j - 1, 0), hd] >= -thr),
            lambda j: j - 1, i)

    def keys(hd, j, nblk):
        off = pl.multiple_of(j * FOX_TK, FOX_TK)
        return k_ref[hd, pl.ds(off, nblk * FOX_TK), :], v_ref[hd, pl.ds(off, nblk * FOX_TK), :]

    def causal(s):
        r = lax.broadcasted_iota(jnp.int32, s.shape, 0)
        c = lax.broadcasted_iota(jnp.int32, s.shape, 1)
        return jnp.where(c <= r, s, -jnp.inf)

    def fixed_shift():
        def block(hd, j, nblk):
            k, v = keys(hd, j, nblk)
            return _dot(jnp.exp2(_dot_nt(q_ref[hd], k)).astype(BF16), v)

        def diagonal(hd):
            k, v = keys(hd, i, 1)
            return _dot(jnp.exp2(causal(_dot_nt(q_ref[hd], k))).astype(BF16), v)

        accs = [diagonal(hd) for hd in range(nh)]
        for hd in range(nh):
            acc, j_lo = accs[hd], first_block(hd)
            n = i - j_lo
            acc = lax.cond(n % 2 == 1, lambda: acc + block(hd, j_lo, 1), lambda: acc)
            acc = lax.cond((n // 2) % 2 == 1, lambda: acc + block(hd, j_lo + n % 2, 2), lambda: acc)
            first = j_lo + n % 4
            accs[hd] = lax.fori_loop(0, n // 4, lambda p, acc: acc + block(hd, first + 4 * p, 4), acc)
        return accs

    def running_max():
        def one_head(hd):
            q = q_ref[hd]

            def step(j, carry, masked):
                m, acc = carry
                k, v = keys(hd, j, 1)
                s = _dot_nt(q, k)
                if masked:
                    s = causal(s)
                m_new = jnp.maximum(m, jnp.max(s, axis=-1, keepdims=True))
                p = jnp.exp2(s - m_new)
                return m_new, jnp.exp2(m - m_new) * acc + _dot(p.astype(BF16), v)
            init = (jnp.full((tq, 1), -jnp.inf, F32), jnp.zeros((tq, LANES), F32))
            carry = step(i, init, True)
            return lax.fori_loop(first_block(hd), i, lambda j, c: step(j, c, False), carry)[1]
        return [one_head(hd) for hd in range(nh)]

    accs = lax.cond(par_ref[1] > 0.0, fixed_shift, running_max)
    outs = [acc / acc[:, HEAD_DIM:HEAD_DIM + 1] for acc in accs]
    low = lax.broadcasted_iota(jnp.int32, (tq, LANES), 1) < HEAD_DIM
    for p in range(nh // FOX_HEADS_PER_TILE):
        o_ref[p] = jnp.where(low, outs[2 * p], pltpu.roll(outs[2 * p + 1], HEAD_DIM, axis=1)).astype(BF16)


FOX_HEADS_PER_TILE = LANES // HEAD_DIM


def _fox_attn(par, cq0, cend, q, k, v):
    assert FOX_TQ == FOX_TK and FOX_HEADS_PER_TILE == 2
    nh, seq, _ = q.shape
    kv_spec = pl.BlockSpec((nh, seq, LANES), lambda i, *_: (0, 0, 0), pipeline_mode=pl.Buffered(1))
    return pl.pallas_call(
        _fox_attn_kernel,
        grid_spec=pltpu.PrefetchScalarGridSpec(
            num_scalar_prefetch=3,
            grid=(seq // FOX_TQ,),
            in_specs=[pl.BlockSpec((nh, FOX_TQ, LANES), lambda i, *_: (0, i, 0)), kv_spec, kv_spec],
            out_specs=pl.BlockSpec((nh // FOX_HEADS_PER_TILE, FOX_TQ, LANES), lambda i, *_: (0, i, 0))),
        out_shape=jax.ShapeDtypeStruct((nh // FOX_HEADS_PER_TILE, seq, LANES), BF16),
        compiler_params=pltpu.CompilerParams(dimension_semantics=("arbitrary",),
                                             vmem_limit_bytes=VMEM_LIMIT),
        name="fox_attn",
    )(par, cq0, cend, q, k, v)


def _fox_params(qn, kn):
    bound = FOX_BOUND_MARGIN * HEAD_DIM ** 0.5 * jnp.max(jnp.abs(qn), axis=1) * jnp.max(jnp.abs(kn), axis=1)
    shift = jnp.broadcast_to((LOG2E * bound)[:, None, None], (bound.shape[0], 1, LANES))
    par = jnp.stack([F32_EXP_ZERO + 2.0 * bound, (bound < FOX_FAST_MAX_LOGIT).astype(F32)], axis=1)
    return shift, par


def _fox_attention(par, q, k, v, c):
    per_tile = FOX_TQ // ROW_TILE
    heads = slice(SMALL_FF, SMALL_FF + FOX_HEADS)
    cq0 = c[0::per_tile, 0, heads]
    cend = c[per_tile - 1::per_tile, 1, heads]
    return _fox_attn(par, cq0, cend, q, k, v)


def _tile_rows(a, n):
    return jnp.concatenate([a] * n, axis=0)


def _linear_attn_chunks(qd_s, ki_s, ke_s, v_s, dec_s, o_s, st_s, nheads):
    t, w = qd_s.shape
    assert w == nheads * HEAD_DIM
    r = lax.broadcasted_iota(jnp.int32, (w, w), 0)
    c = lax.broadcasted_iota(jnp.int32, (w, w), 1)
    same_head = (r // HEAD_DIM) == (c // HEAD_DIM)
    rr = lax.broadcasted_iota(jnp.int32, (CHUNK, w), 0)
    cc = lax.broadcasted_iota(jnp.int32, (CHUNK, w), 1)
    tril = (cc % CHUNK) <= rr
    state = st_s[...]
    for ci in range(t // CHUNK):
        rows = slice(ci * CHUNK, (ci + 1) * CHUNK)
        last = slice((ci + 1) * CHUNK - 1, (ci + 1) * CHUNK)
        qd = qd_s[rows, :].astype(BF16)
        ke = ke_s[rows, :].astype(BF16)
        vv = v_s[rows, :].astype(BF16)
        k_bd = jnp.where(same_head, _tile_rows(ki_s[rows, :], nheads), 0.0).astype(BF16)
        v_bd = jnp.where(same_head, _tile_rows(v_s[rows, :], nheads), 0.0).astype(BF16)
        attn = jnp.where(tril, _dot_nt(qd, k_bd), 0.0).astype(BF16)
        o_s[rows, :] = _dot(attn, v_bd) + _dot_nt(qd, state.astype(BF16))
        state = state * dec_s[last, :] + jnp.where(same_head, _dot_tn(vv, ke), 0.0)
        yield
    st_s[...] = state


def _gla_body(h, small, w_ref, w2_ref, b2_ref, gn_ref, y_ref, qd_s, ki_s, ke_s, v_s, dec_s, o_s, st_s):
    u = yield from _project(h, w_ref)
    z = _dot(small.astype(BF16), w2_ref[...]) + b2_ref[...]
    log_a = _log_sigmoid(z) * (1.0 / GLA_TAU)
    b, rem = _chunk_cumsum(log_a)
    eb = jnp.exp(b)
    k = u[:, GLA_W:2 * GLA_W]
    qd_s[...] = u[:, 0:GLA_W] * (HEAD_DIM ** -0.5) * eb
    ki_s[...] = k * jnp.exp(-b)
    ke_s[...] = k * jnp.exp(rem)
    v_s[...] = u[:, 2 * GLA_W:3 * GLA_W]
    dec_s[...] = eb
    yield
    yield from _linear_attn_chunks(qd_s, ki_s, ke_s, v_s, dec_s, o_s, st_s, GLA_HEADS)
    o = o_s[...]
    y = o * lax.rsqrt(_group_mean_sq(o, HEAD_DIM) + NORM_EPS) * gn_ref[...]
    y_ref[...] = (y * _silu(u[:, 3 * GLA_W:4 * GLA_W])).astype(BF16)


def _ret_body(h, w_ref, cos_ref, sin_ref, gn_ref, y_ref, qd_s, ki_s, ke_s, v_s, dec_s, o_s, st_s):
    t = h.shape[0]
    u = yield from _project(h, w_ref)
    cos, sin = cos_ref[...], sin_ref[...]
    lane = lax.broadcasted_iota(jnp.int32, (t, RET_W), 1)
    half = HEAD_DIM // 2
    first_half = (lax.broadcasted_iota(jnp.int32, (t, LANES), 1) % HEAD_DIM) < half

    def rotate(a):
        blocks = []
        for b in range(0, RET_W, LANES):
            ab = a[:, b:b + LANES]
            swapped = jnp.where(first_half, pltpu.roll(ab, LANES - half, axis=1),
                                pltpu.roll(ab, half, axis=1))
            blocks.append(ab * cos + swapped * sin)
        return jnp.concatenate(blocks, axis=1)

    q = rotate(u[:, 0:RET_W])
    k = rotate(u[:, RET_W:2 * RET_W]) * (HEAD_DIM ** -0.5)
    row = lax.broadcasted_iota(jnp.int32, (t, RET_W), 0)
    lg = jnp.zeros((t, RET_W), F32)
    for hd in range(RET_HEADS):
        lg = jnp.where(lane // HEAD_DIM == hd, math.log(1.0 - 2.0 ** (-5.0 - hd)), lg)
    pos = (row % CHUNK).astype(F32)
    qw = jnp.exp((pos + 1.0) * lg)
    qd_s[...] = q * qw
    ki_s[...] = k * jnp.exp(-(pos + 1.0) * lg)
    ke_s[...] = k * jnp.exp((CHUNK - 1.0 - pos) * lg)
    v_s[...] = u[:, 2 * RET_W:3 * RET_W]
    dec_s[...] = qw
    yield
    yield from _linear_attn_chunks(qd_s, ki_s, ke_s, v_s, dec_s, o_s, st_s, RET_HEADS)
    o = o_s[...]
    y = o * lax.rsqrt(_group_mean_sq(o, HEAD_DIM) + NORM_EPS) * gn_ref[...]
    y_ref[...] = (y * _silu(u[:, 3 * RET_W:4 * RET_W])).astype(BF16)


def _linear_scratch(t, width, nheads):
    return [pltpu.VMEM((t, width), F32)] * 6 + [pltpu.VMEM((width, width), F32)]


SSD_PROJ = SSD_INNER + SSD_CONV_CH


def _ssd_body(h, small, w_ref, cw_ref, cb_ref, dtb_row_ref, alog_row_ref, d_row_ref, gn_ref, y_ref,
              xp_s, xs_s, bc_s, cs_s, ecs_s, xdt_s, xdtw_s, cst_s, o_s, st_s):
    t = h.shape[0]
    nc = t // CHUNK
    u = yield from _project(h, w_ref)
    z = u[:, 0:SSD_INNER]

    xp_s[SUBLANES:SUBLANES + t, :] = u[:, SSD_INNER:SSD_PROJ]
    conv = cb_ref[...] + cw_ref[SSD_CONV - 1:SSD_CONV, :] * xp_s[SUBLANES:SUBLANES + t, :]
    for kk in range(SSD_CONV - 1):
        off = SUBLANES - (SSD_CONV - 1) + kk
        conv = conv + cw_ref[kk:kk + 1, :] * xp_s[off:off + t, :]
    xp_s[0:SUBLANES, :] = xp_s[t:t + SUBLANES, :]
    xbc = _silu(conv)
    xs = xbc[:, 0:SSD_INNER]
    xs_s[...] = xs
    bc_s[...] = xbc[:, SSD_INNER:SSD_CONV_CH]

    dt = _softplus(small + dtb_row_ref[...])
    dta = dt * -jnp.exp(alog_row_ref[...])
    cs, _ = _chunk_cumsum(dta)
    cs_t = cs.T[SMALL_DT:SMALL_DT + SSD_HEADS, :]
    hpg = SSD_HEADS // SSD_GROUPS
    gcols = hpg * HEAD_DIM
    for ci in range(nc):
        for g in range(SSD_GROUPS):
            cst_s[ci, g:g + 1, :] = jnp.concatenate(
                [cs_t[g * hpg + rr:g * hpg + rr + 1, ci * CHUNK:(ci + 1) * CHUNK] for rr in range(hpg)], axis=1)

    er = lax.broadcasted_iota(jnp.int32, (LANES, SSD_INNER), 0)
    ec = lax.broadcasted_iota(jnp.int32, (LANES, SSD_INNER), 1)
    expand = jnp.where(ec // HEAD_DIM == er - SMALL_DT, 1.0, 0.0).astype(BF16)
    cs_x = _dot_sel2(cs, expand)
    xdt = xs * _dot_sel2(dt, expand)
    cs_s[...] = cs_x
    ecs_s[...] = jnp.exp(cs_x)
    xdt_s[...] = xdt
    xdtw_s[...] = xdt * jnp.exp(_chunk_last(cs_x) - cs_x)

    r = lax.broadcasted_iota(jnp.int32, (gcols, gcols), 0)
    c = lax.broadcasted_iota(jnp.int32, (gcols, gcols), 1)
    same_head = (r // HEAD_DIM) == (c // HEAD_DIM)
    rr = lax.broadcasted_iota(jnp.int32, (CHUNK, gcols), 0)
    cc = lax.broadcasted_iota(jnp.int32, (CHUNK, gcols), 1)
    tril = (cc % CHUNK) <= rr
    gw = SSD_GROUPS * SSD_STATE

    states = [st_s[g] for g in range(SSD_GROUPS)]
    for ci in range(nc):
        rows = slice(ci * CHUNK, (ci + 1) * CHUNK)
        last = slice((ci + 1) * CHUNK - 1, (ci + 1) * CHUNK)
        for g in range(SSD_GROUPS):
            gsl = slice(g * gcols, (g + 1) * gcols)
            bm = bc_s[rows, g * SSD_STATE:(g + 1) * SSD_STATE].astype(BF16)
            cm = bc_s[rows, gw + g * SSD_STATE:gw + (g + 1) * SSD_STATE].astype(BF16)
            cb = _dot_nt(cm, _tile_rows(bm, hpg))
            decay = jnp.exp(jnp.where(tril, cs_s[rows, gsl] - cst_s[ci, g:g + 1, :], -jnp.inf))
            x_bd = jnp.where(same_head, _tile_rows(xdt_s[rows, gsl], hpg), 0.0).astype(BF16)
            y_inter = _dot(cm, states[g].astype(BF16)) * ecs_s[rows, gsl]
            o_s[rows, gsl] = _dot((cb * decay).astype(BF16), x_bd) + y_inter
            states[g] = (states[g] * ecs_s[last, gsl]
                         + _dot_tn(bm, xdtw_s[rows, gsl].astype(BF16)))
        yield
    for g in range(SSD_GROUPS):
        st_s[g] = states[g]

    y = (o_s[...] + d_row_ref[...] * xs_s[...]) * _silu(z)
    gwid = SSD_INNER // SSD_GROUPS
    normed = []
    for g in range(SSD_GROUPS):
        yg = y[:, g * gwid:(g + 1) * gwid]
        normed.append(yg * lax.rsqrt(jnp.mean(yg * yg, axis=-1, keepdims=True) + NORM_EPS))
    y_ref[...] = (jnp.concatenate(normed, axis=1) * gn_ref[...]).astype(BF16)


N_FOX_IN, N_GLA_IN, N_RET_IN, N_SSD_IN = 5, 4, 4, 7
N_LINEAR_SCRATCH = 7
N_SSD_SCRATCH = 10


def _mixers_kernel(x_ref, ln_ref, ws_ref, *refs):
    it = iter(refs)
    take = lambda n: [next(it) for _ in range(n)]
    fox_in, gla_in, ret_in, ssd_in = take(N_FOX_IN), take(N_GLA_IN), take(N_RET_IN), take(N_SSD_IN)
    fox_out, (yb_ref, yc_ref, yd_ref) = take(4), take(3)
    (carry_ref,), gla_s, ret_s, ssd_s = take(1), take(N_LINEAR_SCRATCH), take(N_LINEAR_SCRATCH), take(N_SSD_SCRATCH)

    @pl.when(pl.program_id(0) == 0)
    def _():
        carry_ref[...] = jnp.zeros_like(carry_ref)
        gla_s[-1][...] = jnp.zeros_like(gla_s[-1])
        ret_s[-1][...] = jnp.zeros_like(ret_s[-1])
        ssd_s[-1][...] = jnp.zeros_like(ssd_s[-1])
        ssd_s[0][0:SUBLANES, :] = jnp.zeros((SUBLANES, SSD_CONV_CH), F32)

    h = _rmsnorm(x_ref[...], ln_ref[...]).astype(BF16)
    small = _dot_nt(h, ws_ref[...])
    waiting = [_ssd_body(h, small, *ssd_in, yd_ref, *ssd_s), _gla_body(h, small, *gla_in, yb_ref, *gla_s),
               _ret_body(h, *ret_in, yc_ref, *ret_s), _fox_prep_body(h, small, *fox_in, *fox_out, carry_ref)]
    pending = []
    while waiting or pending:
        if waiting:
            pending.append(waiting.pop(0))
        for body in list(pending):
            if next(body, "done") == "done":
                pending.remove(body)


def _mixers(l, x, ln, w_small, fox_in, gla_in, ret_in, ssd_in, cos_t, sin_t):
    seq = x.shape[0]
    t = ROW_TILE
    row = lambda w: pl.BlockSpec((t, w), lambda i: (i, 0))
    heads = pl.BlockSpec((FOX_HEADS, t, LANES), lambda i: (0, i, 0))
    head_shape = jax.ShapeDtypeStruct((FOX_HEADS, seq, LANES), BF16)
    w_ret, ret_gn = ret_in
    wide = pltpu.VMEM((t, SSD_INNER), F32)
    return pl.pallas_call(
        _mixers_kernel,
        grid=(seq // t,),
        in_specs=([row(D_MODEL), _layer(ln, l), _layer(w_small, l)] + [_layer(a, l) for a in fox_in]
                  + [_layer(a, l) for a in gla_in]
                  + [_layer(w_ret, l), row(LANES), row(LANES), _layer(ret_gn, l)]
                  + [_layer(a, l) for a in ssd_in]),
        out_specs=[heads, heads, heads, pl.BlockSpec((1, SUBLANES, LANES), lambda i: (i, 0, 0)),
                   row(GLA_W), row(RET_W), row(SSD_INNER)],
        out_shape=[head_shape, head_shape, head_shape,
                   jax.ShapeDtypeStruct((seq // t, SUBLANES, LANES), F32),
                   jax.ShapeDtypeStruct((seq, GLA_W), BF16),
                   jax.ShapeDtypeStruct((seq, RET_W), BF16),
                   jax.ShapeDtypeStruct((seq, SSD_INNER), BF16)],
        scratch_shapes=([pltpu.VMEM((SUBLANES, LANES), F32)]
                        + _linear_scratch(t, GLA_W, GLA_HEADS) + _linear_scratch(t, RET_W, RET_HEADS)
                        + [pltpu.VMEM((t + 2 * SUBLANES, SSD_CONV_CH), F32),
                           wide,
                           pltpu.VMEM((t, 2 * SSD_GROUPS * SSD_STATE), F32),
                           wide, wide, wide, wide,
                           pltpu.VMEM((t // CHUNK, SSD_GROUPS, SSD_INNER // SSD_GROUPS), F32),
                           wide,
                           pltpu.VMEM((SSD_GROUPS, SSD_STATE, SSD_INNER // SSD_GROUPS), F32)]),
        compiler_params=pltpu.CompilerParams(dimension_semantics=("arbitrary",),
                                             vmem_limit_bytes=VMEM_LIMIT),
        name="mixers",
    )(x, ln, w_small, *fox_in, *gla_in, w_ret, cos_t, sin_t, ret_gn, *ssd_in)


def _merge_ffn_kernel(x_ref, ln1_ref, wg_ref, ya_ref, yb_ref, yc_ref, yd_ref,
                      wa_ref, wb_ref, wc_ref, wd_ref, wo_ref, ln2_ref, w1_ref, w2_ref, out_ref):
    x = x_ref[...]
    h = _rmsnorm(x, ln1_ref[...]).astype(BF16)

    def gate(b):
        return _sigmoid(_dot_nt(h, wg_ref[b * D_MODEL:(b + 1) * D_MODEL, :]))

    up_a = _dot(ya_ref[0], wa_ref[0:LANES, :])
    for p in range(1, FOX_HEADS // FOX_HEADS_PER_TILE):
        up_a = up_a + _dot(ya_ref[p], wa_ref[p * LANES:(p + 1) * LANES, :])
    merged = gate(0) * up_a
    merged = merged + gate(1) * _dot(yb_ref[...], wb_ref[...])
    merged = merged + gate(2) * _dot(yc_ref[...], wc_ref[...])
    merged = merged + gate(3) * _dot(yd_ref[...], wd_ref[...])
    x = x + _dot(merged.astype(BF16), wo_ref[...])

    h = _rmsnorm(x, ln2_ref[...]).astype(BF16)
    act = _silu(_dot(h, w1_ref[:, 0:FFN_HIDDEN])) * _dot(h, w1_ref[:, FFN_HIDDEN:2 * FFN_HIDDEN])
    out_ref[...] = x + _dot(act.astype(BF16), w2_ref[...])


def _merge_ffn(l, x, ln1, w_gates, y_a, y_b, y_c, y_d, w_up_a, w_up_b, w_up_c, w_up_d, w_out, ln2, w1, w2):
    seq = x.shape[0]
    t = FFN_TILE
    row = lambda w: pl.BlockSpec((t, w), lambda i: (i, 0))
    return pl.pallas_call(
        _merge_ffn_kernel,
        grid=(seq // t,),
        in_specs=[row(D_MODEL), _layer(ln1, l), _layer(w_gates, l),
                  pl.BlockSpec((FOX_HEADS // FOX_HEADS_PER_TILE, t, LANES), lambda i: (0, i, 0)),
                  row(GLA_W), row(RET_W), row(SSD_INNER),
                  _layer(w_up_a, l), _layer(w_up_b, l), _layer(w_up_c, l),
                  _layer(w_up_d, l), _layer(w_out, l), _layer(ln2, l), _layer(w1, l), _layer(w2, l)],
        out_specs=row(D_MODEL),
        out_shape=jax.ShapeDtypeStruct((seq, D_MODEL), F32),
        compiler_params=pltpu.CompilerParams(dimension_semantics=("arbitrary",),
                                             vmem_limit_bytes=VMEM_LIMIT),
        name="merge_swiglu",
    )(x, ln1, w_gates, y_a, y_b, y_c, y_d, w_up_a, w_up_b, w_up_c, w_up_d, w_out, ln2, w1, w2)


def _rows(a):
    return a.reshape(a.shape[0], 1, -1)


def _in_proj_weights(w_in):
    wt = jnp.transpose(w_in, (0, 2, 1)).astype(BF16)
    offs = np.concatenate([[0], np.cumsum(IN_SPLITS)])
    (fq, fk, fv, ff, gq, gk, gv, glr, gr, rq, rk, rv, rg, z, xbc, dt, gates) = [
        wt[:, offs[n]:offs[n + 1], :] for n in range(len(IN_SPLITS))]
    cat = lambda parts: jnp.concatenate(parts, axis=1)
    gap = lambda n: jnp.zeros((wt.shape[0], n, wt.shape[2]), BF16)
    w_small = cat([ff, glr, gap(SMALL_DT - SMALL_GLR - GLA_LOWRANK), dt, gap(LANES - SMALL_DT - SSD_HEADS)])
    return w_small, cat([fq, fk, fv]), cat([gq, gk, gv, gr]), cat([rq, rk, rv, rg]), cat([z, xbc]), gates


def kernel(x, positions, ln1, ln2, w_in, fox_bf, fox_qn, fox_kn, gla_w2, gla_b, gla_norm, ret_norm,
           ssd_conv_w, ssd_conv_b, ssd_dt_bias, ssd_a_log, ssd_d, ssd_norm,
           w_up_a, w_up_b, w_up_c, w_up_d, w_out, w_ffn_in, w_ffn_out):
    bsz, seq, d = x.shape
    assert bsz == 1 and d == D_MODEL and seq % ROW_TILE == 0 and seq % FOX_TQ == 0 and FOX_TQ % ROW_TILE == 0
    depth = ln1.shape[0]
    xr = x.reshape(seq, d)
    cos_t, sin_t = _rotary_tables(positions, seq)

    bf16 = lambda w: w.astype(BF16)
    ln1_r, ln2_r = _rows(ln1), _rows(ln2)
    fox_bf_r = _rows(jnp.pad(fox_bf, ((0, 0), (SMALL_FF, LANES - SMALL_FF - FOX_HEADS))))
    fox_qn_r, fox_kn_r = _rows(jnp.tile(fox_qn, (1, 2))), _rows(jnp.tile(fox_kn, (1, 2)))
    fox_shift_r, fox_par = _fox_params(fox_qn, fox_kn)
    gla_w2_p = bf16(jnp.pad(gla_w2, ((0, 0), (SMALL_GLR, LANES - SMALL_GLR - GLA_LOWRANK), (0, 0))))
    gla_b_r, gla_gn_r = _rows(gla_b), _rows(jnp.tile(gla_norm, (1, GLA_HEADS)))
    ret_gn_r = _rows(jnp.tile(ret_norm, (1, RET_HEADS)))
    conv_b_r = _rows(ssd_conv_b)
    at_dt = lambda a: _rows(jnp.pad(a, ((0, 0), (SMALL_DT, LANES - SMALL_DT - SSD_HEADS))))
    dtb_r, alog_r = at_dt(ssd_dt_bias), at_dt(ssd_a_log)
    d_r, ssd_gn_r = _rows(jnp.repeat(ssd_d, HEAD_DIM, axis=1)), _rows(ssd_norm)
    up_a, up_b, up_c, up_d, w_o = bf16(w_up_a), bf16(w_up_b), bf16(w_up_c), bf16(w_up_d), bf16(w_out)
    w1, w2 = bf16(w_ffn_in), bf16(w_ffn_out)
    w_small, w_fox, w_gla, w_ret, w_ssd, w_gates = _in_proj_weights(w_in)

    for l in range(depth):
        q, k, v, c, y_b, y_c, y_d = _mixers(
            l, xr, ln1_r, w_small,
            (w_fox, fox_bf_r, fox_qn_r, fox_kn_r, fox_shift_r),
            (w_gla, gla_w2_p, gla_b_r, gla_gn_r),
            (w_ret, ret_gn_r),
            (w_ssd, ssd_conv_w, conv_b_r, dtb_r, alog_r, d_r, ssd_gn_r),
            cos_t, sin_t)
        y_a = _fox_attention(fox_par[l], q, k, v, c)
        xr = _merge_ffn(l, xr, ln1_r, w_gates, y_a, y_b, y_c, y_d, up_a, up_b, up_c, up_d, w_o, ln2_r, w1, w2)
    return xr.reshape(bsz, seq, d)
```

```python
import math

import numpy as np
import jax
import jax.numpy as jnp
from jax import lax
from jax.experimental import pallas as pl
from jax.experimental.pallas import tpu as pltpu

D_MODEL = 1024
HEAD_DIM = 64
CHUNK = 64
FOX_HEADS = 4
GLA_HEADS = 4
GLA_LOWRANK = 16
GLA_TAU = 16.0
RET_HEADS = 4
ROPE_THETA = 10000.0
SSD_HEADS = 8
SSD_GROUPS = 2
SSD_STATE = 64
SSD_CONV = 4
SSD_INNER = SSD_HEADS * HEAD_DIM
SSD_CONV_CH = SSD_INNER + 2 * SSD_GROUPS * SSD_STATE
FOX_W = FOX_HEADS * HEAD_DIM
GLA_W = GLA_HEADS * HEAD_DIM
RET_W = RET_HEADS * HEAD_DIM
N_BRANCH = 4
FFN_HIDDEN = ((8 * D_MODEL + 3 * 256 - 1) // (3 * 256)) * 256
NORM_EPS = 1e-6
IN_SPLITS = (FOX_W, FOX_W, FOX_W, FOX_HEADS,
             GLA_W, GLA_W, GLA_W, GLA_LOWRANK, GLA_W,
             RET_W, RET_W, RET_W, RET_W,
             SSD_INNER, SSD_CONV_CH, SSD_HEADS,
             N_BRANCH * D_MODEL)

LANES = 128
SUBLANES = 8
VMEM_LIMIT = 56 * 1024 * 1024

ROW_TILE = 512
FOX_TQ = 512
FOX_TK = 512
FFN_TILE = 512
PROJ_PIECE = 512

F32 = jnp.float32
BF16 = jnp.bfloat16


def _rmsnorm(x, g):
    return x * lax.rsqrt(jnp.mean(x * x, axis=-1, keepdims=True) + NORM_EPS) * g


def _log_sigmoid(x):
    return jnp.minimum(x, 0.0) - jnp.log(1.0 + jnp.exp(-jnp.abs(x)))


def _softplus(x):
    return jnp.maximum(x, 0.0) + jnp.log(1.0 + jnp.exp(-jnp.abs(x)))


def _sigmoid(x):
    return 0.5 * jnp.tanh(0.5 * x) + 0.5


def _silu(x):
    return x * _sigmoid(x)


def _split3(x):
    hi = x.astype(BF16)
    r1 = x - hi.astype(F32)
    mid = r1.astype(BF16)
    lo = (r1 - mid.astype(F32)).astype(BF16)
    return hi, mid, lo


def _split2(x):
    hi = x.astype(BF16)
    return hi, (x - hi.astype(F32)).astype(BF16)


def _dot(a, b):
    return jnp.dot(a, b, preferred_element_type=F32)


def _dot_nt(a, b):
    return lax.dot_general(a, b, (((1,), (1,)), ((), ())), preferred_element_type=F32)


def _dot_tn(a, b):
    return lax.dot_general(a, b, (((0,), (0,)), ((), ())), preferred_element_type=F32)


def _sel_dot(mat, x):
    hi, mid, lo = _split3(x)
    return _dot(mat, hi) + _dot(mat, mid) + _dot(mat, lo)


def _dot_sel2(x, mat):
    hi, lo = _split2(x)
    return _dot(hi, mat) + _dot(lo, mat)


def _chunk_cumsum(x):
    t = x.shape[0]
    r = lax.broadcasted_iota(jnp.int32, (t, t), 0)
    c = lax.broadcasted_iota(jnp.int32, (t, t), 1)
    incl = jnp.where(((r // CHUNK) == (c // CHUNK)) & (c <= r), 1.0, 0.0).astype(BF16)
    hi, lo = _split2(x)
    b = _dot(incl, hi) + _dot(incl, lo)
    return b, _chunk_last(b) - b


def _chunk_last(b):
    t, w = b.shape
    return jnp.concatenate([jnp.broadcast_to(b[e - 1:e, :], (CHUNK, w)) for e in range(CHUNK, t + 1, CHUNK)],
                           axis=0)


def _group_mean_sq(o, group):
    w = o.shape[-1]
    r = lax.broadcasted_iota(jnp.int32, (w, w), 0)
    c = lax.broadcasted_iota(jnp.int32, (w, w), 1)
    bd = jnp.where((r // group) == (c // group), 1.0, 0.0).astype(BF16)
    return _dot_sel2(o * o, bd) * (1.0 / group)


def _project(h, wt_ref):
    n = wt_ref.shape[0]
    parts = []
    for a in range(0, n, PROJ_PIECE):
        parts.append(_dot_nt(h, wt_ref[a:min(a + PROJ_PIECE, n), :]))
        yield
    return jnp.concatenate(parts, axis=1)


def _layer(stacked, l):
    tail = tuple(stacked.shape[1:])
    return pl.BlockSpec((None,) + tail, lambda *_: (l,) + (0,) * len(tail), pipeline_mode=pl.Buffered(1))


def _rotary_kernel(pos_ref, inv_ref, sign_ref, cos_ref, sin_ref):
    ang = pos_ref[...] * inv_ref[...]
    cos_ref[...] = jnp.cos(ang)
    sin_ref[...] = jnp.sin(ang) * sign_ref[...]


def _rotary_tables(positions, seq):
    half = HEAD_DIM // 2
    inv = ROPE_THETA ** (-jnp.arange(half, dtype=F32) / half)
    reps = LANES // HEAD_DIM
    inv_row = jnp.tile(inv, 2 * reps).reshape(1, LANES)
    sign_row = jnp.tile(jnp.concatenate([-jnp.ones((half,), F32), jnp.ones((half,), F32)]),
                        reps).reshape(1, LANES)
    pos = positions.astype(F32).reshape(seq, 1)
    t = ROW_TILE
    return pl.pallas_call(
        _rotary_kernel,
        grid=(seq // t,),
        in_specs=[pl.BlockSpec((t, 1), lambda i: (i, 0)),
                  pl.BlockSpec((1, LANES), lambda i: (0, 0)),
                  pl.BlockSpec((1, LANES), lambda i: (0, 0))],
        out_specs=[pl.BlockSpec((t, LANES), lambda i: (i, 0)),
                   pl.BlockSpec((t, LANES), lambda i: (i, 0))],
        out_shape=[jax.ShapeDtypeStruct((seq, LANES), F32)] * 2,
        name="rotary_tables",
    )(pos, inv_row, sign_row)


SMALL_FF = 0
SMALL_GLR = FOX_HEADS
SMALL_DT = 24
LOG2E = math.log2(math.e)
F32_EXP_ZERO = 88.0
FOX_FAST_MAX_LOGIT = 40.0
FOX_BOUND_MARGIN = 1.02


def _fox_prep_body(h, small, w_ref, bf_ref, qn_ref, kn_ref, shift_ref, q_out, k_out, v_out, c_out, carry_ref):
    t = h.shape[0]
    u = yield from _project(h, w_ref)
    ls = _log_sigmoid(small + bf_ref[...])
    r = lax.broadcasted_iota(jnp.int32, (t, t), 0)
    cc = lax.broadcasted_iota(jnp.int32, (t, t), 1)
    tri = jnp.where(cc <= r, 1.0, 0.0).astype(BF16)
    c = _sel_dot(tri, ls) + carry_ref[0:1, :]
    carry_ref[0:1, :] = c[t - 1:t, :]
    c_out[0] = jnp.concatenate([c[0:1, :], c[t - 1:t, :], jnp.zeros((SUBLANES - 2, LANES), F32)], axis=0)
    c_hi, c_mid, c_lo = _split3(c * LOG2E)
    c_hi, c_mid, c_lo = c_hi.astype(F32), c_mid.astype(F32), c_lo.astype(F32)

    lane = lax.broadcasted_iota(jnp.int32, (t, LANES), 1)
    low = lane < HEAD_DIM
    j = lane - HEAD_DIM
    shift = shift_ref[...]

    def head_norm(pair, gain, mult):
        sq = pair * pair
        s_lo = jnp.sum(jnp.where(low, sq, 0.0), axis=-1, keepdims=True)
        s_hi = jnp.sum(jnp.where(low, 0.0, sq), axis=-1, keepdims=True)
        ms = jnp.where(low, s_lo, s_hi) * (1.0 / HEAD_DIM)
        return pair * lax.rsqrt(ms + NORM_EPS) * (gain * mult)

    for p in range(FOX_HEADS // 2):
        qp = head_norm(u[:, p * LANES:(p + 1) * LANES], qn_ref[...], HEAD_DIM ** -0.5 * LOG2E)
        kp = head_norm(u[:, FOX_W + p * LANES:FOX_W + (p + 1) * LANES], kn_ref[...], 1.0)
        vp = u[:, 2 * FOX_W + p * LANES:2 * FOX_W + (p + 1) * LANES]
        for sub in range(2):
            hd = 2 * p + sub
            if sub == 1:
                qp, kp, vp = (pltpu.roll(a, HEAD_DIM, axis=1) for a in (qp, kp, vp))
            ch, cm, cl = (jnp.broadcast_to(a[:, SMALL_FF + hd:SMALL_FF + hd + 1], (t, LANES)) for a in (c_hi, c_mid, c_lo))
            aug_q = jnp.where(j == 0, ch, jnp.where(j == 1, cm, jnp.where(j == 2, cl,
                              jnp.where(j < 7, 1.0, 0.0))))
            aug_k = jnp.where(j < 3, 1.0, jnp.where(j == 3, -ch, jnp.where(j == 4, -cm,
                              jnp.where(j == 5, -cl, jnp.where(j == 6, -shift, 0.0)))))
            aug_v = jnp.where(j == 0, 1.0, 0.0)
            q_out[hd] = jnp.where(low, qp, aug_q).astype(BF16)
            k_out[hd] = jnp.where(low, kp, aug_k).astype(BF16)
            v_out[hd] = jnp.where(low, vp, aug_v).astype(BF16)


def _fox_attn_kernel(par_ref, cq0_ref, cend_ref, q_ref, k_ref, v_ref, o_ref):
    i = pl.program_id(0)
    nh, tq, _ = q_ref.shape
    thr = par_ref[0]

    def first_block(hd):
        c0 = cq0_ref[i, hd]
        return lax.while_loop(
            lambda j: jnp.logical_and(j > 0, c0 - cend_ref[jnp.maximum(j - 1, 0), hd] >= -thr),
            lambda j: j - 1, i)

    def keys(hd, j, nblk):
        off = pl.multiple_of(j * FOX_TK, FOX_TK)
        return k_ref[hd, pl.ds(off, nblk * FOX_TK), :], v_ref[hd, pl.ds(off, nblk * FOX_TK), :]

    def causal(s):
        r = lax.broadcasted_iota(jnp.int32, s.shape, 0)
        c = lax.broadcasted_iota(jnp.int32, s.shape, 1)
        return jnp.where(c <= r, s, -jnp.inf)

    def fixed_shift():
        def block(hd, j, nblk):
            k, v = keys(hd, j, nblk)
            return _dot(jnp.exp2(_dot_nt(q_ref[hd], k)).astype(BF16), v)

        def diagonal(hd):
            k, v = keys(hd, i, 1)
            return _dot(jnp.exp2(causal(_dot_nt(q_ref[hd], k))).astype(BF16), v)

        accs = [diagonal(hd) for hd in range(nh)]
        for hd in range(nh):
            acc, j_lo = accs[hd], first_block(hd)
            n = i - j_lo
            acc = lax.cond(n % 2 == 1, lambda: acc + block(hd, j_lo, 1), lambda: acc)
            acc = lax.cond((n // 2) % 2 == 1, lambda: acc + block(hd, j_lo + n % 2, 2), lambda: acc)
            first = j_lo + n % 4
            accs[hd] = lax.fori_loop(0, n // 4, lambda p, acc: acc + block(hd, first + 4 * p, 4), acc)
        return accs

    def running_max():
        def one_head(hd):
            q = q_ref[hd]

            def step(j, carry, masked):
                m, acc = carry
                k, v = keys(hd, j, 1)
                s = _dot_nt(q, k)
                if masked:
                    s = causal(s)
                m_new = jnp.maximum(m, jnp.max(s, axis=-1, keepdims=True))
                p = jnp.exp2(s - m_new)
                return m_new, jnp.exp2(m - m_new) * acc + _dot(p.astype(BF16), v)
            init = (jnp.full((tq, 1), -jnp.inf, F32), jnp.zeros((tq, LANES), F32))
            carry = step(i, init, True)
            return lax.fori_loop(first_block(hd), i, lambda j, c: step(j, c, False), carry)[1]
        return [one_head(hd) for hd in range(nh)]

    accs = lax.cond(par_ref[1] > 0.0, fixed_shift, running_max)
    outs = [acc / acc[:, HEAD_DIM:HEAD_DIM + 1] for acc in accs]
    low = lax.broadcasted_iota(jnp.int32, (tq, LANES), 1) < HEAD_DIM
    for p in range(nh // FOX_HEADS_PER_TILE):
        o_ref[p] = jnp.where(low, outs[2 * p], pltpu.roll(outs[2 * p + 1], HEAD_DIM, axis=1)).astype(BF16)


FOX_HEADS_PER_TILE = LANES // HEAD_DIM


def _fox_attn(par, cq0, cend, q, k, v):
    assert FOX_TQ == FOX_TK and FOX_HEADS_PER_TILE == 2
    nh, seq, _ = q.shape
    kv_spec = pl.BlockSpec((nh, seq, LANES), lambda i, *_: (0, 0, 0), pipeline_mode=pl.Buffered(1))
    return pl.pallas_call(
        _fox_attn_kernel,
        grid_spec=pltpu.PrefetchScalarGridSpec(
            num_scalar_prefetch=3,
            grid=(seq // FOX_TQ,),
            in_specs=[pl.BlockSpec((nh, FOX_TQ, LANES), lambda i, *_: (0, i, 0)), kv_spec, kv_spec],
            out_specs=pl.BlockSpec((nh // FOX_HEADS_PER_TILE, FOX_TQ, LANES), lambda i, *_: (0, i, 0))),
        out_shape=jax.ShapeDtypeStruct((nh // FOX_HEADS_PER_TILE, seq, LANES), BF16),
        compiler_params=pltpu.CompilerParams(dimension_semantics=("arbitrary",),
                                             vmem_limit_bytes=VMEM_LIMIT),
        name="fox_attn",
    )(par, cq0, cend, q, k, v)


def _fox_params(qn, kn):
    bound = FOX_BOUND_MARGIN * HEAD_DIM ** 0.5 * jnp.max(jnp.abs(qn), axis=1) * jnp.max(jnp.abs(kn), axis=1)
    shift = jnp.broadcast_to((LOG2E * bound)[:, None, None], (bound.shape[0], 1, LANES))
    par = jnp.stack([F32_EXP_ZERO + 2.0 * bound, (bound < FOX_FAST_MAX_LOGIT).astype(F32)], axis=1)
    return shift, par


def _fox_attention(par, q, k, v, c):
    per_tile = FOX_TQ // ROW_TILE
    heads = slice(SMALL_FF, SMALL_FF + FOX_HEADS)
    cq0 = c[0::per_tile, 0, heads]
    cend = c[per_tile - 1::per_tile, 1, heads]
    return _fox_attn(par, cq0, cend, q, k, v)


def _tile_rows(a, n):
    return jnp.concatenate([a] * n, axis=0)


def _linear_attn_chunks(qd_s, ki_s, ke_s, v_s, dec_s, o_s, st_s, nheads):
    t, w = qd_s.shape
    assert w == nheads * HEAD_DIM
    r = lax.broadcasted_iota(jnp.int32, (w, w), 0)
    c = lax.broadcasted_iota(jnp.int32, (w, w), 1)
    same_head = (r // HEAD_DIM) == (c // HEAD_DIM)
    rr = lax.broadcasted_iota(jnp.int32, (CHUNK, w), 0)
    cc = lax.broadcasted_iota(jnp.int32, (CHUNK, w), 1)
    tril = (cc % CHUNK) <= rr
    state = st_s[...]
    for ci in range(t // CHUNK):
        rows = slice(ci * CHUNK, (ci + 1) * CHUNK)
        last = slice((ci + 1) * CHUNK - 1, (ci + 1) * CHUNK)
        qd = qd_s[rows, :].astype(BF16)
        ke = ke_s[rows, :].astype(BF16)
        vv = v_s[rows, :].astype(BF16)
        k_bd = jnp.where(same_head, _tile_rows(ki_s[rows, :], nheads), 0.0).astype(BF16)
        v_bd = jnp.where(same_head, _tile_rows(v_s[rows, :], nheads), 0.0).astype(BF16)
        attn = jnp.where(tril, _dot_nt(qd, k_bd), 0.0).astype(BF16)
        o_s[rows, :] = _dot(attn, v_bd) + _dot_nt(qd, state.astype(BF16))
        state = state * dec_s[last, :] + jnp.where(same_head, _dot_tn(vv, ke), 0.0)
        yield
    st_s[...] = state


def _gla_body(h, small, w_ref, w2_ref, b2_ref, gn_ref, y_ref, qd_s, ki_s, ke_s, v_s, dec_s, o_s, st_s):
    u = yield from _project(h, w_ref)
    z = _dot(small.astype(BF16), w2_ref[...]) + b2_ref[...]
    log_a = _log_sigmoid(z) * (1.0 / GLA_TAU)
    b, rem = _chunk_cumsum(log_a)
    eb = jnp.exp(b)
    k = u[:, GLA_W:2 * GLA_W]
    qd_s[...] = u[:, 0:GLA_W] * (HEAD_DIM ** -0.5) * eb
    ki_s[...] = k * jnp.exp(-b)
    ke_s[...] = k * jnp.exp(rem)
    v_s[...] = u[:, 2 * GLA_W:3 * GLA_W]
    dec_s[...] = eb
    yield
    yield from _linear_attn_chunks(qd_s, ki_s, ke_s, v_s, dec_s, o_s, st_s, GLA_HEADS)
    o = o_s[...]
    y = o * lax.rsqrt(_group_mean_sq(o, HEAD_DIM) + NORM_EPS) * gn_ref[...]
    y_ref[...] = (y * _silu(u[:, 3 * GLA_W:4 * GLA_W])).astype(BF16)


def _ret_body(h, w_ref, cos_ref, sin_ref, gn_ref, y_ref, qd_s, ki_s, ke_s, v_s, dec_s, o_s, st_s):
    t = h.shape[0]
    u = yield from _project(h, w_ref)
    cos, sin = cos_ref[...], sin_ref[...]
    lane = lax.broadcasted_iota(jnp.int32, (t, RET_W), 1)
    half = HEAD_DIM // 2
    first_half = (lax.broadcasted_iota(jnp.int32, (t, LANES), 1) % HEAD_DIM) < half

    def rotate(a):
        blocks = []
        for b in range(0, RET_W, LANES):
            ab = a[:, b:b + LANES]
            swapped = jnp.where(first_half, pltpu.roll(ab, LANES - half, axis=1),
                                pltpu.roll(ab, half, axis=1))
            blocks.append(ab * cos + swapped * sin)
        return jnp.concatenate(blocks, axis=1)

    q = rotate(u[:, 0:RET_W])
    k = rotate(u[:, RET_W:2 * RET_W]) * (HEAD_DIM ** -0.5)
    row = lax.broadcasted_iota(jnp.int32, (t, RET_W), 0)
    lg = jnp.zeros((t, RET_W), F32)
    for hd in range(RET_HEADS):
        lg = jnp.where(lane // HEAD_DIM == hd, math.log(1.0 - 2.0 ** (-5.0 - hd)), lg)
    pos = (row % CHUNK).astype(F32)
    qw = jnp.exp((pos + 1.0) * lg)
    qd_s[...] = q * qw
    ki_s[...] = k * jnp.exp(-(pos + 1.0) * lg)
    ke_s[...] = k * jnp.exp((CHUNK - 1.0 - pos) * lg)
    v_s[...] = u[:, 2 * RET_W:3 * RET_W]
    dec_s[...] = qw
    yield
    yield from _linear_attn_chunks(qd_s, ki_s, ke_s, v_s, dec_s, o_s, st_s, RET_HEADS)
    o = o_s[...]
    y = o * lax.rsqrt(_group_mean_sq(o, HEAD_DIM) + NORM_EPS) * gn_ref[...]
    y_ref[...] = (y * _silu(u[:, 3 * RET_W:4 * RET_W])).astype(BF16)


def _linear_scratch(t, width, nheads):
    return [pltpu.VMEM((t, width), F32)] * 6 + [pltpu.VMEM((width, width), F32)]


SSD_PROJ = SSD_INNER + SSD_CONV_CH


def _ssd_body(h, small, w_ref, cw_ref, cb_ref, dtb_row_ref, alog_row_ref, d_row_ref, gn_ref, y_ref,
              xp_s, xs_s, bc_s, cs_s, ecs_s, xdt_s, xdtw_s, cst_s, o_s, st_s):
    t = h.shape[0]
    nc = t // CHUNK
    u = yield from _project(h, w_ref)
    z = u[:, 0:SSD_INNER]

    xp_s[SUBLANES:SUBLANES + t, :] = u[:, SSD_INNER:SSD_PROJ]
    conv = cb_ref[...] + cw_ref[SSD_CONV - 1:SSD_CONV, :] * xp_s[SUBLANES:SUBLANES + t, :]
    for kk in range(SSD_CONV - 1):
        off = SUBLANES - (SSD_CONV - 1) + kk
        conv = conv + cw_ref[kk:kk + 1, :] * xp_s[off:off + t, :]
    xp_s[0:SUBLANES, :] = xp_s[t:t + SUBLANES, :]
    xbc = _silu(conv)
    xs = xbc[:, 0:SSD_INNER]
    xs_s[...] = xs
    bc_s[...] = xbc[:, SSD_INNER:SSD_CONV_CH]

    dt = _softplus(small + dtb_row_ref[...])
    dta = dt * -jnp.exp(alog_row_ref[...])
    cs, _ = _chunk_cumsum(dta)
    cs_t = cs.T[SMALL_DT:SMALL_DT + SSD_HEADS, :]
    hpg = SSD_HEADS // SSD_GROUPS
    gcols = hpg * HEAD_DIM
    for ci in range(nc):
        for g in range(SSD_GROUPS):
            cst_s[ci, g:g + 1, :] = jnp.concatenate(
                [cs_t[g * hpg + rr:g * hpg + rr + 1, ci * CHUNK:(ci + 1) * CHUNK] for rr in range(hpg)], axis=1)

    er = lax.broadcasted_iota(jnp.int32, (LANES, SSD_INNER), 0)
    ec = lax.broadcasted_iota(jnp.int32, (LANES, SSD_INNER), 1)
    expand = jnp.where(ec // HEAD_DIM == er - SMALL_DT, 1.0, 0.0).astype(BF16)
    cs_x = _dot_sel2(cs, expand)
    xdt = xs * _dot_sel2(dt, expand)
    cs_s[...] = cs_x
    ecs_s[...] = jnp.exp(cs_x)
    xdt_s[...] = xdt
    xdtw_s[...] = xdt * jnp.exp(_chunk_last(cs_x) - cs_x)

    r = lax.broadcasted_iota(jnp.int32, (gcols, gcols), 0)
    c = lax.broadcasted_iota(jnp.int32, (gcols, gcols), 1)
    same_head = (r // HEAD_DIM) == (c // HEAD_DIM)
    rr = lax.broadcasted_iota(jnp.int32, (CHUNK, gcols), 0)
    cc = lax.broadcasted_iota(jnp.int32, (CHUNK, gcols), 1)
    tril = (cc % CHUNK) <= rr
    gw = SSD_GROUPS * SSD_STATE

    states = [st_s[g] for g in range(SSD_GROUPS)]
    for ci in range(nc):
        rows = slice(ci * CHUNK, (ci + 1) * CHUNK)
        last = slice((ci + 1) * CHUNK - 1, (ci + 1) * CHUNK)
        for g in range(SSD_GROUPS):
            gsl = slice(g * gcols, (g + 1) * gcols)
            bm = bc_s[rows, g * SSD_STATE:(g + 1) * SSD_STATE].astype(BF16)
            cm = bc_s[rows, gw + g * SSD_STATE:gw + (g + 1) * SSD_STATE].astype(BF16)
            cb = _dot_nt(cm, _tile_rows(bm, hpg))
            decay = jnp.exp(jnp.where(tril, cs_s[rows, gsl] - cst_s[ci, g:g + 1, :], -jnp.inf))
            x_bd = jnp.where(same_head, _tile_rows(xdt_s[rows, gsl], hpg), 0.0).astype(BF16)
            y_inter = _dot(cm, states[g].astype(BF16)) * ecs_s[rows, gsl]
            o_s[rows, gsl] = _dot((cb * decay).astype(BF16), x_bd) + y_inter
            states[g] = (states[g] * ecs_s[last, gsl]
                         + _dot_tn(bm, xdtw_s[rows, gsl].astype(BF16)))
        yield
    for g in range(SSD_GROUPS):
        st_s[g] = states[g]

    y = (o_s[...] + d_row_ref[...] * xs_s[...]) * _silu(z)
    gwid = SSD_INNER // SSD_GROUPS
    normed = []
    for g in range(SSD_GROUPS):
        yg = y[:, g * gwid:(g + 1) * gwid]
        normed.append(yg * lax.rsqrt(jnp.mean(yg * yg, axis=-1, keepdims=True) + NORM_EPS))
    y_ref[...] = (jnp.concatenate(normed, axis=1) * gn_ref[...]).astype(BF16)


N_FOX_IN, N_GLA_IN, N_RET_IN, N_SSD_IN = 5, 4, 4, 7
N_LINEAR_SCRATCH = 7
N_SSD_SCRATCH = 10


def _mixers_kernel(x_ref, ln_ref, ws_ref, *refs):
    it = iter(refs)
    take = lambda n: [next(it) for _ in range(n)]
    fox_in, gla_in, ret_in, ssd_in = take(N_FOX_IN), take(N_GLA_IN), take(N_RET_IN), take(N_SSD_IN)
    fox_out, (yb_ref, yc_ref, yd_ref) = take(4), take(3)
    (carry_ref,), gla_s, ret_s, ssd_s = take(1), take(N_LINEAR_SCRATCH), take(N_LINEAR_SCRATCH), take(N_SSD_SCRATCH)

    @pl.when(pl.program_id(0) == 0)
    def _():
        carry_ref[...] = jnp.zeros_like(carry_ref)
        gla_s[-1][...] = jnp.zeros_like(gla_s[-1])
        ret_s[-1][...] = jnp.zeros_like(ret_s[-1])
        ssd_s[-1][...] = jnp.zeros_like(ssd_s[-1])
        ssd_s[0][0:SUBLANES, :] = jnp.zeros((SUBLANES, SSD_CONV_CH), F32)

    h = _rmsnorm(x_ref[...], ln_ref[...]).astype(BF16)
    small = _dot_nt(h, ws_ref[...])
    waiting = [_ssd_body(h, small, *ssd_in, yd_ref, *ssd_s), _gla_body(h, small, *gla_in, yb_ref, *gla_s),
               _ret_body(h, *ret_in, yc_ref, *ret_s), _fox_prep_body(h, small, *fox_in, *fox_out, carry_ref)]
    pending = []
    while waiting or pending:
        if waiting:
            pending.append(waiting.pop(0))
        for body in list(pending):
            if next(body, "done") == "done":
                pending.remove(body)


def _mixers(l, x, ln, w_small, fox_in, gla_in, ret_in, ssd_in, cos_t, sin_t):
    seq = x.shape[0]
    t = ROW_TILE
    row = lambda w: pl.BlockSpec((t, w), lambda i: (i, 0))
    heads = pl.BlockSpec((FOX_HEADS, t, LANES), lambda i: (0, i, 0))
    head_shape = jax.ShapeDtypeStruct((FOX_HEADS, seq, LANES), BF16)
    w_ret, ret_gn = ret_in
    wide = pltpu.VMEM((t, SSD_INNER), F32)
    return pl.pallas_call(
        _mixers_kernel,
        grid=(seq // t,),
        in_specs=([row(D_MODEL), _layer(ln, l), _layer(w_small, l)] + [_layer(a, l) for a in fox_in]
                  + [_layer(a, l) for a in gla_in]
                  + [_layer(w_ret, l), row(LANES), row(LANES), _layer(ret_gn, l)]
                  + [_layer(a, l) for a in ssd_in]),
        out_specs=[heads, heads, heads, pl.BlockSpec((1, SUBLANES, LANES), lambda i: (i, 0, 0)),
                   row(GLA_W), row(RET_W), row(SSD_INNER)],
        out_shape=[head_shape, head_shape, head_shape,
                   jax.ShapeDtypeStruct((seq // t, SUBLANES, LANES), F32),
                   jax.ShapeDtypeStruct((seq, GLA_W), BF16),
                   jax.ShapeDtypeStruct((seq, RET_W), BF16),
                   jax.ShapeDtypeStruct((seq, SSD_INNER), BF16)],
        scratch_shapes=([pltpu.VMEM((SUBLANES, LANES), F32)]
                        + _linear_scratch(t, GLA_W, GLA_HEADS) + _linear_scratch(t, RET_W, RET_HEADS)
                        + [pltpu.VMEM((t + 2 * SUBLANES, SSD_CONV_CH), F32),
                           wide,
                           pltpu.VMEM((t, 2 * SSD_GROUPS * SSD_STATE), F32),
                           wide, wide, wide, wide,
                           pltpu.VMEM((t // CHUNK, SSD_GROUPS, SSD_INNER // SSD_GROUPS), F32),
                           wide,
                           pltpu.VMEM((SSD_GROUPS, SSD_STATE, SSD_INNER // SSD_GROUPS), F32)]),
        compiler_params=pltpu.CompilerParams(dimension_semantics=("arbitrary",),
                                             vmem_limit_bytes=VMEM_LIMIT),
        name="mixers",
    )(x, ln, w_small, *fox_in, *gla_in, w_ret, cos_t, sin_t, ret_gn, *ssd_in)


def _merge_ffn_kernel(x_ref, ln1_ref, wg_ref, ya_ref, yb_ref, yc_ref, yd_ref,
                      wa_ref, wb_ref, wc_ref, wd_ref, wo_ref, ln2_ref, w1_ref, w2_ref, out_ref):
    x = x_ref[...]
    h = _rmsnorm(x, ln1_ref[...]).astype(BF16)

    def gate(b):
        return _sigmoid(_dot_nt(h, wg_ref[b * D_MODEL:(b + 1) * D_MODEL, :]))

    up_a = _dot(ya_ref[0], wa_ref[0:LANES, :])
    for p in range(1, FOX_HEADS // FOX_HEADS_PER_TILE):
        up_a = up_a + _dot(ya_ref[p], wa_ref[p * LANES:(p + 1) * LANES, :])
    merged = gate(0) * up_a
    merged = merged + gate(1) * _dot(yb_ref[...], wb_ref[...])
    merged = merged + gate(2) * _dot(yc_ref[...], wc_ref[...])
    merged = merged + gate(3) * _dot(yd_ref[...], wd_ref[...])
    x = x + _dot(merged.astype(BF16), wo_ref[...])

    h = _rmsnorm(x, ln2_ref[...]).astype(BF16)
    act = _silu(_dot(h, w1_ref[:, 0:FFN_HIDDEN])) * _dot(h, w1_ref[:, FFN_HIDDEN:2 * FFN_HIDDEN])
    out_ref[...] = x + _dot(act.astype(BF16), w2_ref[...])


def _merge_ffn(l, x, ln1, w_gates, y_a, y_b, y_c, y_d, w_up_a, w_up_b, w_up_c, w_up_d, w_out, ln2, w1, w2):
    seq = x.shape[0]
    t = FFN_TILE
    row = lambda w: pl.BlockSpec((t, w), lambda i: (i, 0))
    return pl.pallas_call(
        _merge_ffn_kernel,
        grid=(seq // t,),
        in_specs=[row(D_MODEL), _layer(ln1, l), _layer(w_gates, l),
                  pl.BlockSpec((FOX_HEADS // FOX_HEADS_PER_TILE, t, LANES), lambda i: (0, i, 0)),
                  row(GLA_W), row(RET_W), row(SSD_INNER),
                  _layer(w_up_a, l), _layer(w_up_b, l), _layer(w_up_c, l),
                  _layer(w_up_d, l), _layer(w_out, l), _layer(ln2, l), _layer(w1, l), _layer(w2, l)],
        out_specs=row(D_MODEL),
        out_shape=jax.ShapeDtypeStruct((seq, D_MODEL), F32),
        compiler_params=pltpu.CompilerParams(dimension_semantics=("arbitrary",),
                                             vmem_limit_bytes=VMEM_LIMIT),
        name="merge_swiglu",
    )(x, ln1, w_gates, y_a, y_b, y_c, y_d, w_up_a, w_up_b, w_up_c, w_up_d, w_out, ln2, w1, w2)


def _rows(a):
    return a.reshape(a.shape[0], 1, -1)


def _in_proj_weights(w_in):
    offs = np.concatenate([[0], np.cumsum(IN_SPLITS)])
    (fq, fk, fv, ff, gq, gk, gv, glr, gr, rq, rk, rv, rg, z, xbc, dt, gates) = [
        w_in[:, :, offs[n]:offs[n + 1]] for n in range(len(IN_SPLITS))]
    group = lambda parts: jnp.transpose(jnp.concatenate(parts, axis=2), (0, 2, 1)).astype(BF16)
    gap = lambda n: jnp.zeros(w_in.shape[:2] + (n,), w_in.dtype)
    w_small = group([ff, glr, gap(SMALL_DT - SMALL_GLR - GLA_LOWRANK), dt, gap(LANES - SMALL_DT - SSD_HEADS)])
    return w_small, group([fq, fk, fv]), group([gq, gk, gv, gr]), group([rq, rk, rv, rg]), group([z, xbc]), group([gates])


def kernel(x, positions, ln1, ln2, w_in, fox_bf, fox_qn, fox_kn, gla_w2, gla_b, gla_norm, ret_norm,
           ssd_conv_w, ssd_conv_b, ssd_dt_bias, ssd_a_log, ssd_d, ssd_norm,
           w_up_a, w_up_b, w_up_c, w_up_d, w_out, w_ffn_in, w_ffn_out):
    bsz, seq, d = x.shape
    assert bsz == 1 and d == D_MODEL and seq % ROW_TILE == 0 and seq % FOX_TQ == 0 and FOX_TQ % ROW_TILE == 0
    depth = ln1.shape[0]
    xr = x.reshape(seq, d)
    cos_t, sin_t = _rotary_tables(positions, seq)

    bf16 = lambda w: w.astype(BF16)
    ln1_r, ln2_r = _rows(ln1), _rows(ln2)
    fox_bf_r = _rows(jnp.pad(fox_bf, ((0, 0), (SMALL_FF, LANES - SMALL_FF - FOX_HEADS))))
    fox_qn_r, fox_kn_r = _rows(jnp.tile(fox_qn, (1, 2))), _rows(jnp.tile(fox_kn, (1, 2)))
    fox_shift_r, fox_par = _fox_params(fox_qn, fox_kn)
    gla_w2_p = bf16(jnp.pad(gla_w2, ((0, 0), (SMALL_GLR, LANES - SMALL_GLR - GLA_LOWRANK), (0, 0))))
    gla_b_r, gla_gn_r = _rows(gla_b), _rows(jnp.tile(gla_norm, (1, GLA_HEADS)))
    ret_gn_r = _rows(jnp.tile(ret_norm, (1, RET_HEADS)))
    conv_b_r = _rows(ssd_conv_b)
    at_dt = lambda a: _rows(jnp.pad(a, ((0, 0), (SMALL_DT, LANES - SMALL_DT - SSD_HEADS))))
    dtb_r, alog_r = at_dt(ssd_dt_bias), at_dt(ssd_a_log)
    d_r, ssd_gn_r = _rows(jnp.repeat(ssd_d, HEAD_DIM, axis=1)), _rows(ssd_norm)
    up_a, up_b, up_c, up_d, w_o = bf16(w_up_a), bf16(w_up_b), bf16(w_up_c), bf16(w_up_d), bf16(w_out)
    w1, w2 = bf16(w_ffn_in), bf16(w_ffn_out)
    w_small, w_fox, w_gla, w_ret, w_ssd, w_gates = _in_proj_weights(w_in)

    for l in range(depth):
        q, k, v, c, y_b, y_c, y_d = _mixers(
            l, xr, ln1_r, w_small,
            (w_fox, fox_bf_r, fox_qn_r, fox_kn_r, fox_shift_r),
            (w_gla, gla_w2_p, gla_b_r, gla_gn_r),
            (w_ret, ret_gn_r),
            (w_ssd, ssd_conv_w, conv_b_r, dtb_r, alog_r, d_r, ssd_gn_r),
            cos_t, sin_t)
        y_a = _fox_attention(fox_par[l], q, k, v, c)
        xr = _merge_ffn(l, xr, ln1_r, w_gates, y_a, y_b, y_c, y_d, up_a, up_b, up_c, up_d, w_o, ln2_r, w1, w2)
    return xr.reshape(bsz, seq, d)
```

```python
import math

import numpy as np
import jax
import jax.numpy as jnp
from jax import lax
from jax.experimental import pallas as pl
from jax.experimental.pallas import tpu as pltpu

D_MODEL = 1024
HEAD_DIM = 64
CHUNK = 64
FOX_HEADS = 4
GLA_HEADS = 4
GLA_LOWRANK = 16
GLA_TAU = 16.0
RET_HEADS = 4
ROPE_THETA = 10000.0
SSD_HEADS = 8
SSD_GROUPS = 2
SSD_STATE = 64
SSD_CONV = 4
SSD_INNER = SSD_HEADS * HEAD_DIM
SSD_CONV_CH = SSD_INNER + 2 * SSD_GROUPS * SSD_STATE
FOX_W = FOX_HEADS * HEAD_DIM
GLA_W = GLA_HEADS * HEAD_DIM
RET_W = RET_HEADS * HEAD_DIM
N_BRANCH = 4
FFN_HIDDEN = ((8 * D_MODEL + 3 * 256 - 1) // (3 * 256)) * 256
NORM_EPS = 1e-6
IN_SPLITS = (FOX_W, FOX_W, FOX_W, FOX_HEADS,
             GLA_W, GLA_W, GLA_W, GLA_LOWRANK, GLA_W,
             RET_W, RET_W, RET_W, RET_W,
             SSD_INNER, SSD_CONV_CH, SSD_HEADS,
             N_BRANCH * D_MODEL)

LANES = 128
SUBLANES = 8
VMEM_LIMIT = 56 * 1024 * 1024

ROW_TILE = 512
FOX_TQ = 512
FOX_TK = 512
FFN_TILE = 512
PROJ_PIECE = 512

F32 = jnp.float32
BF16 = jnp.bfloat16


def _rmsnorm(x, g):
    return x * lax.rsqrt(jnp.mean(x * x, axis=-1, keepdims=True) + NORM_EPS) * g


def _log_sigmoid(x):
    return jnp.minimum(x, 0.0) - jnp.log(1.0 + jnp.exp(-jnp.abs(x)))


def _softplus(x):
    return jnp.maximum(x, 0.0) + jnp.log(1.0 + jnp.exp(-jnp.abs(x)))


def _sigmoid(x):
    return 0.5 * jnp.tanh(0.5 * x) + 0.5


def _silu(x):
    return x * _sigmoid(x)


def _split3(x):
    hi = x.astype(BF16)
    r1 = x - hi.astype(F32)
    mid = r1.astype(BF16)
    lo = (r1 - mid.astype(F32)).astype(BF16)
    return hi, mid, lo


def _split2(x):
    hi = x.astype(BF16)
    return hi, (x - hi.astype(F32)).astype(BF16)


def _dot(a, b):
    return jnp.dot(a, b, preferred_element_type=F32)


def _dot_nt(a, b):
    return lax.dot_general(a, b, (((1,), (1,)), ((), ())), preferred_element_type=F32)


def _dot_tn(a, b):
    return lax.dot_general(a, b, (((0,), (0,)), ((), ())), preferred_element_type=F32)


def _sel_dot(mat, x):
    hi, mid, lo = _split3(x)
    return _dot(mat, hi) + _dot(mat, mid) + _dot(mat, lo)


def _dot_sel2(x, mat):
    hi, lo = _split2(x)
    return _dot(hi, mat) + _dot(lo, mat)


def _chunk_cumsum(x):
    t = x.shape[0]
    r = lax.broadcasted_iota(jnp.int32, (t, t), 0)
    c = lax.broadcasted_iota(jnp.int32, (t, t), 1)
    incl = jnp.where(((r // CHUNK) == (c // CHUNK)) & (c <= r), 1.0, 0.0).astype(BF16)
    hi, lo = _split2(x)
    b = _dot(incl, hi) + _dot(incl, lo)
    return b, _chunk_last(b) - b


def _chunk_last(b):
    t, w = b.shape
    return jnp.concatenate([jnp.broadcast_to(b[e - 1:e, :], (CHUNK, w)) for e in range(CHUNK, t + 1, CHUNK)],
                           axis=0)


def _group_mean_sq(o, group):
    w = o.shape[-1]
    r = lax.broadcasted_iota(jnp.int32, (w, w), 0)
    c = lax.broadcasted_iota(jnp.int32, (w, w), 1)
    bd = jnp.where((r // group) == (c // group), 1.0, 0.0).astype(BF16)
    return _dot_sel2(o * o, bd) * (1.0 / group)


def _project(h, wt_ref):
    n = wt_ref.shape[0]
    parts = []
    for a in range(0, n, PROJ_PIECE):
        parts.append(_dot_nt(h, wt_ref[a:min(a + PROJ_PIECE, n), :]))
        yield
    return jnp.concatenate(parts, axis=1)


def _layer(stacked, l):
    tail = tuple(stacked.shape[1:])
    return pl.BlockSpec((None,) + tail, lambda *_: (l,) + (0,) * len(tail), pipeline_mode=pl.Buffered(1))


def _rotary_kernel(pos_ref, inv_ref, sign_ref, cos_ref, sin_ref):
    ang = pos_ref[...] * inv_ref[...]
    cos_ref[...] = jnp.cos(ang)
    sin_ref[...] = jnp.sin(ang) * sign_ref[...]


def _rotary_tables(positions, seq):
    half = HEAD_DIM // 2
    inv = ROPE_THETA ** (-jnp.arange(half, dtype=F32) / half)
    reps = LANES // HEAD_DIM
    inv_row = jnp.tile(inv, 2 * reps).reshape(1, LANES)
    sign_row = jnp.tile(jnp.concatenate([-jnp.ones((half,), F32), jnp.ones((half,), F32)]),
                        reps).reshape(1, LANES)
    pos = positions.astype(F32).reshape(seq, 1)
    t = ROW_TILE
    return pl.pallas_call(
        _rotary_kernel,
        grid=(seq // t,),
        in_specs=[pl.BlockSpec((t, 1), lambda i: (i, 0)),
                  pl.BlockSpec((1, LANES), lambda i: (0, 0)),
                  pl.BlockSpec((1, LANES), lambda i: (0, 0))],
        out_specs=[pl.BlockSpec((t, LANES), lambda i: (i, 0)),
                   pl.BlockSpec((t, LANES), lambda i: (i, 0))],
        out_shape=[jax.ShapeDtypeStruct((seq, LANES), F32)] * 2,
        name="rotary_tables",
    )(pos, inv_row, sign_row)


SMALL_FF = 0
SMALL_GLR = FOX_HEADS
SMALL_DT = 24
LOG2E = math.log2(math.e)
F32_EXP_ZERO = 88.0
FOX_FAST_MAX_LOGIT = 40.0
FOX_BOUND_MARGIN = 1.02


def _fox_prep_body(h, small, w_ref, bf_ref, qn_ref, kn_ref, shift_ref, q_out, k_out, v_out, c_out, carry_ref):
    t = h.shape[0]
    u = yield from _project(h, w_ref)
    ls = _log_sigmoid(small + bf_ref[...])
    r = lax.broadcasted_iota(jnp.int32, (t, t), 0)
    cc = lax.broadcasted_iota(jnp.int32, (t, t), 1)
    tri = jnp.where(cc <= r, 1.0, 0.0).astype(BF16)
    c = _sel_dot(tri, ls) + carry_ref[0:1, :]
    carry_ref[0:1, :] = c[t - 1:t, :]
    c_out[0] = jnp.concatenate([c[0:1, :], c[t - 1:t, :], jnp.zeros((SUBLANES - 2, LANES), F32)], axis=0)
    c_hi, c_mid, c_lo = _split3(c * LOG2E)
    c_hi, c_mid, c_lo = c_hi.astype(F32), c_mid.astype(F32), c_lo.astype(F32)

    lane = lax.broadcasted_iota(jnp.int32, (t, LANES), 1)
    low = lane < HEAD_DIM
    j = lane - HEAD_DIM
    shift = shift_ref[...]

    def head_norm(pair, gain, mult):
        sq = pair * pair
        s_lo = jnp.sum(jnp.where(low, sq, 0.0), axis=-1, keepdims=True)
        s_hi = jnp.sum(jnp.where(low, 0.0, sq), axis=-1, keepdims=True)
        ms = jnp.where(low, s_lo, s_hi) * (1.0 / HEAD_DIM)
        return pair * lax.rsqrt(ms + NORM_EPS) * (gain * mult)

    for p in range(FOX_HEADS // 2):
        qp = head_norm(u[:, p * LANES:(p + 1) * LANES], qn_ref[...], HEAD_DIM ** -0.5 * LOG2E)
        kp = head_norm(u[:, FOX_W + p * LANES:FOX_W + (p + 1) * LANES], kn_ref[...], 1.0)
        vp = u[:, 2 * FOX_W + p * LANES:2 * FOX_W + (p + 1) * LANES]
        for sub in range(2):
            hd = 2 * p + sub
            if sub == 1:
                qp, kp, vp = (pltpu.roll(a, HEAD_DIM, axis=1) for a in (qp, kp, vp))
            ch, cm, cl = (jnp.broadcast_to(a[:, SMALL_FF + hd:SMALL_FF + hd + 1], (t, LANES)) for a in (c_hi, c_mid, c_lo))
            aug_q = jnp.where(j == 0, ch, jnp.where(j == 1, cm, jnp.where(j == 2, cl,
                              jnp.where(j < 7, 1.0, 0.0))))
            aug_k = jnp.where(j < 3, 1.0, jnp.where(j == 3, -ch, jnp.where(j == 4, -cm,
                              jnp.where(j == 5, -cl, jnp.where(j == 6, -shift, 0.0)))))
            aug_v = jnp.where(j == 0, 1.0, 0.0)
            q_out[hd] = jnp.where(low, qp, aug_q).astype(BF16)
            k_out[hd] = jnp.where(low, kp, aug_k).astype(BF16)
            v_out[hd] = jnp.where(low, vp, aug_v).astype(BF16)


def _fox_attn_kernel(par_ref, cq0_ref, cend_ref, q_ref, k_ref, v_ref, o_ref):
    i = pl.program_id(0)
    nh, tq, _ = q_ref.shape
    thr = par_ref[0]

    def first_block(hd):
        c0 = cq0_ref[i, hd]
        return lax.while_loop(
            lambda j: jnp.logical_and(j > 0, c0 - cend_ref[jnp.maximum(j - 1, 0), hd] >= -thr),
            lambda j: j - 1, i)

    def keys(hd, j, nblk):
        off = pl.multiple_of(j * FOX_TK, FOX_TK)
        return k_ref[hd, pl.ds(off, nblk * FOX_TK), :], v_ref[hd, pl.ds(off, nblk * FOX_TK), :]

    def causal(s):
        r = lax.broadcasted_iota(jnp.int32, s.shape, 0)
        c = lax.broadcasted_iota(jnp.int32, s.shape, 1)
        return jnp.where(c <= r, s, -jnp.inf)

    def fixed_shift():
        def block(hd, j, nblk):
            k, v = keys(hd, j, nblk)
            return _dot(jnp.exp2(_dot_nt(q_ref[hd], k)).astype(BF16), v)

        def diagonal(hd):
            k, v = keys(hd, i, 1)
            return _dot(jnp.exp2(causal(_dot_nt(q_ref[hd], k))).astype(BF16), v)

        accs = [diagonal(hd) for hd in range(nh)]
        for hd in range(nh):
            acc, j_lo = accs[hd], first_block(hd)
            n4 = (i - j_lo + 3) // 4 * 4
            j_lo = jnp.where(n4 <= i, i - n4, j_lo)
            n = i - j_lo
            acc = lax.cond(n % 2 == 1, lambda: acc + block(hd, j_lo, 1), lambda: acc)
            acc = lax.cond((n // 2) % 2 == 1, lambda: acc + block(hd, j_lo + n % 2, 2), lambda: acc)
            first = j_lo + n % 4
            accs[hd] = lax.fori_loop(0, n // 4, lambda p, acc: acc + block(hd, first + 4 * p, 4), acc)
        return accs

    def running_max():
        def one_head(hd):
            q = q_ref[hd]

            def step(j, carry, masked):
                m, acc = carry
                k, v = keys(hd, j, 1)
                s = _dot_nt(q, k)
                if masked:
                    s = causal(s)
                m_new = jnp.maximum(m, jnp.max(s, axis=-1, keepdims=True))
                p = jnp.exp2(s - m_new)
                return m_new, jnp.exp2(m - m_new) * acc + _dot(p.astype(BF16), v)
            init = (jnp.full((tq, 1), -jnp.inf, F32), jnp.zeros((tq, LANES), F32))
            carry = step(i, init, True)
            return lax.fori_loop(first_block(hd), i, lambda j, c: step(j, c, False), carry)[1]
        return [one_head(hd) for hd in range(nh)]

    accs = lax.cond(par_ref[1] > 0.0, fixed_shift, running_max)
    outs = [acc / acc[:, HEAD_DIM:HEAD_DIM + 1] for acc in accs]
    low = lax.broadcasted_iota(jnp.int32, (tq, LANES), 1) < HEAD_DIM
    for p in range(nh // FOX_HEADS_PER_TILE):
        o_ref[p] = jnp.where(low, outs[2 * p], pltpu.roll(outs[2 * p + 1], HEAD_DIM, axis=1)).astype(BF16)


FOX_HEADS_PER_TILE = LANES // HEAD_DIM


def _fox_attn(par, cq0, cend, q, k, v):
    assert FOX_TQ == FOX_TK and FOX_HEADS_PER_TILE == 2
    nh, seq, _ = q.shape
    kv_spec = pl.BlockSpec((nh, seq, LANES), lambda i, *_: (0, 0, 0), pipeline_mode=pl.Buffered(1))
    return pl.pallas_call(
        _fox_attn_kernel,
        grid_spec=pltpu.PrefetchScalarGridSpec(
            num_scalar_prefetch=3,
            grid=(seq // FOX_TQ,),
            in_specs=[pl.BlockSpec((nh, FOX_TQ, LANES), lambda i, *_: (0, i, 0)), kv_spec, kv_spec],
            out_specs=pl.BlockSpec((nh // FOX_HEADS_PER_TILE, FOX_TQ, LANES), lambda i, *_: (0, i, 0))),
        out_shape=jax.ShapeDtypeStruct((nh // FOX_HEADS_PER_TILE, seq, LANES), BF16),
        compiler_params=pltpu.CompilerParams(dimension_semantics=("arbitrary",),
                                             vmem_limit_bytes=VMEM_LIMIT),
        name="fox_attn",
    )(par, cq0, cend, q, k, v)


def _fox_params(qn, kn):
    bound = FOX_BOUND_MARGIN * HEAD_DIM ** 0.5 * jnp.max(jnp.abs(qn), axis=1) * jnp.max(jnp.abs(kn), axis=1)
    shift = jnp.broadcast_to((LOG2E * bound)[:, None, None], (bound.shape[0], 1, LANES))
    par = jnp.stack([F32_EXP_ZERO + 2.0 * bound, (bound < FOX_FAST_MAX_LOGIT).astype(F32)], axis=1)
    return shift, par


def _fox_attention(par, q, k, v, c):
    per_tile = FOX_TQ // ROW_TILE
    heads = slice(SMALL_FF, SMALL_FF + FOX_HEADS)
    cq0 = c[0::per_tile, 0, heads]
    cend = c[per_tile - 1::per_tile, 1, heads]
    return _fox_attn(par, cq0, cend, q, k, v)


def _tile_rows(a, n):
    return jnp.concatenate([a] * n, axis=0)


def _linear_attn_chunks(qd_s, ki_s, ke_s, v_s, dec_s, o_s, st_s, nheads):
    t, w = qd_s.shape
    assert w == nheads * HEAD_DIM
    r = lax.broadcasted_iota(jnp.int32, (w, w), 0)
    c = lax.broadcasted_iota(jnp.int32, (w, w), 1)
    same_head = (r // HEAD_DIM) == (c // HEAD_DIM)
    rr = lax.broadcasted_iota(jnp.int32, (CHUNK, w), 0)
    cc = lax.broadcasted_iota(jnp.int32, (CHUNK, w), 1)
    tril = (cc % CHUNK) <= rr
    state = st_s[...]
    for ci in range(t // CHUNK):
        rows = slice(ci * CHUNK, (ci + 1) * CHUNK)
        last = slice((ci + 1) * CHUNK - 1, (ci + 1) * CHUNK)
        qd = qd_s[rows, :].astype(BF16)
        ke = ke_s[rows, :].astype(BF16)
        vv = v_s[rows, :].astype(BF16)
        k_bd = jnp.where(same_head, _tile_rows(ki_s[rows, :], nheads), 0.0).astype(BF16)
        v_bd = jnp.where(same_head, _tile_rows(v_s[rows, :], nheads), 0.0).astype(BF16)
        attn = jnp.where(tril, _dot_nt(qd, k_bd), 0.0).astype(BF16)
        o_s[rows, :] = _dot(attn, v_bd) + _dot_nt(qd, state.astype(BF16))
        state = state * dec_s[last, :] + jnp.where(same_head, _dot_tn(vv, ke), 0.0)
        yield
    st_s[...] = state


def _gla_body(h, small, w_ref, w2_ref, b2_ref, gn_ref, y_ref, qd_s, ki_s, ke_s, v_s, dec_s, o_s, st_s):
    u = yield from _project(h, w_ref)
    z = _dot(small.astype(BF16), w2_ref[...]) + b2_ref[...]
    log_a = _log_sigmoid(z) * (1.0 / GLA_TAU)
    b, rem = _chunk_cumsum(log_a)
    eb = jnp.exp(b)
    k = u[:, GLA_W:2 * GLA_W]
    qd_s[...] = u[:, 0:GLA_W] * (HEAD_DIM ** -0.5) * eb
    ki_s[...] = k * jnp.exp(-b)
    ke_s[...] = k * jnp.exp(rem)
    v_s[...] = u[:, 2 * GLA_W:3 * GLA_W]
    dec_s[...] = eb
    yield
    yield from _linear_attn_chunks(qd_s, ki_s, ke_s, v_s, dec_s, o_s, st_s, GLA_HEADS)
    o = o_s[...]
    y = o * lax.rsqrt(_group_mean_sq(o, HEAD_DIM) + NORM_EPS) * gn_ref[...]
    y_ref[...] = (y * _silu(u[:, 3 * GLA_W:4 * GLA_W])).astype(BF16)


def _ret_body(h, w_ref, cos_ref, sin_ref, gn_ref, y_ref, qd_s, ki_s, ke_s, v_s, dec_s, o_s, st_s):
    t = h.shape[0]
    u = yield from _project(h, w_ref)
    cos, sin = cos_ref[...], sin_ref[...]
    lane = lax.broadcasted_iota(jnp.int32, (t, RET_W), 1)
    half = HEAD_DIM // 2
    first_half = (lax.broadcasted_iota(jnp.int32, (t, LANES), 1) % HEAD_DIM) < half

    def rotate(a):
        blocks = []
        for b in range(0, RET_W, LANES):
            ab = a[:, b:b + LANES]
            swapped = jnp.where(first_half, pltpu.roll(ab, LANES - half, axis=1),
                                pltpu.roll(ab, half, axis=1))
            blocks.append(ab * cos + swapped * sin)
        return jnp.concatenate(blocks, axis=1)

    q = rotate(u[:, 0:RET_W])
    k = rotate(u[:, RET_W:2 * RET_W]) * (HEAD_DIM ** -0.5)
    row = lax.broadcasted_iota(jnp.int32, (t, RET_W), 0)
    lg = jnp.zeros((t, RET_W), F32)
    for hd in range(RET_HEADS):
        lg = jnp.where(lane // HEAD_DIM == hd, math.log(1.0 - 2.0 ** (-5.0 - hd)), lg)
    pos = (row % CHUNK).astype(F32)
    qw = jnp.exp((pos + 1.0) * lg)
    qd_s[...] = q * qw
    ki_s[...] = k * jnp.exp(-(pos + 1.0) * lg)
    ke_s[...] = k * jnp.exp((CHUNK - 1.0 - pos) * lg)
    v_s[...] = u[:, 2 * RET_W:3 * RET_W]
    dec_s[...] = qw
    yield
    yield from _linear_attn_chunks(qd_s, ki_s, ke_s, v_s, dec_s, o_s, st_s, RET_HEADS)
    o = o_s[...]
    y = o * lax.rsqrt(_group_mean_sq(o, HEAD_DIM) + NORM_EPS) * gn_ref[...]
    y_ref[...] = (y * _silu(u[:, 3 * RET_W:4 * RET_W])).astype(BF16)


def _linear_scratch(t, width, nheads):
    return [pltpu.VMEM((t, width), F32)] * 6 + [pltpu.VMEM((width, width), F32)]


SSD_PROJ = SSD_INNER + SSD_CONV_CH


def _ssd_body(h, small, w_ref, cw_ref, cb_ref, dtb_row_ref, alog_row_ref, d_row_ref, gn_ref, y_ref,
              xp_s, xs_s, bc_s, cs_s, ecs_s, xdt_s, xdtw_s, cst_s, o_s, st_s):
    t = h.shape[0]
    nc = t // CHUNK
    u = yield from _project(h, w_ref)
    z = u[:, 0:SSD_INNER]

    xp_s[SUBLANES:SUBLANES + t, :] = u[:, SSD_INNER:SSD_PROJ]
    conv = cb_ref[...] + cw_ref[SSD_CONV - 1:SSD_CONV, :] * xp_s[SUBLANES:SUBLANES + t, :]
    for kk in range(SSD_CONV - 1):
        off = SUBLANES - (SSD_CONV - 1) + kk
        conv = conv + cw_ref[kk:kk + 1, :] * xp_s[off:off + t, :]
    xp_s[0:SUBLANES, :] = xp_s[t:t + SUBLANES, :]
    xbc = _silu(conv)
    xs = xbc[:, 0:SSD_INNER]
    xs_s[...] = xs
    bc_s[...] = xbc[:, SSD_INNER:SSD_CONV_CH]

    dt = _softplus(small + dtb_row_ref[...])
    dta = dt * -jnp.exp(alog_row_ref[...])
    cs, _ = _chunk_cumsum(dta)
    cs_t = cs.T[SMALL_DT:SMALL_DT + SSD_HEADS, :]
    hpg = SSD_HEADS // SSD_GROUPS
    gcols = hpg * HEAD_DIM
    for ci in range(nc):
        for g in range(SSD_GROUPS):
            cst_s[ci, g:g + 1, :] = jnp.concatenate(
                [cs_t[g * hpg + rr:g * hpg + rr + 1, ci * CHUNK:(ci + 1) * CHUNK] for rr in range(hpg)], axis=1)

    er = lax.broadcasted_iota(jnp.int32, (LANES, SSD_INNER), 0)
    ec = lax.broadcasted_iota(jnp.int32, (LANES, SSD_INNER), 1)
    expand = jnp.where(ec // HEAD_DIM == er - SMALL_DT, 1.0, 0.0).astype(BF16)
    cs_x = _dot_sel2(cs, expand)
    xdt = xs * _dot_sel2(dt, expand)
    cs_s[...] = cs_x
    ecs_s[...] = jnp.exp(cs_x)
    xdt_s[...] = xdt
    xdtw_s[...] = xdt * jnp.exp(_chunk_last(cs_x) - cs_x)

    r = lax.broadcasted_iota(jnp.int32, (gcols, gcols), 0)
    c = lax.broadcasted_iota(jnp.int32, (gcols, gcols), 1)
    same_head = (r // HEAD_DIM) == (c // HEAD_DIM)
    rr = lax.broadcasted_iota(jnp.int32, (CHUNK, gcols), 0)
    cc = lax.broadcasted_iota(jnp.int32, (CHUNK, gcols), 1)
    tril = (cc % CHUNK) <= rr
    gw = SSD_GROUPS * SSD_STATE

    states = [st_s[g] for g in range(SSD_GROUPS)]
    for ci in range(nc):
        rows = slice(ci * CHUNK, (ci + 1) * CHUNK)
        last = slice((ci + 1) * CHUNK - 1, (ci + 1) * CHUNK)
        for g in range(SSD_GROUPS):
            gsl = slice(g * gcols, (g + 1) * gcols)
            bm = bc_s[rows, g * SSD_STATE:(g + 1) * SSD_STATE].astype(BF16)
            cm = bc_s[rows, gw + g * SSD_STATE:gw + (g + 1) * SSD_STATE].astype(BF16)
            cb = _dot_nt(cm, _tile_rows(bm, hpg))
            decay = jnp.exp(jnp.where(tril, cs_s[rows, gsl] - cst_s[ci, g:g + 1, :], -jnp.inf))
            x_bd = jnp.where(same_head, _tile_rows(xdt_s[rows, gsl], hpg), 0.0).astype(BF16)
            y_inter = _dot(cm, states[g].astype(BF16)) * ecs_s[rows, gsl]
            o_s[rows, gsl] = _dot((cb * decay).astype(BF16), x_bd) + y_inter
            states[g] = (states[g] * ecs_s[last, gsl]
                         + _dot_tn(bm, xdtw_s[rows, gsl].astype(BF16)))
        yield
    for g in range(SSD_GROUPS):
        st_s[g] = states[g]

    y = (o_s[...] + d_row_ref[...] * xs_s[...]) * _silu(z)
    gwid = SSD_INNER // SSD_GROUPS
    normed = []
    for g in range(SSD_GROUPS):
        yg = y[:, g * gwid:(g + 1) * gwid]
        normed.append(yg * lax.rsqrt(jnp.mean(yg * yg, axis=-1, keepdims=True) + NORM_EPS))
    y_ref[...] = (jnp.concatenate(normed, axis=1) * gn_ref[...]).astype(BF16)


N_FOX_IN, N_GLA_IN, N_RET_IN, N_SSD_IN = 5, 4, 4, 7
N_LINEAR_SCRATCH = 7
N_SSD_SCRATCH = 10


def _mixers_kernel(x_ref, ln_ref, ws_ref, *refs):
    it = iter(refs)
    take = lambda n: [next(it) for _ in range(n)]
    fox_in, gla_in, ret_in, ssd_in = take(N_FOX_IN), take(N_GLA_IN), take(N_RET_IN), take(N_SSD_IN)
    fox_out, (yb_ref, yc_ref, yd_ref) = take(4), take(3)
    (carry_ref,), gla_s, ret_s, ssd_s = take(1), take(N_LINEAR_SCRATCH), take(N_LINEAR_SCRATCH), take(N_SSD_SCRATCH)

    @pl.when(pl.program_id(0) == 0)
    def _():
        carry_ref[...] = jnp.zeros_like(carry_ref)
        gla_s[-1][...] = jnp.zeros_like(gla_s[-1])
        ret_s[-1][...] = jnp.zeros_like(ret_s[-1])
        ssd_s[-1][...] = jnp.zeros_like(ssd_s[-1])
        ssd_s[0][0:SUBLANES, :] = jnp.zeros((SUBLANES, SSD_CONV_CH), F32)

    h = _rmsnorm(x_ref[...], ln_ref[...]).astype(BF16)
    small = _dot_nt(h, ws_ref[...])
    waiting = [_ssd_body(h, small, *ssd_in, yd_ref, *ssd_s), _gla_body(h, small, *gla_in, yb_ref, *gla_s),
               _ret_body(h, *ret_in, yc_ref, *ret_s), _fox_prep_body(h, small, *fox_in, *fox_out, carry_ref)]
    pending = []
    while waiting or pending:
        if waiting:
            pending.append(waiting.pop(0))
        for body in list(pending):
            if next(body, "done") == "done":
                pending.remove(body)


def _mixers(l, x, ln, w_small, fox_in, gla_in, ret_in, ssd_in, cos_t, sin_t):
    seq = x.shape[0]
    t = ROW_TILE
    row = lambda w: pl.BlockSpec((t, w), lambda i: (i, 0))
    heads = pl.BlockSpec((FOX_HEADS, t, LANES), lambda i: (0, i, 0))
    head_shape = jax.ShapeDtypeStruct((FOX_HEADS, seq, LANES), BF16)
    w_ret, ret_gn = ret_in
    wide = pltpu.VMEM((t, SSD_INNER), F32)
    return pl.pallas_call(
        _mixers_kernel,
        grid=(seq // t,),
        in_specs=([row(D_MODEL), _layer(ln, l), _layer(w_small, l)] + [_layer(a, l) for a in fox_in]
                  + [_layer(a, l) for a in gla_in]
                  + [_layer(w_ret, l), row(LANES), row(LANES), _layer(ret_gn, l)]
                  + [_layer(a, l) for a in ssd_in]),
        out_specs=[heads, heads, heads, pl.BlockSpec((1, SUBLANES, LANES), lambda i: (i, 0, 0)),
                   row(GLA_W), row(RET_W), row(SSD_INNER)],
        out_shape=[head_shape, head_shape, head_shape,
                   jax.ShapeDtypeStruct((seq // t, SUBLANES, LANES), F32),
                   jax.ShapeDtypeStruct((seq, GLA_W), BF16),
                   jax.ShapeDtypeStruct((seq, RET_W), BF16),
                   jax.ShapeDtypeStruct((seq, SSD_INNER), BF16)],
        scratch_shapes=([pltpu.VMEM((SUBLANES, LANES), F32)]
                        + _linear_scratch(t, GLA_W, GLA_HEADS) + _linear_scratch(t, RET_W, RET_HEADS)
                        + [pltpu.VMEM((t + 2 * SUBLANES, SSD_CONV_CH), F32),
                           wide,
                           pltpu.VMEM((t, 2 * SSD_GROUPS * SSD_STATE), F32),
                           wide, wide, wide, wide,
                           pltpu.VMEM((t // CHUNK, SSD_GROUPS, SSD_INNER // SSD_GROUPS), F32),
                           wide,
                           pltpu.VMEM((SSD_GROUPS, SSD_STATE, SSD_INNER // SSD_GROUPS), F32)]),
        compiler_params=pltpu.CompilerParams(dimension_semantics=("arbitrary",),
                                             vmem_limit_bytes=VMEM_LIMIT),
        name="mixers",
    )(x, ln, w_small, *fox_in, *gla_in, w_ret, cos_t, sin_t, ret_gn, *ssd_in)


def _merge_ffn_kernel(x_ref, ln1_ref, wg_ref, ya_ref, yb_ref, yc_ref, yd_ref,
                      wa_ref, wb_ref, wc_ref, wd_ref, wo_ref, ln2_ref, w1_ref, w2_ref, out_ref):
    x = x_ref[...]
    h = _rmsnorm(x, ln1_ref[...]).astype(BF16)

    def gate(b):
        return _sigmoid(_dot_nt(h, wg_ref[b * D_MODEL:(b + 1) * D_MODEL, :]))

    up_a = _dot(ya_ref[0], wa_ref[0:LANES, :])
    for p in range(1, FOX_HEADS // FOX_HEADS_PER_TILE):
        up_a = up_a + _dot(ya_ref[p], wa_ref[p * LANES:(p + 1) * LANES, :])
    merged = gate(0) * up_a
    merged = merged + gate(1) * _dot(yb_ref[...], wb_ref[...])
    merged = merged + gate(2) * _dot(yc_ref[...], wc_ref[...])
    merged = merged + gate(3) * _dot(yd_ref[...], wd_ref[...])
    x = x + _dot(merged.astype(BF16), wo_ref[...])

    h = _rmsnorm(x, ln2_ref[...]).astype(BF16)
    act = _silu(_dot(h, w1_ref[:, 0:FFN_HIDDEN])) * _dot(h, w1_ref[:, FFN_HIDDEN:2 * FFN_HIDDEN])
    out_ref[...] = x + _dot(act.astype(BF16), w2_ref[...])


def _merge_ffn(l, x, ln1, w_gates, y_a, y_b, y_c, y_d, w_up_a, w_up_b, w_up_c, w_up_d, w_out, ln2, w1, w2):
    seq = x.shape[0]
    t = FFN_TILE
    row = lambda w: pl.BlockSpec((t, w), lambda i: (i, 0))
    return pl.pallas_call(
        _merge_ffn_kernel,
        grid=(seq // t,),
        in_specs=[row(D_MODEL), _layer(ln1, l), _layer(w_gates, l),
                  pl.BlockSpec((FOX_HEADS // FOX_HEADS_PER_TILE, t, LANES), lambda i: (0, i, 0)),
                  row(GLA_W), row(RET_W), row(SSD_INNER),
                  _layer(w_up_a, l), _layer(w_up_b, l), _layer(w_up_c, l),
                  _layer(w_up_d, l), _layer(w_out, l), _layer(ln2, l), _layer(w1, l), _layer(w2, l)],
        out_specs=row(D_MODEL),
        out_shape=jax.ShapeDtypeStruct((seq, D_MODEL), F32),
        compiler_params=pltpu.CompilerParams(dimension_semantics=("arbitrary",),
                                             vmem_limit_bytes=VMEM_LIMIT),
        name="merge_swiglu",
    )(x, ln1, w_gates, y_a, y_b, y_c, y_d, w_up_a, w_up_b, w_up_c, w_up_d, w_out, ln2, w1, w2)


def _rows(a):
    return a.reshape(a.shape[0], 1, -1)


def _in_proj_weights(w_in):
    offs = np.concatenate([[0], np.cumsum(IN_SPLITS)])
    (fq, fk, fv, ff, gq, gk, gv, glr, gr, rq, rk, rv, rg, z, xbc, dt, gates) = [
        w_in[:, :, offs[n]:offs[n + 1]] for n in range(len(IN_SPLITS))]
    group = lambda parts: jnp.transpose(jnp.concatenate(parts, axis=2), (0, 2, 1)).astype(BF16)
    gap = lambda n: jnp.zeros(w_in.shape[:2] + (n,), w_in.dtype)
    w_small = group([ff, glr, gap(SMALL_DT - SMALL_GLR - GLA_LOWRANK), dt, gap(LANES - SMALL_DT - SSD_HEADS)])
    return w_small, group([fq, fk, fv]), group([gq, gk, gv, gr]), group([rq, rk, rv, rg]), group([z, xbc]), group([gates])


def kernel(x, positions, ln1, ln2, w_in, fox_bf, fox_qn, fox_kn, gla_w2, gla_b, gla_norm, ret_norm,
           ssd_conv_w, ssd_conv_b, ssd_dt_bias, ssd_a_log, ssd_d, ssd_norm,
           w_up_a, w_up_b, w_up_c, w_up_d, w_out, w_ffn_in, w_ffn_out):
    bsz, seq, d = x.shape
    assert bsz == 1 and d == D_MODEL and seq % ROW_TILE == 0 and seq % FOX_TQ == 0 and FOX_TQ % ROW_TILE == 0
    depth = ln1.shape[0]
    xr = x.reshape(seq, d)
    cos_t, sin_t = _rotary_tables(positions, seq)

    bf16 = lambda w: w.astype(BF16)
    ln1_r, ln2_r = _rows(ln1), _rows(ln2)
    fox_bf_r = _rows(jnp.pad(fox_bf, ((0, 0), (SMALL_FF, LANES - SMALL_FF - FOX_HEADS))))
    fox_qn_r, fox_kn_r = _rows(jnp.tile(fox_qn, (1, 2))), _rows(jnp.tile(fox_kn, (1, 2)))
    fox_shift_r, fox_par = _fox_params(fox_qn, fox_kn)
    gla_w2_p = bf16(jnp.pad(gla_w2, ((0, 0), (SMALL_GLR, LANES - SMALL_GLR - GLA_LOWRANK), (0, 0))))
    gla_b_r, gla_gn_r = _rows(gla_b), _rows(jnp.tile(gla_norm, (1, GLA_HEADS)))
    ret_gn_r = _rows(jnp.tile(ret_norm, (1, RET_HEADS)))
    conv_b_r = _rows(ssd_conv_b)
    at_dt = lambda a: _rows(jnp.pad(a, ((0, 0), (SMALL_DT, LANES - SMALL_DT - SSD_HEADS))))
    dtb_r, alog_r = at_dt(ssd_dt_bias), at_dt(ssd_a_log)
    d_r, ssd_gn_r = _rows(jnp.repeat(ssd_d, HEAD_DIM, axis=1)), _rows(ssd_norm)
    up_a, up_b, up_c, up_d, w_o = bf16(w_up_a), bf16(w_up_b), bf16(w_up_c), bf16(w_up_d), bf16(w_out)
    w1, w2 = bf16(w_ffn_in), bf16(w_ffn_out)
    w_small, w_fox, w_gla, w_ret, w_ssd, w_gates = _in_proj_weights(w_in)

    for l in range(depth):
        q, k, v, c, y_b, y_c, y_d = _mixers(
            l, xr, ln1_r, w_small,
            (w_fox, fox_bf_r, fox_qn_r, fox_kn_r, fox_shift_r),
            (w_gla, gla_w2_p, gla_b_r, gla_gn_r),
            (w_ret, ret_gn_r),
            (w_ssd, ssd_conv_w, conv_b_r, dtb_r, alog_r, d_r, ssd_gn_r),
            cos_t, sin_t)
        y_a = _fox_attention(fox_par[l], q, k, v, c)
        xr = _merge_ffn(l, xr, ln1_r, w_gates, y_a, y_b, y_c, y_d, up_a, up_b, up_c, up_d, w_o, ln2_r, w1, w2)
    return xr.reshape(bsz, seq, d)
```

```python
import math

import numpy as np
import jax
import jax.numpy as jnp
from jax import lax
from jax.experimental import pallas as pl
from jax.experimental.pallas import tpu as pltpu

D_MODEL = 1024
HEAD_DIM = 64
CHUNK = 64
FOX_HEADS = 4
GLA_HEADS = 4
GLA_LOWRANK = 16
GLA_TAU = 16.0
RET_HEADS = 4
ROPE_THETA = 10000.0
SSD_HEADS = 8
SSD_GROUPS = 2
SSD_STATE = 64
SSD_CONV = 4
SSD_INNER = SSD_HEADS * HEAD_DIM
SSD_CONV_CH = SSD_INNER + 2 * SSD_GROUPS * SSD_STATE
FOX_W = FOX_HEADS * HEAD_DIM
GLA_W = GLA_HEADS * HEAD_DIM
RET_W = RET_HEADS * HEAD_DIM
N_BRANCH = 4
FFN_HIDDEN = ((8 * D_MODEL + 3 * 256 - 1) // (3 * 256)) * 256
NORM_EPS = 1e-6
IN_SPLITS = (FOX_W, FOX_W, FOX_W, FOX_HEADS,
             GLA_W, GLA_W, GLA_W, GLA_LOWRANK, GLA_W,
             RET_W, RET_W, RET_W, RET_W,
             SSD_INNER, SSD_CONV_CH, SSD_HEADS,
             N_BRANCH * D_MODEL)

LANES = 128
SUBLANES = 8
VMEM_LIMIT = 56 * 1024 * 1024

ROW_TILE = 512
FOX_TQ = 512
FOX_TK = 512
FFN_TILE = 512
PROJ_PIECE = 512

F32 = jnp.float32
BF16 = jnp.bfloat16


def _rmsnorm(x, g):
    return x * lax.rsqrt(jnp.mean(x * x, axis=-1, keepdims=True) + NORM_EPS) * g


def _log_sigmoid(x):
    return jnp.minimum(x, 0.0) - jnp.log(1.0 + jnp.exp(-jnp.abs(x)))


def _softplus(x):
    return jnp.maximum(x, 0.0) + jnp.log(1.0 + jnp.exp(-jnp.abs(x)))


def _sigmoid(x):
    return 0.5 * jnp.tanh(0.5 * x) + 0.5


def _silu(x):
    return x * _sigmoid(x)


def _split3(x):
    hi = x.astype(BF16)
    r1 = x - hi.astype(F32)
    mid = r1.astype(BF16)
    lo = (r1 - mid.astype(F32)).astype(BF16)
    return hi, mid, lo


def _split2(x):
    hi = x.astype(BF16)
    return hi, (x - hi.astype(F32)).astype(BF16)


def _dot(a, b):
    return jnp.dot(a, b, preferred_element_type=F32)


def _dot_nt(a, b):
    return lax.dot_general(a, b, (((1,), (1,)), ((), ())), preferred_element_type=F32)


def _dot_tn(a, b):
    return lax.dot_general(a, b, (((0,), (0,)), ((), ())), preferred_element_type=F32)


def _sel_dot(mat, x):
    hi, mid, lo = _split3(x)
    return _dot(mat, hi) + _dot(mat, mid) + _dot(mat, lo)


def _dot_sel2(x, mat):
    hi, lo = _split2(x)
    return _dot(hi, mat) + _dot(lo, mat)


def _chunk_cumsum(x):
    t = x.shape[0]
    r = lax.broadcasted_iota(jnp.int32, (t, t), 0)
    c = lax.broadcasted_iota(jnp.int32, (t, t), 1)
    incl = jnp.where(((r // CHUNK) == (c // CHUNK)) & (c <= r), 1.0, 0.0).astype(BF16)
    hi, lo = _split2(x)
    b = _dot(incl, hi) + _dot(incl, lo)
    return b, _chunk_last(b) - b


def _chunk_last(b):
    t, w = b.shape
    return jnp.concatenate([jnp.broadcast_to(b[e - 1:e, :], (CHUNK, w)) for e in range(CHUNK, t + 1, CHUNK)],
                           axis=0)


def _group_mean_sq(o, group):
    w = o.shape[-1]
    r = lax.broadcasted_iota(jnp.int32, (w, w), 0)
    c = lax.broadcasted_iota(jnp.int32, (w, w), 1)
    bd = jnp.where((r // group) == (c // group), 1.0, 0.0).astype(BF16)
    return _dot_sel2(o * o, bd) * (1.0 / group)


def _project(h, wt_ref):
    n = wt_ref.shape[0]
    parts = []
    for a in range(0, n, PROJ_PIECE):
        parts.append(_dot_nt(h, wt_ref[a:min(a + PROJ_PIECE, n), :]))
        yield
    return jnp.concatenate(parts, axis=1)


def _layer(stacked, l):
    tail = tuple(stacked.shape[1:])
    return pl.BlockSpec((None,) + tail, lambda *_: (l,) + (0,) * len(tail), pipeline_mode=pl.Buffered(1))


def _rotary_kernel(pos_ref, inv_ref, sign_ref, cos_ref, sin_ref):
    ang = pos_ref[...] * inv_ref[...]
    cos_ref[...] = jnp.cos(ang)
    sin_ref[...] = jnp.sin(ang) * sign_ref[...]


def _rotary_tables(positions, seq):
    half = HEAD_DIM // 2
    inv = ROPE_THETA ** (-jnp.arange(half, dtype=F32) / half)
    reps = LANES // HEAD_DIM
    inv_row = jnp.tile(inv, 2 * reps).reshape(1, LANES)
    sign_row = jnp.tile(jnp.concatenate([-jnp.ones((half,), F32), jnp.ones((half,), F32)]),
                        reps).reshape(1, LANES)
    pos = positions.astype(F32).reshape(seq, 1)
    t = ROW_TILE
    return pl.pallas_call(
        _rotary_kernel,
        grid=(seq // t,),
        in_specs=[pl.BlockSpec((t, 1), lambda i: (i, 0)),
                  pl.BlockSpec((1, LANES), lambda i: (0, 0)),
                  pl.BlockSpec((1, LANES), lambda i: (0, 0))],
        out_specs=[pl.BlockSpec((t, LANES), lambda i: (i, 0)),
                   pl.BlockSpec((t, LANES), lambda i: (i, 0))],
        out_shape=[jax.ShapeDtypeStruct((seq, LANES), F32)] * 2,
        name="rotary_tables",
    )(pos, inv_row, sign_row)


SMALL_FF = 0
SMALL_GLR = FOX_HEADS
SMALL_DT = 24
LOG2E = math.log2(math.e)
F32_EXP_ZERO = 88.0
FOX_FAST_MAX_LOGIT = 40.0
FOX_BOUND_MARGIN = 1.02


def _fox_prep_body(h, small, w_ref, bf_ref, qn_ref, kn_ref, shift_ref, q_out, k_out, v_out, c_out, carry_ref):
    t = h.shape[0]
    u = yield from _project(h, w_ref)
    ls = _log_sigmoid(small + bf_ref[...])
    r = lax.broadcasted_iota(jnp.int32, (t, t), 0)
    cc = lax.broadcasted_iota(jnp.int32, (t, t), 1)
    tri = jnp.where(cc <= r, 1.0, 0.0).astype(BF16)
    c = _sel_dot(tri, ls) + carry_ref[0:1, :]
    carry_ref[0:1, :] = c[t - 1:t, :]
    c_out[0] = jnp.concatenate([c[0:1, :], c[t - 1:t, :], jnp.zeros((SUBLANES - 2, LANES), F32)], axis=0)
    c_hi, c_mid, c_lo = _split3(c * LOG2E)
    c_hi, c_mid, c_lo = c_hi.astype(F32), c_mid.astype(F32), c_lo.astype(F32)

    lane = lax.broadcasted_iota(jnp.int32, (t, LANES), 1)
    low = lane < HEAD_DIM
    j = lane - HEAD_DIM
    shift = shift_ref[...]

    def head_norm(pair, gain, mult):
        sq = pair * pair
        s_lo = jnp.sum(jnp.where(low, sq, 0.0), axis=-1, keepdims=True)
        s_hi = jnp.sum(jnp.where(low, 0.0, sq), axis=-1, keepdims=True)
        ms = jnp.where(low, s_lo, s_hi) * (1.0 / HEAD_DIM)
        return pair * lax.rsqrt(ms + NORM_EPS) * (gain * mult)

    for p in range(FOX_HEADS // 2):
        qp = head_norm(u[:, p * LANES:(p + 1) * LANES], qn_ref[...], HEAD_DIM ** -0.5 * LOG2E)
        kp = head_norm(u[:, FOX_W + p * LANES:FOX_W + (p + 1) * LANES], kn_ref[...], 1.0)
        vp = u[:, 2 * FOX_W + p * LANES:2 * FOX_W + (p + 1) * LANES]
        for sub in range(2):
            hd = 2 * p + sub
            if sub == 1:
                qp, kp, vp = (pltpu.roll(a, HEAD_DIM, axis=1) for a in (qp, kp, vp))
            ch, cm, cl = (jnp.broadcast_to(a[:, SMALL_FF + hd:SMALL_FF + hd + 1], (t, LANES)) for a in (c_hi, c_mid, c_lo))
            aug_q = jnp.where(j == 0, ch, jnp.where(j == 1, cm, jnp.where(j == 2, cl,
                              jnp.where(j < 7, 1.0, 0.0))))
            aug_k = jnp.where(j < 3, 1.0, jnp.where(j == 3, -ch, jnp.where(j == 4, -cm,
                              jnp.where(j == 5, -cl, jnp.where(j == 6, -shift, 0.0)))))
            aug_v = jnp.where(j == 0, 1.0, 0.0)
            q_out[hd] = jnp.where(low, qp, aug_q).astype(BF16)
            k_out[hd] = jnp.where(low, kp, aug_k).astype(BF16)
            v_out[hd] = jnp.where(low, vp, aug_v).astype(BF16)


def _fox_attn_kernel(par_ref, cq0_ref, cend_ref, q_ref, k_ref, v_ref, o_ref):
    i = pl.program_id(0)
    nh, tq, _ = q_ref.shape
    thr = par_ref[0]

    def first_block(hd):
        c0 = cq0_ref[i, hd]
        return lax.while_loop(
            lambda j: jnp.logical_and(j > 0, c0 - cend_ref[jnp.maximum(j - 1, 0), hd] >= -thr),
            lambda j: j - 1, i)

    def keys(hd, j, nblk):
        off = pl.multiple_of(j * FOX_TK, FOX_TK)
        return k_ref[hd, pl.ds(off, nblk * FOX_TK), :], v_ref[hd, pl.ds(off, nblk * FOX_TK), :]

    def causal(s):
        r = lax.broadcasted_iota(jnp.int32, s.shape, 0)
        c = lax.broadcasted_iota(jnp.int32, s.shape, 1)
        return jnp.where(c <= r, s, -jnp.inf)

    def fixed_shift():
        def block(hd, j, nblk):
            k, v = keys(hd, j, nblk)
            return _dot(jnp.exp2(_dot_nt(q_ref[hd], k)).astype(BF16), v)

        def diagonal(hd):
            k, v = keys(hd, i, 1)
            return _dot(jnp.exp2(causal(_dot_nt(q_ref[hd], k))).astype(BF16), v)

        accs = [diagonal(hd) for hd in range(nh)]
        for hd in range(nh):
            acc, j_lo = accs[hd], first_block(hd)
            n = i - j_lo
            acc = lax.cond(n % 2 == 1, lambda: acc + block(hd, j_lo, 1), lambda: acc)
            first = j_lo + n % 2
            accs[hd] = lax.fori_loop(0, n // 2, lambda p, acc: acc + block(hd, first + 2 * p, 2), acc)
        return accs

    def running_max():
        def one_head(hd):
            q = q_ref[hd]

            def step(j, carry, masked):
                m, acc = carry
                k, v = keys(hd, j, 1)
                s = _dot_nt(q, k)
                if masked:
                    s = causal(s)
                m_new = jnp.maximum(m, jnp.max(s, axis=-1, keepdims=True))
                p = jnp.exp2(s - m_new)
                return m_new, jnp.exp2(m - m_new) * acc + _dot(p.astype(BF16), v)
            init = (jnp.full((tq, 1), -jnp.inf, F32), jnp.zeros((tq, LANES), F32))
            carry = step(i, init, True)
            return lax.fori_loop(first_block(hd), i, lambda j, c: step(j, c, False), carry)[1]
        return [one_head(hd) for hd in range(nh)]

    accs = lax.cond(par_ref[1] > 0.0, fixed_shift, running_max)
    outs = [acc / acc[:, HEAD_DIM:HEAD_DIM + 1] for acc in accs]
    low = lax.broadcasted_iota(jnp.int32, (tq, LANES), 1) < HEAD_DIM
    for p in range(nh // FOX_HEADS_PER_TILE):
        o_ref[p] = jnp.where(low, outs[2 * p], pltpu.roll(outs[2 * p + 1], HEAD_DIM, axis=1)).astype(BF16)


FOX_HEADS_PER_TILE = LANES // HEAD_DIM


def _fox_attn(par, cq0, cend, q, k, v):
    assert FOX_TQ == FOX_TK and FOX_HEADS_PER_TILE == 2
    nh, seq, _ = q.shape
    kv_spec = pl.BlockSpec((nh, seq, LANES), lambda i, *_: (0, 0, 0), pipeline_mode=pl.Buffered(1))
    return pl.pallas_call(
        _fox_attn_kernel,
        grid_spec=pltpu.PrefetchScalarGridSpec(
            num_scalar_prefetch=3,
            grid=(seq // FOX_TQ,),
            in_specs=[pl.BlockSpec((nh, FOX_TQ, LANES), lambda i, *_: (0, i, 0)), kv_spec, kv_spec],
            out_specs=pl.BlockSpec((nh // FOX_HEADS_PER_TILE, FOX_TQ, LANES), lambda i, *_: (0, i, 0))),
        out_shape=jax.ShapeDtypeStruct((nh // FOX_HEADS_PER_TILE, seq, LANES), BF16),
        compiler_params=pltpu.CompilerParams(dimension_semantics=("arbitrary",),
                                             vmem_limit_bytes=VMEM_LIMIT),
        name="fox_attn",
    )(par, cq0, cend, q, k, v)


def _fox_params(qn, kn):
    bound = FOX_BOUND_MARGIN * HEAD_DIM ** 0.5 * jnp.max(jnp.abs(qn), axis=1) * jnp.max(jnp.abs(kn), axis=1)
    shift = jnp.broadcast_to((LOG2E * bound)[:, None, None], (bound.shape[0], 1, LANES))
    par = jnp.stack([F32_EXP_ZERO + 2.0 * bound, (bound < FOX_FAST_MAX_LOGIT).astype(F32)], axis=1)
    return shift, par


def _fox_attention(par, q, k, v, c):
    per_tile = FOX_TQ // ROW_TILE
    heads = slice(SMALL_FF, SMALL_FF + FOX_HEADS)
    cq0 = c[0::per_tile, 0, heads]
    cend = c[per_tile - 1::per_tile, 1, heads]
    return _fox_attn(par, cq0, cend, q, k, v)


def _tile_rows(a, n):
    return jnp.concatenate([a] * n, axis=0)


def _linear_attn_chunks(qd_s, ki_s, ke_s, v_s, dec_s, o_s, st_s, nheads):
    t, w = qd_s.shape
    assert w == nheads * HEAD_DIM
    r = lax.broadcasted_iota(jnp.int32, (w, w), 0)
    c = lax.broadcasted_iota(jnp.int32, (w, w), 1)
    same_head = (r // HEAD_DIM) == (c // HEAD_DIM)
    rr = lax.broadcasted_iota(jnp.int32, (CHUNK, w), 0)
    cc = lax.broadcasted_iota(jnp.int32, (CHUNK, w), 1)
    tril = (cc % CHUNK) <= rr
    state = st_s[...]
    for ci in range(t // CHUNK):
        rows = slice(ci * CHUNK, (ci + 1) * CHUNK)
        last = slice((ci + 1) * CHUNK - 1, (ci + 1) * CHUNK)
        qd = qd_s[rows, :].astype(BF16)
        ke = ke_s[rows, :].astype(BF16)
        vv = v_s[rows, :].astype(BF16)
        k_bd = jnp.where(same_head, _tile_rows(ki_s[rows, :], nheads), 0.0).astype(BF16)
        v_bd = jnp.where(same_head, _tile_rows(v_s[rows, :], nheads), 0.0).astype(BF16)
        attn = jnp.where(tril, _dot_nt(qd, k_bd), 0.0).astype(BF16)
        o_s[rows, :] = _dot(attn, v_bd) + _dot_nt(qd, state.astype(BF16))
        state = state * dec_s[last, :] + jnp.where(same_head, _dot_tn(vv, ke), 0.0)
        yield
    st_s[...] = state


def _gla_body(h, small, w_ref, w2_ref, b2_ref, gn_ref, y_ref, qd_s, ki_s, ke_s, v_s, dec_s, o_s, st_s):
    u = yield from _project(h, w_ref)
    z = _dot(small.astype(BF16), w2_ref[...]) + b2_ref[...]
    log_a = _log_sigmoid(z) * (1.0 / GLA_TAU)
    b, rem = _chunk_cumsum(log_a)
    eb = jnp.exp(b)
    k = u[:, GLA_W:2 * GLA_W]
    qd_s[...] = u[:, 0:GLA_W] * (HEAD_DIM ** -0.5) * eb
    ki_s[...] = k * jnp.exp(-b)
    ke_s[...] = k * jnp.exp(rem)
    v_s[...] = u[:, 2 * GLA_W:3 * GLA_W]
    dec_s[...] = eb
    yield
    yield from _linear_attn_chunks(qd_s, ki_s, ke_s, v_s, dec_s, o_s, st_s, GLA_HEADS)
    o = o_s[...]
    y = o * lax.rsqrt(_group_mean_sq(o, HEAD_DIM) + NORM_EPS) * gn_ref[...]
    y_ref[...] = (y * _silu(u[:, 3 * GLA_W:4 * GLA_W])).astype(BF16)


def _ret_body(h, w_ref, cos_ref, sin_ref, gn_ref, y_ref, qd_s, ki_s, ke_s, v_s, dec_s, o_s, st_s):
    t = h.shape[0]
    u = yield from _project(h, w_ref)
    cos, sin = cos_ref[...], sin_ref[...]
    lane = lax.broadcasted_iota(jnp.int32, (t, RET_W), 1)
    half = HEAD_DIM // 2
    first_half = (lax.broadcasted_iota(jnp.int32, (t, LANES), 1) % HEAD_DIM) < half

    def rotate(a):
        blocks = []
        for b in range(0, RET_W, LANES):
            ab = a[:, b:b + LANES]
            swapped = jnp.where(first_half, pltpu.roll(ab, LANES - half, axis=1),
                                pltpu.roll(ab, half, axis=1))
            blocks.append(ab * cos + swapped * sin)
        return jnp.concatenate(blocks, axis=1)

    q = rotate(u[:, 0:RET_W])
    k = rotate(u[:, RET_W:2 * RET_W]) * (HEAD_DIM ** -0.5)
    row = lax.broadcasted_iota(jnp.int32, (t, RET_W), 0)
    lg = jnp.zeros((t, RET_W), F32)
    for hd in range(RET_HEADS):
        lg = jnp.where(lane // HEAD_DIM == hd, math.log(1.0 - 2.0 ** (-5.0 - hd)), lg)
    pos = (row % CHUNK).astype(F32)
    qw = jnp.exp((pos + 1.0) * lg)
    qd_s[...] = q * qw
    ki_s[...] = k * jnp.exp(-(pos + 1.0) * lg)
    ke_s[...] = k * jnp.exp((CHUNK - 1.0 - pos) * lg)
    v_s[...] = u[:, 2 * RET_W:3 * RET_W]
    dec_s[...] = qw
    yield
    yield from _linear_attn_chunks(qd_s, ki_s, ke_s, v_s, dec_s, o_s, st_s, RET_HEADS)
    o = o_s[...]
    y = o * lax.rsqrt(_group_mean_sq(o, HEAD_DIM) + NORM_EPS) * gn_ref[...]
    y_ref[...] = (y * _silu(u[:, 3 * RET_W:4 * RET_W])).astype(BF16)


def _linear_scratch(t, width, nheads):
    return [pltpu.VMEM((t, width), F32)] * 6 + [pltpu.VMEM((width, width), F32)]


SSD_PROJ = SSD_INNER + SSD_CONV_CH


def _ssd_body(h, small, w_ref, cw_ref, cb_ref, dtb_row_ref, alog_row_ref, d_row_ref, gn_ref, y_ref,
              xp_s, xs_s, bc_s, cs_s, ecs_s, xdt_s, xdtw_s, cst_s, o_s, st_s):
    t = h.shape[0]
    nc = t // CHUNK
    u = yield from _project(h, w_ref)
    z = u[:, 0:SSD_INNER]

    xp_s[SUBLANES:SUBLANES + t, :] = u[:, SSD_INNER:SSD_PROJ]
    conv = cb_ref[...] + cw_ref[SSD_CONV - 1:SSD_CONV, :] * xp_s[SUBLANES:SUBLANES + t, :]
    for kk in range(SSD_CONV - 1):
        off = SUBLANES - (SSD_CONV - 1) + kk
        conv = conv + cw_ref[kk:kk + 1, :] * xp_s[off:off + t, :]
    xp_s[0:SUBLANES, :] = xp_s[t:t + SUBLANES, :]
    xbc = _silu(conv)
    xs = xbc[:, 0:SSD_INNER]
    xs_s[...] = xs
    bc_s[...] = xbc[:, SSD_INNER:SSD_CONV_CH]

    dt = _softplus(small + dtb_row_ref[...])
    dta = dt * -jnp.exp(alog_row_ref[...])
    cs, _ = _chunk_cumsum(dta)
    cs_t = cs.T[SMALL_DT:SMALL_DT + SSD_HEADS, :]
    hpg = SSD_HEADS // SSD_GROUPS
    gcols = hpg * HEAD_DIM
    for ci in range(nc):
        for g in range(SSD_GROUPS):
            cst_s[ci, g:g + 1, :] = jnp.concatenate(
                [cs_t[g * hpg + rr:g * hpg + rr + 1, ci * CHUNK:(ci + 1) * CHUNK] for rr in range(hpg)], axis=1)

    er = lax.broadcasted_iota(jnp.int32, (LANES, SSD_INNER), 0)
    ec = lax.broadcasted_iota(jnp.int32, (LANES, SSD_INNER), 1)
    expand = jnp.where(ec // HEAD_DIM == er - SMALL_DT, 1.0, 0.0).astype(BF16)
    cs_x = _dot_sel2(cs, expand)
    xdt = xs * _dot_sel2(dt, expand)
    cs_s[...] = cs_x
    ecs_s[...] = jnp.exp(cs_x)
    xdt_s[...] = xdt
    xdtw_s[...] = xdt * jnp.exp(_chunk_last(cs_x) - cs_x)

    r = lax.broadcasted_iota(jnp.int32, (gcols, gcols), 0)
    c = lax.broadcasted_iota(jnp.int32, (gcols, gcols), 1)
    same_head = (r // HEAD_DIM) == (c // HEAD_DIM)
    rr = lax.broadcasted_iota(jnp.int32, (CHUNK, gcols), 0)
    cc = lax.broadcasted_iota(jnp.int32, (CHUNK, gcols), 1)
    tril = (cc % CHUNK) <= rr
    gw = SSD_GROUPS * SSD_STATE

    states = [st_s[g] for g in range(SSD_GROUPS)]
    for ci in range(nc):
        rows = slice(ci * CHUNK, (ci + 1) * CHUNK)
        last = slice((ci + 1) * CHUNK - 1, (ci + 1) * CHUNK)
        for g in range(SSD_GROUPS):
            gsl = slice(g * gcols, (g + 1) * gcols)
            bm = bc_s[rows, g * SSD_STATE:(g + 1) * SSD_STATE].astype(BF16)
            cm = bc_s[rows, gw + g * SSD_STATE:gw + (g + 1) * SSD_STATE].astype(BF16)
            cb = _dot_nt(cm, _tile_rows(bm, hpg))
            decay = jnp.exp(jnp.where(tril, cs_s[rows, gsl] - cst_s[ci, g:g + 1, :], -jnp.inf))
            x_bd = jnp.where(same_head, _tile_rows(xdt_s[rows, gsl], hpg), 0.0).astype(BF16)
            y_inter = _dot(cm, states[g].astype(BF16)) * ecs_s[rows, gsl]
            o_s[rows, gsl] = _dot((cb * decay).astype(BF16), x_bd) + y_inter
            states[g] = (states[g] * ecs_s[last, gsl]
                         + _dot_tn(bm, xdtw_s[rows, gsl].astype(BF16)))
        yield
    for g in range(SSD_GROUPS):
        st_s[g] = states[g]

    y = (o_s[...] + d_row_ref[...] * xs_s[...]) * _silu(z)
    gwid = SSD_INNER // SSD_GROUPS
    normed = []
    for g in range(SSD_GROUPS):
        yg = y[:, g * gwid:(g + 1) * gwid]
        normed.append(yg * lax.rsqrt(jnp.mean(yg * yg, axis=-1, keepdims=True) + NORM_EPS))
    y_ref[...] = (jnp.concatenate(normed, axis=1) * gn_ref[...]).astype(BF16)


N_FOX_IN, N_GLA_IN, N_RET_IN, N_SSD_IN = 5, 4, 4, 7
N_LINEAR_SCRATCH = 7
N_SSD_SCRATCH = 10


def _mixers_kernel(x_ref, ln_ref, ws_ref, *refs):
    it = iter(refs)
    take = lambda n: [next(it) for _ in range(n)]
    fox_in, gla_in, ret_in, ssd_in = take(N_FOX_IN), take(N_GLA_IN), take(N_RET_IN), take(N_SSD_IN)
    fox_out, (yb_ref, yc_ref, yd_ref) = take(4), take(3)
    (carry_ref,), gla_s, ret_s, ssd_s = take(1), take(N_LINEAR_SCRATCH), take(N_LINEAR_SCRATCH), take(N_SSD_SCRATCH)

    @pl.when(pl.program_id(0) == 0)
    def _():
        carry_ref[...] = jnp.zeros_like(carry_ref)
        gla_s[-1][...] = jnp.zeros_like(gla_s[-1])
        ret_s[-1][...] = jnp.zeros_like(ret_s[-1])
        ssd_s[-1][...] = jnp.zeros_like(ssd_s[-1])
        ssd_s[0][0:SUBLANES, :] = jnp.zeros((SUBLANES, SSD_CONV_CH), F32)

    h = _rmsnorm(x_ref[...], ln_ref[...]).astype(BF16)
    small = _dot_nt(h, ws_ref[...])
    waiting = [_ssd_body(h, small, *ssd_in, yd_ref, *ssd_s), _gla_body(h, small, *gla_in, yb_ref, *gla_s),
               _ret_body(h, *ret_in, yc_ref, *ret_s), _fox_prep_body(h, small, *fox_in, *fox_out, carry_ref)]
    pending = []
    while waiting or pending:
        if waiting:
            pending.append(waiting.pop(0))
        for body in list(pending):
            if next(body, "done") == "done":
                pending.remove(body)


def _mixers(l, x, ln, w_small, fox_in, gla_in, ret_in, ssd_in, cos_t, sin_t):
    seq = x.shape[0]
    t = ROW_TILE
    row = lambda w: pl.BlockSpec((t, w), lambda i: (i, 0))
    heads = pl.BlockSpec((FOX_HEADS, t, LANES), lambda i: (0, i, 0))
    head_shape = jax.ShapeDtypeStruct((FOX_HEADS, seq, LANES), BF16)
    w_ret, ret_gn = ret_in
    wide = pltpu.VMEM((t, SSD_INNER), F32)
    return pl.pallas_call(
        _mixers_kernel,
        grid=(seq // t,),
        in_specs=([row(D_MODEL), _layer(ln, l), _layer(w_small, l)] + [_layer(a, l) for a in fox_in]
                  + [_layer(a, l) for a in gla_in]
                  + [_layer(w_ret, l), row(LANES), row(LANES), _layer(ret_gn, l)]
                  + [_layer(a, l) for a in ssd_in]),
        out_specs=[heads, heads, heads, pl.BlockSpec((1, SUBLANES, LANES), lambda i: (i, 0, 0)),
                   row(GLA_W), row(RET_W), row(SSD_INNER)],
        out_shape=[head_shape, head_shape, head_shape,
                   jax.ShapeDtypeStruct((seq // t, SUBLANES, LANES), F32),
                   jax.ShapeDtypeStruct((seq, GLA_W), BF16),
                   jax.ShapeDtypeStruct((seq, RET_W), BF16),
                   jax.ShapeDtypeStruct((seq, SSD_INNER), BF16)],
        scratch_shapes=([pltpu.VMEM((SUBLANES, LANES), F32)]
                        + _linear_scratch(t, GLA_W, GLA_HEADS) + _linear_scratch(t, RET_W, RET_HEADS)
                        + [pltpu.VMEM((t + 2 * SUBLANES, SSD_CONV_CH), F32),
                           wide,
                           pltpu.VMEM((t, 2 * SSD_GROUPS * SSD_STATE), F32),
                           wide, wide, wide, wide,
                           pltpu.VMEM((t // CHUNK, SSD_GROUPS, SSD_INNER // SSD_GROUPS), F32),
                           wide,
                           pltpu.VMEM((SSD_GROUPS, SSD_STATE, SSD_INNER // SSD_GROUPS), F32)]),
        compiler_params=pltpu.CompilerParams(dimension_semantics=("arbitrary",),
                                             vmem_limit_bytes=VMEM_LIMIT),
        name="mixers",
    )(x, ln, w_small, *fox_in, *gla_in, w_ret, cos_t, sin_t, ret_gn, *ssd_in)


def _merge_ffn_kernel(x_ref, ln1_ref, wg_ref, ya_ref, yb_ref, yc_ref, yd_ref,
                      wa_ref, wb_ref, wc_ref, wd_ref, wo_ref, ln2_ref, w1_ref, w2_ref, out_ref):
    x = x_ref[...]
    h = _rmsnorm(x, ln1_ref[...]).astype(BF16)

    def gate(b):
        return _sigmoid(_dot_nt(h, wg_ref[b * D_MODEL:(b + 1) * D_MODEL, :]))

    up_a = _dot(ya_ref[0], wa_ref[0:LANES, :])
    for p in range(1, FOX_HEADS // FOX_HEADS_PER_TILE):
        up_a = up_a + _dot(ya_ref[p], wa_ref[p * LANES:(p + 1) * LANES, :])
    merged = gate(0) * up_a
    merged = merged + gate(1) * _dot(yb_ref[...], wb_ref[...])
    merged = merged + gate(2) * _dot(yc_ref[...], wc_ref[...])
    merged = merged + gate(3) * _dot(yd_ref[...], wd_ref[...])
    x = x + _dot(merged.astype(BF16), wo_ref[...])

    h = _rmsnorm(x, ln2_ref[...]).astype(BF16)
    act = _silu(_dot(h, w1_ref[:, 0:FFN_HIDDEN])) * _dot(h, w1_ref[:, FFN_HIDDEN:2 * FFN_HIDDEN])
    out_ref[...] = x + _dot(act.astype(BF16), w2_ref[...])


def _merge_ffn(l, x, ln1, w_gates, y_a, y_b, y_c, y_d, w_up_a, w_up_b, w_up_c, w_up_d, w_out, ln2, w1, w2):
    seq = x.shape[0]
    t = FFN_TILE
    row = lambda w: pl.BlockSpec((t, w), lambda i: (i, 0))
    return pl.pallas_call(
        _merge_ffn_kernel,
        grid=(seq // t,),
        in_specs=[row(D_MODEL), _layer(ln1, l), _layer(w_gates, l),
                  pl.BlockSpec((FOX_HEADS // FOX_HEADS_PER_TILE, t, LANES), lambda i: (0, i, 0)),
                  row(GLA_W), row(RET_W), row(SSD_INNER),
                  _layer(w_up_a, l), _layer(w_up_b, l), _layer(w_up_c, l),
                  _layer(w_up_d, l), _layer(w_out, l), _layer(ln2, l), _layer(w1, l), _layer(w2, l)],
        out_specs=row(D_MODEL),
        out_shape=jax.ShapeDtypeStruct((seq, D_MODEL), F32),
        compiler_params=pltpu.CompilerParams(dimension_semantics=("arbitrary",),
                                             vmem_limit_bytes=VMEM_LIMIT),
        name="merge_swiglu",
    )(x, ln1, w_gates, y_a, y_b, y_c, y_d, w_up_a, w_up_b, w_up_c, w_up_d, w_out, ln2, w1, w2)


def _rows(a):
    return a.reshape(a.shape[0], 1, -1)


def _in_proj_weights(w_in):
    offs = np.concatenate([[0], np.cumsum(IN_SPLITS)])
    (fq, fk, fv, ff, gq, gk, gv, glr, gr, rq, rk, rv, rg, z, xbc, dt, gates) = [
        w_in[:, :, offs[n]:offs[n + 1]] for n in range(len(IN_SPLITS))]
    group = lambda parts: jnp.transpose(jnp.concatenate(parts, axis=2), (0, 2, 1)).astype(BF16)
    gap = lambda n: jnp.zeros(w_in.shape[:2] + (n,), w_in.dtype)
    w_small = group([ff, glr, gap(SMALL_DT - SMALL_GLR - GLA_LOWRANK), dt, gap(LANES - SMALL_DT - SSD_HEADS)])
    return w_small, group([fq, fk, fv]), group([gq, gk, gv, gr]), group([rq, rk, rv, rg]), group([z, xbc]), group([gates])


def kernel(x, positions, ln1, ln2, w_in, fox_bf, fox_qn, fox_kn, gla_w2, gla_b, gla_norm, ret_norm,
           ssd_conv_w, ssd_conv_b, ssd_dt_bias, ssd_a_log, ssd_d, ssd_norm,
           w_up_a, w_up_b, w_up_c, w_up_d, w_out, w_ffn_in, w_ffn_out):
    bsz, seq, d = x.shape
    assert bsz == 1 and d == D_MODEL and seq % ROW_TILE == 0 and seq % FOX_TQ == 0 and FOX_TQ % ROW_TILE == 0
    depth = ln1.shape[0]
    xr = x.reshape(seq, d)
    cos_t, sin_t = _rotary_tables(positions, seq)

    bf16 = lambda w: w.astype(BF16)
    ln1_r, ln2_r = _rows(ln1), _rows(ln2)
    fox_bf_r = _rows(jnp.pad(fox_bf, ((0, 0), (SMALL_FF, LANES - SMALL_FF - FOX_HEADS))))
    fox_qn_r, fox_kn_r = _rows(jnp.tile(fox_qn, (1, 2))), _rows(jnp.tile(fox_kn, (1, 2)))
    fox_shift_r, fox_par = _fox_params(fox_qn, fox_kn)
    gla_w2_p = bf16(jnp.pad(gla_w2, ((0, 0), (SMALL_GLR, LANES - SMALL_GLR - GLA_LOWRANK), (0, 0))))
    gla_b_r, gla_gn_r = _rows(gla_b), _rows(jnp.tile(gla_norm, (1, GLA_HEADS)))
    ret_gn_r = _rows(jnp.tile(ret_norm, (1, RET_HEADS)))
    conv_b_r = _rows(ssd_conv_b)
    at_dt = lambda a: _rows(jnp.pad(a, ((0, 0), (SMALL_DT, LANES - SMALL_DT - SSD_HEADS))))
    dtb_r, alog_r = at_dt(ssd_dt_bias), at_dt(ssd_a_log)
    d_r, ssd_gn_r = _rows(jnp.repeat(ssd_d, HEAD_DIM, axis=1)), _rows(ssd_norm)
    up_a, up_b, up_c, up_d, w_o = bf16(w_up_a), bf16(w_up_b), bf16(w_up_c), bf16(w_up_d), bf16(w_out)
    w1, w2 = bf16(w_ffn_in), bf16(w_ffn_out)
    w_small, w_fox, w_gla, w_ret, w_ssd, w_gates = _in_proj_weights(w_in)

    for l in range(depth):
        q, k, v, c, y_b, y_c, y_d = _mixers(
            l, xr, ln1_r, w_small,
            (w_fox, fox_bf_r, fox_qn_r, fox_kn_r, fox_shift_r),
            (w_gla, gla_w2_p, gla_b_r, gla_gn_r),
            (w_ret, ret_gn_r),
            (w_ssd, ssd_conv_w, conv_b_r, dtb_r, alog_r, d_r, ssd_gn_r),
            cos_t, sin_t)
        y_a = _fox_attention(fox_par[l], q, k, v, c)
        xr = _merge_ffn(l, xr, ln1_r, w_gates, y_a, y_b, y_c, y_d, up_a, up_b, up_c, up_d, w_o, ln2_r, w1, w2)
    return xr.reshape(bsz, seq, d)
```

```python
import math

import numpy as np
import jax
import jax.numpy as jnp
from jax import lax
from jax.experimental import pallas as pl
from jax.experimental.pallas import tpu as pltpu

D_MODEL = 1024
HEAD_DIM = 64
CHUNK = 64
FOX_HEADS = 4
GLA_HEADS = 4
GLA_LOWRANK = 16
GLA_TAU = 16.0
RET_HEADS = 4
ROPE_THETA = 10000.0
SSD_HEADS = 8
SSD_GROUPS = 2
SSD_STATE = 64
SSD_CONV = 4
SSD_INNER = SSD_HEADS * HEAD_DIM
SSD_CONV_CH = SSD_INNER + 2 * SSD_GROUPS * SSD_STATE
FOX_W = FOX_HEADS * HEAD_DIM
GLA_W = GLA_HEADS * HEAD_DIM
RET_W = RET_HEADS * HEAD_DIM
N_BRANCH = 4
FFN_HIDDEN = ((8 * D_MODEL + 3 * 256 - 1) // (3 * 256)) * 256
NORM_EPS = 1e-6
IN_SPLITS = (FOX_W, FOX_W, FOX_W, FOX_HEADS,
             GLA_W, GLA_W, GLA_W, GLA_LOWRANK, GLA_W,
             RET_W, RET_W, RET_W, RET_W,
             SSD_INNER, SSD_CONV_CH, SSD_HEADS,
             N_BRANCH * D_MODEL)

LANES = 128
SUBLANES = 8
VMEM_LIMIT = 56 * 1024 * 1024

ROW_TILE = 512
FOX_TQ = 512
FOX_TK = 512
FFN_TILE = 512
PROJ_PIECE = 512

F32 = jnp.float32
BF16 = jnp.bfloat16


def _rmsnorm(x, g):
    return x * lax.rsqrt(jnp.mean(x * x, axis=-1, keepdims=True) + NORM_EPS) * g


def _log_sigmoid(x):
    return jnp.minimum(x, 0.0) - jnp.log(1.0 + jnp.exp(-jnp.abs(x)))


def _softplus(x):
    return jnp.maximum(x, 0.0) + jnp.log(1.0 + jnp.exp(-jnp.abs(x)))


def _sigmoid(x):
    return 0.5 * jnp.tanh(0.5 * x) + 0.5


def _silu(x):
    return x * _sigmoid(x)


def _split3(x):
    hi = x.astype(BF16)
    r1 = x - hi.astype(F32)
    mid = r1.astype(BF16)
    lo = (r1 - mid.astype(F32)).astype(BF16)
    return hi, mid, lo


def _split2(x):
    hi = x.astype(BF16)
    return hi, (x - hi.astype(F32)).astype(BF16)


def _dot(a, b):
    return jnp.dot(a, b, preferred_element_type=F32)


def _dot_nt(a, b):
    return lax.dot_general(a, b, (((1,), (1,)), ((), ())), preferred_element_type=F32)


def _dot_tn(a, b):
    return lax.dot_general(a, b, (((0,), (0,)), ((), ())), preferred_element_type=F32)


def _sel_dot(mat, x):
    hi, mid, lo = _split3(x)
    return _dot(mat, hi) + _dot(mat, mid) + _dot(mat, lo)


def _dot_sel2(x, mat):
    hi, lo = _split2(x)
    return _dot(hi, mat) + _dot(lo, mat)


def _chunk_cumsum(x):
    t = x.shape[0]
    r = lax.broadcasted_iota(jnp.int32, (t, t), 0)
    c = lax.broadcasted_iota(jnp.int32, (t, t), 1)
    incl = jnp.where(((r // CHUNK) == (c // CHUNK)) & (c <= r), 1.0, 0.0).astype(BF16)
    hi, lo = _split2(x)
    b = _dot(incl, hi) + _dot(incl, lo)
    return b, _chunk_last(b) - b


def _chunk_last(b):
    t, w = b.shape
    return jnp.concatenate([jnp.broadcast_to(b[e - 1:e, :], (CHUNK, w)) for e in range(CHUNK, t + 1, CHUNK)],
                           axis=0)


def _group_mean_sq(o, group):
    w = o.shape[-1]
    r = lax.broadcasted_iota(jnp.int32, (w, w), 0)
    c = lax.broadcasted_iota(jnp.int32, (w, w), 1)
    bd = jnp.where((r // group) == (c // group), 1.0, 0.0).astype(BF16)
    return _dot_sel2(o * o, bd) * (1.0 / group)


def _project(h, wt_ref):
    n = wt_ref.shape[0]
    parts = []
    for a in range(0, n, PROJ_PIECE):
        parts.append(_dot_nt(h, wt_ref[a:min(a + PROJ_PIECE, n), :]))
        yield
    return jnp.concatenate(parts, axis=1)


def _layer(stacked, l):
    tail = tuple(stacked.shape[1:])
    return pl.BlockSpec((None,) + tail, lambda *_: (l,) + (0,) * len(tail), pipeline_mode=pl.Buffered(1))


def _rotary_kernel(pos_ref, inv_ref, sign_ref, cos_ref, sin_ref):
    ang = pos_ref[...] * inv_ref[...]
    cos_ref[...] = jnp.cos(ang)
    sin_ref[...] = jnp.sin(ang) * sign_ref[...]


def _rotary_tables(positions, seq):
    half = HEAD_DIM // 2
    inv = ROPE_THETA ** (-jnp.arange(half, dtype=F32) / half)
    reps = LANES // HEAD_DIM
    inv_row = jnp.tile(inv, 2 * reps).reshape(1, LANES)
    sign_row = jnp.tile(jnp.concatenate([-jnp.ones((half,), F32), jnp.ones((half,), F32)]),
                        reps).reshape(1, LANES)
    pos = positions.astype(F32).reshape(seq, 1)
    t = ROW_TILE
    return pl.pallas_call(
        _rotary_kernel,
        grid=(seq // t,),
        in_specs=[pl.BlockSpec((t, 1), lambda i: (i, 0)),
                  pl.BlockSpec((1, LANES), lambda i: (0, 0)),
                  pl.BlockSpec((1, LANES), lambda i: (0, 0))],
        out_specs=[pl.BlockSpec((t, LANES), lambda i: (i, 0)),
                   pl.BlockSpec((t, LANES), lambda i: (i, 0))],
        out_shape=[jax.ShapeDtypeStruct((seq, LANES), F32)] * 2,
        name="rotary_tables",
    )(pos, inv_row, sign_row)


SMALL_FF = 0
SMALL_GLR = FOX_HEADS
SMALL_DT = 24
LOG2E = math.log2(math.e)
F32_EXP_ZERO = 88.0
FOX_FAST_MAX_LOGIT = 40.0
FOX_BOUND_MARGIN = 1.02


def _fox_prep_body(h, small, w_ref, bf_ref, qn_ref, kn_ref, shift_ref, q_out, k_out, v_out, c_out, carry_ref):
    t = h.shape[0]
    u = yield from _project(h, w_ref)
    ls = _log_sigmoid(small + bf_ref[...])
    r = lax.broadcasted_iota(jnp.int32, (t, t), 0)
    cc = lax.broadcasted_iota(jnp.int32, (t, t), 1)
    tri = jnp.where(cc <= r, 1.0, 0.0).astype(BF16)
    c = _sel_dot(tri, ls) + carry_ref[0:1, :]
    carry_ref[0:1, :] = c[t - 1:t, :]
    c_out[0] = jnp.concatenate([c[0:1, :], c[t - 1:t, :], jnp.zeros((SUBLANES - 2, LANES), F32)], axis=0)
    c_hi, c_mid, c_lo = _split3(c * LOG2E)
    c_hi, c_mid, c_lo = c_hi.astype(F32), c_mid.astype(F32), c_lo.astype(F32)

    lane = lax.broadcasted_iota(jnp.int32, (t, LANES), 1)
    low = lane < HEAD_DIM
    j = lane - HEAD_DIM
    shift = shift_ref[...]

    def head_norm(pair, gain, mult):
        sq = pair * pair
        s_lo = jnp.sum(jnp.where(low, sq, 0.0), axis=-1, keepdims=True)
        s_hi = jnp.sum(jnp.where(low, 0.0, sq), axis=-1, keepdims=True)
        ms = jnp.where(low, s_lo, s_hi) * (1.0 / HEAD_DIM)
        return pair * lax.rsqrt(ms + NORM_EPS) * (gain * mult)

    for p in range(FOX_HEADS // 2):
        qp = head_norm(u[:, p * LANES:(p + 1) * LANES], qn_ref[...], HEAD_DIM ** -0.5 * LOG2E)
        kp = head_norm(u[:, FOX_W + p * LANES:FOX_W + (p + 1) * LANES], kn_ref[...], 1.0)
        vp = u[:, 2 * FOX_W + p * LANES:2 * FOX_W + (p + 1) * LANES]
        for sub in range(2):
            hd = 2 * p + sub
            if sub == 1:
                qp, kp, vp = (pltpu.roll(a, HEAD_DIM, axis=1) for a in (qp, kp, vp))
            ch, cm, cl = (jnp.broadcast_to(a[:, SMALL_FF + hd:SMALL_FF + hd + 1], (t, LANES)) for a in (c_hi, c_mid, c_lo))
            aug_q = jnp.where(j == 0, ch, jnp.where(j == 1, cm, jnp.where(j == 2, cl,
                              jnp.where(j < 7, 1.0, 0.0))))
            aug_k = jnp.where(j < 3, 1.0, jnp.where(j == 3, -ch, jnp.where(j == 4, -cm,
                              jnp.where(j == 5, -cl, jnp.where(j == 6, -shift, 0.0)))))
            aug_v = jnp.where(j == 0, 1.0, 0.0)
            q_out[hd] = jnp.where(low, qp, aug_q).astype(BF16)
            k_out[hd] = jnp.where(low, kp, aug_k).astype(BF16)
            v_out[hd] = jnp.where(low, vp, aug_v).astype(BF16)


def _fox_attn_kernel(par_ref, cq0_ref, cend_ref, q_ref, k_ref, v_ref, o_ref):
    i = pl.program_id(0)
    nh, tq, _ = q_ref.shape
    thr = par_ref[0]

    def first_block(hd):
        c0 = cq0_ref[i, hd]
        return lax.while_loop(
            lambda j: jnp.logical_and(j > 0, c0 - cend_ref[jnp.maximum(j - 1, 0), hd] >= -thr),
            lambda j: j - 1, i)

    def keys(hd, j, nblk):
        off = pl.multiple_of(j * FOX_TK, FOX_TK)
        return k_ref[hd, pl.ds(off, nblk * FOX_TK), :], v_ref[hd, pl.ds(off, nblk * FOX_TK), :]

    def causal(s, row0=0):
        r = lax.broadcasted_iota(jnp.int32, s.shape, 0) + row0
        c = lax.broadcasted_iota(jnp.int32, s.shape, 1)
        return jnp.where(c <= r, s, -jnp.inf)

    def fixed_shift():
        def block(hd, j, nblk):
            k, v = keys(hd, j, nblk)
            return _dot(jnp.exp2(_dot_nt(q_ref[hd], k)).astype(BF16), v)

        def diagonal(hd):
            k, v = keys(hd, i, 1)
            q, half = q_ref[hd], tq // 2
            top = _dot(jnp.exp2(causal(_dot_nt(q[0:half], k[0:half]))).astype(BF16), v[0:half])
            bot = _dot(jnp.exp2(causal(_dot_nt(q[half:tq], k), half)).astype(BF16), v)
            return jnp.concatenate([top, bot], axis=0)

        accs = [diagonal(hd) for hd in range(nh)]
        for hd in range(nh):
            acc, j_lo = accs[hd], first_block(hd)
            n = i - j_lo
            acc = lax.cond(n % 2 == 1, lambda: acc + block(hd, j_lo, 1), lambda: acc)
            acc = lax.cond((n // 2) % 2 == 1, lambda: acc + block(hd, j_lo + n % 2, 2), lambda: acc)
            first = j_lo + n % 4
            accs[hd] = lax.fori_loop(0, n // 4, lambda p, acc: acc + block(hd, first + 4 * p, 4), acc)
        return accs

    def running_max():
        def one_head(hd):
            q = q_ref[hd]

            def step(j, carry, masked):
                m, acc = carry
                k, v = keys(hd, j, 1)
                s = _dot_nt(q, k)
                if masked:
                    s = causal(s)
                m_new = jnp.maximum(m, jnp.max(s, axis=-1, keepdims=True))
                p = jnp.exp2(s - m_new)
                return m_new, jnp.exp2(m - m_new) * acc + _dot(p.astype(BF16), v)
            init = (jnp.full((tq, 1), -jnp.inf, F32), jnp.zeros((tq, LANES), F32))
            carry = step(i, init, True)
            return lax.fori_loop(first_block(hd), i, lambda j, c: step(j, c, False), carry)[1]
        return [one_head(hd) for hd in range(nh)]

    accs = lax.cond(par_ref[1] > 0.0, fixed_shift, running_max)
    outs = [acc / acc[:, HEAD_DIM:HEAD_DIM + 1] for acc in accs]
    low = lax.broadcasted_iota(jnp.int32, (tq, LANES), 1) < HEAD_DIM
    for p in range(nh // FOX_HEADS_PER_TILE):
        o_ref[p] = jnp.where(low, outs[2 * p], pltpu.roll(outs[2 * p + 1], HEAD_DIM, axis=1)).astype(BF16)


FOX_HEADS_PER_TILE = LANES // HEAD_DIM


def _fox_attn(par, cq0, cend, q, k, v):
    assert FOX_TQ == FOX_TK and FOX_HEADS_PER_TILE == 2
    nh, seq, _ = q.shape
    kv_spec = pl.BlockSpec((nh, seq, LANES), lambda i, *_: (0, 0, 0), pipeline_mode=pl.Buffered(1))
    return pl.pallas_call(
        _fox_attn_kernel,
        grid_spec=pltpu.PrefetchScalarGridSpec(
            num_scalar_prefetch=3,
            grid=(seq // FOX_TQ,),
            in_specs=[pl.BlockSpec((nh, FOX_TQ, LANES), lambda i, *_: (0, i, 0)), kv_spec, kv_spec],
            out_specs=pl.BlockSpec((nh // FOX_HEADS_PER_TILE, FOX_TQ, LANES), lambda i, *_: (0, i, 0))),
        out_shape=jax.ShapeDtypeStruct((nh // FOX_HEADS_PER_TILE, seq, LANES), BF16),
        compiler_params=pltpu.CompilerParams(dimension_semantics=("arbitrary",),
                                             vmem_limit_bytes=VMEM_LIMIT),
        name="fox_attn",
    )(par, cq0, cend, q, k, v)


def _fox_params(qn, kn):
    bound = FOX_BOUND_MARGIN * HEAD_DIM ** 0.5 * jnp.max(jnp.abs(qn), axis=1) * jnp.max(jnp.abs(kn), axis=1)
    shift = jnp.broadcast_to((LOG2E * bound)[:, None, None], (bound.shape[0], 1, LANES))
    par = jnp.stack([F32_EXP_ZERO + 2.0 * bound, (bound < FOX_FAST_MAX_LOGIT).astype(F32)], axis=1)
    return shift, par


def _fox_attention(par, q, k, v, c):
    per_tile = FOX_TQ // ROW_TILE
    heads = slice(SMALL_FF, SMALL_FF + FOX_HEADS)
    cq0 = c[0::per_tile, 0, heads]
    cend = c[per_tile - 1::per_tile, 1, heads]
    return _fox_attn(par, cq0, cend, q, k, v)


def _tile_rows(a, n):
    return jnp.concatenate([a] * n, axis=0)


def _linear_attn_chunks(qd_s, ki_s, ke_s, v_s, dec_s, o_s, st_s, nheads):
    t, w = qd_s.shape
    assert w == nheads * HEAD_DIM
    r = lax.broadcasted_iota(jnp.int32, (w, w), 0)
    c = lax.broadcasted_iota(jnp.int32, (w, w), 1)
    same_head = (r // HEAD_DIM) == (c // HEAD_DIM)
    rr = lax.broadcasted_iota(jnp.int32, (CHUNK, w), 0)
    cc = lax.broadcasted_iota(jnp.int32, (CHUNK, w), 1)
    tril = (cc % CHUNK) <= rr
    state = st_s[...]
    for ci in range(t // CHUNK):
        rows = slice(ci * CHUNK, (ci + 1) * CHUNK)
        last = slice((ci + 1) * CHUNK - 1, (ci + 1) * CHUNK)
        qd = qd_s[rows, :].astype(BF16)
        ke = ke_s[rows, :].astype(BF16)
        vv = v_s[rows, :].astype(BF16)
        k_bd = jnp.where(same_head, _tile_rows(ki_s[rows, :], nheads), 0.0).astype(BF16)
        v_bd = jnp.where(same_head, _tile_rows(v_s[rows, :], nheads), 0.0).astype(BF16)
        attn = jnp.where(tril, _dot_nt(qd, k_bd), 0.0).astype(BF16)
        o_s[rows, :] = _dot(attn, v_bd) + _dot_nt(qd, state.astype(BF16))
        state = state * dec_s[last, :] + jnp.where(same_head, _dot_tn(vv, ke), 0.0)
        yield
    st_s[...] = state


def _gla_body(h, small, w_ref, w2_ref, b2_ref, gn_ref, y_ref, qd_s, ki_s, ke_s, v_s, dec_s, o_s, st_s):
    u = yield from _project(h, w_ref)
    z = _dot(small.astype(BF16), w2_ref[...]) + b2_ref[...]
    log_a = _log_sigmoid(z) * (1.0 / GLA_TAU)
    b, rem = _chunk_cumsum(log_a)
    eb = jnp.exp(b)
    k = u[:, GLA_W:2 * GLA_W]
    qd_s[...] = u[:, 0:GLA_W] * (HEAD_DIM ** -0.5) * eb
    ki_s[...] = k * jnp.exp(-b)
    ke_s[...] = k * jnp.exp(rem)
    v_s[...] = u[:, 2 * GLA_W:3 * GLA_W]
    dec_s[...] = eb
    yield
    yield from _linear_attn_chunks(qd_s, ki_s, ke_s, v_s, dec_s, o_s, st_s, GLA_HEADS)
    o = o_s[...]
    y = o * lax.rsqrt(_group_mean_sq(o, HEAD_DIM) + NORM_EPS) * gn_ref[...]
    y_ref[...] = (y * _silu(u[:, 3 * GLA_W:4 * GLA_W])).astype(BF16)


def _ret_body(h, w_ref, cos_ref, sin_ref, gn_ref, y_ref, qd_s, ki_s, ke_s, v_s, dec_s, o_s, st_s):
    t = h.shape[0]
    u = yield from _project(h, w_ref)
    cos, sin = cos_ref[...], sin_ref[...]
    lane = lax.broadcasted_iota(jnp.int32, (t, RET_W), 1)
    half = HEAD_DIM // 2
    first_half = (lax.broadcasted_iota(jnp.int32, (t, LANES), 1) % HEAD_DIM) < half

    def rotate(a):
        blocks = []
        for b in range(0, RET_W, LANES):
            ab = a[:, b:b + LANES]
            swapped = jnp.where(first_half, pltpu.roll(ab, LANES - half, axis=1),
                                pltpu.roll(ab, half, axis=1))
            blocks.append(ab * cos + swapped * sin)
        return jnp.concatenate(blocks, axis=1)

    q = rotate(u[:, 0:RET_W])
    k = rotate(u[:, RET_W:2 * RET_W]) * (HEAD_DIM ** -0.5)
    row = lax.broadcasted_iota(jnp.int32, (t, RET_W), 0)
    lg = jnp.zeros((t, RET_W), F32)
    for hd in range(RET_HEADS):
        lg = jnp.where(lane // HEAD_DIM == hd, math.log(1.0 - 2.0 ** (-5.0 - hd)), lg)
    pos = (row % CHUNK).astype(F32)
    qw = jnp.exp((pos + 1.0) * lg)
    qd_s[...] = q * qw
    ki_s[...] = k * jnp.exp(-(pos + 1.0) * lg)
    ke_s[...] = k * jnp.exp((CHUNK - 1.0 - pos) * lg)
    v_s[...] = u[:, 2 * RET_W:3 * RET_W]
    dec_s[...] = qw
    yield
    yield from _linear_attn_chunks(qd_s, ki_s, ke_s, v_s, dec_s, o_s, st_s, RET_HEADS)
    o = o_s[...]
    y = o * lax.rsqrt(_group_mean_sq(o, HEAD_DIM) + NORM_EPS) * gn_ref[...]
    y_ref[...] = (y * _silu(u[:, 3 * RET_W:4 * RET_W])).astype(BF16)


def _linear_scratch(t, width, nheads):
    return [pltpu.VMEM((t, width), F32)] * 6 + [pltpu.VMEM((width, width), F32)]


SSD_PROJ = SSD_INNER + SSD_CONV_CH


def _ssd_body(h, small, w_ref, cw_ref, cb_ref, dtb_row_ref, alog_row_ref, d_row_ref, gn_ref, y_ref,
              xp_s, xs_s, bc_s, cs_s, ecs_s, xdt_s, xdtw_s, cst_s, o_s, st_s):
    t = h.shape[0]
    nc = t // CHUNK
    u = yield from _project(h, w_ref)
    z = u[:, 0:SSD_INNER]

    xp_s[SUBLANES:SUBLANES + t, :] = u[:, SSD_INNER:SSD_PROJ]
    conv = cb_ref[...] + cw_ref[SSD_CONV - 1:SSD_CONV, :] * xp_s[SUBLANES:SUBLANES + t, :]
    for kk in range(SSD_CONV - 1):
        off = SUBLANES - (SSD_CONV - 1) + kk
        conv = conv + cw_ref[kk:kk + 1, :] * xp_s[off:off + t, :]
    xp_s[0:SUBLANES, :] = xp_s[t:t + SUBLANES, :]
    xbc = _silu(conv)
    xs = xbc[:, 0:SSD_INNER]
    xs_s[...] = xs
    bc_s[...] = xbc[:, SSD_INNER:SSD_CONV_CH]

    dt = _softplus(small + dtb_row_ref[...])
    dta = dt * -jnp.exp(alog_row_ref[...])
    cs, _ = _chunk_cumsum(dta)
    cs_t = cs.T[SMALL_DT:SMALL_DT + SSD_HEADS, :]
    hpg = SSD_HEADS // SSD_GROUPS
    gcols = hpg * HEAD_DIM
    for ci in range(nc):
        for g in range(SSD_GROUPS):
            cst_s[ci, g:g + 1, :] = jnp.concatenate(
                [cs_t[g * hpg + rr:g * hpg + rr + 1, ci * CHUNK:(ci + 1) * CHUNK] for rr in range(hpg)], axis=1)

    er = lax.broadcasted_iota(jnp.int32, (LANES, SSD_INNER), 0)
    ec = lax.broadcasted_iota(jnp.int32, (LANES, SSD_INNER), 1)
    expand = jnp.where(ec // HEAD_DIM == er - SMALL_DT, 1.0, 0.0).astype(BF16)
    cs_x = _dot_sel2(cs, expand)
    xdt = xs * _dot_sel2(dt, expand)
    cs_s[...] = cs_x
    ecs_s[...] = jnp.exp(cs_x)
    xdt_s[...] = xdt
    xdtw_s[...] = xdt * jnp.exp(_chunk_last(cs_x) - cs_x)

    r = lax.broadcasted_iota(jnp.int32, (gcols, gcols), 0)
    c = lax.broadcasted_iota(jnp.int32, (gcols, gcols), 1)
    same_head = (r // HEAD_DIM) == (c // HEAD_DIM)
    rr = lax.broadcasted_iota(jnp.int32, (CHUNK, gcols), 0)
    cc = lax.broadcasted_iota(jnp.int32, (CHUNK, gcols), 1)
    tril = (cc % CHUNK) <= rr
    gw = SSD_GROUPS * SSD_STATE

    states = [st_s[g] for g in range(SSD_GROUPS)]
    for ci in range(nc):
        rows = slice(ci * CHUNK, (ci + 1) * CHUNK)
        last = slice((ci + 1) * CHUNK - 1, (ci + 1) * CHUNK)
        for g in range(SSD_GROUPS):
            gsl = slice(g * gcols, (g + 1) * gcols)
            bm = bc_s[rows, g * SSD_STATE:(g + 1) * SSD_STATE].astype(BF16)
            cm = bc_s[rows, gw + g * SSD_STATE:gw + (g + 1) * SSD_STATE].astype(BF16)
            cb = _dot_nt(cm, _tile_rows(bm, hpg))
            decay = jnp.exp(jnp.where(tril, cs_s[rows, gsl] - cst_s[ci, g:g + 1, :], -jnp.inf))
            x_bd = jnp.where(same_head, _tile_rows(xdt_s[rows, gsl], hpg), 0.0).astype(BF16)
            y_inter = _dot(cm, states[g].astype(BF16)) * ecs_s[rows, gsl]
            o_s[rows, gsl] = _dot((cb * decay).astype(BF16), x_bd) + y_inter
            states[g] = (states[g] * ecs_s[last, gsl]
                         + _dot_tn(bm, xdtw_s[rows, gsl].astype(BF16)))
        yield
    for g in range(SSD_GROUPS):
        st_s[g] = states[g]

    y = (o_s[...] + d_row_ref[...] * xs_s[...]) * _silu(z)
    gwid = SSD_INNER // SSD_GROUPS
    normed = []
    for g in range(SSD_GROUPS):
        yg = y[:, g * gwid:(g + 1) * gwid]
        normed.append(yg * lax.rsqrt(jnp.mean(yg * yg, axis=-1, keepdims=True) + NORM_EPS))
    y_ref[...] = (jnp.concatenate(normed, axis=1) * gn_ref[...]).astype(BF16)


N_FOX_IN, N_GLA_IN, N_RET_IN, N_SSD_IN = 5, 4, 4, 7
N_LINEAR_SCRATCH = 7
N_SSD_SCRATCH = 10


def _mixers_kernel(x_ref, ln_ref, ws_ref, *refs):
    it = iter(refs)
    take = lambda n: [next(it) for _ in range(n)]
    fox_in, gla_in, ret_in, ssd_in = take(N_FOX_IN), take(N_GLA_IN), take(N_RET_IN), take(N_SSD_IN)
    fox_out, (yb_ref, yc_ref, yd_ref) = take(4), take(3)
    (carry_ref,), gla_s, ret_s, ssd_s = take(1), take(N_LINEAR_SCRATCH), take(N_LINEAR_SCRATCH), take(N_SSD_SCRATCH)

    @pl.when(pl.program_id(0) == 0)
    def _():
        carry_ref[...] = jnp.zeros_like(carry_ref)
        gla_s[-1][...] = jnp.zeros_like(gla_s[-1])
        ret_s[-1][...] = jnp.zeros_like(ret_s[-1])
        ssd_s[-1][...] = jnp.zeros_like(ssd_s[-1])
        ssd_s[0][0:SUBLANES, :] = jnp.zeros((SUBLANES, SSD_CONV_CH), F32)

    h = _rmsnorm(x_ref[...], ln_ref[...]).astype(BF16)
    small = _dot_nt(h, ws_ref[...])
    waiting = [_ssd_body(h, small, *ssd_in, yd_ref, *ssd_s), _gla_body(h, small, *gla_in, yb_ref, *gla_s),
               _ret_body(h, *ret_in, yc_ref, *ret_s), _fox_prep_body(h, small, *fox_in, *fox_out, carry_ref)]
    pending = []
    while waiting or pending:
        if waiting:
            pending.append(waiting.pop(0))
        for body in list(pending):
            if next(body, "done") == "done":
                pending.remove(body)


def _mixers(l, x, ln, w_small, fox_in, gla_in, ret_in, ssd_in, cos_t, sin_t):
    seq = x.shape[0]
    t = ROW_TILE
    row = lambda w: pl.BlockSpec((t, w), lambda i: (i, 0))
    heads = pl.BlockSpec((FOX_HEADS, t, LANES), lambda i: (0, i, 0))
    head_shape = jax.ShapeDtypeStruct((FOX_HEADS, seq, LANES), BF16)
    w_ret, ret_gn = ret_in
    wide = pltpu.VMEM((t, SSD_INNER), F32)
    return pl.pallas_call(
        _mixers_kernel,
        grid=(seq // t,),
        in_specs=([row(D_MODEL), _layer(ln, l), _layer(w_small, l)] + [_layer(a, l) for a in fox_in]
                  + [_layer(a, l) for a in gla_in]
                  + [_layer(w_ret, l), row(LANES), row(LANES), _layer(ret_gn, l)]
                  + [_layer(a, l) for a in ssd_in]),
        out_specs=[heads, heads, heads, pl.BlockSpec((1, SUBLANES, LANES), lambda i: (i, 0, 0)),
                   row(GLA_W), row(RET_W), row(SSD_INNER)],
        out_shape=[head_shape, head_shape, head_shape,
                   jax.ShapeDtypeStruct((seq // t, SUBLANES, LANES), F32),
                   jax.ShapeDtypeStruct((seq, GLA_W), BF16),
                   jax.ShapeDtypeStruct((seq, RET_W), BF16),
                   jax.ShapeDtypeStruct((seq, SSD_INNER), BF16)],
        scratch_shapes=([pltpu.VMEM((SUBLANES, LANES), F32)]
                        + _linear_scratch(t, GLA_W, GLA_HEADS) + _linear_scratch(t, RET_W, RET_HEADS)
                        + [pltpu.VMEM((t + 2 * SUBLANES, SSD_CONV_CH), F32),
                           wide,
                           pltpu.VMEM((t, 2 * SSD_GROUPS * SSD_STATE), F32),
                           wide, wide, wide, wide,
                           pltpu.VMEM((t // CHUNK, SSD_GROUPS, SSD_INNER // SSD_GROUPS), F32),
                           wide,
                           pltpu.VMEM((SSD_GROUPS, SSD_STATE, SSD_INNER // SSD_GROUPS), F32)]),
        compiler_params=pltpu.CompilerParams(dimension_semantics=("arbitrary",),
                                             vmem_limit_bytes=VMEM_LIMIT),
        name="mixers",
    )(x, ln, w_small, *fox_in, *gla_in, w_ret, cos_t, sin_t, ret_gn, *ssd_in)


def _merge_ffn_kernel(x_ref, ln1_ref, wg_ref, ya_ref, yb_ref, yc_ref, yd_ref,
                      wa_ref, wb_ref, wc_ref, wd_ref, wo_ref, ln2_ref, w1_ref, w2_ref, out_ref):
    x = x_ref[...]
    h = _rmsnorm(x, ln1_ref[...]).astype(BF16)

    def gate(b):
        return _sigmoid(_dot_nt(h, wg_ref[b * D_MODEL:(b + 1) * D_MODEL, :]))

    up_a = _dot(ya_ref[0], wa_ref[0:LANES, :])
    for p in range(1, FOX_HEADS // FOX_HEADS_PER_TILE):
        up_a = up_a + _dot(ya_ref[p], wa_ref[p * LANES:(p + 1) * LANES, :])
    merged = gate(0) * up_a
    merged = merged + gate(1) * _dot(yb_ref[...], wb_ref[...])
    merged = merged + gate(2) * _dot(yc_ref[...], wc_ref[...])
    merged = merged + gate(3) * _dot(yd_ref[...], wd_ref[...])
    x = x + _dot(merged.astype(BF16), wo_ref[...])

    h = _rmsnorm(x, ln2_ref[...]).astype(BF16)
    act = _silu(_dot(h, w1_ref[:, 0:FFN_HIDDEN])) * _dot(h, w1_ref[:, FFN_HIDDEN:2 * FFN_HIDDEN])
    out_ref[...] = x + _dot(act.astype(BF16), w2_ref[...])


def _merge_ffn(l, x, ln1, w_gates, y_a, y_b, y_c, y_d, w_up_a, w_up_b, w_up_c, w_up_d, w_out, ln2, w1, w2):
    seq = x.shape[0]
    t = FFN_TILE
    row = lambda w: pl.BlockSpec((t, w), lambda i: (i, 0))
    return pl.pallas_call(
        _merge_ffn_kernel,
        grid=(seq // t,),
        in_specs=[row(D_MODEL), _layer(ln1, l), _layer(w_gates, l),
                  pl.BlockSpec((FOX_HEADS // FOX_HEADS_PER_TILE, t, LANES), lambda i: (0, i, 0)),
                  row(GLA_W), row(RET_W), row(SSD_INNER),
                  _layer(w_up_a, l), _layer(w_up_b, l), _layer(w_up_c, l),
                  _layer(w_up_d, l), _layer(w_out, l), _layer(ln2, l), _layer(w1, l), _layer(w2, l)],
        out_specs=row(D_MODEL),
        out_shape=jax.ShapeDtypeStruct((seq, D_MODEL), F32),
        compiler_params=pltpu.CompilerParams(dimension_semantics=("arbitrary",),
                                             vmem_limit_bytes=VMEM_LIMIT),
        name="merge_swiglu",
    )(x, ln1, w_gates, y_a, y_b, y_c, y_d, w_up_a, w_up_b, w_up_c, w_up_d, w_out, ln2, w1, w2)


def _rows(a):
    return a.reshape(a.shape[0], 1, -1)


def _in_proj_weights(w_in):
    offs = np.concatenate([[0], np.cumsum(IN_SPLITS)])
    (fq, fk, fv, ff, gq, gk, gv, glr, gr, rq, rk, rv, rg, z, xbc, dt, gates) = [
        w_in[:, :, offs[n]:offs[n + 1]] for n in range(len(IN_SPLITS))]
    group = lambda parts: jnp.transpose(jnp.concatenate(parts, axis=2), (0, 2, 1)).astype(BF16)
    gap = lambda n: jnp.zeros(w_in.shape[:2] + (n,), w_in.dtype)
    w_small = group([ff, glr, gap(SMALL_DT - SMALL_GLR - GLA_LOWRANK), dt, gap(LANES - SMALL_DT - SSD_HEADS)])
    return w_small, group([fq, fk, fv]), group([gq, gk, gv, gr]), group([rq, rk, rv, rg]), group([z, xbc]), group([gates])


def kernel(x, positions, ln1, ln2, w_in, fox_bf, fox_qn, fox_kn, gla_w2, gla_b, gla_norm, ret_norm,
           ssd_conv_w, ssd_conv_b, ssd_dt_bias, ssd_a_log, ssd_d, ssd_norm,
           w_up_a, w_up_b, w_up_c, w_up_d, w_out, w_ffn_in, w_ffn_out):
    bsz, seq, d = x.shape
    assert bsz == 1 and d == D_MODEL and seq % ROW_TILE == 0 and seq % FOX_TQ == 0 and FOX_TQ % ROW_TILE == 0
    depth = ln1.shape[0]
    xr = x.reshape(seq, d)
    cos_t, sin_t = _rotary_tables(positions, seq)

    bf16 = lambda w: w.astype(BF16)
    ln1_r, ln2_r = _rows(ln1), _rows(ln2)
    fox_bf_r = _rows(jnp.pad(fox_bf, ((0, 0), (SMALL_FF, LANES - SMALL_FF - FOX_HEADS))))
    fox_qn_r, fox_kn_r = _rows(jnp.tile(fox_qn, (1, 2))), _rows(jnp.tile(fox_kn, (1, 2)))
    fox_shift_r, fox_par = _fox_params(fox_qn, fox_kn)
    gla_w2_p = bf16(jnp.pad(gla_w2, ((0, 0), (SMALL_GLR, LANES - SMALL_GLR - GLA_LOWRANK), (0, 0))))
    gla_b_r, gla_gn_r = _rows(gla_b), _rows(jnp.tile(gla_norm, (1, GLA_HEADS)))
    ret_gn_r = _rows(jnp.tile(ret_norm, (1, RET_HEADS)))
    conv_b_r = _rows(ssd_conv_b)
    at_dt = lambda a: _rows(jnp.pad(a, ((0, 0), (SMALL_DT, LANES - SMALL_DT - SSD_HEADS))))
    dtb_r, alog_r = at_dt(ssd_dt_bias), at_dt(ssd_a_log)
    d_r, ssd_gn_r = _rows(jnp.repeat(ssd_d, HEAD_DIM, axis=1)), _rows(ssd_norm)
    up_a, up_b, up_c, up_d, w_o = bf16(w_up_a), bf16(w_up_b), bf16(w_up_c), bf16(w_up_d), bf16(w_out)
    w1, w2 = bf16(w_ffn_in), bf16(w_ffn_out)
    w_small, w_fox, w_gla, w_ret, w_ssd, w_gates = _in_proj_weights(w_in)

    for l in range(depth):
        q, k, v, c, y_b, y_c, y_d = _mixers(
            l, xr, ln1_r, w_small,
            (w_fox, fox_bf_r, fox_qn_r, fox_kn_r, fox_shift_r),
            (w_gla, gla_w2_p, gla_b_r, gla_gn_r),
            (w_ret, ret_gn_r),
            (w_ssd, ssd_conv_w, conv_b_r, dtb_r, alog_r, d_r, ssd_gn_r),
            cos_t, sin_t)
        y_a = _fox_attention(fox_par[l], q, k, v, c)
        xr = _merge_ffn(l, xr, ln1_r, w_gates, y_a, y_b, y_c, y_d, up_a, up_b, up_c, up_d, w_o, ln2_r, w1, w2)
    return xr.reshape(bsz, seq, d)
```

```python
import functools
import math

import numpy as np
import jax
import jax.numpy as jnp
from jax import lax
from jax.experimental import pallas as pl
from jax.experimental.pallas import tpu as pltpu

D_MODEL = 1024
HEAD_DIM = 64
CHUNK = 64
FOX_HEADS = 4
GLA_HEADS = 4
GLA_LOWRANK = 16
GLA_TAU = 16.0
RET_HEADS = 4
ROPE_THETA = 10000.0
SSD_HEADS = 8
SSD_GROUPS = 2
SSD_STATE = 64
SSD_CONV = 4
SSD_INNER = SSD_HEADS * HEAD_DIM
SSD_CONV_CH = SSD_INNER + 2 * SSD_GROUPS * SSD_STATE
FOX_W = FOX_HEADS * HEAD_DIM
GLA_W = GLA_HEADS * HEAD_DIM
RET_W = RET_HEADS * HEAD_DIM
N_BRANCH = 4
FFN_HIDDEN = ((8 * D_MODEL + 3 * 256 - 1) // (3 * 256)) * 256
NORM_EPS = 1e-6
IN_SPLITS = (FOX_W, FOX_W, FOX_W, FOX_HEADS,
             GLA_W, GLA_W, GLA_W, GLA_LOWRANK, GLA_W,
             RET_W, RET_W, RET_W, RET_W,
             SSD_INNER, SSD_CONV_CH, SSD_HEADS,
             N_BRANCH * D_MODEL)

LANES = 128
SUBLANES = 8
VMEM_LIMIT = 56 * 1024 * 1024

ROW_TILE = 512
FOX_TQ = 512
FOX_TK = 512
FFN_TILE = 512
PROJ_PIECE = 512

F32 = jnp.float32
BF16 = jnp.bfloat16


def _rmsnorm(x, g):
    return x * lax.rsqrt(jnp.mean(x * x, axis=-1, keepdims=True) + NORM_EPS) * g


def _log_sigmoid(x):
    return jnp.minimum(x, 0.0) - jnp.log(1.0 + jnp.exp(-jnp.abs(x)))


def _softplus(x):
    return jnp.maximum(x, 0.0) + jnp.log(1.0 + jnp.exp(-jnp.abs(x)))


def _sigmoid(x):
    return 0.5 * jnp.tanh(0.5 * x) + 0.5


def _silu(x):
    return x * _sigmoid(x)


def _split3(x):
    hi = x.astype(BF16)
    r1 = x - hi.astype(F32)
    mid = r1.astype(BF16)
    lo = (r1 - mid.astype(F32)).astype(BF16)
    return hi, mid, lo


def _split2(x):
    hi = x.astype(BF16)
    return hi, (x - hi.astype(F32)).astype(BF16)


def _dot(a, b):
    return jnp.dot(a, b, preferred_element_type=F32)


def _dot_nt(a, b):
    return lax.dot_general(a, b, (((1,), (1,)), ((), ())), preferred_element_type=F32)


def _dot_tn(a, b):
    return lax.dot_general(a, b, (((0,), (0,)), ((), ())), preferred_element_type=F32)


def _sel_dot(mat, x):
    hi, mid, lo = _split3(x)
    return _dot(mat, hi) + _dot(mat, mid) + _dot(mat, lo)


def _dot_sel2(x, mat):
    hi, lo = _split2(x)
    return _dot(hi, mat) + _dot(lo, mat)


def _chunk_cumsum(x):
    t = x.shape[0]
    r = lax.broadcasted_iota(jnp.int32, (t, t), 0)
    c = lax.broadcasted_iota(jnp.int32, (t, t), 1)
    incl = jnp.where(((r // CHUNK) == (c // CHUNK)) & (c <= r), 1.0, 0.0).astype(BF16)
    hi, lo = _split2(x)
    b = _dot(incl, hi) + _dot(incl, lo)
    return b, _chunk_last(b) - b


def _chunk_last(b):
    t, w = b.shape
    return jnp.concatenate([jnp.broadcast_to(b[e - 1:e, :], (CHUNK, w)) for e in range(CHUNK, t + 1, CHUNK)],
                           axis=0)


def _group_mean_sq(o, group):
    w = o.shape[-1]
    r = lax.broadcasted_iota(jnp.int32, (w, w), 0)
    c = lax.broadcasted_iota(jnp.int32, (w, w), 1)
    bd = jnp.where((r // group) == (c // group), 1.0, 0.0).astype(BF16)
    return _dot_sel2(o * o, bd) * (1.0 / group)


def _project(h, wt_ref):
    n = wt_ref.shape[0]
    parts = []
    for a in range(0, n, PROJ_PIECE):
        parts.append(_dot_nt(h, wt_ref[a:min(a + PROJ_PIECE, n), :]))
        yield
    return jnp.concatenate(parts, axis=1)


def _layer(stacked, l):
    tail = tuple(stacked.shape[1:])
    return pl.BlockSpec((None,) + tail, lambda *_: (l,) + (0,) * len(tail), pipeline_mode=pl.Buffered(1))


def _rotary_kernel(pos_ref, inv_ref, sign_ref, cos_ref, sin_ref):
    ang = pos_ref[...] * inv_ref[...]
    cos_ref[...] = jnp.cos(ang)
    sin_ref[...] = jnp.sin(ang) * sign_ref[...]


def _rotary_tables(positions, seq):
    half = HEAD_DIM // 2
    inv = ROPE_THETA ** (-jnp.arange(half, dtype=F32) / half)
    reps = LANES // HEAD_DIM
    inv_row = jnp.tile(inv, 2 * reps).reshape(1, LANES)
    sign_row = jnp.tile(jnp.concatenate([-jnp.ones((half,), F32), jnp.ones((half,), F32)]),
                        reps).reshape(1, LANES)
    pos = positions.astype(F32).reshape(seq, 1)
    t = ROW_TILE
    return pl.pallas_call(
        _rotary_kernel,
        grid=(seq // t,),
        in_specs=[pl.BlockSpec((t, 1), lambda i: (i, 0)),
                  pl.BlockSpec((1, LANES), lambda i: (0, 0)),
                  pl.BlockSpec((1, LANES), lambda i: (0, 0))],
        out_specs=[pl.BlockSpec((t, LANES), lambda i: (i, 0)),
                   pl.BlockSpec((t, LANES), lambda i: (i, 0))],
        out_shape=[jax.ShapeDtypeStruct((seq, LANES), F32)] * 2,
        name="rotary_tables",
    )(pos, inv_row, sign_row)


SMALL_FF = 0
SMALL_GLR = FOX_HEADS
SMALL_DT = 24
LOG2E = math.log2(math.e)
F32_EXP_ZERO = 88.0
FOX_FAST_MAX_LOGIT = 40.0
FOX_BOUND_MARGIN = 1.02


def _fox_prep_body(h, small, w_ref, bf_ref, qn_ref, kn_ref, shift_ref, q_out, k_out, v_out, c_out, carry_ref):
    t = h.shape[0]
    u = yield from _project(h, w_ref)
    ls = _log_sigmoid(small + bf_ref[...])
    r = lax.broadcasted_iota(jnp.int32, (t, t), 0)
    cc = lax.broadcasted_iota(jnp.int32, (t, t), 1)
    tri = jnp.where(cc <= r, 1.0, 0.0).astype(BF16)
    c = _sel_dot(tri, ls) + carry_ref[0:1, :]
    carry_ref[0:1, :] = c[t - 1:t, :]
    c_out[0] = jnp.concatenate([c[0:1, :], c[t - 1:t, :], jnp.zeros((SUBLANES - 2, LANES), F32)], axis=0)
    c_hi, c_mid, c_lo = _split3(c * LOG2E)
    c_hi, c_mid, c_lo = c_hi.astype(F32), c_mid.astype(F32), c_lo.astype(F32)

    lane = lax.broadcasted_iota(jnp.int32, (t, LANES), 1)
    low = lane < HEAD_DIM
    j = lane - HEAD_DIM
    shift = shift_ref[...]

    def head_norm(pair, gain, mult):
        sq = pair * pair
        s_lo = jnp.sum(jnp.where(low, sq, 0.0), axis=-1, keepdims=True)
        s_hi = jnp.sum(jnp.where(low, 0.0, sq), axis=-1, keepdims=True)
        ms = jnp.where(low, s_lo, s_hi) * (1.0 / HEAD_DIM)
        return pair * lax.rsqrt(ms + NORM_EPS) * (gain * mult)

    for p in range(FOX_HEADS // 2):
        qp = head_norm(u[:, p * LANES:(p + 1) * LANES], qn_ref[...], HEAD_DIM ** -0.5 * LOG2E)
        kp = head_norm(u[:, FOX_W + p * LANES:FOX_W + (p + 1) * LANES], kn_ref[...], 1.0)
        vp = u[:, 2 * FOX_W + p * LANES:2 * FOX_W + (p + 1) * LANES]
        for sub in range(2):
            hd = 2 * p + sub
            if sub == 1:
                qp, kp, vp = (pltpu.roll(a, HEAD_DIM, axis=1) for a in (qp, kp, vp))
            ch, cm, cl = (jnp.broadcast_to(a[:, SMALL_FF + hd:SMALL_FF + hd + 1], (t, LANES)) for a in (c_hi, c_mid, c_lo))
            aug_q = jnp.where(j == 0, ch, jnp.where(j == 1, cm, jnp.where(j == 2, cl,
                              jnp.where(j < 7, 1.0, 0.0))))
            aug_k = jnp.where(j < 3, 1.0, jnp.where(j == 3, -ch, jnp.where(j == 4, -cm,
                              jnp.where(j == 5, -cl, jnp.where(j == 6, -shift, 0.0)))))
            aug_v = jnp.where(j == 0, 1.0, 0.0)
            q_out[hd] = jnp.where(low, qp, aug_q).astype(BF16)
            k_out[hd] = jnp.where(low, kp, aug_k).astype(BF16)
            v_out[hd] = jnp.where(low, vp, aug_v).astype(BF16)


def _fox_attn_kernel(par_ref, cq0_ref, cend_ref, q_ref, k_hbm, v_hbm, o_ref, k_ref, v_ref, kv_sems):
    i = pl.program_id(0)
    nh, tq, _ = q_ref.shape
    thr = par_ref[0]

    def kv_copies(j):
        rows = pl.ds(pl.multiple_of(j * FOX_TK, FOX_TK), FOX_TK)
        return [pltpu.make_async_copy(src.at[:, rows, :], dst.at[:, rows, :], kv_sems.at[w, j % 2])
                for w, (src, dst) in enumerate(((k_hbm, k_ref), (v_hbm, v_ref)))]

    @pl.when(i == 0)
    def _():
        for cp in kv_copies(0):
            cp.start()

    for cp in kv_copies(i):
        cp.wait()

    @pl.when(i + 1 < pl.num_programs(0))
    def _():
        for cp in kv_copies(i + 1):
            cp.start()

    def first_block(hd):
        c0 = cq0_ref[i, hd]
        return lax.while_loop(
            lambda j: jnp.logical_and(j > 0, c0 - cend_ref[jnp.maximum(j - 1, 0), hd] >= -thr),
            lambda j: j - 1, i)

    def keys(hd, j, nblk):
        off = pl.multiple_of(j * FOX_TK, FOX_TK)
        return k_ref[hd, pl.ds(off, nblk * FOX_TK), :], v_ref[hd, pl.ds(off, nblk * FOX_TK), :]

    def causal(s):
        r = lax.broadcasted_iota(jnp.int32, s.shape, 0)
        c = lax.broadcasted_iota(jnp.int32, s.shape, 1)
        return jnp.where(c <= r, s, -jnp.inf)

    def fixed_shift():
        def block(hd, j, nblk):
            k, v = keys(hd, j, nblk)
            return _dot(jnp.exp2(_dot_nt(q_ref[hd], k)).astype(BF16), v)

        def diagonal(hd):
            k, v = keys(hd, i, 1)
            return _dot(jnp.exp2(causal(_dot_nt(q_ref[hd], k))).astype(BF16), v)

        accs = [diagonal(hd) for hd in range(nh)]
        for hd in range(nh):
            acc, j_lo = accs[hd], first_block(hd)
            n = i - j_lo
            acc = lax.cond(n % 2 == 1, lambda: acc + block(hd, j_lo, 1), lambda: acc)
            acc = lax.cond((n // 2) % 2 == 1, lambda: acc + block(hd, j_lo + n % 2, 2), lambda: acc)
            first = j_lo + n % 4
            accs[hd] = lax.fori_loop(0, n // 4, lambda p, acc: acc + block(hd, first + 4 * p, 4), acc)
        return accs

    def running_max():
        def one_head(hd):
            q = q_ref[hd]

            def step(j, carry, masked):
                m, acc = carry
                k, v = keys(hd, j, 1)
                s = _dot_nt(q, k)
                if masked:
                    s = causal(s)
                m_new = jnp.maximum(m, jnp.max(s, axis=-1, keepdims=True))
                p = jnp.exp2(s - m_new)
                return m_new, jnp.exp2(m - m_new) * acc + _dot(p.astype(BF16), v)
            init = (jnp.full((tq, 1), -jnp.inf, F32), jnp.zeros((tq, LANES), F32))
            carry = step(i, init, True)
            return lax.fori_loop(first_block(hd), i, lambda j, c: step(j, c, False), carry)[1]
        return [one_head(hd) for hd in range(nh)]

    accs = lax.cond(par_ref[1] > 0.0, fixed_shift, running_max)
    outs = [acc / acc[:, HEAD_DIM:HEAD_DIM + 1] for acc in accs]
    low = lax.broadcasted_iota(jnp.int32, (tq, LANES), 1) < HEAD_DIM
    for p in range(nh // FOX_HEADS_PER_TILE):
        o_ref[p] = jnp.where(low, outs[2 * p], pltpu.roll(outs[2 * p + 1], HEAD_DIM, axis=1)).astype(BF16)


FOX_HEADS_PER_TILE = LANES // HEAD_DIM


def _fox_attn(par, cq0, cend, q, k, v):
    assert FOX_TQ == FOX_TK and FOX_HEADS_PER_TILE == 2
    nh, seq, _ = q.shape
    kv_spec = pl.BlockSpec(memory_space=pl.ANY)
    kv_vmem = pltpu.VMEM((nh, seq, LANES), BF16)
    return pl.pallas_call(
        _fox_attn_kernel,
        grid_spec=pltpu.PrefetchScalarGridSpec(
            num_scalar_prefetch=3,
            grid=(seq // FOX_TQ,),
            in_specs=[pl.BlockSpec((nh, FOX_TQ, LANES), lambda i, *_: (0, i, 0)), kv_spec, kv_spec],
            out_specs=pl.BlockSpec((nh // FOX_HEADS_PER_TILE, FOX_TQ, LANES), lambda i, *_: (0, i, 0)),
            scratch_shapes=[kv_vmem, kv_vmem, pltpu.SemaphoreType.DMA((2, 2))]),
        out_shape=jax.ShapeDtypeStruct((nh // FOX_HEADS_PER_TILE, seq, LANES), BF16),
        compiler_params=pltpu.CompilerParams(dimension_semantics=("arbitrary",),
                                             vmem_limit_bytes=VMEM_LIMIT),
        name="fox_attn",
    )(par, cq0, cend, q, k, v)


def _fox_params(qn, kn):
    bound = FOX_BOUND_MARGIN * HEAD_DIM ** 0.5 * jnp.max(jnp.abs(qn), axis=1) * jnp.max(jnp.abs(kn), axis=1)
    shift = jnp.broadcast_to((LOG2E * bound)[:, None, None], (bound.shape[0], 1, LANES))
    par = jnp.stack([F32_EXP_ZERO + 2.0 * bound, (bound < FOX_FAST_MAX_LOGIT).astype(F32)], axis=1)
    return shift, par


def _fox_attention(par, q, k, v, c):
    per_tile = FOX_TQ // ROW_TILE
    heads = slice(SMALL_FF, SMALL_FF + FOX_HEADS)
    cq0 = c[0::per_tile, 0, heads]
    cend = c[per_tile - 1::per_tile, 1, heads]
    return _fox_attn(par, cq0, cend, q, k, v)


def _tile_rows(a, n):
    return jnp.concatenate([a] * n, axis=0)


def _linear_attn_chunks(qd_s, ki_s, ke_s, v_s, dec_s, o_s, st_s, nheads):
    t, w = qd_s.shape
    assert w == nheads * HEAD_DIM
    r = lax.broadcasted_iota(jnp.int32, (w, w), 0)
    c = lax.broadcasted_iota(jnp.int32, (w, w), 1)
    same_head = (r // HEAD_DIM) == (c // HEAD_DIM)
    rr = lax.broadcasted_iota(jnp.int32, (CHUNK, w), 0)
    cc = lax.broadcasted_iota(jnp.int32, (CHUNK, w), 1)
    tril = (cc % CHUNK) <= rr
    state = st_s[...]
    for ci in range(t // CHUNK):
        rows = slice(ci * CHUNK, (ci + 1) * CHUNK)
        last = slice((ci + 1) * CHUNK - 1, (ci + 1) * CHUNK)
        qd = qd_s[rows, :].astype(BF16)
        ke = ke_s[rows, :].astype(BF16)
        vv = v_s[rows, :].astype(BF16)
        k_bd = jnp.where(same_head, _tile_rows(ki_s[rows, :], nheads), 0.0).astype(BF16)
        v_bd = jnp.where(same_head, _tile_rows(v_s[rows, :], nheads), 0.0).astype(BF16)
        attn = jnp.where(tril, _dot_nt(qd, k_bd), 0.0).astype(BF16)
        o_s[rows, :] = _dot(attn, v_bd) + _dot_nt(qd, state.astype(BF16))
        state = state * dec_s[last, :] + jnp.where(same_head, _dot_tn(vv, ke), 0.0)
        yield
    st_s[...] = state


def _gla_body(h, small, w_ref, w2_ref, b2_ref, gn_ref, y_ref, qd_s, ki_s, ke_s, v_s, dec_s, o_s, st_s):
    u = yield from _project(h, w_ref)
    z = _dot(small.astype(BF16), w2_ref[...]) + b2_ref[...]
    log_a = _log_sigmoid(z) * (1.0 / GLA_TAU)
    b, rem = _chunk_cumsum(log_a)
    eb = jnp.exp(b)
    k = u[:, GLA_W:2 * GLA_W]
    qd_s[...] = u[:, 0:GLA_W] * (HEAD_DIM ** -0.5) * eb
    ki_s[...] = k * jnp.exp(-b)
    ke_s[...] = k * jnp.exp(rem)
    v_s[...] = u[:, 2 * GLA_W:3 * GLA_W]
    dec_s[...] = eb
    yield
    yield from _linear_attn_chunks(qd_s, ki_s, ke_s, v_s, dec_s, o_s, st_s, GLA_HEADS)
    o = o_s[...]
    y = o * lax.rsqrt(_group_mean_sq(o, HEAD_DIM) + NORM_EPS) * gn_ref[...]
    y_ref[...] = (y * _silu(u[:, 3 * GLA_W:4 * GLA_W])).astype(BF16)


def _ret_body(h, w_ref, cos_ref, sin_ref, gn_ref, y_ref, qd_s, ki_s, ke_s, v_s, dec_s, o_s, st_s):
    t = h.shape[0]
    u = yield from _project(h, w_ref)
    cos, sin = cos_ref[...], sin_ref[...]
    lane = lax.broadcasted_iota(jnp.int32, (t, RET_W), 1)
    half = HEAD_DIM // 2
    first_half = (lax.broadcasted_iota(jnp.int32, (t, LANES), 1) % HEAD_DIM) < half

    def rotate(a):
        blocks = []
        for b in range(0, RET_W, LANES):
            ab = a[:, b:b + LANES]
            swapped = jnp.where(first_half, pltpu.roll(ab, LANES - half, axis=1),
                                pltpu.roll(ab, half, axis=1))
            blocks.append(ab * cos + swapped * sin)
        return jnp.concatenate(blocks, axis=1)

    q = rotate(u[:, 0:RET_W])
    k = rotate(u[:, RET_W:2 * RET_W]) * (HEAD_DIM ** -0.5)
    row = lax.broadcasted_iota(jnp.int32, (t, RET_W), 0)
    lg = jnp.zeros((t, RET_W), F32)
    for hd in range(RET_HEADS):
        lg = jnp.where(lane // HEAD_DIM == hd, math.log(1.0 - 2.0 ** (-5.0 - hd)), lg)
    pos = (row % CHUNK).astype(F32)
    qw = jnp.exp((pos + 1.0) * lg)
    qd_s[...] = q * qw
    ki_s[...] = k * jnp.exp(-(pos + 1.0) * lg)
    ke_s[...] = k * jnp.exp((CHUNK - 1.0 - pos) * lg)
    v_s[...] = u[:, 2 * RET_W:3 * RET_W]
    dec_s[...] = qw
    yield
    yield from _linear_attn_chunks(qd_s, ki_s, ke_s, v_s, dec_s, o_s, st_s, RET_HEADS)
    o = o_s[...]
    y = o * lax.rsqrt(_group_mean_sq(o, HEAD_DIM) + NORM_EPS) * gn_ref[...]
    y_ref[...] = (y * _silu(u[:, 3 * RET_W:4 * RET_W])).astype(BF16)


def _linear_scratch(t, width, nheads):
    return [pltpu.VMEM((t, width), F32)] * 6 + [pltpu.VMEM((width, width), F32)]


SSD_PROJ = SSD_INNER + SSD_CONV_CH


def _ssd_body(h, small, w_ref, cw_ref, cb_ref, dtb_row_ref, alog_row_ref, d_row_ref, gn_ref, y_ref,
              xp_s, xs_s, bc_s, cs_s, ecs_s, xdt_s, xdtw_s, cst_s, o_s, st_s):
    t = h.shape[0]
    nc = t // CHUNK
    u = yield from _project(h, w_ref)
    z = u[:, 0:SSD_INNER]

    xp_s[SUBLANES:SUBLANES + t, :] = u[:, SSD_INNER:SSD_PROJ]
    conv = cb_ref[...] + cw_ref[SSD_CONV - 1:SSD_CONV, :] * xp_s[SUBLANES:SUBLANES + t, :]
    for kk in range(SSD_CONV - 1):
        off = SUBLANES - (SSD_CONV - 1) + kk
        conv = conv + cw_ref[kk:kk + 1, :] * xp_s[off:off + t, :]
    xp_s[0:SUBLANES, :] = xp_s[t:t + SUBLANES, :]
    xbc = _silu(conv)
    xs = xbc[:, 0:SSD_INNER]
    xs_s[...] = xs
    bc_s[...] = xbc[:, SSD_INNER:SSD_CONV_CH]

    dt = _softplus(small + dtb_row_ref[...])
    dta = dt * -jnp.exp(alog_row_ref[...])
    cs, _ = _chunk_cumsum(dta)
    cs_t = cs.T[SMALL_DT:SMALL_DT + SSD_HEADS, :]
    hpg = SSD_HEADS // SSD_GROUPS
    gcols = hpg * HEAD_DIM
    for ci in range(nc):
        for g in range(SSD_GROUPS):
            cst_s[ci, g:g + 1, :] = jnp.concatenate(
                [cs_t[g * hpg + rr:g * hpg + rr + 1, ci * CHUNK:(ci + 1) * CHUNK] for rr in range(hpg)], axis=1)

    er = lax.broadcasted_iota(jnp.int32, (LANES, SSD_INNER), 0)
    ec = lax.broadcasted_iota(jnp.int32, (LANES, SSD_INNER), 1)
    expand = jnp.where(ec // HEAD_DIM == er - SMALL_DT, 1.0, 0.0).astype(BF16)
    cs_x = _dot_sel2(cs, expand)
    xdt = xs * _dot_sel2(dt, expand)
    cs_s[...] = cs_x
    ecs_s[...] = jnp.exp(cs_x)
    xdt_s[...] = xdt
    xdtw_s[...] = xdt * jnp.exp(_chunk_last(cs_x) - cs_x)

    r = lax.broadcasted_iota(jnp.int32, (gcols, gcols), 0)
    c = lax.broadcasted_iota(jnp.int32, (gcols, gcols), 1)
    same_head = (r // HEAD_DIM) == (c // HEAD_DIM)
    rr = lax.broadcasted_iota(jnp.int32, (CHUNK, gcols), 0)
    cc = lax.broadcasted_iota(jnp.int32, (CHUNK, gcols), 1)
    tril = (cc % CHUNK) <= rr
    gw = SSD_GROUPS * SSD_STATE

    states = [st_s[g] for g in range(SSD_GROUPS)]
    for ci in range(nc):
        rows = slice(ci * CHUNK, (ci + 1) * CHUNK)
        last = slice((ci + 1) * CHUNK - 1, (ci + 1) * CHUNK)
        for g in range(SSD_GROUPS):
            gsl = slice(g * gcols, (g + 1) * gcols)
            bm = bc_s[rows, g * SSD_STATE:(g + 1) * SSD_STATE].astype(BF16)
            cm = bc_s[rows, gw + g * SSD_STATE:gw + (g + 1) * SSD_STATE].astype(BF16)
            cb = _dot_nt(cm, _tile_rows(bm, hpg))
            decay = jnp.exp(jnp.where(tril, cs_s[rows, gsl] - cst_s[ci, g:g + 1, :], -jnp.inf))
            x_bd = jnp.where(same_head, _tile_rows(xdt_s[rows, gsl], hpg), 0.0).astype(BF16)
            y_inter = _dot(cm, states[g].astype(BF16)) * ecs_s[rows, gsl]
            o_s[rows, gsl] = _dot((cb * decay).astype(BF16), x_bd) + y_inter
            states[g] = (states[g] * ecs_s[last, gsl]
                         + _dot_tn(bm, xdtw_s[rows, gsl].astype(BF16)))
        yield
    for g in range(SSD_GROUPS):
        st_s[g] = states[g]

    y = (o_s[...] + d_row_ref[...] * xs_s[...]) * _silu(z)
    gwid = SSD_INNER // SSD_GROUPS
    normed = []
    for g in range(SSD_GROUPS):
        yg = y[:, g * gwid:(g + 1) * gwid]
        normed.append(yg * lax.rsqrt(jnp.mean(yg * yg, axis=-1, keepdims=True) + NORM_EPS))
    y_ref[...] = (jnp.concatenate(normed, axis=1) * gn_ref[...]).astype(BF16)


N_FOX_IN, N_GLA_IN, N_RET_IN, N_SSD_IN = 5, 4, 4, 7
N_LINEAR_SCRATCH = 7
N_SSD_SCRATCH = 10


def _mixers_kernel(x_ref, ln_ref, ws_ref, *refs):
    it = iter(refs)
    take = lambda n: [next(it) for _ in range(n)]
    fox_in, gla_in, ret_in, ssd_in = take(N_FOX_IN), take(N_GLA_IN), take(N_RET_IN), take(N_SSD_IN)
    fox_out, (yb_ref, yc_ref, yd_ref) = take(4), take(3)
    (carry_ref,), gla_s, ret_s, ssd_s = take(1), take(N_LINEAR_SCRATCH), take(N_LINEAR_SCRATCH), take(N_SSD_SCRATCH)

    @pl.when(pl.program_id(0) == 0)
    def _():
        carry_ref[...] = jnp.zeros_like(carry_ref)
        gla_s[-1][...] = jnp.zeros_like(gla_s[-1])
        ret_s[-1][...] = jnp.zeros_like(ret_s[-1])
        ssd_s[-1][...] = jnp.zeros_like(ssd_s[-1])
        ssd_s[0][0:SUBLANES, :] = jnp.zeros((SUBLANES, SSD_CONV_CH), F32)

    h = _rmsnorm(x_ref[...], ln_ref[...]).astype(BF16)
    small = _dot_nt(h, ws_ref[...])
    waiting = [_ssd_body(h, small, *ssd_in, yd_ref, *ssd_s), _gla_body(h, small, *gla_in, yb_ref, *gla_s),
               _ret_body(h, *ret_in, yc_ref, *ret_s), _fox_prep_body(h, small, *fox_in, *fox_out, carry_ref)]
    pending = []
    while waiting or pending:
        if waiting:
            pending.append(waiting.pop(0))
        for body in list(pending):
            if next(body, "done") == "done":
                pending.remove(body)


def _mixers(l, x, ln, w_small, fox_in, gla_in, ret_in, ssd_in, cos_t, sin_t):
    seq = x.shape[0]
    t = ROW_TILE
    row = lambda w: pl.BlockSpec((t, w), lambda i: (i, 0))
    heads = pl.BlockSpec((FOX_HEADS, t, LANES), lambda i: (0, i, 0))
    head_shape = jax.ShapeDtypeStruct((FOX_HEADS, seq, LANES), BF16)
    w_ret, ret_gn = ret_in
    wide = pltpu.VMEM((t, SSD_INNER), F32)
    return pl.pallas_call(
        _mixers_kernel,
        grid=(seq // t,),
        in_specs=([row(D_MODEL), _layer(ln, l), _layer(w_small, l)] + [_layer(a, l) for a in fox_in]
                  + [_layer(a, l) for a in gla_in]
                  + [_layer(w_ret, l), row(LANES), row(LANES), _layer(ret_gn, l)]
                  + [_layer(a, l) for a in ssd_in]),
        out_specs=[heads, heads, heads, pl.BlockSpec((1, SUBLANES, LANES), lambda i: (i, 0, 0)),
                   row(GLA_W), row(RET_W), row(SSD_INNER)],
        out_shape=[head_shape, head_shape, head_shape,
                   jax.ShapeDtypeStruct((seq // t, SUBLANES, LANES), F32),
                   jax.ShapeDtypeStruct((seq, GLA_W), BF16),
                   jax.ShapeDtypeStruct((seq, RET_W), BF16),
                   jax.ShapeDtypeStruct((seq, SSD_INNER), BF16)],
        scratch_shapes=([pltpu.VMEM((SUBLANES, LANES), F32)]
                        + _linear_scratch(t, GLA_W, GLA_HEADS) + _linear_scratch(t, RET_W, RET_HEADS)
                        + [pltpu.VMEM((t + 2 * SUBLANES, SSD_CONV_CH), F32),
                           wide,
                           pltpu.VMEM((t, 2 * SSD_GROUPS * SSD_STATE), F32),
                           wide, wide, wide, wide,
                           pltpu.VMEM((t // CHUNK, SSD_GROUPS, SSD_INNER // SSD_GROUPS), F32),
                           wide,
                           pltpu.VMEM((SSD_GROUPS, SSD_STATE, SSD_INNER // SSD_GROUPS), F32)]),
        compiler_params=pltpu.CompilerParams(dimension_semantics=("arbitrary",),
                                             vmem_limit_bytes=VMEM_LIMIT),
        name="mixers",
    )(x, ln, w_small, *fox_in, *gla_in, w_ret, cos_t, sin_t, ret_gn, *ssd_in)


def _merge_ffn_kernel(layer, x_ref, ln1_ref, wg_ref, ya_ref, yb_ref, yc_ref, yd_ref,
                      wa_ref, wb_ref, wc_ref, wd_ref, wo_ref, ln2_ref, w1_hbm, w2_hbm, out_ref,
                      w1_ref, w2_ref, w_sems):
    ffn_copies = [pltpu.make_async_copy(w1_hbm.at[layer], w1_ref, w_sems.at[0]),
                  pltpu.make_async_copy(w2_hbm.at[layer], w2_ref, w_sems.at[1])]
    first = pl.program_id(0) == 0

    @pl.when(first)
    def _():
        for cp in ffn_copies:
            cp.start()

    x = x_ref[...]
    h = _rmsnorm(x, ln1_ref[...]).astype(BF16)

    def gate(b):
        return _sigmoid(_dot_nt(h, wg_ref[b * D_MODEL:(b + 1) * D_MODEL, :]))

    up_a = _dot(ya_ref[0], wa_ref[0:LANES, :])
    for p in range(1, FOX_HEADS // FOX_HEADS_PER_TILE):
        up_a = up_a + _dot(ya_ref[p], wa_ref[p * LANES:(p + 1) * LANES, :])
    merged = gate(0) * up_a
    merged = merged + gate(1) * _dot(yb_ref[...], wb_ref[...])
    merged = merged + gate(2) * _dot(yc_ref[...], wc_ref[...])
    merged = merged + gate(3) * _dot(yd_ref[...], wd_ref[...])
    x = x + _dot(merged.astype(BF16), wo_ref[...])

    @pl.when(first)
    def _():
        for cp in ffn_copies:
            cp.wait()

    h = _rmsnorm(x, ln2_ref[...]).astype(BF16)
    act = _silu(_dot(h, w1_ref[:, 0:FFN_HIDDEN])) * _dot(h, w1_ref[:, FFN_HIDDEN:2 * FFN_HIDDEN])
    out_ref[...] = x + _dot(act.astype(BF16), w2_ref[...])


def _merge_ffn(l, x, ln1, w_gates, y_a, y_b, y_c, y_d, w_up_a, w_up_b, w_up_c, w_up_d, w_out, ln2, w1, w2):
    seq = x.shape[0]
    t = FFN_TILE
    row = lambda w: pl.BlockSpec((t, w), lambda i: (i, 0))
    whole = pl.BlockSpec(memory_space=pl.ANY)
    return pl.pallas_call(
        functools.partial(_merge_ffn_kernel, l),
        grid=(seq // t,),
        in_specs=[row(D_MODEL), _layer(ln1, l), _layer(w_gates, l),
                  pl.BlockSpec((FOX_HEADS // FOX_HEADS_PER_TILE, t, LANES), lambda i: (0, i, 0)),
                  row(GLA_W), row(RET_W), row(SSD_INNER),
                  _layer(w_up_a, l), _layer(w_up_b, l), _layer(w_up_c, l),
                  _layer(w_up_d, l), _layer(w_out, l), _layer(ln2, l), whole, whole],
        out_specs=row(D_MODEL),
        out_shape=jax.ShapeDtypeStruct((seq, D_MODEL), F32),
        scratch_shapes=[pltpu.VMEM(w1.shape[1:], BF16), pltpu.VMEM(w2.shape[1:], BF16),
                        pltpu.SemaphoreType.DMA((2,))],
        compiler_params=pltpu.CompilerParams(dimension_semantics=("arbitrary",),
                                             vmem_limit_bytes=VMEM_LIMIT),
        name="merge_swiglu",
    )(x, ln1, w_gates, y_a, y_b, y_c, y_d, w_up_a, w_up_b, w_up_c, w_up_d, w_out, ln2, w1, w2)


def _rows(a):
    return a.reshape(a.shape[0], 1, -1)


def _in_proj_weights(w_in):
    offs = np.concatenate([[0], np.cumsum(IN_SPLITS)])
    (fq, fk, fv, ff, gq, gk, gv, glr, gr, rq, rk, rv, rg, z, xbc, dt, gates) = [
        w_in[:, :, offs[n]:offs[n + 1]] for n in range(len(IN_SPLITS))]
    group = lambda parts: jnp.transpose(jnp.concatenate(parts, axis=2), (0, 2, 1)).astype(BF16)
    gap = lambda n: jnp.zeros(w_in.shape[:2] + (n,), w_in.dtype)
    w_small = group([ff, glr, gap(SMALL_DT - SMALL_GLR - GLA_LOWRANK), dt, gap(LANES - SMALL_DT - SSD_HEADS)])
    return w_small, group([fq, fk, fv]), group([gq, gk, gv, gr]), group([rq, rk, rv, rg]), group([z, xbc]), group([gates])


def kernel(x, positions, ln1, ln2, w_in, fox_bf, fox_qn, fox_kn, gla_w2, gla_b, gla_norm, ret_norm,
           ssd_conv_w, ssd_conv_b, ssd_dt_bias, ssd_a_log, ssd_d, ssd_norm,
           w_up_a, w_up_b, w_up_c, w_up_d, w_out, w_ffn_in, w_ffn_out):
    bsz, seq, d = x.shape
    assert bsz == 1 and d == D_MODEL and seq % ROW_TILE == 0 and seq % FOX_TQ == 0 and FOX_TQ % ROW_TILE == 0
    depth = ln1.shape[0]
    xr = x.reshape(seq, d)
    cos_t, sin_t = _rotary_tables(positions, seq)

    bf16 = lambda w: w.astype(BF16)
    ln1_r, ln2_r = _rows(ln1), _rows(ln2)
    fox_bf_r = _rows(jnp.pad(fox_bf, ((0, 0), (SMALL_FF, LANES - SMALL_FF - FOX_HEADS))))
    fox_qn_r, fox_kn_r = _rows(jnp.tile(fox_qn, (1, 2))), _rows(jnp.tile(fox_kn, (1, 2)))
    fox_shift_r, fox_par = _fox_params(fox_qn, fox_kn)
    gla_w2_p = bf16(jnp.pad(gla_w2, ((0, 0), (SMALL_GLR, LANES - SMALL_GLR - GLA_LOWRANK), (0, 0))))
    gla_b_r, gla_gn_r = _rows(gla_b), _rows(jnp.tile(gla_norm, (1, GLA_HEADS)))
    ret_gn_r = _rows(jnp.tile(ret_norm, (1, RET_HEADS)))
    conv_b_r = _rows(ssd_conv_b)
    at_dt = lambda a: _rows(jnp.pad(a, ((0, 0), (SMALL_DT, LANES - SMALL_DT - SSD_HEADS))))
    dtb_r, alog_r = at_dt(ssd_dt_bias), at_dt(ssd_a_log)
    d_r, ssd_gn_r = _rows(jnp.repeat(ssd_d, HEAD_DIM, axis=1)), _rows(ssd_norm)
    up_a, up_b, up_c, up_d, w_o = bf16(w_up_a), bf16(w_up_b), bf16(w_up_c), bf16(w_up_d), bf16(w_out)
    w1, w2 = bf16(w_ffn_in), bf16(w_ffn_out)
    w_small, w_fox, w_gla, w_ret, w_ssd, w_gates = _in_proj_weights(w_in)

    for l in range(depth):
        q, k, v, c, y_b, y_c, y_d = _mixers(
            l, xr, ln1_r, w_small,
            (w_fox, fox_bf_r, fox_qn_r, fox_kn_r, fox_shift_r),
            (w_gla, gla_w2_p, gla_b_r, gla_gn_r),
            (w_ret, ret_gn_r),
            (w_ssd, ssd_conv_w, conv_b_r, dtb_r, alog_r, d_r, ssd_gn_r),
            cos_t, sin_t)
        y_a = _fox_attention(fox_par[l], q, k, v, c)
        xr = _merge_ffn(l, xr, ln1_r, w_gates, y_a, y_b, y_c, y_d, up_a, up_b, up_c, up_d, w_o, ln2_r, w1, w2)
    return xr.reshape(bsz, seq, d)
```

```python
import math

import numpy as np
import jax
import jax.numpy as jnp
from jax import lax
from jax.experimental import pallas as pl
from jax.experimental.pallas import tpu as pltpu

D_MODEL = 1024
HEAD_DIM = 64
CHUNK = 64
FOX_HEADS = 4
GLA_HEADS = 4
GLA_LOWRANK = 16
GLA_TAU = 16.0
RET_HEADS = 4
ROPE_THETA = 10000.0
SSD_HEADS = 8
SSD_GROUPS = 2
SSD_STATE = 64
SSD_CONV = 4
SSD_INNER = SSD_HEADS * HEAD_DIM
SSD_CONV_CH = SSD_INNER + 2 * SSD_GROUPS * SSD_STATE
FOX_W = FOX_HEADS * HEAD_DIM
GLA_W = GLA_HEADS * HEAD_DIM
RET_W = RET_HEADS * HEAD_DIM
N_BRANCH = 4
FFN_HIDDEN = ((8 * D_MODEL + 3 * 256 - 1) // (3 * 256)) * 256
NORM_EPS = 1e-6
IN_SPLITS = (FOX_W, FOX_W, FOX_W, FOX_HEADS,
             GLA_W, GLA_W, GLA_W, GLA_LOWRANK, GLA_W,
             RET_W, RET_W, RET_W, RET_W,
             SSD_INNER, SSD_CONV_CH, SSD_HEADS,
             N_BRANCH * D_MODEL)

LANES = 128
SUBLANES = 8
VMEM_LIMIT = 56 * 1024 * 1024

ROW_TILE = 512
FOX_TQ = 512
FOX_TK = 512
FFN_TILE = 512
PROJ_PIECE = 512

F32 = jnp.float32
BF16 = jnp.bfloat16


def _rmsnorm(x, g):
    return x * lax.rsqrt(jnp.mean(x * x, axis=-1, keepdims=True) + NORM_EPS) * g


def _log_sigmoid(x):
    return jnp.minimum(x, 0.0) - jnp.log(1.0 + jnp.exp(-jnp.abs(x)))


def _softplus(x):
    return jnp.maximum(x, 0.0) + jnp.log(1.0 + jnp.exp(-jnp.abs(x)))


def _sigmoid(x):
    return 0.5 * jnp.tanh(0.5 * x) + 0.5


def _silu(x):
    return x * _sigmoid(x)


def _split3(x):
    hi = x.astype(BF16)
    r1 = x - hi.astype(F32)
    mid = r1.astype(BF16)
    lo = (r1 - mid.astype(F32)).astype(BF16)
    return hi, mid, lo


def _split2(x):
    hi = x.astype(BF16)
    return hi, (x - hi.astype(F32)).astype(BF16)


def _dot(a, b):
    return jnp.dot(a, b, preferred_element_type=F32)


def _dot_nt(a, b):
    return lax.dot_general(a, b, (((1,), (1,)), ((), ())), preferred_element_type=F32)


def _dot_tn(a, b):
    return lax.dot_general(a, b, (((0,), (0,)), ((), ())), preferred_element_type=F32)


def _sel_dot(mat, x):
    hi, mid, lo = _split3(x)
    return _dot(mat, hi) + _dot(mat, mid) + _dot(mat, lo)


def _dot_sel2(x, mat):
    hi, lo = _split2(x)
    return _dot(hi, mat) + _dot(lo, mat)


def _chunk_cumsum(x):
    t = x.shape[0]
    r = lax.broadcasted_iota(jnp.int32, (t, t), 0)
    c = lax.broadcasted_iota(jnp.int32, (t, t), 1)
    incl = jnp.where(((r // CHUNK) == (c // CHUNK)) & (c <= r), 1.0, 0.0).astype(BF16)
    hi, lo = _split2(x)
    b = _dot(incl, hi) + _dot(incl, lo)
    return b, _chunk_last(b) - b


def _chunk_last(b):
    t, w = b.shape
    return jnp.concatenate([jnp.broadcast_to(b[e - 1:e, :], (CHUNK, w)) for e in range(CHUNK, t + 1, CHUNK)],
                           axis=0)


def _group_mean_sq(o, group):
    w = o.shape[-1]
    r = lax.broadcasted_iota(jnp.int32, (w, w), 0)
    c = lax.broadcasted_iota(jnp.int32, (w, w), 1)
    bd = jnp.where((r // group) == (c // group), 1.0, 0.0).astype(BF16)
    return _dot_sel2(o * o, bd) * (1.0 / group)


def _project(h, wt_ref):
    n = wt_ref.shape[0]
    parts = []
    for a in range(0, n, PROJ_PIECE):
        parts.append(_dot_nt(h, wt_ref[a:min(a + PROJ_PIECE, n), :]))
        yield
    return jnp.concatenate(parts, axis=1)


def _layer(stacked, l):
    tail = tuple(stacked.shape[1:])
    return pl.BlockSpec((None,) + tail, lambda *_: (l,) + (0,) * len(tail), pipeline_mode=pl.Buffered(1))


def _rotary_kernel(pos_ref, inv_ref, sign_ref, cos_ref, sin_ref):
    ang = pos_ref[...] * inv_ref[...]
    cos_ref[...] = jnp.cos(ang)
    sin_ref[...] = jnp.sin(ang) * sign_ref[...]


def _rotary_tables(positions, seq):
    half = HEAD_DIM // 2
    inv = ROPE_THETA ** (-jnp.arange(half, dtype=F32) / half)
    reps = LANES // HEAD_DIM
    inv_row = jnp.tile(inv, 2 * reps).reshape(1, LANES)
    sign_row = jnp.tile(jnp.concatenate([-jnp.ones((half,), F32), jnp.ones((half,), F32)]),
                        reps).reshape(1, LANES)
    pos = positions.astype(F32).reshape(seq, 1)
    t = ROW_TILE
    return pl.pallas_call(
        _rotary_kernel,
        grid=(seq // t,),
        in_specs=[pl.BlockSpec((t, 1), lambda i: (i, 0)),
                  pl.BlockSpec((1, LANES), lambda i: (0, 0)),
                  pl.BlockSpec((1, LANES), lambda i: (0, 0))],
        out_specs=[pl.BlockSpec((t, LANES), lambda i: (i, 0)),
                   pl.BlockSpec((t, LANES), lambda i: (i, 0))],
        out_shape=[jax.ShapeDtypeStruct((seq, LANES), F32)] * 2,
        name="rotary_tables",
    )(pos, inv_row, sign_row)


SMALL_FF = 0
SMALL_GLR = FOX_HEADS
SMALL_DT = 24
LOG2E = math.log2(math.e)
F32_EXP_ZERO = 88.0
FOX_FAST_MAX_LOGIT = 40.0
FOX_BOUND_MARGIN = 1.02


def _fox_prep_body(h, small, w_ref, bf_ref, qn_ref, kn_ref, shift_ref, q_out, k_out, v_out, c_out, carry_ref):
    t = h.shape[0]
    u = yield from _project(h, w_ref)
    ls = _log_sigmoid(small + bf_ref[...])
    r = lax.broadcasted_iota(jnp.int32, (t, t), 0)
    cc = lax.broadcasted_iota(jnp.int32, (t, t), 1)
    tri = jnp.where(cc <= r, 1.0, 0.0).astype(BF16)
    c = _sel_dot(tri, ls) + carry_ref[0:1, :]
    carry_ref[0:1, :] = c[t - 1:t, :]
    c_out[0] = jnp.concatenate([c[0:1, :], c[t - 1:t, :], jnp.zeros((SUBLANES - 2, LANES), F32)], axis=0)
    c_hi, c_mid, c_lo = _split3(c * LOG2E)
    c_hi, c_mid, c_lo = c_hi.astype(F32), c_mid.astype(F32), c_lo.astype(F32)

    lane = lax.broadcasted_iota(jnp.int32, (t, LANES), 1)
    low = lane < HEAD_DIM
    j = lane - HEAD_DIM
    shift = shift_ref[...]

    def head_norm(pair, gain, mult):
        sq = pair * pair
        s_lo = jnp.sum(jnp.where(low, sq, 0.0), axis=-1, keepdims=True)
        s_hi = jnp.sum(jnp.where(low, 0.0, sq), axis=-1, keepdims=True)
        ms = jnp.where(low, s_lo, s_hi) * (1.0 / HEAD_DIM)
        return pair * lax.rsqrt(ms + NORM_EPS) * (gain * mult)

    for p in range(FOX_HEADS // 2):
        qp = head_norm(u[:, p * LANES:(p + 1) * LANES], qn_ref[...], HEAD_DIM ** -0.5 * LOG2E)
        kp = head_norm(u[:, FOX_W + p * LANES:FOX_W + (p + 1) * LANES], kn_ref[...], 1.0)
        vp = u[:, 2 * FOX_W + p * LANES:2 * FOX_W + (p + 1) * LANES]
        for sub in range(2):
            hd = 2 * p + sub
            if sub == 1:
                qp, kp, vp = (pltpu.roll(a, HEAD_DIM, axis=1) for a in (qp, kp, vp))
            ch, cm, cl = (jnp.broadcast_to(a[:, SMALL_FF + hd:SMALL_FF + hd + 1], (t, LANES)) for a in (c_hi, c_mid, c_lo))
            aug_q = jnp.where(j == 0, ch, jnp.where(j == 1, cm, jnp.where(j == 2, cl,
                              jnp.where(j < 7, 1.0, 0.0))))
            aug_k = jnp.where(j < 3, 1.0, jnp.where(j == 3, -ch, jnp.where(j == 4, -cm,
                              jnp.where(j == 5, -cl, jnp.where(j == 6, -shift, 0.0)))))
            aug_v = jnp.where(j == 0, 1.0, 0.0)
            q_out[hd] = jnp.where(low, qp, aug_q).astype(BF16)
            k_out[hd] = jnp.where(low, kp, aug_k).astype(BF16)
            v_out[hd] = jnp.where(low, vp, aug_v).astype(BF16)


def _fox_attn_kernel(par_ref, cq0_ref, cend_ref, q_ref, k_hbm, v_hbm, o_ref, k_ref, v_ref, kv_sems):
    i = pl.program_id(0)
    nh, tq, _ = q_ref.shape
    thr = par_ref[0]

    def kv_copies(j):
        rows = pl.ds(pl.multiple_of(j * FOX_TK, FOX_TK), FOX_TK)
        return [pltpu.make_async_copy(src.at[:, rows, :], dst.at[:, rows, :], kv_sems.at[w, j % 2])
                for w, (src, dst) in enumerate(((k_hbm, k_ref), (v_hbm, v_ref)))]

    @pl.when(i == 0)
    def _():
        for cp in kv_copies(0):
            cp.start()

    for cp in kv_copies(i):
        cp.wait()

    @pl.when(i + 1 < pl.num_programs(0))
    def _():
        for cp in kv_copies(i + 1):
            cp.start()

    def first_block(hd):
        c0 = cq0_ref[i, hd]
        return lax.while_loop(
            lambda j: jnp.logical_and(j > 0, c0 - cend_ref[jnp.maximum(j - 1, 0), hd] >= -thr),
            lambda j: j - 1, i)

    def keys(hd, j, nblk):
        off = pl.multiple_of(j * FOX_TK, FOX_TK)
        return k_ref[hd, pl.ds(off, nblk * FOX_TK), :], v_ref[hd, pl.ds(off, nblk * FOX_TK), :]

    def causal(s):
        r = lax.broadcasted_iota(jnp.int32, s.shape, 0)
        c = lax.broadcasted_iota(jnp.int32, s.shape, 1)
        return jnp.where(c <= r, s, -jnp.inf)

    def fixed_shift():
        def block(hd, j, nblk):
            k, v = keys(hd, j, nblk)
            return _dot(jnp.exp2(_dot_nt(q_ref[hd], k)).astype(BF16), v)

        def diagonal(hd):
            k, v = keys(hd, i, 1)
            return _dot(jnp.exp2(causal(_dot_nt(q_ref[hd], k))).astype(BF16), v)

        accs = [diagonal(hd) for hd in range(nh)]
        for hd in range(nh):
            acc, j_lo = accs[hd], first_block(hd)
            n = i - j_lo
            acc = lax.cond(n % 2 == 1, lambda: acc + block(hd, j_lo, 1), lambda: acc)
            acc = lax.cond((n // 2) % 2 == 1, lambda: acc + block(hd, j_lo + n % 2, 2), lambda: acc)
            first = j_lo + n % 4
            accs[hd] = lax.fori_loop(0, n // 4, lambda p, acc: acc + block(hd, first + 4 * p, 4), acc)
        return accs

    def running_max():
        def one_head(hd):
            q = q_ref[hd]

            def step(j, carry, masked):
                m, acc = carry
                k, v = keys(hd, j, 1)
                s = _dot_nt(q, k)
                if masked:
                    s = causal(s)
                m_new = jnp.maximum(m, jnp.max(s, axis=-1, keepdims=True))
                p = jnp.exp2(s - m_new)
                return m_new, jnp.exp2(m - m_new) * acc + _dot(p.astype(BF16), v)
            init = (jnp.full((tq, 1), -jnp.inf, F32), jnp.zeros((tq, LANES), F32))
            carry = step(i, init, True)
            return lax.fori_loop(first_block(hd), i, lambda j, c: step(j, c, False), carry)[1]
        return [one_head(hd) for hd in range(nh)]

    accs = lax.cond(par_ref[1] > 0.0, fixed_shift, running_max)
    outs = [acc / acc[:, HEAD_DIM:HEAD_DIM + 1] for acc in accs]
    low = lax.broadcasted_iota(jnp.int32, (tq, LANES), 1) < HEAD_DIM
    for p in range(nh // FOX_HEADS_PER_TILE):
        o_ref[p] = jnp.where(low, outs[2 * p], pltpu.roll(outs[2 * p + 1], HEAD_DIM, axis=1)).astype(BF16)


FOX_HEADS_PER_TILE = LANES // HEAD_DIM


def _fox_attn(par, cq0, cend, q, k, v):
    assert FOX_TQ == FOX_TK and FOX_HEADS_PER_TILE == 2
    nh, seq, _ = q.shape
    kv_spec = pl.BlockSpec(memory_space=pl.ANY)
    kv_vmem = pltpu.VMEM((nh, seq, LANES), BF16)
    return pl.pallas_call(
        _fox_attn_kernel,
        grid_spec=pltpu.PrefetchScalarGridSpec(
            num_scalar_prefetch=3,
            grid=(seq // FOX_TQ,),
            in_specs=[pl.BlockSpec((nh, FOX_TQ, LANES), lambda i, *_: (0, i, 0)), kv_spec, kv_spec],
            out_specs=pl.BlockSpec((nh // FOX_HEADS_PER_TILE, FOX_TQ, LANES), lambda i, *_: (0, i, 0)),
            scratch_shapes=[kv_vmem, kv_vmem, pltpu.SemaphoreType.DMA((2, 2))]),
        out_shape=jax.ShapeDtypeStruct((nh // FOX_HEADS_PER_TILE, seq, LANES), BF16),
        compiler_params=pltpu.CompilerParams(dimension_semantics=("arbitrary",),
                                             vmem_limit_bytes=VMEM_LIMIT),
        name="fox_attn",
    )(par, cq0, cend, q, k, v)


def _fox_params(qn, kn):
    bound = FOX_BOUND_MARGIN * HEAD_DIM ** 0.5 * jnp.max(jnp.abs(qn), axis=1) * jnp.max(jnp.abs(kn), axis=1)
    shift = jnp.broadcast_to((LOG2E * bound)[:, None, None], (bound.shape[0], 1, LANES))
    par = jnp.stack([F32_EXP_ZERO + 2.0 * bound, (bound < FOX_FAST_MAX_LOGIT).astype(F32)], axis=1)
    return shift, par


def _fox_attention(par, q, k, v, c):
    per_tile = FOX_TQ // ROW_TILE
    heads = slice(SMALL_FF, SMALL_FF + FOX_HEADS)
    cq0 = c[0::per_tile, 0, heads]
    cend = c[per_tile - 1::per_tile, 1, heads]
    return _fox_attn(par, cq0, cend, q, k, v)


def _tile_rows(a, n):
    return jnp.concatenate([a] * n, axis=0)


def _linear_attn_chunks(qd_s, ki_s, ke_s, v_s, dec_s, o_s, st_s, nheads):
    t, w = qd_s.shape
    assert w == nheads * HEAD_DIM
    r = lax.broadcasted_iota(jnp.int32, (w, w), 0)
    c = lax.broadcasted_iota(jnp.int32, (w, w), 1)
    same_head = (r // HEAD_DIM) == (c // HEAD_DIM)
    rr = lax.broadcasted_iota(jnp.int32, (CHUNK, w), 0)
    cc = lax.broadcasted_iota(jnp.int32, (CHUNK, w), 1)
    tril = (cc % CHUNK) <= rr
    state = st_s[...]
    for ci in range(t // CHUNK):
        rows = slice(ci * CHUNK, (ci + 1) * CHUNK)
        last = slice((ci + 1) * CHUNK - 1, (ci + 1) * CHUNK)
        qd = qd_s[rows, :].astype(BF16)
        ke = ke_s[rows, :].astype(BF16)
        vv = v_s[rows, :].astype(BF16)
        k_bd = jnp.where(same_head, _tile_rows(ki_s[rows, :], nheads), 0.0).astype(BF16)
        v_bd = jnp.where(same_head, _tile_rows(v_s[rows, :], nheads), 0.0).astype(BF16)
        attn = jnp.where(tril, _dot_nt(qd, k_bd), 0.0).astype(BF16)
        o_s[rows, :] = _dot(attn, v_bd) + _dot_nt(qd, state.astype(BF16))
        state = state * dec_s[last, :] + jnp.where(same_head, _dot_tn(vv, ke), 0.0)
        yield
    st_s[...] = state


def _gla_body(h, small, w_ref, w2_ref, b2_ref, gn_ref, y_ref, qd_s, ki_s, ke_s, v_s, dec_s, o_s, st_s):
    u = yield from _project(h, w_ref)
    z = _dot(small.astype(BF16), w2_ref[...]) + b2_ref[...]
    log_a = _log_sigmoid(z) * (1.0 / GLA_TAU)
    b, rem = _chunk_cumsum(log_a)
    eb = jnp.exp(b)
    k = u[:, GLA_W:2 * GLA_W]
    qd_s[...] = u[:, 0:GLA_W] * (HEAD_DIM ** -0.5) * eb
    ki_s[...] = k * jnp.exp(-b)
    ke_s[...] = k * jnp.exp(rem)
    v_s[...] = u[:, 2 * GLA_W:3 * GLA_W]
    dec_s[...] = eb
    yield
    yield from _linear_attn_chunks(qd_s, ki_s, ke_s, v_s, dec_s, o_s, st_s, GLA_HEADS)
    o = o_s[...]
    y = o * lax.rsqrt(_group_mean_sq(o, HEAD_DIM) + NORM_EPS) * gn_ref[...]
    y_ref[...] = (y * _silu(u[:, 3 * GLA_W:4 * GLA_W])).astype(BF16)


def _ret_body(h, w_ref, cos_ref, sin_ref, gn_ref, y_ref, qd_s, ki_s, ke_s, v_s, dec_s, o_s, st_s):
    t = h.shape[0]
    u = yield from _project(h, w_ref)
    cos, sin = cos_ref[...], sin_ref[...]
    lane = lax.broadcasted_iota(jnp.int32, (t, RET_W), 1)
    half = HEAD_DIM // 2
    first_half = (lax.broadcasted_iota(jnp.int32, (t, LANES), 1) % HEAD_DIM) < half

    def rotate(a):
        blocks = []
        for b in range(0, RET_W, LANES):
            ab = a[:, b:b + LANES]
            swapped = jnp.where(first_half, pltpu.roll(ab, LANES - half, axis=1),
                                pltpu.roll(ab, half, axis=1))
            blocks.append(ab * cos + swapped * sin)
        return jnp.concatenate(blocks, axis=1)

    q = rotate(u[:, 0:RET_W])
    k = rotate(u[:, RET_W:2 * RET_W]) * (HEAD_DIM ** -0.5)
    row = lax.broadcasted_iota(jnp.int32, (t, RET_W), 0)
    lg = jnp.zeros((t, RET_W), F32)
    for hd in range(RET_HEADS):
        lg = jnp.where(lane // HEAD_DIM == hd, math.log(1.0 - 2.0 ** (-5.0 - hd)), lg)
    pos = (row % CHUNK).astype(F32)
    qw = jnp.exp((pos + 1.0) * lg)
    qd_s[...] = q * qw
    ki_s[...] = k * jnp.exp(-(pos + 1.0) * lg)
    ke_s[...] = k * jnp.exp((CHUNK - 1.0 - pos) * lg)
    v_s[...] = u[:, 2 * RET_W:3 * RET_W]
    dec_s[...] = qw
    yield
    yield from _linear_attn_chunks(qd_s, ki_s, ke_s, v_s, dec_s, o_s, st_s, RET_HEADS)
    o = o_s[...]
    y = o * lax.rsqrt(_group_mean_sq(o, HEAD_DIM) + NORM_EPS) * gn_ref[...]
    y_ref[...] = (y * _silu(u[:, 3 * RET_W:4 * RET_W])).astype(BF16)


def _linear_scratch(t, width, nheads):
    return [pltpu.VMEM((t, width), F32)] * 6 + [pltpu.VMEM((width, width), F32)]


SSD_PROJ = SSD_INNER + SSD_CONV_CH


def _ssd_body(h, small, w_ref, cw_ref, cb_ref, dtb_row_ref, alog_row_ref, d_row_ref, gn_ref, y_ref,
              xp_s, xs_s, bc_s, cs_s, ecs_s, xdt_s, xdtw_s, cst_s, o_s, st_s):
    t = h.shape[0]
    nc = t // CHUNK
    u = yield from _project(h, w_ref)
    z = u[:, 0:SSD_INNER]

    xp_s[SUBLANES:SUBLANES + t, :] = u[:, SSD_INNER:SSD_PROJ]
    conv = cb_ref[...] + cw_ref[SSD_CONV - 1:SSD_CONV, :] * xp_s[SUBLANES:SUBLANES + t, :]
    for kk in range(SSD_CONV - 1):
        off = SUBLANES - (SSD_CONV - 1) + kk
        conv = conv + cw_ref[kk:kk + 1, :] * xp_s[off:off + t, :]
    xp_s[0:SUBLANES, :] = xp_s[t:t + SUBLANES, :]
    xbc = _silu(conv)
    xs = xbc[:, 0:SSD_INNER]
    xs_s[...] = xs
    bc_s[...] = xbc[:, SSD_INNER:SSD_CONV_CH]

    dt = _softplus(small + dtb_row_ref[...])
    dta = dt * -jnp.exp(alog_row_ref[...])
    cs, _ = _chunk_cumsum(dta)
    cs_t = cs.T[SMALL_DT:SMALL_DT + SSD_HEADS, :]
    hpg = SSD_HEADS // SSD_GROUPS
    gcols = hpg * HEAD_DIM
    for ci in range(nc):
        for g in range(SSD_GROUPS):
            cst_s[ci, g:g + 1, :] = jnp.concatenate(
                [cs_t[g * hpg + rr:g * hpg + rr + 1, ci * CHUNK:(ci + 1) * CHUNK] for rr in range(hpg)], axis=1)

    er = lax.broadcasted_iota(jnp.int32, (LANES, SSD_INNER), 0)
    ec = lax.broadcasted_iota(jnp.int32, (LANES, SSD_INNER), 1)
    expand = jnp.where(ec // HEAD_DIM == er - SMALL_DT, 1.0, 0.0).astype(BF16)
    cs_x = _dot_sel2(cs, expand)
    xdt = xs * _dot_sel2(dt, expand)
    cs_s[...] = cs_x
    ecs_s[...] = jnp.exp(cs_x)
    xdt_s[...] = xdt
    xdtw_s[...] = xdt * jnp.exp(_chunk_last(cs_x) - cs_x)

    r = lax.broadcasted_iota(jnp.int32, (gcols, gcols), 0)
    c = lax.broadcasted_iota(jnp.int32, (gcols, gcols), 1)
    same_head = (r // HEAD_DIM) == (c // HEAD_DIM)
    rr = lax.broadcasted_iota(jnp.int32, (CHUNK, gcols), 0)
    cc = lax.broadcasted_iota(jnp.int32, (CHUNK, gcols), 1)
    tril = (cc % CHUNK) <= rr
    gw = SSD_GROUPS * SSD_STATE

    states = [st_s[g] for g in range(SSD_GROUPS)]
    for ci in range(nc):
        rows = slice(ci * CHUNK, (ci + 1) * CHUNK)
        last = slice((ci + 1) * CHUNK - 1, (ci + 1) * CHUNK)
        for g in range(SSD_GROUPS):
            gsl = slice(g * gcols, (g + 1) * gcols)
            bm = bc_s[rows, g * SSD_STATE:(g + 1) * SSD_STATE].astype(BF16)
            cm = bc_s[rows, gw + g * SSD_STATE:gw + (g + 1) * SSD_STATE].astype(BF16)
            cb = _dot_nt(cm, _tile_rows(bm, hpg))
            decay = jnp.exp(jnp.where(tril, cs_s[rows, gsl] - cst_s[ci, g:g + 1, :], -jnp.inf))
            x_bd = jnp.where(same_head, _tile_rows(xdt_s[rows, gsl], hpg), 0.0).astype(BF16)
            y_inter = _dot(cm, states[g].astype(BF16)) * ecs_s[rows, gsl]
            o_s[rows, gsl] = _dot((cb * decay).astype(BF16), x_bd) + y_inter
            states[g] = (states[g] * ecs_s[last, gsl]
                         + _dot_tn(bm, xdtw_s[rows, gsl].astype(BF16)))
        yield
    for g in range(SSD_GROUPS):
        st_s[g] = states[g]

    y = (o_s[...] + d_row_ref[...] * xs_s[...]) * _silu(z)
    gwid = SSD_INNER // SSD_GROUPS
    normed = []
    for g in range(SSD_GROUPS):
        yg = y[:, g * gwid:(g + 1) * gwid]
        normed.append(yg * lax.rsqrt(jnp.mean(yg * yg, axis=-1, keepdims=True) + NORM_EPS))
    y_ref[...] = (jnp.concatenate(normed, axis=1) * gn_ref[...]).astype(BF16)


N_FOX_IN, N_GLA_IN, N_RET_IN, N_SSD_IN = 5, 4, 4, 7
N_LINEAR_SCRATCH = 7
N_SSD_SCRATCH = 10


def _mixers_kernel(x_ref, ln_ref, ws_ref, *refs):
    it = iter(refs)
    take = lambda n: [next(it) for _ in range(n)]
    fox_in, gla_in, ret_in, ssd_in = take(N_FOX_IN), take(N_GLA_IN), take(N_RET_IN), take(N_SSD_IN)
    fox_out, (yb_ref, yc_ref, yd_ref) = take(4), take(3)
    (carry_ref,), gla_s, ret_s, ssd_s = take(1), take(N_LINEAR_SCRATCH), take(N_LINEAR_SCRATCH), take(N_SSD_SCRATCH)

    @pl.when(pl.program_id(0) == 0)
    def _():
        carry_ref[...] = jnp.zeros_like(carry_ref)
        gla_s[-1][...] = jnp.zeros_like(gla_s[-1])
        ret_s[-1][...] = jnp.zeros_like(ret_s[-1])
        ssd_s[-1][...] = jnp.zeros_like(ssd_s[-1])
        ssd_s[0][0:SUBLANES, :] = jnp.zeros((SUBLANES, SSD_CONV_CH), F32)

    h = _rmsnorm(x_ref[...], ln_ref[...]).astype(BF16)
    small = _dot_nt(h, ws_ref[...])
    waiting = [_ssd_body(h, small, *ssd_in, yd_ref, *ssd_s), _gla_body(h, small, *gla_in, yb_ref, *gla_s),
               _ret_body(h, *ret_in, yc_ref, *ret_s), _fox_prep_body(h, small, *fox_in, *fox_out, carry_ref)]
    pending = []
    while waiting or pending:
        if waiting:
            pending.append(waiting.pop(0))
        for body in list(pending):
            if next(body, "done") == "done":
                pending.remove(body)


def _mixers(l, x, ln, w_small, fox_in, gla_in, ret_in, ssd_in, cos_t, sin_t):
    seq = x.shape[0]
    t = ROW_TILE
    row = lambda w: pl.BlockSpec((t, w), lambda i: (i, 0))
    heads = pl.BlockSpec((FOX_HEADS, t, LANES), lambda i: (0, i, 0))
    head_shape = jax.ShapeDtypeStruct((FOX_HEADS, seq, LANES), BF16)
    w_ret, ret_gn = ret_in
    wide = pltpu.VMEM((t, SSD_INNER), F32)
    return pl.pallas_call(
        _mixers_kernel,
        grid=(seq // t,),
        in_specs=([row(D_MODEL), _layer(ln, l), _layer(w_small, l)] + [_layer(a, l) for a in fox_in]
                  + [_layer(a, l) for a in gla_in]
                  + [_layer(w_ret, l), row(LANES), row(LANES), _layer(ret_gn, l)]
                  + [_layer(a, l) for a in ssd_in]),
        out_specs=[heads, heads, heads, pl.BlockSpec((1, SUBLANES, LANES), lambda i: (i, 0, 0)),
                   row(GLA_W), row(RET_W), row(SSD_INNER)],
        out_shape=[head_shape, head_shape, head_shape,
                   jax.ShapeDtypeStruct((seq // t, SUBLANES, LANES), F32),
                   jax.ShapeDtypeStruct((seq, GLA_W), BF16),
                   jax.ShapeDtypeStruct((seq, RET_W), BF16),
                   jax.ShapeDtypeStruct((seq, SSD_INNER), BF16)],
        scratch_shapes=([pltpu.VMEM((SUBLANES, LANES), F32)]
                        + _linear_scratch(t, GLA_W, GLA_HEADS) + _linear_scratch(t, RET_W, RET_HEADS)
                        + [pltpu.VMEM((t + 2 * SUBLANES, SSD_CONV_CH), F32),
                           wide,
                           pltpu.VMEM((t, 2 * SSD_GROUPS * SSD_STATE), F32),
                           wide, wide, wide, wide,
                           pltpu.VMEM((t // CHUNK, SSD_GROUPS, SSD_INNER // SSD_GROUPS), F32),
                           wide,
                           pltpu.VMEM((SSD_GROUPS, SSD_STATE, SSD_INNER // SSD_GROUPS), F32)]),
        compiler_params=pltpu.CompilerParams(dimension_semantics=("arbitrary",),
                                             vmem_limit_bytes=VMEM_LIMIT),
        name="mixers",
    )(x, ln, w_small, *fox_in, *gla_in, w_ret, cos_t, sin_t, ret_gn, *ssd_in)


def _merge_ffn_kernel(x_ref, ln1_ref, wg_ref, ya_ref, yb_ref, yc_ref, yd_ref,
                      wa_ref, wb_ref, wc_ref, wd_ref, wo_ref, ln2_ref, w1_ref, w2_ref, out_ref):
    x = x_ref[...]
    h = _rmsnorm(x, ln1_ref[...]).astype(BF16)

    def gate(b):
        return _sigmoid(_dot_nt(h, wg_ref[b * D_MODEL:(b + 1) * D_MODEL, :]))

    up_a = _dot(ya_ref[0], wa_ref[0:LANES, :])
    for p in range(1, FOX_HEADS // FOX_HEADS_PER_TILE):
        up_a = up_a + _dot(ya_ref[p], wa_ref[p * LANES:(p + 1) * LANES, :])
    merged = gate(0) * up_a
    merged = merged + gate(1) * _dot(yb_ref[...], wb_ref[...])
    merged = merged + gate(2) * _dot(yc_ref[...], wc_ref[...])
    merged = merged + gate(3) * _dot(yd_ref[...], wd_ref[...])
    x = x + _dot(merged.astype(BF16), wo_ref[...])

    h = _rmsnorm(x, ln2_ref[...]).astype(BF16)
    act = _silu(_dot(h, w1_ref[:, 0:FFN_HIDDEN])) * _dot(h, w1_ref[:, FFN_HIDDEN:2 * FFN_HIDDEN])
    out_ref[...] = x + _dot(act.astype(BF16), w2_ref[...])


def _merge_ffn(l, x, ln1, w_gates, y_a, y_b, y_c, y_d, w_up_a, w_up_b, w_up_c, w_up_d, w_out, ln2, w1, w2):
    seq = x.shape[0]
    t = FFN_TILE
    row = lambda w: pl.BlockSpec((t, w), lambda i: (i, 0))
    return pl.pallas_call(
        _merge_ffn_kernel,
        grid=(seq // t,),
        in_specs=[row(D_MODEL), _layer(ln1, l), _layer(w_gates, l),
                  pl.BlockSpec((FOX_HEADS // FOX_HEADS_PER_TILE, t, LANES), lambda i: (0, i, 0)),
                  row(GLA_W), row(RET_W), row(SSD_INNER),
                  _layer(w_up_a, l), _layer(w_up_b, l), _layer(w_up_c, l),
                  _layer(w_up_d, l), _layer(w_out, l), _layer(ln2, l), _layer(w1, l), _layer(w2, l)],
        out_specs=row(D_MODEL),
        out_shape=jax.ShapeDtypeStruct((seq, D_MODEL), F32),
        compiler_params=pltpu.CompilerParams(dimension_semantics=("arbitrary",),
                                             vmem_limit_bytes=VMEM_LIMIT),
        name="merge_swiglu",
    )(x, ln1, w_gates, y_a, y_b, y_c, y_d, w_up_a, w_up_b, w_up_c, w_up_d, w_out, ln2, w1, w2)


def _rows(a):
    return a.reshape(a.shape[0], 1, -1)


def _in_proj_weights(w_in):
    offs = np.concatenate([[0], np.cumsum(IN_SPLITS)])
    (fq, fk, fv, ff, gq, gk, gv, glr, gr, rq, rk, rv, rg, z, xbc, dt, gates) = [
        w_in[:, :, offs[n]:offs[n + 1]] for n in range(len(IN_SPLITS))]
    group = lambda parts: jnp.transpose(jnp.concatenate(parts, axis=2), (0, 2, 1)).astype(BF16)
    gap = lambda n: jnp.zeros(w_in.shape[:2] + (n,), w_in.dtype)
    w_small = group([ff, glr, gap(SMALL_DT - SMALL_GLR - GLA_LOWRANK), dt, gap(LANES - SMALL_DT - SSD_HEADS)])
    return w_small, group([fq, fk, fv]), group([gq, gk, gv, gr]), group([rq, rk, rv, rg]), group([z, xbc]), group([gates])


def kernel(x, positions, ln1, ln2, w_in, fox_bf, fox_qn, fox_kn, gla_w2, gla_b, gla_norm, ret_norm,
           ssd_conv_w, ssd_conv_b, ssd_dt_bias, ssd_a_log, ssd_d, ssd_norm,
           w_up_a, w_up_b, w_up_c, w_up_d, w_out, w_ffn_in, w_ffn_out):
    bsz, seq, d = x.shape
    assert bsz == 1 and d == D_MODEL and seq % ROW_TILE == 0 and seq % FOX_TQ == 0 and FOX_TQ % ROW_TILE == 0
    depth = ln1.shape[0]
    xr = x.reshape(seq, d)
    cos_t, sin_t = _rotary_tables(positions, seq)

    bf16 = lambda w: w.astype(BF16)
    ln1_r, ln2_r = _rows(ln1), _rows(ln2)
    fox_bf_r = _rows(jnp.pad(fox_bf, ((0, 0), (SMALL_FF, LANES - SMALL_FF - FOX_HEADS))))
    fox_qn_r, fox_kn_r = _rows(jnp.tile(fox_qn, (1, 2))), _rows(jnp.tile(fox_kn, (1, 2)))
    fox_shift_r, fox_par = _fox_params(fox_qn, fox_kn)
    gla_w2_p = bf16(jnp.pad(gla_w2, ((0, 0), (SMALL_GLR, LANES - SMALL_GLR - GLA_LOWRANK), (0, 0))))
    gla_b_r, gla_gn_r = _rows(gla_b), _rows(jnp.tile(gla_norm, (1, GLA_HEADS)))
    ret_gn_r = _rows(jnp.tile(ret_norm, (1, RET_HEADS)))
    conv_b_r = _rows(ssd_conv_b)
    at_dt = lambda a: _rows(jnp.pad(a, ((0, 0), (SMALL_DT, LANES - SMALL_DT - SSD_HEADS))))
    dtb_r, alog_r = at_dt(ssd_dt_bias), at_dt(ssd_a_log)
    d_r, ssd_gn_r = _rows(jnp.repeat(ssd_d, HEAD_DIM, axis=1)), _rows(ssd_norm)
    up_a, up_b, up_c, up_d, w_o = bf16(w_up_a), bf16(w_up_b), bf16(w_up_c), bf16(w_up_d), bf16(w_out)
    w1, w2 = bf16(w_ffn_in), bf16(w_ffn_out)
    w_small, w_fox, w_gla, w_ret, w_ssd, w_gates = _in_proj_weights(w_in)

    for l in range(depth):
        q, k, v, c, y_b, y_c, y_d = _mixers(
            l, xr, ln1_r, w_small,
            (w_fox, fox_bf_r, fox_qn_r, fox_kn_r, fox_shift_r),
            (w_gla, gla_w2_p, gla_b_r, gla_gn_r),
            (w_ret, ret_gn_r),
            (w_ssd, ssd_conv_w, conv_b_r, dtb_r, alog_r, d_r, ssd_gn_r),
            cos_t, sin_t)
        y_a = _fox_attention(fox_par[l], q, k, v, c)
        xr = _merge_ffn(l, xr, ln1_r, w_gates, y_a, y_b, y_c, y_d, up_a, up_b, up_c, up_d, w_o, ln2_r, w1, w2)
    return xr.reshape(bsz, seq, d)
```

```python
import math

import numpy as np
import jax
import jax.numpy as jnp
from jax import lax
from jax.experimental import pallas as pl
from jax.experimental.pallas import tpu as pltpu

D_MODEL = 1024
HEAD_DIM = 64
CHUNK = 64
FOX_HEADS = 4
GLA_HEADS = 4
GLA_LOWRANK = 16
GLA_TAU = 16.0
RET_HEADS = 4
ROPE_THETA = 10000.0
SSD_HEADS = 8
SSD_GROUPS = 2
SSD_STATE = 64
SSD_CONV = 4
SSD_INNER = SSD_HEADS * HEAD_DIM
SSD_CONV_CH = SSD_INNER + 2 * SSD_GROUPS * SSD_STATE
FOX_W = FOX_HEADS * HEAD_DIM
GLA_W = GLA_HEADS * HEAD_DIM
RET_W = RET_HEADS * HEAD_DIM
N_BRANCH = 4
FFN_HIDDEN = ((8 * D_MODEL + 3 * 256 - 1) // (3 * 256)) * 256
NORM_EPS = 1e-6
IN_SPLITS = (FOX_W, FOX_W, FOX_W, FOX_HEADS,
             GLA_W, GLA_W, GLA_W, GLA_LOWRANK, GLA_W,
             RET_W, RET_W, RET_W, RET_W,
             SSD_INNER, SSD_CONV_CH, SSD_HEADS,
             N_BRANCH * D_MODEL)

LANES = 128
SUBLANES = 8
VMEM_LIMIT = 56 * 1024 * 1024

ROW_TILE = 512
FOX_TQ = 512
FOX_TK = 512
FFN_TILE = 512
PROJ_PIECE = 512

F32 = jnp.float32
BF16 = jnp.bfloat16


def _rmsnorm(x, g):
    return x * lax.rsqrt(jnp.mean(x * x, axis=-1, keepdims=True) + NORM_EPS) * g


def _log_sigmoid(x):
    return jnp.minimum(x, 0.0) - jnp.log(1.0 + jnp.exp(-jnp.abs(x)))


def _softplus(x):
    return jnp.maximum(x, 0.0) + jnp.log(1.0 + jnp.exp(-jnp.abs(x)))


def _sigmoid(x):
    return 0.5 * jnp.tanh(0.5 * x) + 0.5


def _silu(x):
    return x * _sigmoid(x)


def _split3(x):
    hi = x.astype(BF16)
    r1 = x - hi.astype(F32)
    mid = r1.astype(BF16)
    lo = (r1 - mid.astype(F32)).astype(BF16)
    return hi, mid, lo


def _split2(x):
    hi = x.astype(BF16)
    return hi, (x - hi.astype(F32)).astype(BF16)


def _dot(a, b):
    return jnp.dot(a, b, preferred_element_type=F32)


def _dot_nt(a, b):
    return lax.dot_general(a, b, (((1,), (1,)), ((), ())), preferred_element_type=F32)


def _dot_tn(a, b):
    return lax.dot_general(a, b, (((0,), (0,)), ((), ())), preferred_element_type=F32)


def _sel_dot(mat, x):
    hi, mid, lo = _split3(x)
    return _dot(mat, hi) + _dot(mat, mid) + _dot(mat, lo)


def _dot_sel2(x, mat):
    hi, lo = _split2(x)
    return _dot(hi, mat) + _dot(lo, mat)


def _chunk_cumsum(x):
    t = x.shape[0]
    r = lax.broadcasted_iota(jnp.int32, (t, t), 0)
    c = lax.broadcasted_iota(jnp.int32, (t, t), 1)
    incl = jnp.where(((r // CHUNK) == (c // CHUNK)) & (c <= r), 1.0, 0.0).astype(BF16)
    hi, lo = _split2(x)
    b = _dot(incl, hi) + _dot(incl, lo)
    return b, _chunk_last(b) - b


def _chunk_last(b):
    t, w = b.shape
    return jnp.concatenate([jnp.broadcast_to(b[e - 1:e, :], (CHUNK, w)) for e in range(CHUNK, t + 1, CHUNK)],
                           axis=0)


def _group_mean_sq(o, group):
    w = o.shape[-1]
    r = lax.broadcasted_iota(jnp.int32, (w, w), 0)
    c = lax.broadcasted_iota(jnp.int32, (w, w), 1)
    bd = jnp.where((r // group) == (c // group), 1.0, 0.0).astype(BF16)
    return _dot_sel2(o * o, bd) * (1.0 / group)


def _project(h, wt_ref):
    n = wt_ref.shape[0]
    parts = []
    for a in range(0, n, PROJ_PIECE):
        parts.append(_dot_nt(h, wt_ref[a:min(a + PROJ_PIECE, n), :]))
        yield
    return jnp.concatenate(parts, axis=1)


def _layer(stacked, l):
    tail = tuple(stacked.shape[1:])
    return pl.BlockSpec((None,) + tail, lambda *_: (l,) + (0,) * len(tail), pipeline_mode=pl.Buffered(1))


def _rotary_kernel(pos_ref, inv_ref, sign_ref, cos_ref, sin_ref):
    ang = pos_ref[...] * inv_ref[...]
    cos_ref[...] = jnp.cos(ang)
    sin_ref[...] = jnp.sin(ang) * sign_ref[...]


def _rotary_tables(positions, seq):
    half = HEAD_DIM // 2
    inv = ROPE_THETA ** (-jnp.arange(half, dtype=F32) / half)
    reps = LANES // HEAD_DIM
    inv_row = jnp.tile(inv, 2 * reps).reshape(1, LANES)
    sign_row = jnp.tile(jnp.concatenate([-jnp.ones((half,), F32), jnp.ones((half,), F32)]),
                        reps).reshape(1, LANES)
    pos = positions.astype(F32).reshape(seq, 1)
    t = ROW_TILE
    return pl.pallas_call(
        _rotary_kernel,
        grid=(seq // t,),
        in_specs=[pl.BlockSpec((t, 1), lambda i: (i, 0)),
                  pl.BlockSpec((1, LANES), lambda i: (0, 0)),
                  pl.BlockSpec((1, LANES), lambda i: (0, 0))],
        out_specs=[pl.BlockSpec((t, LANES), lambda i: (i, 0)),
                   pl.BlockSpec((t, LANES), lambda i: (i, 0))],
        out_shape=[jax.ShapeDtypeStruct((seq, LANES), F32)] * 2,
        name="rotary_tables",
    )(pos, inv_row, sign_row)


SMALL_FF = 0
SMALL_GLR = FOX_HEADS
SMALL_DT = 24
LOG2E = math.log2(math.e)
F32_EXP_ZERO = 88.0
FOX_FAST_MAX_LOGIT = 40.0
FOX_BOUND_MARGIN = 1.02


def _fox_prep_body(h, small, w_ref, bf_ref, qn_ref, kn_ref, shift_ref, q_out, k_out, v_out, c_out, carry_ref):
    t = h.shape[0]
    u = yield from _project(h, w_ref)
    ls = _log_sigmoid(small + bf_ref[...])
    r = lax.broadcasted_iota(jnp.int32, (t, t), 0)
    cc = lax.broadcasted_iota(jnp.int32, (t, t), 1)
    tri = jnp.where(cc <= r, 1.0, 0.0).astype(BF16)
    c = _sel_dot(tri, ls) + carry_ref[0:1, :]
    carry_ref[0:1, :] = c[t - 1:t, :]
    c_out[0] = jnp.concatenate([c[0:1, :], c[t - 1:t, :], jnp.zeros((SUBLANES - 2, LANES), F32)], axis=0)
    c_hi, c_mid, c_lo = _split3(c * LOG2E)
    c_hi, c_mid, c_lo = c_hi.astype(F32), c_mid.astype(F32), c_lo.astype(F32)

    lane = lax.broadcasted_iota(jnp.int32, (t, LANES), 1)
    low = lane < HEAD_DIM
    j = lane - HEAD_DIM
    shift = shift_ref[...]

    def head_norm(pair, gain, mult):
        sq = pair * pair
        s_lo = jnp.sum(jnp.where(low, sq, 0.0), axis=-1, keepdims=True)
        s_hi = jnp.sum(jnp.where(low, 0.0, sq), axis=-1, keepdims=True)
        ms = jnp.where(low, s_lo, s_hi) * (1.0 / HEAD_DIM)
        return pair * lax.rsqrt(ms + NORM_EPS) * (gain * mult)

    for p in range(FOX_HEADS // 2):
        qp = head_norm(u[:, p * LANES:(p + 1) * LANES], qn_ref[...], HEAD_DIM ** -0.5 * LOG2E)
        kp = head_norm(u[:, FOX_W + p * LANES:FOX_W + (p + 1) * LANES], kn_ref[...], 1.0)
        vp = u[:, 2 * FOX_W + p * LANES:2 * FOX_W + (p + 1) * LANES]
        for sub in range(2):
            hd = 2 * p + sub
            if sub == 1:
                qp, kp, vp = (pltpu.roll(a, HEAD_DIM, axis=1) for a in (qp, kp, vp))
            ch, cm, cl = (jnp.broadcast_to(a[:, SMALL_FF + hd:SMALL_FF + hd + 1], (t, LANES)) for a in (c_hi, c_mid, c_lo))
            aug_q = jnp.where(j == 0, ch, jnp.where(j == 1, cm, jnp.where(j == 2, cl,
                              jnp.where(j < 7, 1.0, 0.0))))
            aug_k = jnp.where(j < 3, 1.0, jnp.where(j == 3, -ch, jnp.where(j == 4, -cm,
                              jnp.where(j == 5, -cl, jnp.where(j == 6, -shift, 0.0)))))
            aug_v = jnp.where(j == 0, 1.0, 0.0)
            q_out[hd] = jnp.where(low, qp, aug_q).astype(BF16)
            k_out[hd] = jnp.where(low, kp, aug_k).astype(BF16)
            v_out[hd] = jnp.where(low, vp, aug_v).astype(BF16)


def _fox_attn_kernel(par_ref, cq0_ref, cend_ref, q_ref, k_hbm, v_hbm, o_ref, k_ref, v_ref, kv_sems):
    i = pl.program_id(0)
    nh, tq, _ = q_ref.shape
    thr = par_ref[0]

    def kv_copies(j):
        rows = pl.ds(pl.multiple_of(j * FOX_TK, FOX_TK), FOX_TK)
        return [pltpu.make_async_copy(src.at[:, rows, :], dst.at[:, rows, :], kv_sems.at[w, j % 2])
                for w, (src, dst) in enumerate(((k_hbm, k_ref), (v_hbm, v_ref)))]

    @pl.when(i == 0)
    def _():
        for cp in kv_copies(0):
            cp.start()

    for cp in kv_copies(i):
        cp.wait()

    @pl.when(i + 1 < pl.num_programs(0))
    def _():
        for cp in kv_copies(i + 1):
            cp.start()

    def first_block(hd):
        c0 = cq0_ref[i, hd]
        return lax.while_loop(
            lambda j: jnp.logical_and(j > 0, c0 - cend_ref[jnp.maximum(j - 1, 0), hd] >= -thr),
            lambda j: j - 1, i)

    def keys(hd, j, nblk):
        off = pl.multiple_of(j * FOX_TK, FOX_TK)
        return k_ref[hd, pl.ds(off, nblk * FOX_TK), :], v_ref[hd, pl.ds(off, nblk * FOX_TK), :]

    def causal(s):
        r = lax.broadcasted_iota(jnp.int32, s.shape, 0)
        c = lax.broadcasted_iota(jnp.int32, s.shape, 1)
        return jnp.where(c <= r, s, -jnp.inf)

    def fixed_shift():
        def block(hd, j, nblk):
            k, v = keys(hd, j, nblk)
            return _dot(jnp.exp2(_dot_nt(q_ref[hd], k)).astype(BF16), v)

        def diagonal(hd):
            k, v = keys(hd, i, 1)
            return _dot(jnp.exp2(causal(_dot_nt(q_ref[hd], k))).astype(BF16), v)

        accs = [diagonal(hd) for hd in range(nh)]
        for hd in range(nh):
            acc, j_lo = accs[hd], first_block(hd)
            n = i - j_lo
            acc = lax.cond(n % 2 == 1, lambda: acc + block(hd, j_lo, 1), lambda: acc)
            acc = lax.cond((n // 2) % 2 == 1, lambda: acc + block(hd, j_lo + n % 2, 2), lambda: acc)
            first = j_lo + n % 4
            accs[hd] = lax.fori_loop(0, n // 4, lambda p, acc: acc + block(hd, first + 4 * p, 4), acc)
        return accs

    def running_max():
        def one_head(hd):
            q = q_ref[hd]

            def step(j, carry, masked):
                m, acc = carry
                k, v = keys(hd, j, 1)
                s = _dot_nt(q, k)
                if masked:
                    s = causal(s)
                m_new = jnp.maximum(m, jnp.max(s, axis=-1, keepdims=True))
                p = jnp.exp2(s - m_new)
                return m_new, jnp.exp2(m - m_new) * acc + _dot(p.astype(BF16), v)
            init = (jnp.full((tq, 1), -jnp.inf, F32), jnp.zeros((tq, LANES), F32))
            carry = step(i, init, True)
            return lax.fori_loop(first_block(hd), i, lambda j, c: step(j, c, False), carry)[1]
        return [one_head(hd) for hd in range(nh)]

    accs = lax.cond(par_ref[1] > 0.0, fixed_shift, running_max)
    outs = [acc / acc[:, HEAD_DIM:HEAD_DIM + 1] for acc in accs]
    low = lax.broadcasted_iota(jnp.int32, (tq, LANES), 1) < HEAD_DIM
    for p in range(nh // FOX_HEADS_PER_TILE):
        o_ref[p] = jnp.where(low, outs[2 * p], pltpu.roll(outs[2 * p + 1], HEAD_DIM, axis=1)).astype(BF16)


FOX_HEADS_PER_TILE = LANES // HEAD_DIM


def _fox_attn(par, cq0, cend, q, k, v):
    assert FOX_TQ == FOX_TK and FOX_HEADS_PER_TILE == 2
    nh, seq, _ = q.shape
    kv_spec = pl.BlockSpec(memory_space=pl.ANY)
    kv_vmem = pltpu.VMEM((nh, seq, LANES), BF16)
    return pl.pallas_call(
        _fox_attn_kernel,
        grid_spec=pltpu.PrefetchScalarGridSpec(
            num_scalar_prefetch=3,
            grid=(seq // FOX_TQ,),
            in_specs=[pl.BlockSpec((nh, FOX_TQ, LANES), lambda i, *_: (0, i, 0)), kv_spec, kv_spec],
            out_specs=pl.BlockSpec((nh // FOX_HEADS_PER_TILE, FOX_TQ, LANES), lambda i, *_: (0, i, 0)),
            scratch_shapes=[kv_vmem, kv_vmem, pltpu.SemaphoreType.DMA((2, 2))]),
        out_shape=jax.ShapeDtypeStruct((nh // FOX_HEADS_PER_TILE, seq, LANES), BF16),
        compiler_params=pltpu.CompilerParams(dimension_semantics=("arbitrary",),
                                             vmem_limit_bytes=VMEM_LIMIT),
        name="fox_attn",
    )(par, cq0, cend, q, k, v)


def _fox_params(qn, kn):
    bound = FOX_BOUND_MARGIN * HEAD_DIM ** 0.5 * jnp.max(jnp.abs(qn), axis=1) * jnp.max(jnp.abs(kn), axis=1)
    shift = jnp.broadcast_to((LOG2E * bound)[:, None, None], (bound.shape[0], 1, LANES))
    par = jnp.stack([F32_EXP_ZERO + 2.0 * bound, (bound < FOX_FAST_MAX_LOGIT).astype(F32)], axis=1)
    return shift, par


def _fox_attention(par, q, k, v, c):
    per_tile = FOX_TQ // ROW_TILE
    heads = slice(SMALL_FF, SMALL_FF + FOX_HEADS)
    cq0 = c[0::per_tile, 0, heads]
    cend = c[per_tile - 1::per_tile, 1, heads]
    return _fox_attn(par, cq0, cend, q, k, v)


def _tile_rows(a, n):
    return jnp.concatenate([a] * n, axis=0)


def _linear_attn_chunks(qd_s, ki_s, ke_s, v_s, dec_s, o_s, st_s, nheads):
    t, w = qd_s.shape
    assert w == nheads * HEAD_DIM
    r = lax.broadcasted_iota(jnp.int32, (w, w), 0)
    c = lax.broadcasted_iota(jnp.int32, (w, w), 1)
    same_head = (r // HEAD_DIM) == (c // HEAD_DIM)
    rr = lax.broadcasted_iota(jnp.int32, (CHUNK, w), 0)
    cc = lax.broadcasted_iota(jnp.int32, (CHUNK, w), 1)
    tril = (cc % CHUNK) <= rr
    state = st_s[...]
    for ci in range(t // CHUNK):
        rows = slice(ci * CHUNK, (ci + 1) * CHUNK)
        last = slice((ci + 1) * CHUNK - 1, (ci + 1) * CHUNK)
        qd = qd_s[rows, :].astype(BF16)
        ke = ke_s[rows, :].astype(BF16)
        vv = v_s[rows, :].astype(BF16)
        k_bd = jnp.where(same_head, _tile_rows(ki_s[rows, :], nheads), 0.0).astype(BF16)
        v_bd = jnp.where(same_head, _tile_rows(v_s[rows, :], nheads), 0.0).astype(BF16)
        attn = jnp.where(tril, _dot_nt(qd, k_bd), 0.0).astype(BF16)
        o_s[rows, :] = _dot(attn, v_bd) + _dot_nt(qd, state.astype(BF16))
        state = state * dec_s[last, :] + jnp.where(same_head, _dot_tn(vv, ke), 0.0)
        yield
    st_s[...] = state


def _gla_body(h, small, w_ref, w2_ref, b2_ref, gn_ref, y_ref, qd_s, ki_s, ke_s, v_s, dec_s, o_s, st_s):
    u = yield from _project(h, w_ref)
    z = _dot(small.astype(BF16), w2_ref[...]) + b2_ref[...]
    log_a = _log_sigmoid(z) * (1.0 / GLA_TAU)
    b, rem = _chunk_cumsum(log_a)
    eb = jnp.exp(b)
    k = u[:, GLA_W:2 * GLA_W]
    qd_s[...] = u[:, 0:GLA_W] * (HEAD_DIM ** -0.5) * eb
    ki_s[...] = k * jnp.exp(-b)
    ke_s[...] = k * jnp.exp(rem)
    v_s[...] = u[:, 2 * GLA_W:3 * GLA_W]
    dec_s[...] = eb
    yield
    yield from _linear_attn_chunks(qd_s, ki_s, ke_s, v_s, dec_s, o_s, st_s, GLA_HEADS)
    o = o_s[...]
    y = o * lax.rsqrt(_group_mean_sq(o, HEAD_DIM) + NORM_EPS) * gn_ref[...]
    y_ref[...] = (y * _silu(u[:, 3 * GLA_W:4 * GLA_W])).astype(BF16)


def _ret_body(h, w_ref, cos_ref, sin_ref, gn_ref, y_ref, qd_s, ki_s, ke_s, v_s, dec_s, o_s, st_s):
    t = h.shape[0]
    u = yield from _project(h, w_ref)
    cos, sin = cos_ref[...], sin_ref[...]
    lane = lax.broadcasted_iota(jnp.int32, (t, RET_W), 1)
    half = HEAD_DIM // 2
    first_half = (lax.broadcasted_iota(jnp.int32, (t, LANES), 1) % HEAD_DIM) < half

    def rotate(a):
        blocks = []
        for b in range(0, RET_W, LANES):
            ab = a[:, b:b + LANES]
            swapped = jnp.where(first_half, pltpu.roll(ab, LANES - half, axis=1),
                                pltpu.roll(ab, half, axis=1))
            blocks.append(ab * cos + swapped * sin)
        return jnp.concatenate(blocks, axis=1)

    q = rotate(u[:, 0:RET_W])
    k = rotate(u[:, RET_W:2 * RET_W]) * (HEAD_DIM ** -0.5)
    row = lax.broadcasted_iota(jnp.int32, (t, RET_W), 0)
    lg = jnp.zeros((t, RET_W), F32)
    for hd in range(RET_HEADS):
        lg = jnp.where(lane // HEAD_DIM == hd, math.log(1.0 - 2.0 ** (-5.0 - hd)), lg)
    pos = (row % CHUNK).astype(F32)
    qw = jnp.exp((pos + 1.0) * lg)
    qd_s[...] = q * qw
    ki_s[...] = k * jnp.exp(-(pos + 1.0) * lg)
    ke_s[...] = k * jnp.exp((CHUNK - 1.0 - pos) * lg)
    v_s[...] = u[:, 2 * RET_W:3 * RET_W]
    dec_s[...] = qw
    yield
    yield from _linear_attn_chunks(qd_s, ki_s, ke_s, v_s, dec_s, o_s, st_s, RET_HEADS)
    o = o_s[...]
    y = o * lax.rsqrt(_group_mean_sq(o, HEAD_DIM) + NORM_EPS) * gn_ref[...]
    y_ref[...] = (y * _silu(u[:, 3 * RET_W:4 * RET_W])).astype(BF16)


def _linear_scratch(t, width, nheads):
    return [pltpu.VMEM((t, width), F32)] * 6 + [pltpu.VMEM((width, width), F32)]


SSD_PROJ = SSD_INNER + SSD_CONV_CH


def _ssd_body(h, small, w_ref, cw_ref, cb_ref, dtb_row_ref, alog_row_ref, d_row_ref, gn_ref, y_ref,
              xp_s, xs_s, bc_s, cs_s, ecs_s, xdt_s, xdtw_s, cst_s, o_s, st_s):
    t = h.shape[0]
    nc = t // CHUNK
    u = yield from _project(h, w_ref)
    z = u[:, 0:SSD_INNER]

    xp_s[SUBLANES:SUBLANES + t, :] = u[:, SSD_INNER:SSD_PROJ]
    conv = cb_ref[...] + cw_ref[SSD_CONV - 1:SSD_CONV, :] * xp_s[SUBLANES:SUBLANES + t, :]
    for kk in range(SSD_CONV - 1):
        off = SUBLANES - (SSD_CONV - 1) + kk
        conv = conv + cw_ref[kk:kk + 1, :] * xp_s[off:off + t, :]
    xp_s[0:SUBLANES, :] = xp_s[t:t + SUBLANES, :]
    xbc = _silu(conv)
    xs = xbc[:, 0:SSD_INNER]
    xs_s[...] = xs
    bc_s[...] = xbc[:, SSD_INNER:SSD_CONV_CH]

    dt = _softplus(small + dtb_row_ref[...])
    dta = dt * -jnp.exp(alog_row_ref[...])
    cs, _ = _chunk_cumsum(dta)
    cs_t = cs.T[SMALL_DT:SMALL_DT + SSD_HEADS, :]
    hpg = SSD_HEADS // SSD_GROUPS
    gcols = hpg * HEAD_DIM
    for ci in range(nc):
        for g in range(SSD_GROUPS):
            cst_s[ci, g:g + 1, :] = jnp.concatenate(
                [cs_t[g * hpg + rr:g * hpg + rr + 1, ci * CHUNK:(ci + 1) * CHUNK] for rr in range(hpg)], axis=1)

    er = lax.broadcasted_iota(jnp.int32, (LANES, SSD_INNER), 0)
    ec = lax.broadcasted_iota(jnp.int32, (LANES, SSD_INNER), 1)
    expand = jnp.where(ec // HEAD_DIM == er - SMALL_DT, 1.0, 0.0).astype(BF16)
    cs_x = _dot_sel2(cs, expand)
    xdt = xs * _dot_sel2(dt, expand)
    cs_s[...] = cs_x
    ecs_s[...] = jnp.exp(cs_x)
    xdt_s[...] = xdt
    xdtw_s[...] = xdt * jnp.exp(_chunk_last(cs_x) - cs_x)

    r = lax.broadcasted_iota(jnp.int32, (gcols, gcols), 0)
    c = lax.broadcasted_iota(jnp.int32, (gcols, gcols), 1)
    same_head = (r // HEAD_DIM) == (c // HEAD_DIM)
    rr = lax.broadcasted_iota(jnp.int32, (CHUNK, gcols), 0)
    cc = lax.broadcasted_iota(jnp.int32, (CHUNK, gcols), 1)
    tril = (cc % CHUNK) <= rr
    gw = SSD_GROUPS * SSD_STATE

    states = [st_s[g] for g in range(SSD_GROUPS)]
    for ci in range(nc):
        rows = slice(ci * CHUNK, (ci + 1) * CHUNK)
        last = slice((ci + 1) * CHUNK - 1, (ci + 1) * CHUNK)
        for g in range(SSD_GROUPS):
            gsl = slice(g * gcols, (g + 1) * gcols)
            bm = bc_s[rows, g * SSD_STATE:(g + 1) * SSD_STATE].astype(BF16)
            cm = bc_s[rows, gw + g * SSD_STATE:gw + (g + 1) * SSD_STATE].astype(BF16)
            cb = _dot_nt(cm, _tile_rows(bm, hpg))
            decay = jnp.exp(jnp.where(tril, cs_s[rows, gsl] - cst_s[ci, g:g + 1, :], -jnp.inf))
            x_bd = jnp.where(same_head, _tile_rows(xdt_s[rows, gsl], hpg), 0.0).astype(BF16)
            y_inter = _dot(cm, states[g].astype(BF16)) * ecs_s[rows, gsl]
            o_s[rows, gsl] = _dot((cb * decay).astype(BF16), x_bd) + y_inter
            states[g] = (states[g] * ecs_s[last, gsl]
                         + _dot_tn(bm, xdtw_s[rows, gsl].astype(BF16)))
        yield
    for g in range(SSD_GROUPS):
        st_s[g] = states[g]

    y = (o_s[...] + d_row_ref[...] * xs_s[...]) * _silu(z)
    gwid = SSD_INNER // SSD_GROUPS
    normed = []
    for g in range(SSD_GROUPS):
        yg = y[:, g * gwid:(g + 1) * gwid]
        normed.append(yg * lax.rsqrt(jnp.mean(yg * yg, axis=-1, keepdims=True) + NORM_EPS))
    y_ref[...] = (jnp.concatenate(normed, axis=1) * gn_ref[...]).astype(BF16)


N_FOX_IN, N_GLA_IN, N_RET_IN, N_SSD_IN = 5, 4, 4, 7
N_LINEAR_SCRATCH = 7
N_SSD_SCRATCH = 10


def _mixers_kernel(x_ref, ln_ref, ws_ref, *refs):
    it = iter(refs)
    take = lambda n: [next(it) for _ in range(n)]
    fox_in, gla_in, ret_in, ssd_in = take(N_FOX_IN), take(N_GLA_IN), take(N_RET_IN), take(N_SSD_IN)
    fox_out, (yb_ref, yc_ref, yd_ref) = take(4), take(3)
    (carry_ref,), gla_s, ret_s, ssd_s = take(1), take(N_LINEAR_SCRATCH), take(N_LINEAR_SCRATCH), take(N_SSD_SCRATCH)

    @pl.when(pl.program_id(0) == 0)
    def _():
        carry_ref[...] = jnp.zeros_like(carry_ref)
        gla_s[-1][...] = jnp.zeros_like(gla_s[-1])
        ret_s[-1][...] = jnp.zeros_like(ret_s[-1])
        ssd_s[-1][...] = jnp.zeros_like(ssd_s[-1])
        ssd_s[0][0:SUBLANES, :] = jnp.zeros((SUBLANES, SSD_CONV_CH), F32)

    h = _rmsnorm(x_ref[...], ln_ref[...]).astype(BF16)
    small = _dot_nt(h, ws_ref[...])
    waiting = [_ssd_body(h, small, *ssd_in, yd_ref, *ssd_s), _gla_body(h, small, *gla_in, yb_ref, *gla_s),
               _ret_body(h, *ret_in, yc_ref, *ret_s), _fox_prep_body(h, small, *fox_in, *fox_out, carry_ref)]
    pending = []
    while waiting or pending:
        if waiting:
            pending.append(waiting.pop(0))
        for body in list(pending):
            if next(body, "done") == "done":
                pending.remove(body)


def _mixers(l, x, ln, w_small, fox_in, gla_in, ret_in, ssd_in, cos_t, sin_t):
    seq = x.shape[0]
    t = ROW_TILE
    row = lambda w: pl.BlockSpec((t, w), lambda i: (i, 0))
    heads = pl.BlockSpec((FOX_HEADS, t, LANES), lambda i: (0, i, 0))
    head_shape = jax.ShapeDtypeStruct((FOX_HEADS, seq, LANES), BF16)
    w_ret, ret_gn = ret_in
    wide = pltpu.VMEM((t, SSD_INNER), F32)
    return pl.pallas_call(
        _mixers_kernel,
        grid=(seq // t,),
        in_specs=([row(D_MODEL), _layer(ln, l), _layer(w_small, l)] + [_layer(a, l) for a in fox_in]
                  + [_layer(a, l) for a in gla_in]
                  + [_layer(w_ret, l), row(LANES), row(LANES), _layer(ret_gn, l)]
                  + [_layer(a, l) for a in ssd_in]),
        out_specs=[heads, heads, heads, pl.BlockSpec((1, SUBLANES, LANES), lambda i: (i, 0, 0)),
                   row(GLA_W), row(RET_W), row(SSD_INNER)],
        out_shape=[head_shape, head_shape, head_shape,
                   jax.ShapeDtypeStruct((seq // t, SUBLANES, LANES), F32),
                   jax.ShapeDtypeStruct((seq, GLA_W), BF16),
                   jax.ShapeDtypeStruct((seq, RET_W), BF16),
                   jax.ShapeDtypeStruct((seq, SSD_INNER), BF16)],
        scratch_shapes=([pltpu.VMEM((SUBLANES, LANES), F32)]
                        + _linear_scratch(t, GLA_W, GLA_HEADS) + _linear_scratch(t, RET_W, RET_HEADS)
                        + [pltpu.VMEM((t + 2 * SUBLANES, SSD_CONV_CH), F32),
                           wide,
                           pltpu.VMEM((t, 2 * SSD_GROUPS * SSD_STATE), F32),
                           wide, wide, wide, wide,
                           pltpu.VMEM((t // CHUNK, SSD_GROUPS, SSD_INNER // SSD_GROUPS), F32),
                           wide,
                           pltpu.VMEM((SSD_GROUPS, SSD_STATE, SSD_INNER // SSD_GROUPS), F32)]),
        compiler_params=pltpu.CompilerParams(dimension_semantics=("arbitrary",),
                                             vmem_limit_bytes=VMEM_LIMIT),
        name="mixers",
    )(x, ln, w_small, *fox_in, *gla_in, w_ret, cos_t, sin_t, ret_gn, *ssd_in)


def _merge_ffn_kernel(x_ref, ln1_ref, wg_ref, ya_ref, yb_ref, yc_ref, yd_ref,
                      wa_ref, wb_ref, wc_ref, wd_ref, wo_ref, ln2_ref, w1_ref, w2_ref, out_ref):
    x = x_ref[...]
    h = _rmsnorm(x, ln1_ref[...]).astype(BF16)

    def gate(b):
        return _sigmoid(_dot_nt(h, wg_ref[b * D_MODEL:(b + 1) * D_MODEL, :]))

    up_a = _dot(ya_ref[0], wa_ref[0:LANES, :])
    for p in range(1, FOX_HEADS // FOX_HEADS_PER_TILE):
        up_a = up_a + _dot(ya_ref[p], wa_ref[p * LANES:(p + 1) * LANES, :])
    merged = gate(0) * up_a
    merged = merged + gate(1) * _dot(yb_ref[...], wb_ref[...])
    merged = merged + gate(2) * _dot(yc_ref[...], wc_ref[...])
    merged = merged + gate(3) * _dot(yd_ref[...], wd_ref[...])
    x = x + _dot(merged.astype(BF16), wo_ref[...])

    h = _rmsnorm(x, ln2_ref[...]).astype(BF16)
    out = x
    split = 6 * 256
    for a, b in ((0, split), (split, FFN_HIDDEN)):
        act = _silu(_dot(h, w1_ref[:, a:b])) * _dot(h, w1_ref[:, FFN_HIDDEN + a:FFN_HIDDEN + b])
        out = out + _dot(act.astype(BF16), w2_ref[a:b, :])
    out_ref[...] = out


def _merge_ffn(l, x, ln1, w_gates, y_a, y_b, y_c, y_d, w_up_a, w_up_b, w_up_c, w_up_d, w_out, ln2, w1, w2):
    seq = x.shape[0]
    t = FFN_TILE
    row = lambda w: pl.BlockSpec((t, w), lambda i: (i, 0))
    return pl.pallas_call(
        _merge_ffn_kernel,
        grid=(seq // t,),
        in_specs=[row(D_MODEL), _layer(ln1, l), _layer(w_gates, l),
                  pl.BlockSpec((FOX_HEADS // FOX_HEADS_PER_TILE, t, LANES), lambda i: (0, i, 0)),
                  row(GLA_W), row(RET_W), row(SSD_INNER),
                  _layer(w_up_a, l), _layer(w_up_b, l), _layer(w_up_c, l),
                  _layer(w_up_d, l), _layer(w_out, l), _layer(ln2, l), _layer(w1, l), _layer(w2, l)],
        out_specs=row(D_MODEL),
        out_shape=jax.ShapeDtypeStruct((seq, D_MODEL), F32),
        compiler_params=pltpu.CompilerParams(dimension_semantics=("arbitrary",),
                                             vmem_limit_bytes=VMEM_LIMIT),
        name="merge_swiglu",
    )(x, ln1, w_gates, y_a, y_b, y_c, y_d, w_up_a, w_up_b, w_up_c, w_up_d, w_out, ln2, w1, w2)


def _rows(a):
    return a.reshape(a.shape[0], 1, -1)


def _in_proj_weights(w_in):
    offs = np.concatenate([[0], np.cumsum(IN_SPLITS)])
    (fq, fk, fv, ff, gq, gk, gv, glr, gr, rq, rk, rv, rg, z, xbc, dt, gates) = [
        w_in[:, :, offs[n]:offs[n + 1]] for n in range(len(IN_SPLITS))]
    group = lambda parts: jnp.transpose(jnp.concatenate(parts, axis=2), (0, 2, 1)).astype(BF16)
    gap = lambda n: jnp.zeros(w_in.shape[:2] + (n,), w_in.dtype)
    w_small = group([ff, glr, gap(SMALL_DT - SMALL_GLR - GLA_LOWRANK), dt, gap(LANES - SMALL_DT - SSD_HEADS)])
    return w_small, group([fq, fk, fv]), group([gq, gk, gv, gr]), group([rq, rk, rv, rg]), group([z, xbc]), group([gates])


def kernel(x, positions, ln1, ln2, w_in, fox_bf, fox_qn, fox_kn, gla_w2, gla_b, gla_norm, ret_norm,
           ssd_conv_w, ssd_conv_b, ssd_dt_bias, ssd_a_log, ssd_d, ssd_norm,
           w_up_a, w_up_b, w_up_c, w_up_d, w_out, w_ffn_in, w_ffn_out):
    bsz, seq, d = x.shape
    assert bsz == 1 and d == D_MODEL and seq % ROW_TILE == 0 and seq % FOX_TQ == 0 and FOX_TQ % ROW_TILE == 0
    depth = ln1.shape[0]
    xr = x.reshape(seq, d)
    cos_t, sin_t = _rotary_tables(positions, seq)

    bf16 = lambda w: w.astype(BF16)
    ln1_r, ln2_r = _rows(ln1), _rows(ln2)
    fox_bf_r = _rows(jnp.pad(fox_bf, ((0, 0), (SMALL_FF, LANES - SMALL_FF - FOX_HEADS))))
    fox_qn_r, fox_kn_r = _rows(jnp.tile(fox_qn, (1, 2))), _rows(jnp.tile(fox_kn, (1, 2)))
    fox_shift_r, fox_par = _fox_params(fox_qn, fox_kn)
    gla_w2_p = bf16(jnp.pad(gla_w2, ((0, 0), (SMALL_GLR, LANES - SMALL_GLR - GLA_LOWRANK), (0, 0))))
    gla_b_r, gla_gn_r = _rows(gla_b), _rows(jnp.tile(gla_norm, (1, GLA_HEADS)))
    ret_gn_r = _rows(jnp.tile(ret_norm, (1, RET_HEADS)))
    conv_b_r = _rows(ssd_conv_b)
    at_dt = lambda a: _rows(jnp.pad(a, ((0, 0), (SMALL_DT, LANES - SMALL_DT - SSD_HEADS))))
    dtb_r, alog_r = at_dt(ssd_dt_bias), at_dt(ssd_a_log)
    d_r, ssd_gn_r = _rows(jnp.repeat(ssd_d, HEAD_DIM, axis=1)), _rows(ssd_norm)
    up_a, up_b, up_c, up_d, w_o = bf16(w_up_a), bf16(w_up_b), bf16(w_up_c), bf16(w_up_d), bf16(w_out)
    w1, w2 = bf16(w_ffn_in), bf16(w_ffn_out)
    w_small, w_fox, w_gla, w_ret, w_ssd, w_gates = _in_proj_weights(w_in)

    for l in range(depth):
        q, k, v, c, y_b, y_c, y_d = _mixers(
            l, xr, ln1_r, w_small,
            (w_fox, fox_bf_r, fox_qn_r, fox_kn_r, fox_shift_r),
            (w_gla, gla_w2_p, gla_b_r, gla_gn_r),
            (w_ret, ret_gn_r),
            (w_ssd, ssd_conv_w, conv_b_r, dtb_r, alog_r, d_r, ssd_gn_r),
            cos_t, sin_t)
        y_a = _fox_attention(fox_par[l], q, k, v, c)
        xr = _merge_ffn(l, xr, ln1_r, w_gates, y_a, y_b, y_c, y_d, up_a, up_b, up_c, up_d, w_o, ln2_r, w1, w2)
    return xr.reshape(bsz, seq, d)
```
